```python
import math
import jax
import jax.numpy as jnp
from jax import lax
import numpy as np

D_MODEL = 1024
BATCH = 1
SEQ = 16384
DEPTH = 4
DEC_BATCH = 16
DEC_SEQ = 4096
PAST_LEN = 128

HC = D_MODEL // 2
HYENA_ORDER = 2
FILTER_EMB = 33
FILTER_BANDS = (FILTER_EMB - 1) // 2
FILTER_HIDDEN = 64
DECAY_TARGET = 1e-2
FAST_DECAY_PCT = 0.3
SLOW_DECAY_PCT = 1.5
DECAY_SHIFT = 0.05
SHORT_CONV = 3
B_HEAD_DIM = 64
B_HEADS = (D_MODEL // 2) // B_HEAD_DIM
B_KV_HEADS = 2
B_GROUP = B_HEADS // B_KV_HEADS
B_WIDTH = B_HEADS * B_HEAD_DIM
B_KV_WIDTH = B_KV_HEADS * B_HEAD_DIM
B_WINDOW = 128
B_BLOCK = 128
C_PATTERNS = ((128, 1), (512, 4), (2048, 16))
N_PAT = len(C_PATTERNS)
C_HEADS = 8
C_HEAD_DIM = D_MODEL // C_HEADS
C_WIDTH = C_HEADS * C_HEAD_DIM
C_BLOCK = 64
ROPE_THETA = 500000.0
ROPE_FRACTION = 4
NORM_EPS = 1e-6
NEG_INF = -1e30
EVEN_IN = 4 * HC + 2 * B_WIDTH + 2 * B_KV_WIDTH
EVEN_INNER = HC + B_WIDTH
ODD_IN = N_PAT * 3 * C_WIDTH + C_WIDTH
N_EVEN = (DEPTH + 1) // 2
N_ODD = DEPTH // 2

kernel_name = "hybrid_hyena_swa_dilated_encoder"


def rmsnorm(x, g):
    xf = x.astype(jnp.float32)
    y = xf * jax.lax.rsqrt(jnp.mean(xf * xf, axis=-1, keepdims=True) + NORM_EPS)
    return (y * g.astype(jnp.float32)).astype(x.dtype)


def rope(x, pos):
    hd = x.shape[-1]
    rot = hd // ROPE_FRACTION
    half = rot // 2
    inv = jnp.power(ROPE_THETA, -2.0 * jnp.arange(half, dtype=jnp.float32) / rot)
    ang = pos.astype(jnp.float32)[:, None] * inv[None, :]
    cos = jnp.cos(ang)[:, None, :]
    sin = jnp.sin(ang)[:, None, :]
    xf = x.astype(jnp.float32)
    x1, x2 = xf[..., :half], xf[..., half:rot]
    out = jnp.concatenate([x1 * cos - x2 * sin, x2 * cos + x1 * sin, xf[..., rot:]], axis=-1)
    return out.astype(x.dtype)


def short_conv(u, w, b):
    up = jnp.pad(u, ((0, 0), (1, 1), (0, 0)))
    return up[:, :-2] * w[0] + up[:, 1:-1] * w[1] + up[:, 2:] * w[2] + b


def hyena_filter_spectrum(L, w1, b1, f1, w2, b2, f2, w3):
    f32 = jnp.float32
    t = jnp.linspace(0.0, 1.0, L, dtype=f32)[:, None]
    wpos = 2.0 * math.pi * jnp.arange(L, dtype=f32)[:, None] / L
    bands = jnp.linspace(1e-4, FILTER_BANDS - 1, FILTER_BANDS, dtype=f32)[None, :]
    feats = jnp.concatenate([t, jnp.cos(bands * wpos), -jnp.sin(bands * wpos)], axis=-1)
    h = jnp.sin(f1.astype(f32) * (feats @ w1.astype(f32) + b1.astype(f32)))
    h = jnp.sin(f2.astype(f32) * (h @ w2.astype(f32) + b2.astype(f32)))
    h = (h @ w3.astype(f32)).reshape(L, HYENA_ORDER, 2, HC)
    max_decay = math.log(DECAY_TARGET) / FAST_DECAY_PCT
    min_decay = math.log(DECAY_TARGET) / SLOW_DECAY_PCT
    deltas = jnp.abs(jnp.linspace(min_decay, max_decay, HC, dtype=f32))
    window = jnp.exp(-t * deltas[None, :]) + DECAY_SHIFT
    h = h * window[:, None, None, :]
    fwd, bwd = h[:, :, 0], h[:, :, 1]
    filt = jnp.concatenate([fwd[:1] + bwd[:1], fwd[1:], jnp.zeros_like(fwd[:1]), bwd[1:][::-1]], axis=0)
    return jnp.fft.rfft(filt, axis=0)


def hyena_mix(u, spec, hyena_d):
    L = u.shape[1]
    x1, x2, z = jnp.split(u.astype(jnp.float32), 3, axis=-1)
    gates = (x1, x2)
    for o in range(HYENA_ORDER):
        zf = jnp.fft.rfft(z, n=2 * L, axis=1)
        conv = jnp.fft.irfft(zf * spec[None, :, o, :], n=2 * L, axis=1)[:, :L]
        z = gates[o] * (conv + hyena_d[o].astype(jnp.float32) * z)
    return z.astype(u.dtype)


def band_attention(q, k, v, radius, block, n_valid, sink=None):
    B, L, Hk, G, hd = q.shape
    nb = L // block
    qb = q.reshape(B, nb, block, Hk, G, hd)
    pad = ((0, 0), (block, block), (0, 0), (0, 0))
    kp = jnp.pad(k, pad).reshape(B, nb + 2, block, Hk, hd)
    vp = jnp.pad(v, pad).reshape(B, nb + 2, block, Hk, hd)
    kw = jnp.concatenate([kp[:, :-2], kp[:, 1:-1], kp[:, 2:]], axis=2)
    vw = jnp.concatenate([vp[:, :-2], vp[:, 1:-1], vp[:, 2:]], axis=2).astype(jnp.float32)
    qpos = jnp.arange(L).reshape(nb, block)
    kpos = jnp.arange(nb)[:, None] * block - block + jnp.arange(3 * block)[None, :]
    rel = kpos[:, None, :] - qpos[:, :, None]
    valid = (jnp.abs(rel) <= radius) & (kpos[:, None, :] >= 0) & (kpos[:, None, :] < n_valid)
    s = jnp.einsum('bnqhgd,bnkhd->bnhgqk', qb, kw, preferred_element_type=jnp.float32) * (hd ** -0.5)
    s = jnp.where(valid[None, :, None, None], s, NEG_INF)
    m = jnp.max(s, axis=-1)
    if sink is not None:
        sink_b = sink.astype(jnp.float32).reshape(1, 1, Hk, G, 1)
        m = jnp.maximum(m, sink_b)
    p = jnp.exp(s - m[..., None])
    denom = jnp.sum(p, axis=-1)
    if sink is not None:
        denom = denom + jnp.exp(sink_b - m)
    o = jnp.einsum('bnhgqk,bnkhd->bnqhgd', p, vw)
    o = o / jnp.transpose(denom, (0, 1, 4, 2, 3))[..., None]
    lse = jnp.transpose(m + jnp.log(denom), (0, 1, 4, 2, 3)).reshape(B, L, Hk, G)
    return o.reshape(B, L, Hk, G, hd).astype(q.dtype), lse


def dilated_attention(q, k, v, window, dilation):
    B, L, H, hd = q.shape
    radius = window // (2 * dilation)
    Ls = L // dilation
    Lp = -(-Ls // C_BLOCK) * C_BLOCK

    def strided(t):
        t = t.reshape(B, Ls, dilation, H, hd).swapaxes(1, 2).reshape(B * dilation, Ls, H, hd)
        return jnp.pad(t, ((0, 0), (0, Lp - Ls), (0, 0), (0, 0)))

    o, lse = band_attention(strided(q)[:, :, :, None], strided(k), strided(v), radius, C_BLOCK, Ls)
    o = o[:, :Ls, :, 0].reshape(B, dilation, Ls, H, hd).swapaxes(1, 2).reshape(B, L, H, hd)
    lse = lse[:, :Ls, :, 0].reshape(B, dilation, Ls, H).swapaxes(1, 2).reshape(B, L, H)
    return o, lse


def hybrid_layer(x, norm_g, w_in, conv_w, conv_b, fw1, fb1, ff1, fw2, fb2, ff2, fw3, hyena_d, sink, w_out):
    B, L, _ = x.shape
    pos = jnp.arange(L)
    proj = rmsnorm(x, norm_g) @ w_in
    cuts = [3 * HC, 4 * HC, 4 * HC + B_WIDTH, 4 * HC + B_WIDTH + B_KV_WIDTH,
            4 * HC + B_WIDTH + 2 * B_KV_WIDTH]
    hy_in, hy_gate, q, k, v, at_gate = jnp.split(proj, cuts, axis=-1)
    spec = hyena_filter_spectrum(L, fw1, fb1, ff1, fw2, fb2, ff2, fw3)
    hy = hyena_mix(short_conv(hy_in, conv_w, conv_b), spec, hyena_d)
    q = rope(q.reshape(B, L, B_HEADS, B_HEAD_DIM), pos).reshape(B, L, B_KV_HEADS, B_GROUP, B_HEAD_DIM)
    k = rope(k.reshape(B, L, B_KV_HEADS, B_HEAD_DIM), pos)
    v = v.reshape(B, L, B_KV_HEADS, B_HEAD_DIM)
    at, _ = band_attention(q, k, v, B_WINDOW, B_BLOCK, L, sink)
    mixed = jnp.concatenate([hy * jax.nn.silu(hy_gate),
                             at.reshape(B, L, B_WIDTH) * jax.nn.silu(at_gate)], axis=-1)
    return x + mixed @ w_out


def dilated_layer(x, norm_g, w_in, w_out):
    B, L, _ = x.shape
    pos = jnp.arange(L)
    proj = rmsnorm(x, norm_g) @ w_in
    qkv = proj[..., :N_PAT * 3 * C_WIDTH].reshape(B, L, N_PAT, 3, C_HEADS, C_HEAD_DIM)
    gate = proj[..., N_PAT * 3 * C_WIDTH:]
    outs, lses = [], []
    for g, (window, dilation) in enumerate(C_PATTERNS):
        q = rope(qkv[:, :, g, 0], pos)
        k = rope(qkv[:, :, g, 1], pos)
        o, lse = dilated_attention(q, k, qkv[:, :, g, 2], window, dilation)
        outs.append(o)
        lses.append(lse)
    alpha = jax.nn.softmax(jnp.stack(lses), axis=0)
    o = jnp.einsum('pblh,pblhd->blhd', alpha, jnp.stack(outs).astype(jnp.float32))
    y = o.reshape(B, L, C_WIDTH).astype(x.dtype) * jax.nn.silu(gate)
    return x + y @ w_out


def trunk(x, a_norm, a_w_in, a_conv_w, a_conv_b, a_filt_w1, a_filt_b1, a_filt_f1, a_filt_w2,
          a_filt_b2, a_filt_f2, a_filt_w3, a_hyena_d, a_sink, a_w_out, c_norm, c_w_in, c_w_out,
          final_norm):
    for layer in range(DEPTH):
        i = layer // 2
        if layer % 2 == 0:
            x = hybrid_layer(x, a_norm[i], a_w_in[i], a_conv_w[i], a_conv_b[i], a_filt_w1[i],
                             a_filt_b1[i], a_filt_f1[i], a_filt_w2[i], a_filt_b2[i], a_filt_f2[i],
                             a_filt_w3[i], a_hyena_d[i], a_sink[i], a_w_out[i])
        else:
            x = dilated_layer(x, c_norm[i], c_w_in[i], c_w_out[i])
    return rmsnorm(x, final_norm)


def setup_inputs(seed: int = 0) -> dict:
    key = jax.random.key(seed)
    ks = jax.random.split(key, 20)
    f32 = jnp.float32

    def nrm(k, shape, scale):
        return scale * jax.random.normal(k, shape, f32)

    return {
        "x_prompt": nrm(ks[0], (BATCH, SEQ, D_MODEL), 1.0),
        "x_sample": nrm(ks[1], (DEC_BATCH, DEC_SEQ, D_MODEL), 1.0),
        "a_norm": 1.0 + nrm(ks[2], (N_EVEN, D_MODEL), 0.05),
        "a_w_in": nrm(ks[3], (N_EVEN, D_MODEL, EVEN_IN), D_MODEL ** -0.5),
        "a_conv_w": nrm(ks[4], (N_EVEN, SHORT_CONV, 3 * HC), SHORT_CONV ** -0.5),
        "a_conv_b": nrm(ks[5], (N_EVEN, 3 * HC), 0.02),
        "a_filt_w1": nrm(ks[6], (N_EVEN, FILTER_EMB, FILTER_HIDDEN), FILTER_EMB ** -0.5),
        "a_filt_b1": nrm(ks[7], (N_EVEN, FILTER_HIDDEN), 0.1),
        "a_filt_f1": 1.0 + nrm(ks[8], (N_EVEN, FILTER_HIDDEN), 0.01),
        "a_filt_w2": nrm(ks[9], (N_EVEN, FILTER_HIDDEN, FILTER_HIDDEN), FILTER_HIDDEN ** -0.5),
        "a_filt_b2": nrm(ks[10], (N_EVEN, FILTER_HIDDEN), 0.1),
        "a_filt_f2": 1.0 + nrm(ks[11], (N_EVEN, FILTER_HIDDEN), 0.01),
        "a_filt_w3": nrm(ks[12], (N_EVEN, FILTER_HIDDEN, HYENA_ORDER * 2 * HC), 0.03 * FILTER_HIDDEN ** -0.5),
        "a_hyena_d": nrm(ks[13], (N_EVEN, HYENA_ORDER, HC), 0.5),
        "a_sink": nrm(ks[14], (N_EVEN, B_HEADS), 0.5),
        "a_w_out": nrm(ks[15], (N_EVEN, EVEN_INNER, D_MODEL), 0.5 * EVEN_INNER ** -0.5),
        "c_norm": 1.0 + nrm(ks[16], (N_ODD, D_MODEL), 0.05),
        "c_w_in": nrm(ks[17], (N_ODD, D_MODEL, ODD_IN), D_MODEL ** -0.5),
        "c_w_out": nrm(ks[18], (N_ODD, C_WIDTH, D_MODEL), 0.5 * C_WIDTH ** -0.5),
        "final_norm": 1.0 + nrm(ks[19], (D_MODEL,), 0.05),
    }


def reference(x_prompt, x_sample, a_norm, a_w_in, a_conv_w, a_conv_b, a_filt_w1, a_filt_b1, a_filt_f1,
              a_filt_w2, a_filt_b2, a_filt_f2, a_filt_w3, a_hyena_d, a_sink, a_w_out, c_norm, c_w_in,
              c_w_out, final_norm):
    y_prompt = trunk(x_prompt, a_norm, a_w_in, a_conv_w, a_conv_b, a_filt_w1, a_filt_b1, a_filt_f1,
                     a_filt_w2, a_filt_b2, a_filt_f2, a_filt_w3, a_hyena_d, a_sink, a_w_out, c_norm,
                     c_w_in, c_w_out, final_norm)
    y_sample = trunk(x_sample, a_norm, a_w_in, a_conv_w, a_conv_b, a_filt_w1, a_filt_b1, a_filt_f1,
                     a_filt_w2, a_filt_b2, a_filt_f2, a_filt_w3, a_hyena_d, a_sink, a_w_out, c_norm,
                     c_w_in, c_w_out, final_norm)
    return (y_prompt, y_sample)
```

```python
import functools
import math

import numpy as np
import jax
import jax.numpy as jnp
from jax import lax
from jax.experimental import pallas as pl
from jax.experimental.pallas import tpu as pltpu

F32 = jnp.float32
BF16 = jnp.bfloat16

D_MODEL = 1024
HC = 512
FILTER_BANDS = 16
DECAY_TARGET = 1e-2
FAST_DECAY_PCT = 0.3
SLOW_DECAY_PCT = 1.5
DECAY_SHIFT = 0.05
B_HEAD_DIM = 64
B_HEADS = 8
B_KV_HEADS = 2
B_GROUP = 4
B_BLOCK = 128
C_PATTERNS = ((128, 1), (512, 4), (2048, 16))
C_HEADS = 8
C_HEAD_DIM = 128
C_RADIUS = 64
ROPE_THETA = 500000.0
NORM_EPS = 1e-6
NEG_INF = -1e30

LANES = 128
N2 = 128
K1_BLOCK = 8
VMEM_LIMIT = 56 * 1024 * 1024


def _cparams(n_axes):
    return pltpu.CompilerParams(dimension_semantics=("arbitrary",) * n_axes,
                                vmem_limit_bytes=VMEM_LIMIT)


def _rope_chunk(blk, cos, sa, sb, half):
    return (blk * cos + pltpu.roll(blk, LANES - half, 1) * sa + pltpu.roll(blk, half, 1) * sb)


def _norm_proj_kernel(*refs, n_rope, half, widths):
    if n_rope:
        x_ref, g_ref, w_ref, cos_ref, sa_ref, sb_ref = refs[:6]
        outs = refs[6:]
    else:
        x_ref, g_ref, w_ref = refs[:3]
        outs = refs[3:]
    x = x_ref[...]
    ms = jnp.mean(x * x, axis=-1, keepdims=True)
    xn = (x * lax.rsqrt(ms + NORM_EPS) * g_ref[...]).astype(BF16)
    acc = jnp.dot(xn, w_ref[...], preferred_element_type=F32)
    col = 0
    for o_ref, width in zip(outs, widths):
        if col < n_rope:
            cos, sa, sb = cos_ref[...], sa_ref[...], sb_ref[...]
            for ci in range(width // LANES):
                src = slice(col + ci * LANES, col + (ci + 1) * LANES)
                o_ref[:, ci * LANES:(ci + 1) * LANES] = _rope_chunk(acc[:, src], cos, sa, sb, half)
        else:
            o_ref[...] = acc[:, col:col + width]
        col += width


def norm_proj(x2d, g, w_bf, seq_len, *, tm, tn, rope=None, half=0, widths=None):
    m, d = x2d.shape
    n = w_bf.shape[1]
    assert m % tm == 0 and n % tn == 0 and seq_len % tm == 0
    in_specs = [pl.BlockSpec((tm, d), lambda j, i: (i, 0)),
                pl.BlockSpec((1, d), lambda j, i: (0, 0)),
                pl.BlockSpec((d, tn), lambda j, i: (0, j))]
    args = [x2d, g.reshape(1, d), w_bf]
    if rope is not None:
        tiles_per_seq = seq_len // tm
        for t in rope:
            in_specs.append(pl.BlockSpec((tm, LANES), lambda j, i: (i % tiles_per_seq, 0)))
            args.append(t)
    if widths is None:
        n_rope = tn if rope is not None else 0
        kern_widths = (tn,)
        out_shape = jax.ShapeDtypeStruct((m, n), F32)
        out_specs = pl.BlockSpec((tm, tn), lambda j, i: (i, j))
    else:
        assert tn == n and sum(widths) == n
        n_rope = widths[0] if rope is not None else 0
        kern_widths = tuple(widths)
        out_shape = tuple(jax.ShapeDtypeStruct((m, wd), F32) for wd in widths)
        out_specs = tuple(pl.BlockSpec((tm, wd), lambda j, i: (i, 0)) for wd in widths)
    return pl.pallas_call(
        functools.partial(_norm_proj_kernel, n_rope=n_rope, half=half, widths=kern_widths),
        grid=(n // tn, m // tm), in_specs=in_specs, out_specs=out_specs, out_shape=out_shape,
        compiler_params=_cparams(2), name="norm_proj")(*args)


def rope_tables(seq_len, head_dim):
    rot = head_dim // 4
    half = rot // 2
    inv = jnp.power(ROPE_THETA, -2.0 * jnp.arange(half, dtype=F32) / rot)
    ang = jnp.arange(seq_len).astype(F32)[:, None] * inv[None, :]
    cos, sin = jnp.cos(ang), jnp.sin(ang)
    lane = np.arange(LANES) % head_dim
    idx = lane % half
    in_rot = jnp.asarray(lane < rot)[None, :]
    first = jnp.asarray(lane < half)[None, :]
    cos_t = jnp.where(in_rot, cos[:, idx], 1.0)
    sa = jnp.where(first, -sin[:, idx], 0.0)
    sb = jnp.where(in_rot & ~first, sin[:, idx], 0.0)
    return (cos_t, sa, sb), half


def _short_conv_kernel(cur_ref, prev_ref, next_ref, w_ref, b_ref, x1_ref, x2_ref, v_ref, *, tc):
    i = pl.program_id(1)
    u = cur_ref[:, :3 * HC]
    row = lax.broadcasted_iota(jnp.int32, u.shape, 0)
    prev_row = jnp.where(i > 0, prev_ref[7:8, :3 * HC], 0.0)
    next_row = jnp.where(i < pl.num_programs(1) - 1, next_ref[0:1, :3 * HC], 0.0)
    um1 = jnp.where(row == 0, prev_row, pltpu.roll(u, 1, 0))
    up1 = jnp.where(row == tc - 1, next_row, pltpu.roll(u, tc - 1, 0))
    w = w_ref[...]
    y = um1 * w[0:1, :] + u * w[1:2, :] + up1 * w[2:3, :] + b_ref[...]
    g = cur_ref[:, 3 * HC:]
    x1_ref[...] = y[:, :HC]
    x2_ref[...] = y[:, HC:2 * HC] * (g / (1.0 + jnp.exp(-g)))
    v_ref[...] = y[:, 2 * HC:]


def short_conv(hy, conv_w, conv_b, batch, seq_len, *, tc=256):
    width = hy.shape[1]
    h3 = hy.reshape(batch, seq_len, width)
    nb8 = seq_len // 8
    out = jax.ShapeDtypeStruct((batch, seq_len, HC), F32)
    ospec = pl.BlockSpec((None, tc, HC), lambda b, i: (b, i, 0))
    return pl.pallas_call(
        functools.partial(_short_conv_kernel, tc=tc),
        grid=(batch, seq_len // tc),
        in_specs=[pl.BlockSpec((None, tc, width), lambda b, i: (b, i, 0)),
                  pl.BlockSpec((None, 8, width), lambda b, i: (b, jnp.maximum(i * (tc // 8) - 1, 0), 0)),
                  pl.BlockSpec((None, 8, width), lambda b, i: (b, jnp.minimum((i + 1) * (tc // 8), nb8 - 1), 0)),
                  pl.BlockSpec((3, 3 * HC), lambda b, i: (0, 0)),
                  pl.BlockSpec((1, 3 * HC), lambda b, i: (0, 0))],
        out_specs=(ospec, ospec, ospec), out_shape=(out, out, out),
        compiler_params=_cparams(2), name="short_conv")(h3, h3, h3, conv_w, conv_b.reshape(1, -1))


def _filter_kernel(band_ref, w1_ref, b1_ref, f1_ref, w2_ref, b2_ref, f2_ref, w3_ref, dl_ref, o_ref,
                   *, seq_len, tr):
    n = pl.program_id(0) * tr + lax.broadcasted_iota(jnp.int32, (tr, 1), 0)
    j = jnp.where(n < seq_len, n, 2 * seq_len - n).astype(F32)
    t = j * (1.0 / (seq_len - 1))
    wpos = (2.0 * math.pi) * j / seq_len
    lane = lax.broadcasted_iota(jnp.int32, (tr, LANES), 1)
    arg = band_ref[...] * wpos
    feats = jnp.where(lane == 0, t,
                      jnp.where(lane <= FILTER_BANDS, jnp.cos(arg),
                                jnp.where(lane <= 2 * FILTER_BANDS, -jnp.sin(arg), 0.0)))
    h = jnp.dot(feats.astype(BF16), w1_ref[...], preferred_element_type=F32) + b1_ref[...]
    h = jnp.sin(f1_ref[...] * h)
    h = jnp.dot(h.astype(BF16), w2_ref[...], preferred_element_type=F32) + b2_ref[...]
    h = jnp.sin(f2_ref[...] * h)
    h = jnp.dot(h.astype(BF16), w3_ref[...], preferred_element_type=F32)
    win = jnp.exp(-t * dl_ref[...]) + DECAY_SHIFT
    win2 = jnp.concatenate([win, win], axis=1)
    fwd = h[:, :2 * HC] * win2
    bwd = h[:, 2 * HC:] * win2
    o_ref[...] = (jnp.where(n < seq_len, fwd, 0.0)
                  + jnp.where(n == 0, bwd, 0.0) - jnp.where(n > seq_len, bwd, 0.0))


def hyena_filter(seq_len, w1, b1, f1, w2, b2, f2, w3, *, tr=512):
    hid = w1.shape[1]
    pad = LANES - hid
    w1p = jnp.pad(w1, ((0, LANES - w1.shape[0]), (0, pad))).astype(BF16)
    w2p = jnp.pad(w2, ((0, pad), (0, pad))).astype(BF16)
    w3r = w3.reshape(hid, 2, 2, HC).transpose(0, 2, 1, 3).reshape(hid, 4 * HC)
    w3p = jnp.pad(w3r, ((0, pad), (0, 0))).astype(BF16)
    vec = lambda v: jnp.pad(v, (0, pad)).reshape(1, LANES)
    bands = jnp.linspace(1e-4, FILTER_BANDS - 1, FILTER_BANDS, dtype=F32)
    band_l = jnp.concatenate([jnp.zeros((1,), F32), bands, bands,
                              jnp.zeros((LANES - 1 - 2 * FILTER_BANDS,), F32)]).reshape(1, LANES)
    max_decay = math.log(DECAY_TARGET) / FAST_DECAY_PCT
    min_decay = math.log(DECAY_TARGET) / SLOW_DECAY_PCT
    deltas = jnp.abs(jnp.linspace(min_decay, max_decay, HC, dtype=F32)).reshape(1, HC)
    full = lambda a: pl.BlockSpec(a.shape, lambda i: (0, 0))
    args = [band_l, w1p, vec(b1), vec(f1), w2p, vec(b2), vec(f2), w3p, deltas]
    return pl.pallas_call(
        functools.partial(_filter_kernel, seq_len=seq_len, tr=tr),
        grid=(2 * seq_len // tr,),
        in_specs=[full(a) for a in args],
        out_specs=pl.BlockSpec((tr, 2 * HC), lambda i: (i, 0)),
        out_shape=jax.ShapeDtypeStruct((2 * seq_len, 2 * HC), F32),
        compiler_params=_cparams(1), name="hyena_filter")(*args)


def dft_tables(n1h):
    n1 = 2 * n1h
    n = n1 * N2
    k1 = jnp.arange(n1h, dtype=jnp.int32)
    m1 = (jnp.arange(n1, dtype=jnp.int32)[None, :] * (2 * k1[:, None] + 1)) % (2 * n1)
    a1 = m1.astype(F32) * (math.pi / n1)
    c1, s1 = jnp.cos(a1), jnp.sin(a1)
    w1_full = jnp.concatenate([c1, -s1], axis=0).astype(BF16)
    w1_half = w1_full[:, :n1h]
    w1_inv = jnp.concatenate([c1[:, :n1h].T, -s1[:, :n1h].T], axis=1).astype(BF16)
    k2 = jnp.arange(N2, dtype=jnp.int32)
    n2 = jnp.arange(N2, dtype=jnp.int32)
    f = 2 * (k2[None, :, None] * n1 + k1[:, None, None]) + 1
    m2 = (n2[None, None, :] * f) % (2 * n)
    th = m2.astype(F32) * (math.pi / n)
    c, s = jnp.cos(th), jnp.sin(th)
    mat = jnp.concatenate([jnp.concatenate([c, s], axis=2),
                           jnp.concatenate([-s, c], axis=2)], axis=1)
    return w1_half, w1_full, w1_inv, mat.astype(BF16), jnp.swapaxes(mat, 1, 2).astype(BF16)


def _stage1_kernel(w_ref, x_ref, o_ref):
    o_ref[...] = jnp.dot(w_ref[...], x_ref[...].astype(BF16), preferred_element_type=F32).astype(o_ref.dtype)


def dft_stage1(w, x3, *, tn):
    b, k, xw = x3.shape
    r = w.shape[0]
    assert xw % tn == 0 and w.shape[1] == k
    return pl.pallas_call(
        _stage1_kernel, grid=(b, xw // tn),
        in_specs=[pl.BlockSpec((r, k), lambda bi, j: (0, 0)),
                  pl.BlockSpec((None, k, tn), lambda bi, j: (bi, 0, j))],
        out_specs=pl.BlockSpec((None, r, tn), lambda bi, j: (bi, 0, j)),
        out_shape=jax.ShapeDtypeStruct((b, r, xw), BF16),
        compiler_params=_cparams(2), name="dft_stage1")(w, x3)


def _stage2_filter_kernel(m_ref, a_ref, o_ref):
    for kk in range(K1_BLOCK):
        xs = jnp.concatenate([a_ref[0, kk], a_ref[1, kk]], axis=0)
        o_ref[kk] = jnp.dot(m_ref[kk], xs, preferred_element_type=F32)


def dft_stage2_filter(mat, a5):
    _, _, n1h, _, c = a5.shape
    return pl.pallas_call(
        _stage2_filter_kernel, grid=(n1h // K1_BLOCK,),
        in_specs=[pl.BlockSpec((K1_BLOCK, 2 * N2, 2 * N2), lambda kb: (kb, 0, 0)),
                  pl.BlockSpec((None, 2, K1_BLOCK, N2, c), lambda kb: (0, 0, kb, 0, 0))],
        out_specs=pl.BlockSpec((K1_BLOCK, 2 * N2, c), lambda kb: (kb, 0, 0)),
        out_shape=jax.ShapeDtypeStruct((n1h, 2 * N2, c), F32),
        compiler_params=_cparams(1), name="dft_stage2_filter")(mat, a5)


def _stage2_conv_kernel(m_ref, mt_ref, h_ref, a_ref, o_ref):
    for kk in range(K1_BLOCK):
        xs = jnp.concatenate([a_ref[0, kk], a_ref[1, kk]], axis=0)
        z = jnp.dot(m_ref[kk], xs, preferred_element_type=F32)
        zr, zi = z[:N2], z[N2:]
        hr, hi = h_ref[kk, :N2], h_ref[kk, N2:]
        y = jnp.concatenate([zr * hr - zi * hi, zr * hi + zi * hr], axis=0).astype(BF16)
        g = jnp.dot(mt_ref[kk], y, preferred_element_type=F32)
        o_ref[0, kk] = g[:N2].astype(BF16)
        o_ref[1, kk] = g[N2:].astype(BF16)


def dft_stage2_conv(mat, mat_t, spec, order, a5):
    b, _, n1h, _, c = a5.shape
    return pl.pallas_call(
        _stage2_conv_kernel, grid=(n1h // K1_BLOCK, b),
        in_specs=[pl.BlockSpec((K1_BLOCK, 2 * N2, 2 * N2), lambda kb, bi: (kb, 0, 0)),
                  pl.BlockSpec((K1_BLOCK, 2 * N2, 2 * N2), lambda kb, bi: (kb, 0, 0)),
                  pl.BlockSpec((K1_BLOCK, 2 * N2, c), lambda kb, bi: (kb, 0, order)),
                  pl.BlockSpec((None, 2, K1_BLOCK, N2, c), lambda kb, bi: (bi, 0, kb, 0, 0))],
        out_specs=pl.BlockSpec((None, 2, K1_BLOCK, N2, c), lambda kb, bi: (bi, 0, kb, 0, 0)),
        out_shape=jax.ShapeDtypeStruct(a5.shape, BF16),
        compiler_params=_cparams(2), name="dft_stage2_conv")(mat, mat_t, spec, a5)


def _stage1_inv_kernel(w_ref, g_ref, z_ref, gate_ref, d_ref, o_ref, *, scale):
    y = jnp.dot(w_ref[...], g_ref[...], preferred_element_type=F32) * scale
    o_ref[...] = gate_ref[...] * (y + d_ref[...] * z_ref[...])


def dft_stage1_inv(w_inv, g3, z3, gate3, d_tiled, *, tn, scale):
    b, r2, xw = g3.shape
    n1h = w_inv.shape[0]
    col = pl.BlockSpec((None, n1h, tn), lambda bi, j: (bi, 0, j))
    return pl.pallas_call(
        functools.partial(_stage1_inv_kernel, scale=scale),
        grid=(b, xw // tn),
        in_specs=[pl.BlockSpec((n1h, r2), lambda bi, j: (0, 0)),
                  pl.BlockSpec((None, r2, tn), lambda bi, j: (bi, 0, j)),
                  col, col,
                  pl.BlockSpec((1, tn), lambda bi, j: (0, 0))],
        out_specs=col,
        out_shape=jax.ShapeDtypeStruct((b, n1h, xw), F32),
        compiler_params=_cparams(2), name="dft_stage1_inv")(w_inv, g3, z3, gate3, d_tiled)


def _silu(g):
    return g / (1.0 + jnp.exp(-g))


def _band_mask(t, halo, i, nblk):
    qi = lax.broadcasted_iota(jnp.int32, (t, t + 2 * halo), 0)
    kj = lax.broadcasted_iota(jnp.int32, (t, t + 2 * halo), 1)
    rel = kj - halo - qi
    return ((jnp.abs(rel) <= halo) & ((kj >= halo) | (i > 0)) & ((kj < halo + t) | (i < nblk - 1)))


def _band_attn_kernel(sink_ref, q_ref, kp_ref, kc_ref, kn_ref, vp_ref, vc_ref, vn_ref, gate_ref, o_ref):
    t, hd = B_BLOCK, B_HEAD_DIM
    valid = _band_mask(t, t, pl.program_id(1), pl.num_programs(1))
    valid = jnp.concatenate([valid] * B_GROUP, axis=0)
    q = q_ref[...]
    kcat = jnp.concatenate([kp_ref[...], kc_ref[...], kn_ref[...]], axis=0)
    vcat = jnp.concatenate([vp_ref[...], vc_ref[...], vn_ref[...]], axis=0)
    heads = []
    for kh in range(B_KV_HEADS):
        k = kcat[:, kh * hd:(kh + 1) * hd].astype(BF16)
        v = vcat[:, kh * hd:(kh + 1) * hd].astype(BF16)
        qg = jnp.concatenate([q[:, (kh * B_GROUP + g) * hd:(kh * B_GROUP + g + 1) * hd]
                              for g in range(B_GROUP)], axis=0).astype(BF16)
        s = lax.dot_general(qg, k, (((1,), (1,)), ((), ())), preferred_element_type=F32) * (hd ** -0.5)
        s = jnp.where(valid, s, NEG_INF)
        sink = jnp.concatenate([jnp.full((t, 1), sink_ref[kh * B_GROUP + g], F32) for g in range(B_GROUP)], axis=0)
        m = jnp.maximum(jnp.max(s, axis=-1, keepdims=True), sink)
        p = jnp.exp(s - m)
        denom = jnp.sum(p, axis=-1, keepdims=True) + jnp.exp(sink - m)
        o = jnp.dot(p.astype(BF16), v, preferred_element_type=F32) / denom
        heads += [o[g * t:(g + 1) * t] for g in range(B_GROUP)]
    o_ref[...] = jnp.concatenate(heads, axis=1) * _silu(gate_ref[...])


def band_attn(qk, att, sink, batch, seq_len):
    t = B_BLOCK
    nblk = seq_len // t
    qk3 = qk.reshape(batch, seq_len, qk.shape[1])
    att3 = att.reshape(batch, seq_len, att.shape[1])
    kv_col = B_HEADS * B_HEAD_DIM // LANES
    prev = lambda b, i: (b, jnp.maximum(i - 1, 0), kv_col)
    cur = lambda b, i: (b, i, kv_col)
    nxt = lambda b, i: (b, jnp.minimum(i + 1, nblk - 1), kv_col)
    kvspec = lambda f: pl.BlockSpec((None, t, LANES), f)
    wide = pl.BlockSpec((None, t, B_HEADS * B_HEAD_DIM), lambda b, i: (b, i, 0))
    out = pl.pallas_call(
        _band_attn_kernel, grid=(batch, nblk),
        in_specs=[pl.BlockSpec(memory_space=pltpu.SMEM), wide,
                  kvspec(prev), kvspec(cur), kvspec(nxt), kvspec(prev), kvspec(cur), kvspec(nxt), wide],
        out_specs=wide,
        out_shape=jax.ShapeDtypeStruct((batch, seq_len, B_HEADS * B_HEAD_DIM), F32),
        compiler_params=_cparams(2), name="band_attn")(sink, qk3, qk3, qk3, qk3, att3, att3, att3, att3)
    return out.reshape(batch * seq_len, -1)


def _dil_attn_kernel(q_ref, kp_ref, kc_ref, kn_ref, vp_ref, vc_ref, vn_ref, o_ref, lse_ref, *, t):
    hd = C_HEAD_DIM
    valid = _band_mask(t, C_RADIUS, pl.program_id(2), pl.num_programs(2))
    lane = lax.broadcasted_iota(jnp.int32, (t, LANES), 1)
    lse_tile = jnp.zeros((t, LANES), F32)
    for h in range(C_HEADS):
        sl = slice(h * hd, (h + 1) * hd)
        q = q_ref[:, sl].astype(BF16)
        k = jnp.concatenate([kp_ref[:, sl], kc_ref[:, sl], kn_ref[:, sl]], axis=0).astype(BF16)
        v = jnp.concatenate([vp_ref[:, sl], vc_ref[:, sl], vn_ref[:, sl]], axis=0).astype(BF16)
        s = lax.dot_general(q, k, (((1,), (1,)), ((), ())), preferred_element_type=F32) * (hd ** -0.5)
        s = jnp.where(valid, s, NEG_INF)
        m = jnp.max(s, axis=-1, keepdims=True)
        p = jnp.exp(s - m)
        denom = jnp.sum(p, axis=-1, keepdims=True)
        o_ref[:, sl] = jnp.dot(p.astype(BF16), v, preferred_element_type=F32) / denom
        lse_tile = jnp.where(lane == h, m + jnp.log(denom), lse_tile)
    lse_ref[...] = lse_tile


def dil_attn(pr, pp, pattern, dilation, batch, seq_len, *, t=128):
    w = C_HEADS * C_HEAD_DIM
    ls = seq_len // dilation
    nblk = ls // t
    halo_per_t = t // C_RADIUS
    n_halo = ls // C_RADIUS
    pr3 = pr.reshape(batch, ls, dilation * pr.shape[1])
    pp3 = pp.reshape(batch, ls, dilation * pp.shape[1])
    nr, npl = pr.shape[1] // w, pp.shape[1] // w
    qcol = lambda r: r * nr + 2 * pattern
    kcol = lambda r: r * nr + 2 * pattern + 1
    vcol = lambda r: r * npl + pattern
    prev = lambda i: jnp.maximum(i * halo_per_t - 1, 0)
    nxt = lambda i: jnp.minimum((i + 1) * halo_per_t, n_halo - 1)
    big = lambda f: pl.BlockSpec((None, t, w), f)
    halo = lambda f: pl.BlockSpec((None, C_RADIUS, w), f)
    o, lse = pl.pallas_call(
        functools.partial(_dil_attn_kernel, t=t), grid=(batch, dilation, nblk),
        in_specs=[big(lambda b, r, i: (b, i, qcol(r))),
                  halo(lambda b, r, i: (b, prev(i), kcol(r))),
                  big(lambda b, r, i: (b, i, kcol(r))),
                  halo(lambda b, r, i: (b, nxt(i), kcol(r))),
                  halo(lambda b, r, i: (b, prev(i), vcol(r))),
                  big(lambda b, r, i: (b, i, vcol(r))),
                  halo(lambda b, r, i: (b, nxt(i), vcol(r)))],
        out_specs=(big(lambda b, r, i: (b, i, r)),
                   pl.BlockSpec((None, t, LANES), lambda b, r, i: (b, i, r))),
        out_shape=(jax.ShapeDtypeStruct((batch, ls, dilation * w), F32),
                   jax.ShapeDtypeStruct((batch, ls, dilation * LANES), F32)),
        compiler_params=_cparams(3), name="dil_attn")(pr3, pr3, pr3, pr3, pp3, pp3, pp3)
    return o.reshape(batch * seq_len, w), lse.reshape(batch * seq_len, LANES)


def _out_proj_even_kernel(hy_ref, at_ref, x_ref, w_ref, o_ref):
    mixed = jnp.concatenate([hy_ref[...], at_ref[...]], axis=1).astype(BF16)
    o_ref[...] = x_ref[...] + jnp.dot(mixed, w_ref[...], preferred_element_type=F32)


def out_proj_even(hy, at, x2d, w_bf, *, tm=512):
    m, d = x2d.shape
    half = pl.BlockSpec((tm, HC), lambda i: (i, 0))
    full = pl.BlockSpec((tm, d), lambda i: (i, 0))
    return pl.pallas_call(
        _out_proj_even_kernel, grid=(m // tm,),
        in_specs=[half, half, full, pl.BlockSpec((d, d), lambda i: (0, 0))],
        out_specs=full, out_shape=jax.ShapeDtypeStruct((m, d), F32),
        compiler_params=_cparams(1), name="out_proj_even")(hy, at, x2d, w_bf)


def _merge_out_kernel(*refs, final):
    o_refs, l_refs = refs[0:3], refs[3:6]
    gate_ref, x_ref, w_ref = refs[6:9]
    out_ref = refs[-1]
    ls = [r[...] for r in l_refs]
    mx = jnp.maximum(jnp.maximum(ls[0], ls[1]), ls[2])
    es = [jnp.exp(l - mx) for l in ls]
    den = es[0] + es[1] + es[2]
    alphas = [e / den for e in es]
    hd = C_HEAD_DIM
    parts = []
    for h in range(C_HEADS):
        sl = slice(h * hd, (h + 1) * hd)
        parts.append(sum(alphas[g][:, h:h + 1] * o_refs[g][:, sl] for g in range(3)))
    y = (jnp.concatenate(parts, axis=1) * _silu(gate_ref[...])).astype(BF16)
    r = x_ref[...] + jnp.dot(y, w_ref[...], preferred_element_type=F32)
    if final:
        ms = jnp.mean(r * r, axis=-1, keepdims=True)
        r = r * lax.rsqrt(ms + NORM_EPS) * refs[9][...]
    out_ref[...] = r


def merge_out(outs, lses, pp, x2d, w_bf, final_g=None, *, tm=512):
    m, d = x2d.shape
    full = pl.BlockSpec((tm, d), lambda i: (i, 0))
    lspec = pl.BlockSpec((tm, LANES), lambda i: (i, 0))
    gate_col = pp.shape[1] // d - 1
    in_specs = [full] * 3 + [lspec] * 3 + [pl.BlockSpec((tm, d), lambda i: (i, gate_col)), full,
                                           pl.BlockSpec((d, d), lambda i: (0, 0))]
    args = list(outs) + list(lses) + [pp, x2d, w_bf]
    if final_g is not None:
        in_specs.append(pl.BlockSpec((1, d), lambda i: (0, 0)))
        args.append(final_g.reshape(1, d))
    return pl.pallas_call(
        functools.partial(_merge_out_kernel, final=final_g is not None), grid=(m // tm,),
        in_specs=in_specs, out_specs=full, out_shape=jax.ShapeDtypeStruct((m, d), F32),
        compiler_params=_cparams(1), name="merge_out")(*args)


def _even_w_in(w_in):
    return jnp.concatenate([w_in[:, 2048:2688], w_in[:, :2048], w_in[:, 2816:], w_in[:, 2688:2816]], axis=1).astype(BF16)


def _odd_w_in(w_in):
    w = C_HEADS * C_HEAD_DIM
    blk = lambda g, t: w_in[:, (3 * g + t) * w:(3 * g + t + 1) * w]
    roped = jnp.concatenate([blk(g, t) for g in range(3) for t in range(2)], axis=1).astype(BF16)
    plain = jnp.concatenate([blk(g, 2) for g in range(3)] + [w_in[:, 9 * w:]], axis=1).astype(BF16)
    return roped, plain


def hyena_spectrum(seq_len, tabs, fw1, fb1, ff1, fw2, fb2, ff2, fw3):
    n1h = seq_len // N2
    _, w1_full, _, mat, _ = tabs
    filt = hyena_filter(seq_len, fw1, fb1, ff1, fw2, fb2, ff2, fw3)
    a = dft_stage1(w1_full, filt.reshape(1, 2 * n1h, N2 * 2 * HC), tn=4096)
    return dft_stage2_filter(mat, a.reshape(1, 2, n1h, N2, 2 * HC))


def hyena_conv(z, gate, d_vec, spec, order, tabs, *, tn=4096):
    batch, seq_len, c = z.shape
    n1h = seq_len // N2
    w1_half, _, w1_inv, mat, mat_t = tabs
    z3 = z.reshape(batch, n1h, N2 * c)
    a = dft_stage1(w1_half, z3, tn=tn)
    g = dft_stage2_conv(mat, mat_t, spec, order, a.reshape(batch, 2, n1h, N2, c))
    d_tiled = jnp.tile(d_vec.reshape(1, c), (1, tn // c))
    out = dft_stage1_inv(w1_inv, g.reshape(batch, 2 * n1h, N2 * c), z3, gate.reshape(batch, n1h, N2 * c),
                         d_tiled, tn=tn, scale=1.0 / (n1h * N2))
    return out.reshape(batch, seq_len, c)


def hybrid_layer(x2d, batch, seq_len, norm_g, w_in_bf, conv_w, conv_b, hyena_d, sink, w_out_bf, spec, tabs, rope):
    rope_t, half = rope
    qk, hy, att = norm_proj(x2d, norm_g, w_in_bf, seq_len, tm=256, tn=w_in_bf.shape[1], rope=rope_t, half=half,
                            widths=(640, 2048, 640))
    x1, x2s, v = short_conv(hy, conv_w, conv_b, batch, seq_len)
    z = hyena_conv(v, x1, hyena_d[0], spec, 0, tabs)
    z = hyena_conv(z, x2s, hyena_d[1], spec, 1, tabs)
    at = band_attn(qk, att, sink, batch, seq_len)
    return out_proj_even(z.reshape(batch * seq_len, HC), at, x2d, w_out_bf)


def dilated_layer(x2d, batch, seq_len, norm_g, w_rope_bf, w_plain_bf, w_out_bf, rope, final_g):
    rope_t, half = rope
    pr = norm_proj(x2d, norm_g, w_rope_bf, seq_len, tm=512, tn=2048, rope=rope_t, half=half)
    pp = norm_proj(x2d, norm_g, w_plain_bf, seq_len, tm=512, tn=2048)
    outs, lses = [], []
    for gi, (_, dilation) in enumerate(C_PATTERNS):
        o, lse = dil_attn(pr, pp, gi, dilation, batch, seq_len)
        outs.append(o)
        lses.append(lse)
    return merge_out(outs, lses, pp, x2d, w_out_bf, final_g)


def kernel(x_prompt, x_sample, a_norm, a_w_in, a_conv_w, a_conv_b, a_filt_w1, a_filt_b1, a_filt_f1, a_filt_w2, a_filt_b2, a_filt_f2, a_filt_w3, a_hyena_d, a_sink, a_w_out, c_norm, c_w_in, c_w_out, final_norm):
    depth = a_norm.shape[0] + c_norm.shape[0]
    xs = [x_prompt, x_sample]
    shapes = [(x.shape[0], x.shape[1]) for x in xs]
    acts = [x.reshape(-1, D_MODEL) for x in xs]
    seq_lens = sorted({s[1] for s in shapes})
    tabs = {sl: dft_tables(sl // N2) for sl in seq_lens}
    rope_even = {sl: rope_tables(sl, B_HEAD_DIM) for sl in seq_lens}
    rope_odd = {sl: rope_tables(sl, C_HEAD_DIM) for sl in seq_lens}
    for layer in range(depth):
        i = layer // 2
        if layer % 2 == 0:
            w_in_bf = _even_w_in(a_w_in[i])
            w_out_bf = a_w_out[i].astype(BF16)
            specs = {sl: hyena_spectrum(sl, tabs[sl], a_filt_w1[i], a_filt_b1[i], a_filt_f1[i], a_filt_w2[i],
                                        a_filt_b2[i], a_filt_f2[i], a_filt_w3[i]) for sl in seq_lens}
            acts = [hybrid_layer(x2d, b, sl, a_norm[i], w_in_bf, a_conv_w[i], a_conv_b[i], a_hyena_d[i], a_sink[i],
                                 w_out_bf, specs[sl], tabs[sl], rope_even[sl])
                    for x2d, (b, sl) in zip(acts, shapes)]
        else:
            w_rope_bf, w_plain_bf = _odd_w_in(c_w_in[i])
            w_out_bf = c_w_out[i].astype(BF16)
            final_g = final_norm if layer == depth - 1 else None
            acts = [dilated_layer(x2d, b, sl, c_norm[i], w_rope_bf, w_plain_bf, w_out_bf, rope_odd[sl], final_g)
                    for x2d, (b, sl) in zip(acts, shapes)]
    assert depth % 2 == 0
    return tuple(a.reshape(b, sl, D_MODEL) for a, (b, sl) in zip(acts, shapes))
```

```python
import functools
import math

import numpy as np
import jax
import jax.numpy as jnp
from jax import lax
from jax.experimental import pallas as pl
from jax.experimental.pallas import tpu as pltpu

F32 = jnp.float32
BF16 = jnp.bfloat16

D_MODEL = 1024
HC = 512
FILTER_BANDS = 16
DECAY_TARGET = 1e-2
FAST_DECAY_PCT = 0.3
SLOW_DECAY_PCT = 1.5
DECAY_SHIFT = 0.05
B_HEAD_DIM = 64
B_HEADS = 8
B_KV_HEADS = 2
B_GROUP = 4
B_BLOCK = 128
C_PATTERNS = ((128, 1), (512, 4), (2048, 16))
C_HEADS = 8
C_HEAD_DIM = 128
C_RADIUS = 64
ROPE_THETA = 500000.0
NORM_EPS = 1e-6
NEG_INF = -1e30

LANES = 128
SUBLANES = 8
N2 = 128
J2 = N2 // SUBLANES
SLABS = HC // LANES
K1_BLOCK = 8
VMEM_LIMIT = 56 * 1024 * 1024


def _cparams(n_axes):
    return pltpu.CompilerParams(dimension_semantics=("arbitrary",) * n_axes,
                                vmem_limit_bytes=VMEM_LIMIT)


def _rope_chunk(blk, cos, sa, sb, half):
    return (blk * cos + pltpu.roll(blk, LANES - half, 1) * sa + pltpu.roll(blk, half, 1) * sb)


def _norm_proj_kernel(*refs, n_rope, half, widths):
    if n_rope:
        x_ref, g_ref, w_ref, cos_ref, sa_ref, sb_ref = refs[:6]
        outs = refs[6:]
    else:
        x_ref, g_ref, w_ref = refs[:3]
        outs = refs[3:]
    x = x_ref[...]
    ms = jnp.mean(x * x, axis=-1, keepdims=True)
    xn = (x * lax.rsqrt(ms + NORM_EPS) * g_ref[...]).astype(BF16)
    acc = jnp.dot(xn, w_ref[...], preferred_element_type=F32)
    col = 0
    for o_ref, width in zip(outs, widths):
        if col < n_rope:
            cos, sa, sb = cos_ref[...], sa_ref[...], sb_ref[...]
            for ci in range(width // LANES):
                src = slice(col + ci * LANES, col + (ci + 1) * LANES)
                o_ref[:, ci * LANES:(ci + 1) * LANES] = _rope_chunk(acc[:, src], cos, sa, sb, half)
        else:
            o_ref[...] = acc[:, col:col + width]
        col += width


def norm_proj(x2d, g, w_bf, seq_len, *, tm, tn, rope=None, half=0, widths=None):
    m, d = x2d.shape
    n = w_bf.shape[1]
    assert m % tm == 0 and n % tn == 0 and seq_len % tm == 0
    in_specs = [pl.BlockSpec((tm, d), lambda j, i: (i, 0)),
                pl.BlockSpec((1, d), lambda j, i: (0, 0)),
                pl.BlockSpec((d, tn), lambda j, i: (0, j))]
    args = [x2d, g.reshape(1, d), w_bf]
    if rope is not None:
        tiles_per_seq = seq_len // tm
        for t in rope:
            in_specs.append(pl.BlockSpec((tm, LANES), lambda j, i: (i % tiles_per_seq, 0)))
            args.append(t)
    if widths is None:
        n_rope = tn if rope is not None else 0
        kern_widths = (tn,)
        out_shape = jax.ShapeDtypeStruct((m, n), F32)
        out_specs = pl.BlockSpec((tm, tn), lambda j, i: (i, j))
    else:
        assert tn == n and sum(widths) == n
        n_rope = widths[0] if rope is not None else 0
        kern_widths = tuple(widths)
        out_shape = tuple(jax.ShapeDtypeStruct((m, wd), F32) for wd in widths)
        out_specs = tuple(pl.BlockSpec((tm, wd), lambda j, i: (i, 0)) for wd in widths)
    return pl.pallas_call(
        functools.partial(_norm_proj_kernel, n_rope=n_rope, half=half, widths=kern_widths),
        grid=(n // tn, m // tm), in_specs=in_specs, out_specs=out_specs, out_shape=out_shape,
        compiler_params=_cparams(2), name="norm_proj")(*args)


def rope_tables(seq_len, head_dim):
    rot = head_dim // 4
    half = rot // 2
    inv = jnp.power(ROPE_THETA, -2.0 * jnp.arange(half, dtype=F32) / rot)
    ang = jnp.arange(seq_len).astype(F32)[:, None] * inv[None, :]
    cos, sin = jnp.cos(ang), jnp.sin(ang)
    lane = np.arange(LANES) % head_dim
    idx = lane % half
    in_rot = jnp.asarray(lane < rot)[None, :]
    first = jnp.asarray(lane < half)[None, :]
    cos_t = jnp.where(in_rot, cos[:, idx], 1.0)
    sa = jnp.where(first, -sin[:, idx], 0.0)
    sb = jnp.where(in_rot & ~first, sin[:, idx], 0.0)
    return (cos_t, sa, sb), half


def _short_conv_kernel(cur_ref, prev_ref, next_ref, w_ref, b_ref, x1_ref, x2_ref, v_ref, *, tc):
    i = pl.program_id(1)
    u = cur_ref[:, :3 * HC]
    row = lax.broadcasted_iota(jnp.int32, u.shape, 0)
    prev_row = jnp.where(i > 0, prev_ref[7:8, :3 * HC], 0.0)
    next_row = jnp.where(i < pl.num_programs(1) - 1, next_ref[0:1, :3 * HC], 0.0)
    um1 = jnp.where(row == 0, prev_row, pltpu.roll(u, 1, 0))
    up1 = jnp.where(row == tc - 1, next_row, pltpu.roll(u, tc - 1, 0))
    w = w_ref[...]
    y = um1 * w[0:1, :] + u * w[1:2, :] + up1 * w[2:3, :] + b_ref[...]
    g = cur_ref[:, 3 * HC:]
    x2 = y[:, HC:2 * HC] * (g / (1.0 + jnp.exp(-g)))
    _store_time_tiles(x1_ref, y[:, :HC])
    _store_time_tiles(x2_ref, x2)
    _store_time_tiles(v_ref, y[:, 2 * HC:])


def _store_time_tiles(ref, val):
    for s in range(SLABS):
        for i in range(val.shape[0] // N2):
            ref[s, :, i] = val[i * N2:(i + 1) * N2, s * LANES:(s + 1) * LANES].reshape(J2, SUBLANES, LANES)


def _load_time_tiles(ref, nb):
    return jnp.concatenate([jnp.concatenate([ref[s, :, i].reshape(N2, LANES) for s in range(SLABS)], axis=1)
                            for i in range(nb)], axis=0)


def short_conv(hy, conv_w, conv_b, batch, seq_len, *, tc=256):
    width = hy.shape[1]
    h3 = hy.reshape(batch, seq_len, width)
    nb8 = seq_len // 8
    out = jax.ShapeDtypeStruct((batch, SLABS, J2, seq_len // N2, SUBLANES, LANES), F32)
    ospec = pl.BlockSpec((None, SLABS, J2, tc // N2, SUBLANES, LANES), lambda b, i: (b, 0, 0, i, 0, 0))
    return pl.pallas_call(
        functools.partial(_short_conv_kernel, tc=tc),
        grid=(batch, seq_len // tc),
        in_specs=[pl.BlockSpec((None, tc, width), lambda b, i: (b, i, 0)),
                  pl.BlockSpec((None, 8, width), lambda b, i: (b, jnp.maximum(i * (tc // 8) - 1, 0), 0)),
                  pl.BlockSpec((None, 8, width), lambda b, i: (b, jnp.minimum((i + 1) * (tc // 8), nb8 - 1), 0)),
                  pl.BlockSpec((3, 3 * HC), lambda b, i: (0, 0)),
                  pl.BlockSpec((1, 3 * HC), lambda b, i: (0, 0))],
        out_specs=(ospec, ospec, ospec), out_shape=(out, out, out),
        compiler_params=_cparams(2), name="short_conv")(h3, h3, h3, conv_w, conv_b.reshape(1, -1))


def _filter_kernel(band_ref, w1_ref, b1_ref, f1_ref, w2_ref, b2_ref, f2_ref, w3_ref, dl_ref, o_ref,
                   *, seq_len, tr):
    n = pl.program_id(0) * tr + lax.broadcasted_iota(jnp.int32, (tr, 1), 0)
    j = jnp.where(n < seq_len, n, 2 * seq_len - n).astype(F32)
    t = j * (1.0 / (seq_len - 1))
    wpos = (2.0 * math.pi) * j / seq_len
    lane = lax.broadcasted_iota(jnp.int32, (tr, LANES), 1)
    arg = band_ref[...] * wpos
    feats = jnp.where(lane == 0, t,
                      jnp.where(lane <= FILTER_BANDS, jnp.cos(arg),
                                jnp.where(lane <= 2 * FILTER_BANDS, -jnp.sin(arg), 0.0)))
    h = jnp.dot(feats.astype(BF16), w1_ref[...], preferred_element_type=F32) + b1_ref[...]
    h = jnp.sin(f1_ref[...] * h)
    h = jnp.dot(h.astype(BF16), w2_ref[...], preferred_element_type=F32) + b2_ref[...]
    h = jnp.sin(f2_ref[...] * h)
    h = jnp.dot(h.astype(BF16), w3_ref[...], preferred_element_type=F32)
    win = jnp.exp(-t * dl_ref[...]) + DECAY_SHIFT
    win2 = jnp.concatenate([win, win], axis=1)
    fwd = h[:, :2 * HC] * win2
    bwd = h[:, 2 * HC:] * win2
    res = (jnp.where(n < seq_len, fwd, 0.0)
           + jnp.where(n == 0, bwd, 0.0) - jnp.where(n > seq_len, bwd, 0.0))
    for order in range(2):
        _store_time_tiles(o_ref.at[order * SLABS:(order + 1) * SLABS], res[:, order * HC:(order + 1) * HC])


def hyena_filter(seq_len, w1, b1, f1, w2, b2, f2, w3, *, tr=512):
    hid = w1.shape[1]
    pad = LANES - hid
    w1p = jnp.pad(w1, ((0, LANES - w1.shape[0]), (0, pad))).astype(BF16)
    w2p = jnp.pad(w2, ((0, pad), (0, pad))).astype(BF16)
    w3r = w3.reshape(hid, 2, 2, HC).transpose(0, 2, 1, 3).reshape(hid, 4 * HC)
    w3p = jnp.pad(w3r, ((0, pad), (0, 0))).astype(BF16)
    vec = lambda v: jnp.pad(v, (0, pad)).reshape(1, LANES)
    bands = jnp.linspace(1e-4, FILTER_BANDS - 1, FILTER_BANDS, dtype=F32)
    band_l = jnp.concatenate([jnp.zeros((1,), F32), bands, bands,
                              jnp.zeros((LANES - 1 - 2 * FILTER_BANDS,), F32)]).reshape(1, LANES)
    max_decay = math.log(DECAY_TARGET) / FAST_DECAY_PCT
    min_decay = math.log(DECAY_TARGET) / SLOW_DECAY_PCT
    deltas = jnp.abs(jnp.linspace(min_decay, max_decay, HC, dtype=F32)).reshape(1, HC)
    full = lambda a: pl.BlockSpec(a.shape, lambda i: (0, 0))
    args = [band_l, w1p, vec(b1), vec(f1), w2p, vec(b2), vec(f2), w3p, deltas]
    return pl.pallas_call(
        functools.partial(_filter_kernel, seq_len=seq_len, tr=tr),
        grid=(2 * seq_len // tr,),
        in_specs=[full(a) for a in args],
        out_specs=pl.BlockSpec((2 * SLABS, J2, tr // N2, SUBLANES, LANES), lambda i: (0, 0, i, 0, 0)),
        out_shape=jax.ShapeDtypeStruct((2 * SLABS, J2, 2 * seq_len // N2, SUBLANES, LANES), F32),
        compiler_params=_cparams(1), name="hyena_filter")(*args)


def dft_tables(n1h):
    n1 = 2 * n1h
    n = n1 * N2
    k1 = jnp.arange(n1h, dtype=jnp.int32)
    m1 = (jnp.arange(n1, dtype=jnp.int32)[None, :] * (2 * k1[:, None] + 1)) % (2 * n1)
    a1 = m1.astype(F32) * (math.pi / n1)
    c1, s1 = jnp.cos(a1), jnp.sin(a1)
    w1_full = jnp.concatenate([c1, -s1], axis=0).astype(BF16)
    w1_half = w1_full[:, :n1h]
    w1_inv = jnp.concatenate([c1[:, :n1h].T, -s1[:, :n1h].T], axis=1).astype(BF16)
    k2 = jnp.arange(N2, dtype=jnp.int32)
    n2 = jnp.arange(N2, dtype=jnp.int32)
    f = 2 * (k2[None, :, None] * n1 + k1[:, None, None]) + 1
    m2 = (n2[None, None, :] * f) % (2 * n)
    th = m2.astype(F32) * (math.pi / n)
    c, s = jnp.cos(th), jnp.sin(th)
    mat = jnp.concatenate([jnp.concatenate([c, s], axis=2),
                           jnp.concatenate([-s, c], axis=2)], axis=1)
    return w1_half, w1_full, w1_inv, mat.astype(BF16), jnp.swapaxes(mat, 1, 2).astype(BF16)


def _rows_of(ref2, m, n_rows):
    return jnp.concatenate([ref2[cb, pl.ds(m, n_rows, stride=SUBLANES), :] for cb in range(SLABS)], axis=1)


def _store_rows(ref2, m, val):
    for cb in range(SLABS):
        ref2[cb, pl.ds(m, val.shape[0], stride=SUBLANES), :] = val[:, cb * LANES:(cb + 1) * LANES]


def _stage1_kernel(w_ref, x_ref, o_ref, *, k, r):
    w = w_ref[...]
    for m in range(SUBLANES):
        _store_rows(o_ref, m, jnp.dot(w, _rows_of(x_ref, m, k).astype(BF16), preferred_element_type=F32))


def dft_stage1(w, x5):
    b, s = x5.shape[:2]
    r, k = w.shape
    assert s % SLABS == 0 and x5.shape[3] == k * SUBLANES
    blk = lambda rows: pl.BlockSpec((None, SLABS, None, rows * SUBLANES, LANES), lambda bi, si, j: (bi, si, j, 0, 0))
    return pl.pallas_call(
        functools.partial(_stage1_kernel, k=k, r=r), grid=(b, s // SLABS, J2),
        in_specs=[pl.BlockSpec((r, k), lambda bi, si, j: (0, 0)), blk(k)],
        out_specs=blk(r),
        out_shape=jax.ShapeDtypeStruct((b, s, J2, r * SUBLANES, LANES), F32),
        compiler_params=_cparams(3), name="dft_stage1")(w, x5)


def _slab_cat(a_ref, kk, n_slabs):
    return jnp.concatenate([jnp.concatenate([a_ref[cb, :, 0, kk].reshape(N2, LANES),
                                             a_ref[cb, :, 1, kk].reshape(N2, LANES)], axis=0)
                            for cb in range(n_slabs)], axis=1).astype(BF16)


def _stage2_filter_kernel(m_ref, a_ref, o_ref):
    for kk in range(K1_BLOCK):
        o_ref[kk] = jnp.dot(m_ref[kk], _slab_cat(a_ref, kk, 2 * SLABS), preferred_element_type=F32)


def dft_stage2_filter(mat, a7):
    n1h = a7.shape[4]
    return pl.pallas_call(
        _stage2_filter_kernel, grid=(n1h // K1_BLOCK,),
        in_specs=[pl.BlockSpec((K1_BLOCK, 2 * N2, 2 * N2), lambda kb: (kb, 0, 0)),
                  pl.BlockSpec((None, 2 * SLABS, J2, 2, K1_BLOCK, SUBLANES, LANES), lambda kb: (0, 0, 0, 0, kb, 0, 0))],
        out_specs=pl.BlockSpec((K1_BLOCK, 2 * N2, 2 * HC), lambda kb: (kb, 0, 0)),
        out_shape=jax.ShapeDtypeStruct((n1h, 2 * N2, 2 * HC), F32),
        compiler_params=_cparams(1), name="dft_stage2_filter")(mat, a7)


def _stage2_conv_kernel(m_ref, mt_ref, h_ref, a_ref, o_ref):
    for kk in range(K1_BLOCK):
        z = jnp.dot(m_ref[kk], _slab_cat(a_ref, kk, SLABS), preferred_element_type=F32)
        zr, zi = z[:N2], z[N2:]
        hr, hi = h_ref[kk, :N2], h_ref[kk, N2:]
        y = jnp.concatenate([zr * hr - zi * hi, zr * hi + zi * hr], axis=0).astype(BF16)
        g = jnp.dot(mt_ref[kk], y, preferred_element_type=F32)
        for cb in range(SLABS):
            o_ref[cb, :, 0, kk] = g[:N2, cb * LANES:(cb + 1) * LANES].reshape(J2, SUBLANES, LANES)
            o_ref[cb, :, 1, kk] = g[N2:, cb * LANES:(cb + 1) * LANES].reshape(J2, SUBLANES, LANES)


def dft_stage2_conv(mat, mat_t, spec, order, a7):
    b, n1h = a7.shape[0], a7.shape[4]
    ablk = pl.BlockSpec((None, SLABS, J2, 2, K1_BLOCK, SUBLANES, LANES), lambda kb, bi: (bi, 0, 0, 0, kb, 0, 0))
    return pl.pallas_call(
        _stage2_conv_kernel, grid=(n1h // K1_BLOCK, b),
        in_specs=[pl.BlockSpec((K1_BLOCK, 2 * N2, 2 * N2), lambda kb, bi: (kb, 0, 0)),
                  pl.BlockSpec((K1_BLOCK, 2 * N2, 2 * N2), lambda kb, bi: (kb, 0, 0)),
                  pl.BlockSpec((K1_BLOCK, 2 * N2, HC), lambda kb, bi: (kb, 0, order)),
                  ablk],
        out_specs=ablk,
        out_shape=jax.ShapeDtypeStruct(a7.shape, F32),
        compiler_params=_cparams(2), name="dft_stage2_conv")(mat, mat_t, spec, a7)


def _stage1_inv_kernel(w_ref, g_ref, z_ref, gate_ref, d_ref, o_ref, *, n1h, scale):
    w, d = w_ref[...], d_ref[...]
    for m in range(SUBLANES):
        y = jnp.dot(w, _rows_of(g_ref, m, 2 * n1h).astype(BF16), preferred_element_type=F32) * scale
        _store_rows(o_ref, m, _rows_of(gate_ref, m, n1h) * (y + d * _rows_of(z_ref, m, n1h)))


def dft_stage1_inv(w_inv, g5, z5, gate5, d_vec, *, scale):
    b = g5.shape[0]
    n1h = w_inv.shape[0]
    blk = lambda rows: pl.BlockSpec((None, SLABS, None, rows * SUBLANES, LANES), lambda bi, j: (bi, 0, j, 0, 0))
    return pl.pallas_call(
        functools.partial(_stage1_inv_kernel, n1h=n1h, scale=scale),
        grid=(b, J2),
        in_specs=[pl.BlockSpec((n1h, 2 * n1h), lambda bi, j: (0, 0)), blk(2 * n1h), blk(n1h), blk(n1h),
                  pl.BlockSpec((1, HC), lambda bi, j: (0, 0))],
        out_specs=blk(n1h),
        out_shape=jax.ShapeDtypeStruct(z5.shape, F32),
        compiler_params=_cparams(2), name="dft_stage1_inv")(w_inv, g5, z5, gate5, d_vec)


def _silu(g):
    return g / (1.0 + jnp.exp(-g))


def _band_mask(t, halo, i, nblk):
    qi = lax.broadcasted_iota(jnp.int32, (t, t + 2 * halo), 0)
    kj = lax.broadcasted_iota(jnp.int32, (t, t + 2 * halo), 1)
    rel = kj - halo - qi
    return ((jnp.abs(rel) <= halo) & ((kj >= halo) | (i > 0)) & ((kj < halo + t) | (i < nblk - 1)))


def _band_attn_kernel(sink_ref, q_ref, kp_ref, kc_ref, kn_ref, vp_ref, vc_ref, vn_ref, gate_ref, o_ref):
    t, hd = B_BLOCK, B_HEAD_DIM
    valid = _band_mask(t, t, pl.program_id(1), pl.num_programs(1))
    valid = jnp.concatenate([valid] * B_GROUP, axis=0)
    q = q_ref[...]
    kcat = jnp.concatenate([kp_ref[...], kc_ref[...], kn_ref[...]], axis=0)
    vcat = jnp.concatenate([vp_ref[...], vc_ref[...], vn_ref[...]], axis=0)
    heads = []
    for kh in range(B_KV_HEADS):
        k = kcat[:, kh * hd:(kh + 1) * hd].astype(BF16)
        v = vcat[:, kh * hd:(kh + 1) * hd].astype(BF16)
        qg = jnp.concatenate([q[:, (kh * B_GROUP + g) * hd:(kh * B_GROUP + g + 1) * hd]
                              for g in range(B_GROUP)], axis=0).astype(BF16)
        s = lax.dot_general(qg, k, (((1,), (1,)), ((), ())), preferred_element_type=F32) * (hd ** -0.5)
        s = jnp.where(valid, s, NEG_INF)
        sink = jnp.concatenate([jnp.full((t, 1), sink_ref[kh * B_GROUP + g], F32) for g in range(B_GROUP)], axis=0)
        m = jnp.maximum(jnp.max(s, axis=-1, keepdims=True), sink)
        p = jnp.exp(s - m)
        denom = jnp.sum(p, axis=-1, keepdims=True) + jnp.exp(sink - m)
        o = jnp.dot(p.astype(BF16), v, preferred_element_type=F32) / denom
        heads += [o[g * t:(g + 1) * t] for g in range(B_GROUP)]
    o_ref[...] = jnp.concatenate(heads, axis=1) * _silu(gate_ref[...])


def band_attn(qk, att, sink, batch, seq_len):
    t = B_BLOCK
    nblk = seq_len // t
    qk3 = qk.reshape(batch, seq_len, qk.shape[1])
    att3 = att.reshape(batch, seq_len, att.shape[1])
    kv_col = B_HEADS * B_HEAD_DIM // LANES
    prev = lambda b, i: (b, jnp.maximum(i - 1, 0), kv_col)
    cur = lambda b, i: (b, i, kv_col)
    nxt = lambda b, i: (b, jnp.minimum(i + 1, nblk - 1), kv_col)
    kvspec = lambda f: pl.BlockSpec((None, t, LANES), f)
    wide = pl.BlockSpec((None, t, B_HEADS * B_HEAD_DIM), lambda b, i: (b, i, 0))
    out = pl.pallas_call(
        _band_attn_kernel, grid=(batch, nblk),
        in_specs=[pl.BlockSpec(memory_space=pltpu.SMEM), wide,
                  kvspec(prev), kvspec(cur), kvspec(nxt), kvspec(prev), kvspec(cur), kvspec(nxt), wide],
        out_specs=wide,
        out_shape=jax.ShapeDtypeStruct((batch, seq_len, B_HEADS * B_HEAD_DIM), F32),
        compiler_params=_cparams(2), name="band_attn")(sink, qk3, qk3, qk3, qk3, att3, att3, att3, att3)
    return out.reshape(batch * seq_len, -1)


def _dil_attn_kernel(q_ref, kp_ref, kc_ref, kn_ref, vp_ref, vc_ref, vn_ref, o_ref, lse_ref, *, t):
    hd = C_HEAD_DIM
    valid = _band_mask(t, C_RADIUS, pl.program_id(1), pl.num_programs(1))
    lane = lax.broadcasted_iota(jnp.int32, (t, LANES), 1)
    lse_tile = jnp.zeros((t, LANES), F32)
    for h in range(C_HEADS):
        sl = slice(h * hd, (h + 1) * hd)
        q = q_ref[:, sl].astype(BF16)
        k = jnp.concatenate([kp_ref[:, sl], kc_ref[:, sl], kn_ref[:, sl]], axis=0).astype(BF16)
        v = jnp.concatenate([vp_ref[:, sl], vc_ref[:, sl], vn_ref[:, sl]], axis=0).astype(BF16)
        s = lax.dot_general(q, k, (((1,), (1,)), ((), ())), preferred_element_type=F32) * (hd ** -0.5)
        s = jnp.where(valid, s, NEG_INF)
        m = jnp.max(s, axis=-1, keepdims=True)
        p = jnp.exp(s - m)
        denom = jnp.sum(p, axis=-1, keepdims=True)
        o_ref[:, sl] = jnp.dot(p.astype(BF16), v, preferred_element_type=F32) / denom
        lse_tile = jnp.where(lane == h, m + jnp.log(denom), lse_tile)
    lse_ref[...] = lse_tile


def dil_attn(qk, vsrc, n_seq, ls, *, t=128):
    w = C_HEADS * C_HEAD_DIM
    nblk = ls // t
    halo_per_t = t // C_RADIUS
    n_halo = ls // C_RADIUS
    qk3 = qk.reshape(n_seq, ls, qk.shape[1])
    v3 = vsrc.reshape(n_seq, ls, vsrc.shape[1])
    prev = lambda i: jnp.maximum(i * halo_per_t - 1, 0)
    nxt = lambda i: jnp.minimum((i + 1) * halo_per_t, n_halo - 1)
    big = lambda f: pl.BlockSpec((None, t, w), f)
    halo = lambda f: pl.BlockSpec((None, C_RADIUS, w), f)
    o, lse = pl.pallas_call(
        functools.partial(_dil_attn_kernel, t=t), grid=(n_seq, nblk),
        in_specs=[big(lambda b, i: (b, i, 0)),
                  halo(lambda b, i: (b, prev(i), 1)),
                  big(lambda b, i: (b, i, 1)),
                  halo(lambda b, i: (b, nxt(i), 1)),
                  halo(lambda b, i: (b, prev(i), 0)),
                  big(lambda b, i: (b, i, 0)),
                  halo(lambda b, i: (b, nxt(i), 0))],
        out_specs=(big(lambda b, i: (b, i, 0)),
                   pl.BlockSpec((None, t, LANES), lambda b, i: (b, i, 0))),
        out_shape=(jax.ShapeDtypeStruct((n_seq, ls, w), F32),
                   jax.ShapeDtypeStruct((n_seq, ls, LANES), F32)),
        compiler_params=_cparams(2), name="dil_attn")(qk3, qk3, qk3, qk3, v3, v3, v3)
    return o.reshape(n_seq * ls, w), lse.reshape(n_seq * ls, LANES)


def _out_proj_even_kernel(hy_ref, at_ref, x_ref, w_ref, o_ref, *, nb):
    mixed = jnp.concatenate([_load_time_tiles(hy_ref, nb), at_ref[...]], axis=1).astype(BF16)
    o_ref[...] = x_ref[...] + jnp.dot(mixed, w_ref[...], preferred_element_type=F32)


def out_proj_even(hy, at, x2d, w_bf, *, tm=512):
    m, d = x2d.shape
    batch, n1h = hy.shape[0], hy.shape[3]
    tps = n1h * N2 // tm
    nb = tm // N2
    half = pl.BlockSpec((tm, HC), lambda b, i: (b * tps + i, 0))
    full = pl.BlockSpec((tm, d), lambda b, i: (b * tps + i, 0))
    return pl.pallas_call(
        functools.partial(_out_proj_even_kernel, nb=nb), grid=(batch, tps),
        in_specs=[pl.BlockSpec((None, SLABS, J2, nb, SUBLANES, LANES), lambda b, i: (b, 0, 0, i, 0, 0)), half, full,
                  pl.BlockSpec((d, d), lambda b, i: (0, 0))],
        out_specs=full, out_shape=jax.ShapeDtypeStruct((m, d), F32),
        compiler_params=_cparams(2), name="out_proj_even")(hy, at, x2d, w_bf)


def _merge_out_kernel(*refs, final):
    o_refs, l_refs = refs[0:3], refs[3:6]
    gate_ref, x_ref, w_ref = refs[6:9]
    out_ref = refs[-1]
    ls = [r[...] for r in l_refs]
    mx = jnp.maximum(jnp.maximum(ls[0], ls[1]), ls[2])
    es = [jnp.exp(l - mx) for l in ls]
    den = es[0] + es[1] + es[2]
    alphas = [e / den for e in es]
    hd = C_HEAD_DIM
    parts = []
    for h in range(C_HEADS):
        sl = slice(h * hd, (h + 1) * hd)
        parts.append(sum(alphas[g][:, h:h + 1] * o_refs[g][:, sl] for g in range(3)))
    y = (jnp.concatenate(parts, axis=1) * _silu(gate_ref[...])).astype(BF16)
    r = x_ref[...] + jnp.dot(y, w_ref[...], preferred_element_type=F32)
    if final:
        ms = jnp.mean(r * r, axis=-1, keepdims=True)
        r = r * lax.rsqrt(ms + NORM_EPS) * refs[9][...]
    out_ref[...] = r


def merge_out(outs, lses, pp, x2d, w_bf, final_g=None, *, tm=512):
    m, d = x2d.shape
    full = pl.BlockSpec((tm, d), lambda i: (i, 0))
    lspec = pl.BlockSpec((tm, LANES), lambda i: (i, 0))
    gate_col = pp.shape[1] // d - 1
    in_specs = [full] * 3 + [lspec] * 3 + [pl.BlockSpec((tm, d), lambda i: (i, gate_col)), full,
                                           pl.BlockSpec((d, d), lambda i: (0, 0))]
    args = list(outs) + list(lses) + [pp, x2d, w_bf]
    if final_g is not None:
        in_specs.append(pl.BlockSpec((1, d), lambda i: (0, 0)))
        args.append(final_g.reshape(1, d))
    return pl.pallas_call(
        functools.partial(_merge_out_kernel, final=final_g is not None), grid=(m // tm,),
        in_specs=in_specs, out_specs=full, out_shape=jax.ShapeDtypeStruct((m, d), F32),
        compiler_params=_cparams(1), name="merge_out")(*args)


def _even_w_in(w_in):
    return jnp.concatenate([w_in[:, 2048:2688], w_in[:, :2048], w_in[:, 2816:], w_in[:, 2688:2816]], axis=1).astype(BF16)


def _odd_w_in(w_in):
    w = C_HEADS * C_HEAD_DIM
    blk = lambda g, t: w_in[:, (3 * g + t) * w:(3 * g + t + 1) * w]
    per = [[blk(g, 0), blk(g, 1), blk(g, 2)] for g in range(len(C_PATTERNS))]
    per[0].append(w_in[:, 9 * w:])
    return [jnp.concatenate(cols, axis=1).astype(BF16) for cols in per]


def hyena_spectrum(seq_len, tabs, fw1, fb1, ff1, fw2, fb2, ff2, fw3):
    n1h = seq_len // N2
    _, w1_full, _, mat, _ = tabs
    filt = hyena_filter(seq_len, fw1, fb1, ff1, fw2, fb2, ff2, fw3)
    a = dft_stage1(w1_full, filt.reshape(1, 2 * SLABS, J2, 2 * n1h * SUBLANES, LANES))
    return dft_stage2_filter(mat, a.reshape(1, 2 * SLABS, J2, 2, n1h, SUBLANES, LANES))


def hyena_conv(z, gate, d_vec, spec, order, tabs):
    batch, n1h = z.shape[0], z.shape[3]
    w1_half, _, w1_inv, mat, mat_t = tabs
    rows = lambda a, r: a.reshape(batch, SLABS, J2, r * SUBLANES, LANES)
    z5 = rows(z, n1h)
    a = dft_stage1(w1_half, z5)
    g = dft_stage2_conv(mat, mat_t, spec, order, a.reshape(batch, SLABS, J2, 2, n1h, SUBLANES, LANES))
    out = dft_stage1_inv(w1_inv, rows(g, 2 * n1h), z5, rows(gate, n1h), d_vec.reshape(1, HC), scale=1.0 / (n1h * N2))
    return out.reshape(z.shape)


def hybrid_layer(x2d, batch, seq_len, norm_g, w_in_bf, conv_w, conv_b, hyena_d, sink, w_out_bf, spec, tabs, rope):
    rope_t, half = rope
    qk, hy, att = norm_proj(x2d, norm_g, w_in_bf, seq_len, tm=256, tn=w_in_bf.shape[1], rope=rope_t, half=half,
                            widths=(640, 2048, 640))
    x1, x2s, v = short_conv(hy, conv_w, conv_b, batch, seq_len)
    z = hyena_conv(v, x1, hyena_d[0], spec, 0, tabs)
    z = hyena_conv(z, x2s, hyena_d[1], spec, 1, tabs)
    at = band_attn(qk, att, sink, batch, seq_len)
    return out_proj_even(z, at, x2d, w_out_bf)


def _to_strided(a2d, batch, seq_len, dilation):
    if dilation == 1:
        return a2d
    w = a2d.shape[1]
    return a2d.reshape(batch, seq_len // dilation, dilation, w).swapaxes(1, 2).reshape(batch * seq_len, w)


def _from_strided(a2d, batch, seq_len, dilation):
    if dilation == 1:
        return a2d
    w = a2d.shape[1]
    return a2d.reshape(batch, dilation, seq_len // dilation, w).swapaxes(1, 2).reshape(batch * seq_len, w)


def dilated_layer(x2d, batch, seq_len, norm_g, w_in_bfs, w_out_bf, ropes, final_g):
    w = C_HEADS * C_HEAD_DIM
    outs, lses, gate_src = [], [], None
    for gi, (_, dilation) in enumerate(C_PATTERNS):
        rope_t, half = ropes[dilation]
        xs = _to_strided(x2d, batch, seq_len, dilation)
        n_cols = w_in_bfs[gi].shape[1]
        qk, vsrc = norm_proj(xs, norm_g, w_in_bfs[gi], seq_len, tm=256, tn=n_cols, rope=rope_t, half=half,
                             widths=(2 * w, n_cols - 2 * w))
        if gi == 0:
            gate_src = vsrc
        o, lse = dil_attn(qk, vsrc, batch * dilation, seq_len // dilation)
        outs.append(_from_strided(o, batch, seq_len, dilation))
        lses.append(_from_strided(lse, batch, seq_len, dilation))
    return merge_out(outs, lses, gate_src, x2d, w_out_bf, final_g)


def kernel(x_prompt, x_sample, a_norm, a_w_in, a_conv_w, a_conv_b, a_filt_w1, a_filt_b1, a_filt_f1, a_filt_w2, a_filt_b2, a_filt_f2, a_filt_w3, a_hyena_d, a_sink, a_w_out, c_norm, c_w_in, c_w_out, final_norm):
    depth = a_norm.shape[0] + c_norm.shape[0]
    xs = [x_prompt, x_sample]
    shapes = [(x.shape[0], x.shape[1]) for x in xs]
    acts = [x.reshape(-1, D_MODEL) for x in xs]
    seq_lens = sorted({s[1] for s in shapes})
    tabs = {sl: dft_tables(sl // N2) for sl in seq_lens}
    rope_even = {sl: rope_tables(sl, B_HEAD_DIM) for sl in seq_lens}
    rope_odd = {}
    for sl in seq_lens:
        tables, half = rope_tables(sl, C_HEAD_DIM)
        rope_odd[sl] = {d: (tuple(_to_strided(t, 1, sl, d) for t in tables), half) for _, d in C_PATTERNS}
    for layer in range(depth):
        i = layer // 2
        if layer % 2 == 0:
            w_in_bf = _even_w_in(a_w_in[i])
            w_out_bf = a_w_out[i].astype(BF16)
            specs = {sl: hyena_spectrum(sl, tabs[sl], a_filt_w1[i], a_filt_b1[i], a_filt_f1[i], a_filt_w2[i],
                                        a_filt_b2[i], a_filt_f2[i], a_filt_w3[i]) for sl in seq_lens}
            acts = [hybrid_layer(x2d, b, sl, a_norm[i], w_in_bf, a_conv_w[i], a_conv_b[i], a_hyena_d[i], a_sink[i],
                                 w_out_bf, specs[sl], tabs[sl], rope_even[sl])
                    for x2d, (b, sl) in zip(acts, shapes)]
        else:
            w_in_bfs = _odd_w_in(c_w_in[i])
            w_out_bf = c_w_out[i].astype(BF16)
            final_g = final_norm if layer == depth - 1 else None
            acts = [dilated_layer(x2d, b, sl, c_norm[i], w_in_bfs, w_out_bf, rope_odd[sl], final_g)
                    for x2d, (b, sl) in zip(acts, shapes)]
    assert depth % 2 == 0
    return tuple(a.reshape(b, sl, D_MODEL) for a, (b, sl) in zip(acts, shapes))
```

```python
import functools
import math

import numpy as np
import jax
import jax.numpy as jnp
from jax import lax
from jax.experimental import pallas as pl
from jax.experimental.pallas import tpu as pltpu

F32 = jnp.float32
BF16 = jnp.bfloat16

D_MODEL = 1024
HC = 512
FILTER_BANDS = 16
DECAY_TARGET = 1e-2
FAST_DECAY_PCT = 0.3
SLOW_DECAY_PCT = 1.5
DECAY_SHIFT = 0.05
B_HEAD_DIM = 64
B_HEADS = 8
B_KV_HEADS = 2
B_GROUP = 4
B_BLOCK = 128
C_PATTERNS = ((128, 1), (512, 4), (2048, 16))
C_HEADS = 8
C_HEAD_DIM = 128
C_RADIUS = 64
ROPE_THETA = 500000.0
NORM_EPS = 1e-6
NEG_INF = -1e30

LANES = 128
SUBLANES = 8
N2 = 128
J2 = N2 // SUBLANES
SLABS = HC // LANES
K1_BLOCK = 8
VMEM_LIMIT = 56 * 1024 * 1024


def _cparams(n_axes):
    return pltpu.CompilerParams(dimension_semantics=("arbitrary",) * n_axes,
                                vmem_limit_bytes=VMEM_LIMIT)


def _rope_chunk(blk, cos, sa, sb, half):
    return (blk * cos + pltpu.roll(blk, LANES - half, 1) * sa + pltpu.roll(blk, half, 1) * sb)


def _norm_proj_kernel(x_ref, g_ref, w_ref, cos_ref, sa_ref, sb_ref, *outs, half, groups, row_chunk):
    g = g_ref[...]
    for r0 in range(0, x_ref.shape[0], row_chunk):
        rows = slice(r0, r0 + row_chunk)
        x = x_ref[rows, :]
        ms = jnp.mean(x * x, axis=-1, keepdims=True)
        xn = (x * lax.rsqrt(ms + NORM_EPS) * g).astype(BF16)
        acc = jnp.dot(xn, w_ref[...], preferred_element_type=F32)
        col = 0
        for gi, (o_ref, (width, dtype)) in enumerate(zip(outs, groups)):
            if gi == 0:
                cos, sa, sb = cos_ref[rows, :], sa_ref[rows, :], sb_ref[rows, :]
                for ci in range(width // LANES):
                    src = slice(col + ci * LANES, col + (ci + 1) * LANES)
                    o_ref[rows, ci * LANES:(ci + 1) * LANES] = _rope_chunk(acc[:, src], cos, sa, sb, half).astype(dtype)
            else:
                o_ref[rows, :] = acc[:, col:col + width].astype(dtype)
            col += width


def norm_proj(x2d, g, w_bf, seq_len, rope, half, groups, *, tm=512, row_chunk=256):
    m, d = x2d.shape
    n = w_bf.shape[1]
    assert m % tm == 0 and seq_len % tm == 0 and sum(wd for wd, _ in groups) == n
    tiles_per_seq = seq_len // tm
    in_specs = [pl.BlockSpec((tm, d), lambda i: (i, 0)),
                pl.BlockSpec((1, d), lambda i: (0, 0)),
                pl.BlockSpec((d, n), lambda i: (0, 0))]
    in_specs += [pl.BlockSpec((tm, LANES), lambda i: (i % tiles_per_seq, 0))] * 3
    return pl.pallas_call(
        functools.partial(_norm_proj_kernel, half=half, groups=tuple(groups), row_chunk=row_chunk),
        grid=(m // tm,), in_specs=in_specs,
        out_specs=tuple(pl.BlockSpec((tm, wd), lambda i: (i, 0)) for wd, _ in groups),
        out_shape=tuple(jax.ShapeDtypeStruct((m, wd), dt) for wd, dt in groups),
        compiler_params=_cparams(1), name="norm_proj")(x2d, g.reshape(1, d), w_bf, *rope)


def rope_tables(seq_len, head_dim):
    rot = head_dim // 4
    half = rot // 2
    inv = jnp.power(ROPE_THETA, -2.0 * jnp.arange(half, dtype=F32) / rot)
    ang = jnp.arange(seq_len).astype(F32)[:, None] * inv[None, :]
    cos, sin = jnp.cos(ang), jnp.sin(ang)
    lane = np.arange(LANES) % head_dim
    idx = lane % half
    in_rot = jnp.asarray(lane < rot)[None, :]
    first = jnp.asarray(lane < half)[None, :]
    cos_t = jnp.where(in_rot, cos[:, idx], 1.0)
    sa = jnp.where(first, -sin[:, idx], 0.0)
    sb = jnp.where(in_rot & ~first, sin[:, idx], 0.0)
    return (cos_t, sa, sb), half


def _short_conv_kernel(cur_ref, prev_ref, next_ref, w_ref, b_ref, x1_ref, x2_ref, v_ref, *, tc):
    i = pl.program_id(1)
    u = cur_ref[:, :3 * HC]
    row = lax.broadcasted_iota(jnp.int32, u.shape, 0)
    prev_row = jnp.where(i > 0, prev_ref[7:8, :3 * HC], 0.0)
    next_row = jnp.where(i < pl.num_programs(1) - 1, next_ref[0:1, :3 * HC], 0.0)
    um1 = jnp.where(row == 0, prev_row, pltpu.roll(u, 1, 0))
    up1 = jnp.where(row == tc - 1, next_row, pltpu.roll(u, tc - 1, 0))
    w = w_ref[...]
    y = um1 * w[0:1, :] + u * w[1:2, :] + up1 * w[2:3, :] + b_ref[...]
    g = cur_ref[:, 3 * HC:]
    x2 = y[:, HC:2 * HC] * (g / (1.0 + jnp.exp(-g)))
    _store_time_tiles(x1_ref, y[:, :HC])
    _store_time_tiles(x2_ref, x2)
    _store_time_tiles(v_ref, y[:, 2 * HC:])


def _store_time_tiles(ref, val):
    for s in range(SLABS):
        for i in range(val.shape[0] // N2):
            ref[s, :, i] = val[i * N2:(i + 1) * N2, s * LANES:(s + 1) * LANES].reshape(J2, SUBLANES, LANES)


def _load_time_tiles(ref, nb):
    return jnp.concatenate([jnp.concatenate([ref[s, :, i].reshape(N2, LANES) for s in range(SLABS)], axis=1)
                            for i in range(nb)], axis=0)


def short_conv(hy, conv_w, conv_b, batch, seq_len, *, tc=256):
    width = hy.shape[1]
    h3 = hy.reshape(batch, seq_len, width)
    nb8 = seq_len // 8
    out = jax.ShapeDtypeStruct((batch, SLABS, J2, seq_len // N2, SUBLANES, LANES), F32)
    ospec = pl.BlockSpec((None, SLABS, J2, tc // N2, SUBLANES, LANES), lambda b, i: (b, 0, 0, i, 0, 0))
    return pl.pallas_call(
        functools.partial(_short_conv_kernel, tc=tc),
        grid=(batch, seq_len // tc),
        in_specs=[pl.BlockSpec((None, tc, width), lambda b, i: (b, i, 0)),
                  pl.BlockSpec((None, 8, width), lambda b, i: (b, jnp.maximum(i * (tc // 8) - 1, 0), 0)),
                  pl.BlockSpec((None, 8, width), lambda b, i: (b, jnp.minimum((i + 1) * (tc // 8), nb8 - 1), 0)),
                  pl.BlockSpec((3, 3 * HC), lambda b, i: (0, 0)),
                  pl.BlockSpec((1, 3 * HC), lambda b, i: (0, 0))],
        out_specs=(ospec, ospec, ospec), out_shape=(out, out, out),
        compiler_params=_cparams(2), name="short_conv")(h3, h3, h3, conv_w, conv_b.reshape(1, -1))


def _filter_kernel(band_ref, w1_ref, b1_ref, f1_ref, w2_ref, b2_ref, f2_ref, w3_ref, dl_ref, o_ref,
                   *, seq_len, tr):
    n = pl.program_id(0) * tr + lax.broadcasted_iota(jnp.int32, (tr, 1), 0)
    j = jnp.where(n < seq_len, n, 2 * seq_len - n).astype(F32)
    t = j * (1.0 / (seq_len - 1))
    wpos = (2.0 * math.pi) * j / seq_len
    lane = lax.broadcasted_iota(jnp.int32, (tr, LANES), 1)
    arg = band_ref[...] * wpos
    feats = jnp.where(lane == 0, t,
                      jnp.where(lane <= FILTER_BANDS, jnp.cos(arg),
                                jnp.where(lane <= 2 * FILTER_BANDS, -jnp.sin(arg), 0.0)))
    h = jnp.dot(feats.astype(BF16), w1_ref[...], preferred_element_type=F32) + b1_ref[...]
    h = jnp.sin(f1_ref[...] * h)
    h = jnp.dot(h.astype(BF16), w2_ref[...], preferred_element_type=F32) + b2_ref[...]
    h = jnp.sin(f2_ref[...] * h)
    h = jnp.dot(h.astype(BF16), w3_ref[...], preferred_element_type=F32)
    win = jnp.exp(-t * dl_ref[...]) + DECAY_SHIFT
    win2 = jnp.concatenate([win, win], axis=1)
    fwd = h[:, :2 * HC] * win2
    bwd = h[:, 2 * HC:] * win2
    res = (jnp.where(n < seq_len, fwd, 0.0)
           + jnp.where(n == 0, bwd, 0.0) - jnp.where(n > seq_len, bwd, 0.0))
    for order in range(2):
        _store_time_tiles(o_ref.at[order * SLABS:(order + 1) * SLABS], res[:, order * HC:(order + 1) * HC])


def hyena_filter(seq_len, w1, b1, f1, w2, b2, f2, w3, *, tr=512):
    hid = w1.shape[1]
    pad = LANES - hid
    w1p = jnp.pad(w1, ((0, LANES - w1.shape[0]), (0, pad))).astype(BF16)
    w2p = jnp.pad(w2, ((0, pad), (0, pad))).astype(BF16)
    w3r = w3.reshape(hid, 2, 2, HC).transpose(0, 2, 1, 3).reshape(hid, 4 * HC)
    w3p = jnp.pad(w3r, ((0, pad), (0, 0))).astype(BF16)
    vec = lambda v: jnp.pad(v, (0, pad)).reshape(1, LANES)
    bands = jnp.linspace(1e-4, FILTER_BANDS - 1, FILTER_BANDS, dtype=F32)
    band_l = jnp.concatenate([jnp.zeros((1,), F32), bands, bands,
                              jnp.zeros((LANES - 1 - 2 * FILTER_BANDS,), F32)]).reshape(1, LANES)
    max_decay = math.log(DECAY_TARGET) / FAST_DECAY_PCT
    min_decay = math.log(DECAY_TARGET) / SLOW_DECAY_PCT
    deltas = jnp.abs(jnp.linspace(min_decay, max_decay, HC, dtype=F32)).reshape(1, HC)
    full = lambda a: pl.BlockSpec(a.shape, lambda i: (0, 0))
    args = [band_l, w1p, vec(b1), vec(f1), w2p, vec(b2), vec(f2), w3p, deltas]
    return pl.pallas_call(
        functools.partial(_filter_kernel, seq_len=seq_len, tr=tr),
        grid=(2 * seq_len // tr,),
        in_specs=[full(a) for a in args],
        out_specs=pl.BlockSpec((2 * SLABS, J2, tr // N2, SUBLANES, LANES), lambda i: (0, 0, i, 0, 0)),
        out_shape=jax.ShapeDtypeStruct((2 * SLABS, J2, 2 * seq_len // N2, SUBLANES, LANES), F32),
        compiler_params=_cparams(1), name="hyena_filter")(*args)


def dft_tables(n1h):
    n1 = 2 * n1h
    n = n1 * N2
    k1 = jnp.arange(n1h, dtype=jnp.int32)
    m1 = (jnp.arange(n1, dtype=jnp.int32)[None, :] * (2 * k1[:, None] + 1)) % (2 * n1)
    a1 = m1.astype(F32) * (math.pi / n1)
    c1, s1 = jnp.cos(a1), jnp.sin(a1)
    w1_full = jnp.concatenate([c1, -s1], axis=0).astype(BF16)
    w1_half = w1_full[:, :n1h]
    w1_inv = jnp.concatenate([c1[:, :n1h].T, -s1[:, :n1h].T], axis=1).astype(BF16)
    k2 = jnp.arange(N2, dtype=jnp.int32)
    n2 = jnp.arange(N2, dtype=jnp.int32)
    f = 2 * (k2[None, :, None] * n1 + k1[:, None, None]) + 1
    m2 = (n2[None, None, :] * f) % (2 * n)
    th = m2.astype(F32) * (math.pi / n)
    c, s = jnp.cos(th), jnp.sin(th)
    mat = jnp.concatenate([jnp.concatenate([c, s], axis=2),
                           jnp.concatenate([-s, c], axis=2)], axis=1)
    return w1_half, w1_full, w1_inv, mat.astype(BF16), jnp.swapaxes(mat, 1, 2).astype(BF16)


def _rows_of(ref2, m, n_rows):
    return jnp.concatenate([ref2[cb, pl.ds(m, n_rows, stride=SUBLANES), :] for cb in range(SLABS)], axis=1)


def _store_rows(ref2, m, val):
    for cb in range(SLABS):
        ref2[cb, pl.ds(m, val.shape[0], stride=SUBLANES), :] = val[:, cb * LANES:(cb + 1) * LANES]


def _stage1_kernel(w_ref, x_ref, o_ref, *, k, r):
    w = w_ref[...]
    for m in range(SUBLANES):
        _store_rows(o_ref, m, jnp.dot(w, _rows_of(x_ref, m, k).astype(BF16), preferred_element_type=F32))


def dft_stage1(w, x5):
    b, s = x5.shape[:2]
    r, k = w.shape
    assert s % SLABS == 0 and x5.shape[3] == k * SUBLANES
    blk = lambda rows: pl.BlockSpec((None, SLABS, None, rows * SUBLANES, LANES), lambda bi, si, j: (bi, si, j, 0, 0))
    return pl.pallas_call(
        functools.partial(_stage1_kernel, k=k, r=r), grid=(b, s // SLABS, J2),
        in_specs=[pl.BlockSpec((r, k), lambda bi, si, j: (0, 0)), blk(k)],
        out_specs=blk(r),
        out_shape=jax.ShapeDtypeStruct((b, s, J2, r * SUBLANES, LANES), F32),
        compiler_params=_cparams(3), name="dft_stage1")(w, x5)


def _slab_cat(a_ref, kk, n_slabs):
    return jnp.concatenate([jnp.concatenate([a_ref[cb, :, 0, kk].reshape(N2, LANES),
                                             a_ref[cb, :, 1, kk].reshape(N2, LANES)], axis=0)
                            for cb in range(n_slabs)], axis=1).astype(BF16)


def _stage2_filter_kernel(m_ref, a_ref, o_ref):
    for kk in range(K1_BLOCK):
        o_ref[kk] = jnp.dot(m_ref[kk], _slab_cat(a_ref, kk, 2 * SLABS), preferred_element_type=F32)


def dft_stage2_filter(mat, a7):
    n1h = a7.shape[4]
    return pl.pallas_call(
        _stage2_filter_kernel, grid=(n1h // K1_BLOCK,),
        in_specs=[pl.BlockSpec((K1_BLOCK, 2 * N2, 2 * N2), lambda kb: (kb, 0, 0)),
                  pl.BlockSpec((None, 2 * SLABS, J2, 2, K1_BLOCK, SUBLANES, LANES), lambda kb: (0, 0, 0, 0, kb, 0, 0))],
        out_specs=pl.BlockSpec((K1_BLOCK, 2 * N2, 2 * HC), lambda kb: (kb, 0, 0)),
        out_shape=jax.ShapeDtypeStruct((n1h, 2 * N2, 2 * HC), F32),
        compiler_params=_cparams(1), name="dft_stage2_filter")(mat, a7)


def _stage2_conv_kernel(m_ref, mt_ref, h_ref, a_ref, o_ref):
    for kk in range(K1_BLOCK):
        z = jnp.dot(m_ref[kk], _slab_cat(a_ref, kk, SLABS), preferred_element_type=F32)
        zr, zi = z[:N2], z[N2:]
        hr, hi = h_ref[kk, :N2], h_ref[kk, N2:]
        y = jnp.concatenate([zr * hr - zi * hi, zr * hi + zi * hr], axis=0).astype(BF16)
        g = jnp.dot(mt_ref[kk], y, preferred_element_type=F32)
        for cb in range(SLABS):
            o_ref[cb, :, 0, kk] = g[:N2, cb * LANES:(cb + 1) * LANES].reshape(J2, SUBLANES, LANES)
            o_ref[cb, :, 1, kk] = g[N2:, cb * LANES:(cb + 1) * LANES].reshape(J2, SUBLANES, LANES)


def dft_stage2_conv(mat, mat_t, spec, order, a7):
    b, n1h = a7.shape[0], a7.shape[4]
    ablk = pl.BlockSpec((None, SLABS, J2, 2, K1_BLOCK, SUBLANES, LANES), lambda kb, bi: (bi, 0, 0, 0, kb, 0, 0))
    return pl.pallas_call(
        _stage2_conv_kernel, grid=(n1h // K1_BLOCK, b),
        in_specs=[pl.BlockSpec((K1_BLOCK, 2 * N2, 2 * N2), lambda kb, bi: (kb, 0, 0)),
                  pl.BlockSpec((K1_BLOCK, 2 * N2, 2 * N2), lambda kb, bi: (kb, 0, 0)),
                  pl.BlockSpec((K1_BLOCK, 2 * N2, HC), lambda kb, bi: (kb, 0, order)),
                  ablk],
        out_specs=ablk,
        out_shape=jax.ShapeDtypeStruct(a7.shape, F32),
        compiler_params=_cparams(2), name="dft_stage2_conv")(mat, mat_t, spec, a7)


def _stage1_inv_kernel(w_ref, g_ref, z_ref, gate_ref, d_ref, o_ref, *, n1h, scale):
    w, d = w_ref[...], d_ref[...]
    for m in range(SUBLANES):
        y = jnp.dot(w, _rows_of(g_ref, m, 2 * n1h).astype(BF16), preferred_element_type=F32) * scale
        _store_rows(o_ref, m, _rows_of(gate_ref, m, n1h) * (y + d * _rows_of(z_ref, m, n1h)))


def dft_stage1_inv(w_inv, g5, z5, gate5, d_vec, *, scale):
    b = g5.shape[0]
    n1h = w_inv.shape[0]
    blk = lambda rows: pl.BlockSpec((None, SLABS, None, rows * SUBLANES, LANES), lambda bi, j: (bi, 0, j, 0, 0))
    return pl.pallas_call(
        functools.partial(_stage1_inv_kernel, n1h=n1h, scale=scale),
        grid=(b, J2),
        in_specs=[pl.BlockSpec((n1h, 2 * n1h), lambda bi, j: (0, 0)), blk(2 * n1h), blk(n1h), blk(n1h),
                  pl.BlockSpec((1, HC), lambda bi, j: (0, 0))],
        out_specs=blk(n1h),
        out_shape=jax.ShapeDtypeStruct(z5.shape, F32),
        compiler_params=_cparams(2), name="dft_stage1_inv")(w_inv, g5, z5, gate5, d_vec)


def _silu(g):
    return g / (1.0 + jnp.exp(-g))


def _band_mask(t, halo, i, nblk):
    qi = lax.broadcasted_iota(jnp.int32, (t, t + 2 * halo), 0)
    kj = lax.broadcasted_iota(jnp.int32, (t, t + 2 * halo), 1)
    rel = kj - halo - qi
    return ((jnp.abs(rel) <= halo) & ((kj >= halo) | (i > 0)) & ((kj < halo + t) | (i < nblk - 1)))


def _head_pair_operands(win):
    x = win.astype(F32)
    swapped = pltpu.roll(x, B_HEAD_DIM, 1)
    low = lax.broadcasted_iota(jnp.int32, x.shape, 1) < B_HEAD_DIM
    place = lambda cond, val: jnp.where(cond, val, 0.0).astype(BF16)
    return ((place(low, x), place(~low, swapped)), (place(low, swapped), place(~low, x)))


def _band_attn_kernel(sink_ref, q_ref, kp_ref, kc_ref, kn_ref, vp_ref, vc_ref, vn_ref, gate_ref, o_ref, *, nsub):
    t, hd = B_BLOCK, B_HEAD_DIM
    kpads = _head_pair_operands(jnp.concatenate([kp_ref[...], kc_ref[...], kn_ref[...]], axis=0))
    vpads = _head_pair_operands(jnp.concatenate([vp_ref[...], vc_ref[...], vn_ref[...]], axis=0))
    n_blocks = pl.num_programs(1) * nsub
    heads = [(pair, par) for pair in range(B_HEADS // 2) for par in range(2)]
    for u in range(nsub):
        valid = _band_mask(t, t, pl.program_id(1) * nsub + u, n_blocks)
        rows = slice(u * t, (u + 1) * t)
        win = slice(u * t, (u + 3) * t)
        scores = [lax.dot_general(q_ref[rows, pair * LANES:(pair + 1) * LANES], kpads[(2 * pair) // B_GROUP][par][win],
                                  (((1,), (1,)), ((), ())), preferred_element_type=F32) for pair, par in heads]
        probs, denoms = [], []
        for (pair, par), s in zip(heads, scores):
            s = jnp.where(valid, s * (hd ** -0.5), NEG_INF)
            sink = sink_ref[2 * pair + par]
            m = jnp.maximum(jnp.max(s, axis=-1, keepdims=True), sink)
            p = jnp.exp(s - m)
            denoms.append(jnp.sum(p, axis=-1, keepdims=True) + jnp.exp(sink - m))
            probs.append(p.astype(BF16))
        outs = [jnp.dot(p, vpads[(2 * pair) // B_GROUP][par][win], preferred_element_type=F32) / d
                for (pair, par), p, d in zip(heads, probs, denoms)]
        for pair in range(B_HEADS // 2):
            lanes = slice(pair * LANES, (pair + 1) * LANES)
            o_ref[rows, lanes] = (outs[2 * pair] + outs[2 * pair + 1]) * _silu(gate_ref[rows, lanes])


def band_attn(qk, v_att, at_gate, sink, batch, seq_len, *, nsub=4):
    t = B_BLOCK
    tq = t * nsub
    nblk = seq_len // t
    qk3 = qk.reshape(batch, seq_len, qk.shape[1])
    v3 = v_att.reshape(batch, seq_len, LANES)
    g3 = at_gate.reshape(batch, seq_len, at_gate.shape[1])
    k_col = B_HEADS * B_HEAD_DIM // LANES
    halo = lambda col, f: pl.BlockSpec((None, t, LANES), lambda b, i: (b, f(i), col))
    cur = lambda col: pl.BlockSpec((None, tq, LANES), lambda b, i: (b, i, col))
    prev = lambda i: jnp.maximum(i * nsub - 1, 0)
    nxt = lambda i: jnp.minimum((i + 1) * nsub, nblk - 1)
    wide = pl.BlockSpec((None, tq, B_HEADS * B_HEAD_DIM), lambda b, i: (b, i, 0))
    out = pl.pallas_call(
        functools.partial(_band_attn_kernel, nsub=nsub), grid=(batch, seq_len // tq),
        in_specs=[pl.BlockSpec(memory_space=pltpu.SMEM), wide,
                  halo(k_col, prev), cur(k_col), halo(k_col, nxt), halo(0, prev), cur(0), halo(0, nxt), wide],
        out_specs=wide,
        out_shape=jax.ShapeDtypeStruct((batch, seq_len, B_HEADS * B_HEAD_DIM), F32),
        compiler_params=_cparams(2), name="band_attn")(sink, qk3, qk3, qk3, qk3, v3, v3, v3, g3)
    return out.reshape(batch * seq_len, -1)


def _dil_attn_kernel(q_ref, kp_ref, kc_ref, kn_ref, vp_ref, vc_ref, vn_ref, o_ref, lse_ref, *, nsub):
    t, hd, halo = 2 * C_RADIUS, C_HEAD_DIM, C_RADIUS
    n_blocks = pl.num_programs(1) * nsub
    lane = lax.broadcasted_iota(jnp.int32, (t, LANES), 1)
    for u in range(nsub):
        valid = _band_mask(t, halo, pl.program_id(1) * nsub + u, n_blocks)
        rows = slice(u * t, (u + 1) * t)
        win = slice(u * t, (u + 1) * t + 2 * halo)
        cols = [slice(h * hd, (h + 1) * hd) for h in range(C_HEADS)]
        window = lambda p_ref, c_ref, n_ref, sl: jnp.concatenate([p_ref[:, sl], c_ref[:, sl], n_ref[:, sl]], axis=0)[win]
        scores = [lax.dot_general(q_ref[rows, sl], window(kp_ref, kc_ref, kn_ref, sl), (((1,), (1,)), ((), ())),
                                  preferred_element_type=F32) for sl in cols]
        probs, denoms = [], []
        lse_tile = jnp.zeros((t, LANES), F32)
        for h, s in enumerate(scores):
            s = jnp.where(valid, s * (hd ** -0.5), NEG_INF)
            m = jnp.max(s, axis=-1, keepdims=True)
            p = jnp.exp(s - m)
            denom = jnp.sum(p, axis=-1, keepdims=True)
            lse_tile = jnp.where(lane == h, m + jnp.log(denom), lse_tile)
            probs.append(p.astype(BF16))
            denoms.append(denom)
        for sl, p, denom in zip(cols, probs, denoms):
            o_ref[rows, sl] = jnp.dot(p, window(vp_ref, vc_ref, vn_ref, sl), preferred_element_type=F32) / denom
        lse_ref[rows, :] = lse_tile


def dil_attn(qk, v, n_seq, ls):
    w = C_HEADS * C_HEAD_DIM
    t = 2 * C_RADIUS
    nsub = 2 if ls % (2 * t) == 0 else 1
    tq = t * nsub
    n_halo = ls // C_RADIUS
    qk3 = qk.reshape(n_seq, ls, qk.shape[1])
    v3 = v.reshape(n_seq, ls, w)
    prev = lambda i: jnp.maximum(i * (tq // C_RADIUS) - 1, 0)
    nxt = lambda i: jnp.minimum((i + 1) * (tq // C_RADIUS), n_halo - 1)
    big = lambda col: pl.BlockSpec((None, tq, w), lambda b, i: (b, i, col))
    halo = lambda col, f: pl.BlockSpec((None, C_RADIUS, w), lambda b, i: (b, f(i), col))
    o, lse = pl.pallas_call(
        functools.partial(_dil_attn_kernel, nsub=nsub), grid=(n_seq, ls // tq),
        in_specs=[big(0), halo(1, prev), big(1), halo(1, nxt), halo(0, prev), big(0), halo(0, nxt)],
        out_specs=(big(0), pl.BlockSpec((None, tq, LANES), lambda b, i: (b, i, 0))),
        out_shape=(jax.ShapeDtypeStruct((n_seq, ls, w), F32),
                   jax.ShapeDtypeStruct((n_seq, ls, LANES), F32)),
        compiler_params=_cparams(2), name="dil_attn")(qk3, qk3, qk3, qk3, v3, v3, v3)
    return o.reshape(n_seq * ls, w), lse.reshape(n_seq * ls, LANES)


def _out_proj_even_kernel(hy_ref, at_ref, x_ref, w_ref, o_ref, *, nb):
    mixed = jnp.concatenate([_load_time_tiles(hy_ref, nb), at_ref[...]], axis=1).astype(BF16)
    o_ref[...] = x_ref[...] + jnp.dot(mixed, w_ref[...], preferred_element_type=F32)


def out_proj_even(hy, at, x2d, w_bf, *, tm=512):
    m, d = x2d.shape
    batch, n1h = hy.shape[0], hy.shape[3]
    tps = n1h * N2 // tm
    nb = tm // N2
    half = pl.BlockSpec((tm, HC), lambda b, i: (b * tps + i, 0))
    full = pl.BlockSpec((tm, d), lambda b, i: (b * tps + i, 0))
    return pl.pallas_call(
        functools.partial(_out_proj_even_kernel, nb=nb), grid=(batch, tps),
        in_specs=[pl.BlockSpec((None, SLABS, J2, nb, SUBLANES, LANES), lambda b, i: (b, 0, 0, i, 0, 0)), half, full,
                  pl.BlockSpec((d, d), lambda b, i: (0, 0))],
        out_specs=full, out_shape=jax.ShapeDtypeStruct((m, d), F32),
        compiler_params=_cparams(2), name="out_proj_even")(hy, at, x2d, w_bf)


def _merge_out_kernel(*refs, final):
    o_refs, l_refs = refs[0:3], refs[3:6]
    gate_ref, x_ref, w_ref = refs[6:9]
    out_ref = refs[-1]
    ls = [r[...] for r in l_refs]
    mx = jnp.maximum(jnp.maximum(ls[0], ls[1]), ls[2])
    es = [jnp.exp(l - mx) for l in ls]
    den = es[0] + es[1] + es[2]
    alphas = [e / den for e in es]
    hd = C_HEAD_DIM
    parts = []
    for h in range(C_HEADS):
        sl = slice(h * hd, (h + 1) * hd)
        parts.append(sum(alphas[g][:, h:h + 1] * o_refs[g][:, sl] for g in range(3)))
    y = (jnp.concatenate(parts, axis=1) * _silu(gate_ref[...])).astype(BF16)
    r = x_ref[...] + jnp.dot(y, w_ref[...], preferred_element_type=F32)
    if final:
        ms = jnp.mean(r * r, axis=-1, keepdims=True)
        r = r * lax.rsqrt(ms + NORM_EPS) * refs[9][...]
    out_ref[...] = r


def merge_out(outs, lses, pp, x2d, w_bf, final_g=None, *, tm=512):
    m, d = x2d.shape
    full = pl.BlockSpec((tm, d), lambda i: (i, 0))
    lspec = pl.BlockSpec((tm, LANES), lambda i: (i, 0))
    gate_col = pp.shape[1] // d - 1
    in_specs = [full] * 3 + [lspec] * 3 + [pl.BlockSpec((tm, d), lambda i: (i, gate_col)), full,
                                           pl.BlockSpec((d, d), lambda i: (0, 0))]
    args = list(outs) + list(lses) + [pp, x2d, w_bf]
    if final_g is not None:
        in_specs.append(pl.BlockSpec((1, d), lambda i: (0, 0)))
        args.append(final_g.reshape(1, d))
    return pl.pallas_call(
        functools.partial(_merge_out_kernel, final=final_g is not None), grid=(m // tm,),
        in_specs=in_specs, out_specs=full, out_shape=jax.ShapeDtypeStruct((m, d), F32),
        compiler_params=_cparams(1), name="merge_out")(*args)


def _even_w_in(w_in):
    return jnp.concatenate([w_in[:, 2048:2688], w_in[:, :2048], w_in[:, 2816:], w_in[:, 2688:2816]], axis=1).astype(BF16)


def _odd_w_in(w_in):
    w = C_HEADS * C_HEAD_DIM
    blk = lambda g, t: w_in[:, (3 * g + t) * w:(3 * g + t + 1) * w]
    per = [[blk(g, 0), blk(g, 1), blk(g, 2)] for g in range(len(C_PATTERNS))]
    per[0].append(w_in[:, 9 * w:])
    return [jnp.concatenate(cols, axis=1).astype(BF16) for cols in per]


def hyena_spectrum(seq_len, tabs, fw1, fb1, ff1, fw2, fb2, ff2, fw3):
    n1h = seq_len // N2
    _, w1_full, _, mat, _ = tabs
    filt = hyena_filter(seq_len, fw1, fb1, ff1, fw2, fb2, ff2, fw3)
    a = dft_stage1(w1_full, filt.reshape(1, 2 * SLABS, J2, 2 * n1h * SUBLANES, LANES))
    return dft_stage2_filter(mat, a.reshape(1, 2 * SLABS, J2, 2, n1h, SUBLANES, LANES))


def hyena_conv(z, gate, d_vec, spec, order, tabs):
    batch, n1h = z.shape[0], z.shape[3]
    w1_half, _, w1_inv, mat, mat_t = tabs
    rows = lambda a, r: a.reshape(batch, SLABS, J2, r * SUBLANES, LANES)
    z5 = rows(z, n1h)
    a = dft_stage1(w1_half, z5)
    g = dft_stage2_conv(mat, mat_t, spec, order, a.reshape(batch, SLABS, J2, 2, n1h, SUBLANES, LANES))
    out = dft_stage1_inv(w1_inv, rows(g, 2 * n1h), z5, rows(gate, n1h), d_vec.reshape(1, HC), scale=1.0 / (n1h * N2))
    return out.reshape(z.shape)


def hybrid_layer(x2d, batch, seq_len, norm_g, w_in_bf, conv_w, conv_b, hyena_d, sink, w_out_bf, spec, tabs, rope):
    rope_t, half = rope
    qk, hy, at_gate, v_att = norm_proj(x2d, norm_g, w_in_bf, seq_len, rope_t, half,
                                       ((640, BF16), (2048, F32), (512, F32), (128, BF16)))
    x1, x2s, v = short_conv(hy, conv_w, conv_b, batch, seq_len)
    z = hyena_conv(v, x1, hyena_d[0], spec, 0, tabs)
    z = hyena_conv(z, x2s, hyena_d[1], spec, 1, tabs)
    at = band_attn(qk, v_att, at_gate, sink, batch, seq_len)
    return out_proj_even(z, at, x2d, w_out_bf)


def _to_strided(a2d, batch, seq_len, dilation):
    if dilation == 1:
        return a2d
    w = a2d.shape[1]
    return a2d.reshape(batch, seq_len // dilation, dilation, w).swapaxes(1, 2).reshape(batch * seq_len, w)


def _from_strided(a2d, batch, seq_len, dilation):
    if dilation == 1:
        return a2d
    w = a2d.shape[1]
    return a2d.reshape(batch, dilation, seq_len // dilation, w).swapaxes(1, 2).reshape(batch * seq_len, w)


def dilated_layer(x2d, batch, seq_len, norm_g, w_in_bfs, w_out_bf, ropes, final_g):
    w = C_HEADS * C_HEAD_DIM
    outs, lses, gate = [], [], None
    for gi, (_, dilation) in enumerate(C_PATTERNS):
        rope_t, half = ropes[dilation]
        xs = _to_strided(x2d, batch, seq_len, dilation)
        groups = ((2 * w, BF16), (w, BF16)) + (((w, F32),) if gi == 0 else ())
        res = norm_proj(xs, norm_g, w_in_bfs[gi], seq_len, rope_t, half, groups)
        if gi == 0:
            gate = res[2]
        o, lse = dil_attn(res[0], res[1], batch * dilation, seq_len // dilation)
        outs.append(_from_strided(o, batch, seq_len, dilation))
        lses.append(_from_strided(lse, batch, seq_len, dilation))
    return merge_out(outs, lses, gate, x2d, w_out_bf, final_g)


def kernel(x_prompt, x_sample, a_norm, a_w_in, a_conv_w, a_conv_b, a_filt_w1, a_filt_b1, a_filt_f1, a_filt_w2, a_filt_b2, a_filt_f2, a_filt_w3, a_hyena_d, a_sink, a_w_out, c_norm, c_w_in, c_w_out, final_norm):
    depth = a_norm.shape[0] + c_norm.shape[0]
    xs = [x_prompt, x_sample]
    shapes = [(x.shape[0], x.shape[1]) for x in xs]
    acts = [x.reshape(-1, D_MODEL) for x in xs]
    seq_lens = sorted({s[1] for s in shapes})
    tabs = {sl: dft_tables(sl // N2) for sl in seq_lens}
    rope_even = {sl: rope_tables(sl, B_HEAD_DIM) for sl in seq_lens}
    rope_odd = {}
    for sl in seq_lens:
        tables, half = rope_tables(sl, C_HEAD_DIM)
        rope_odd[sl] = {d: (tuple(_to_strided(t, 1, sl, d) for t in tables), half) for _, d in C_PATTERNS}
    for layer in range(depth):
        i = layer // 2
        if layer % 2 == 0:
            w_in_bf = _even_w_in(a_w_in[i])
            w_out_bf = a_w_out[i].astype(BF16)
            specs = {sl: hyena_spectrum(sl, tabs[sl], a_filt_w1[i], a_filt_b1[i], a_filt_f1[i], a_filt_w2[i],
                                        a_filt_b2[i], a_filt_f2[i], a_filt_w3[i]) for sl in seq_lens}
            acts = [hybrid_layer(x2d, b, sl, a_norm[i], w_in_bf, a_conv_w[i], a_conv_b[i], a_hyena_d[i], a_sink[i],
                                 w_out_bf, specs[sl], tabs[sl], rope_even[sl])
                    for x2d, (b, sl) in zip(acts, shapes)]
        else:
            w_in_bfs = _odd_w_in(c_w_in[i])
            w_out_bf = c_w_out[i].astype(BF16)
            final_g = final_norm if layer == depth - 1 else None
            acts = [dilated_layer(x2d, b, sl, c_norm[i], w_in_bfs, w_out_bf, rope_odd[sl], final_g)
                    for x2d, (b, sl) in zip(acts, shapes)]
    assert depth % 2 == 0
    return tuple(a.reshape(b, sl, D_MODEL) for a, (b, sl) in zip(acts, shapes))
```

```python
import functools
import math

import numpy as np
import jax
import jax.numpy as jnp
from jax import lax
from jax.experimental import pallas as pl
from jax.experimental.pallas import tpu as pltpu

F32 = jnp.float32
BF16 = jnp.bfloat16

D_MODEL = 1024
HC = 512
FILTER_BANDS = 16
DECAY_TARGET = 1e-2
FAST_DECAY_PCT = 0.3
SLOW_DECAY_PCT = 1.5
DECAY_SHIFT = 0.05
B_HEAD_DIM = 64
B_HEADS = 8
B_KV_HEADS = 2
B_GROUP = 4
B_BLOCK = 128
C_PATTERNS = ((128, 1), (512, 4), (2048, 16))
C_HEADS = 8
C_HEAD_DIM = 128
C_RADIUS = 64
ROPE_THETA = 500000.0
NORM_EPS = 1e-6
NEG_INF = -1e30

LANES = 128
SUBLANES = 8
N2 = 128
J2 = N2 // SUBLANES
SLABS = HC // LANES
K1_BLOCK = 8
VMEM_LIMIT = 56 * 1024 * 1024


def _cparams(n_axes):
    return pltpu.CompilerParams(dimension_semantics=("arbitrary",) * n_axes,
                                vmem_limit_bytes=VMEM_LIMIT)


def _rope_chunk(blk, cos, sa, sb, half):
    return (blk * cos + pltpu.roll(blk, LANES - half, 1) * sa + pltpu.roll(blk, half, 1) * sb)


def _norm_proj_kernel(x_ref, g_ref, w_ref, cos_ref, sa_ref, sb_ref, *outs, half, groups, row_chunk):
    g = g_ref[...]
    for r0 in range(0, x_ref.shape[0], row_chunk):
        rows = slice(r0, r0 + row_chunk)
        acc = jnp.dot(_rms_bf16(x_ref[rows, :], g), w_ref[...], preferred_element_type=F32)
        col = 0
        for gi, (o_ref, (width, dtype)) in enumerate(zip(outs, groups)):
            if gi == 0:
                cos, sa, sb = cos_ref[rows, :], sa_ref[rows, :], sb_ref[rows, :]
                for ci in range(width // LANES):
                    src = slice(col + ci * LANES, col + (ci + 1) * LANES)
                    o_ref[rows, ci * LANES:(ci + 1) * LANES] = _rope_chunk(acc[:, src], cos, sa, sb, half).astype(dtype)
            else:
                o_ref[rows, :] = acc[:, col:col + width].astype(dtype)
            col += width


def norm_proj(x2d, g, w_bf, seq_len, rope, half, groups, *, tm=512, row_chunk=256):
    m, d = x2d.shape
    n = w_bf.shape[1]
    assert m % tm == 0 and seq_len % tm == 0 and sum(wd for wd, _ in groups) == n
    tiles_per_seq = seq_len // tm
    in_specs = [pl.BlockSpec((tm, d), lambda i: (i, 0)),
                pl.BlockSpec((1, d), lambda i: (0, 0)),
                pl.BlockSpec((d, n), lambda i: (0, 0))]
    in_specs += [pl.BlockSpec((tm, LANES), lambda i: (i % tiles_per_seq, 0))] * 3
    return pl.pallas_call(
        functools.partial(_norm_proj_kernel, half=half, groups=tuple(groups), row_chunk=row_chunk),
        grid=(m // tm,), in_specs=in_specs,
        out_specs=tuple(pl.BlockSpec((tm, wd), lambda i: (i, 0)) for wd, _ in groups),
        out_shape=tuple(jax.ShapeDtypeStruct((m, wd), dt) for wd, dt in groups),
        compiler_params=_cparams(1), name="norm_proj")(x2d, g.reshape(1, d), w_bf, *rope)


def _rms_bf16(x, g):
    ms = jnp.mean(x * x, axis=-1, keepdims=True)
    return (x * lax.rsqrt(ms + NORM_EPS) * g).astype(BF16)


_QK_W, _HY_W, _ATT_V_W = 640, 3 * HC, 128
_HY0 = _QK_W
_HYG0 = _HY0 + _HY_W
_ATG0 = _HYG0 + HC
_VATT0 = _ATG0 + HC


def _hybrid_proj_kernel(x_ref, xp_ref, xn_ref, g_ref, w_ref, cos_ref, sa_ref, sb_ref, cw_ref, cb_ref,
                        qk_ref, x1_ref, x2_ref, v_ref, atg_ref, vatt_ref, u_scr, *, half, tiles_per_seq, row_chunk):
    tm = x_ref.shape[0]
    i = pl.program_id(0) % tiles_per_seq
    g = g_ref[...]
    halo = jnp.concatenate([xp_ref[...], xn_ref[...]], axis=0)
    acc_h = jnp.dot(_rms_bf16(halo, g), w_ref[:, _HY0:_HYG0], preferred_element_type=F32)
    prev_row = jnp.where(i > 0, acc_h[SUBLANES - 1:SUBLANES], 0.0)
    next_row = jnp.where(i < tiles_per_seq - 1, acc_h[SUBLANES:SUBLANES + 1], 0.0)
    for r0 in range(0, tm, row_chunk):
        rows = slice(r0, r0 + row_chunk)
        acc = jnp.dot(_rms_bf16(x_ref[rows, :], g), w_ref[...], preferred_element_type=F32)
        cos, sa, sb = cos_ref[rows, :], sa_ref[rows, :], sb_ref[rows, :]
        for ci in range(_QK_W // LANES):
            sl = slice(ci * LANES, (ci + 1) * LANES)
            qk_ref[rows, sl] = _rope_chunk(acc[:, sl], cos, sa, sb, half).astype(BF16)
        u_scr[rows, :] = acc[:, _HY0:_ATG0]
        atg_ref[rows, :] = acc[:, _ATG0:_VATT0]
        vatt_ref[rows, :] = acc[:, _VATT0:].astype(BF16)
    u = u_scr[:, :_HY_W]
    row = lax.broadcasted_iota(jnp.int32, u.shape, 0)
    um1 = jnp.where(row == 0, prev_row, pltpu.roll(u, 1, 0))
    up1 = jnp.where(row == tm - 1, next_row, pltpu.roll(u, tm - 1, 0))
    w = cw_ref[...]
    y = um1 * w[0:1, :] + u * w[1:2, :] + up1 * w[2:3, :] + cb_ref[...]
    _store_time_tiles(x1_ref, y[:, :HC])
    _store_time_tiles(x2_ref, y[:, HC:2 * HC] * _silu(u_scr[:, _HY_W:]))
    _store_time_tiles(v_ref, y[:, 2 * HC:])


def hybrid_proj(x2d, g, w_bf, conv_w, conv_b, batch, seq_len, rope, half, *, tm=512, row_chunk=256):
    m, d = x2d.shape
    n = w_bf.shape[1]
    tiles_per_seq = seq_len // tm
    sub = tm // SUBLANES
    last8 = m // SUBLANES - 1
    hy_shape = jax.ShapeDtypeStruct((batch, SLABS, J2, seq_len // N2, SUBLANES, LANES), F32)
    hy_spec = pl.BlockSpec((None, SLABS, J2, tm // N2, SUBLANES, LANES),
                           lambda t: (t // tiles_per_seq, 0, 0, t % tiles_per_seq, 0, 0))
    row_blk = lambda wd: pl.BlockSpec((tm, wd), lambda t: (t, 0))
    const = lambda shape: pl.BlockSpec(shape, lambda t: (0, 0))
    return pl.pallas_call(
        functools.partial(_hybrid_proj_kernel, half=half, tiles_per_seq=tiles_per_seq, row_chunk=row_chunk),
        grid=(m // tm,),
        in_specs=[row_blk(d),
                  pl.BlockSpec((SUBLANES, d), lambda t: (jnp.maximum(t * sub - 1, 0), 0)),
                  pl.BlockSpec((SUBLANES, d), lambda t: (jnp.minimum((t + 1) * sub, last8), 0)),
                  const((1, d)), const((d, n))]
                 + [pl.BlockSpec((tm, LANES), lambda t: (t % tiles_per_seq, 0))] * 3
                 + [const((3, _HY_W)), const((1, _HY_W))],
        out_specs=(row_blk(_QK_W), hy_spec, hy_spec, hy_spec, row_blk(HC), row_blk(_ATT_V_W)),
        out_shape=(jax.ShapeDtypeStruct((m, _QK_W), BF16), hy_shape, hy_shape, hy_shape,
                   jax.ShapeDtypeStruct((m, HC), F32), jax.ShapeDtypeStruct((m, _ATT_V_W), BF16)),
        scratch_shapes=[pltpu.VMEM((tm, _HY_W + HC), F32)],
        compiler_params=_cparams(1), name="hybrid_proj")(x2d, x2d, x2d, g.reshape(1, d), w_bf, *rope,
                                                         conv_w, conv_b.reshape(1, -1))


def rope_tables(seq_len, head_dim):
    rot = head_dim // 4
    half = rot // 2
    inv = jnp.power(ROPE_THETA, -2.0 * jnp.arange(half, dtype=F32) / rot)
    ang = jnp.arange(seq_len).astype(F32)[:, None] * inv[None, :]
    cos, sin = jnp.cos(ang), jnp.sin(ang)
    lane = np.arange(LANES) % head_dim
    idx = lane % half
    in_rot = jnp.asarray(lane < rot)[None, :]
    first = jnp.asarray(lane < half)[None, :]
    cos_t = jnp.where(in_rot, cos[:, idx], 1.0)
    sa = jnp.where(first, -sin[:, idx], 0.0)
    sb = jnp.where(in_rot & ~first, sin[:, idx], 0.0)
    return (cos_t, sa, sb), half


def _store_time_tiles(ref, val):
    for s in range(SLABS):
        for i in range(val.shape[0] // N2):
            ref[s, :, i] = val[i * N2:(i + 1) * N2, s * LANES:(s + 1) * LANES].reshape(J2, SUBLANES, LANES)


def _load_time_tiles(ref, nb):
    return jnp.concatenate([jnp.concatenate([ref[s, :, i].reshape(N2, LANES) for s in range(SLABS)], axis=1)
                            for i in range(nb)], axis=0)


def _filter_kernel(band_ref, w1_ref, b1_ref, f1_ref, w2_ref, b2_ref, f2_ref, w3_ref, dl_ref, o_ref,
                   *, seq_len, tr):
    n = pl.program_id(0) * tr + lax.broadcasted_iota(jnp.int32, (tr, 1), 0)
    j = jnp.where(n < seq_len, n, 2 * seq_len - n).astype(F32)
    t = j * (1.0 / (seq_len - 1))
    wpos = (2.0 * math.pi) * j / seq_len
    lane = lax.broadcasted_iota(jnp.int32, (tr, LANES), 1)
    arg = band_ref[...] * wpos
    feats = jnp.where(lane == 0, t,
                      jnp.where(lane <= FILTER_BANDS, jnp.cos(arg),
                                jnp.where(lane <= 2 * FILTER_BANDS, -jnp.sin(arg), 0.0)))
    h = jnp.dot(feats.astype(BF16), w1_ref[...], preferred_element_type=F32) + b1_ref[...]
    h = jnp.sin(f1_ref[...] * h)
    h = jnp.dot(h.astype(BF16), w2_ref[...], preferred_element_type=F32) + b2_ref[...]
    h = jnp.sin(f2_ref[...] * h)
    h = jnp.dot(h.astype(BF16), w3_ref[...], preferred_element_type=F32)
    win = jnp.exp(-t * dl_ref[...]) + DECAY_SHIFT
    win2 = jnp.concatenate([win, win], axis=1)
    fwd = h[:, :2 * HC] * win2
    bwd = h[:, 2 * HC:] * win2
    res = (jnp.where(n < seq_len, fwd, 0.0)
           + jnp.where(n == 0, bwd, 0.0) - jnp.where(n > seq_len, bwd, 0.0))
    for order in range(2):
        _store_time_tiles(o_ref.at[order * SLABS:(order + 1) * SLABS], res[:, order * HC:(order + 1) * HC])


def hyena_filter(seq_len, w1, b1, f1, w2, b2, f2, w3, *, tr=512):
    hid = w1.shape[1]
    pad = LANES - hid
    w1p = jnp.pad(w1, ((0, LANES - w1.shape[0]), (0, pad))).astype(BF16)
    w2p = jnp.pad(w2, ((0, pad), (0, pad))).astype(BF16)
    w3r = w3.reshape(hid, 2, 2, HC).transpose(0, 2, 1, 3).reshape(hid, 4 * HC)
    w3p = jnp.pad(w3r, ((0, pad), (0, 0))).astype(BF16)
    vec = lambda v: jnp.pad(v, (0, pad)).reshape(1, LANES)
    bands = jnp.linspace(1e-4, FILTER_BANDS - 1, FILTER_BANDS, dtype=F32)
    band_l = jnp.concatenate([jnp.zeros((1,), F32), bands, bands,
                              jnp.zeros((LANES - 1 - 2 * FILTER_BANDS,), F32)]).reshape(1, LANES)
    max_decay = math.log(DECAY_TARGET) / FAST_DECAY_PCT
    min_decay = math.log(DECAY_TARGET) / SLOW_DECAY_PCT
    deltas = jnp.abs(jnp.linspace(min_decay, max_decay, HC, dtype=F32)).reshape(1, HC)
    full = lambda a: pl.BlockSpec(a.shape, lambda i: (0, 0))
    args = [band_l, w1p, vec(b1), vec(f1), w2p, vec(b2), vec(f2), w3p, deltas]
    return pl.pallas_call(
        functools.partial(_filter_kernel, seq_len=seq_len, tr=tr),
        grid=(2 * seq_len // tr,),
        in_specs=[full(a) for a in args],
        out_specs=pl.BlockSpec((2 * SLABS, J2, tr // N2, SUBLANES, LANES), lambda i: (0, 0, i, 0, 0)),
        out_shape=jax.ShapeDtypeStruct((2 * SLABS, J2, 2 * seq_len // N2, SUBLANES, LANES), F32),
        compiler_params=_cparams(1), name="hyena_filter")(*args)


def dft_tables(n1h):
    n1 = 2 * n1h
    n = n1 * N2
    k1 = jnp.arange(n1h, dtype=jnp.int32)
    m1 = (jnp.arange(n1, dtype=jnp.int32)[None, :] * (2 * k1[:, None] + 1)) % (2 * n1)
    a1 = m1.astype(F32) * (math.pi / n1)
    c1, s1 = jnp.cos(a1), jnp.sin(a1)
    w1_full = jnp.concatenate([c1, -s1], axis=0).astype(BF16)
    w1_half = w1_full[:, :n1h]
    w1_inv = jnp.concatenate([c1[:, :n1h].T, -s1[:, :n1h].T], axis=1).astype(BF16)
    k2 = jnp.arange(N2, dtype=jnp.int32)
    n2 = jnp.arange(N2, dtype=jnp.int32)
    f = 2 * (k2[None, :, None] * n1 + k1[:, None, None]) + 1
    m2 = (n2[None, None, :] * f) % (2 * n)
    th = m2.astype(F32) * (math.pi / n)
    c, s = jnp.cos(th), jnp.sin(th)
    mat = jnp.concatenate([jnp.concatenate([c, s], axis=2),
                           jnp.concatenate([-s, c], axis=2)], axis=1)
    return w1_half, w1_full, w1_inv, mat.astype(BF16), jnp.swapaxes(mat, 1, 2).astype(BF16)


def _rows_of(ref2, m, n_rows):
    return jnp.concatenate([ref2[cb, pl.ds(m, n_rows, stride=SUBLANES), :] for cb in range(SLABS)], axis=1)


def _store_rows(ref2, m, val):
    for cb in range(SLABS):
        ref2[cb, pl.ds(m, val.shape[0], stride=SUBLANES), :] = val[:, cb * LANES:(cb + 1) * LANES]


def _stage1_kernel(w_ref, x_ref, o_ref, *, k, r):
    w = w_ref[...]
    for m in range(SUBLANES):
        _store_rows(o_ref, m, jnp.dot(w, _rows_of(x_ref, m, k).astype(BF16), preferred_element_type=F32))


def dft_stage1(w, x5):
    b, s = x5.shape[:2]
    r, k = w.shape
    assert s % SLABS == 0 and x5.shape[3] == k * SUBLANES
    blk = lambda rows: pl.BlockSpec((None, SLABS, None, rows * SUBLANES, LANES), lambda bi, si, j: (bi, si, j, 0, 0))
    return pl.pallas_call(
        functools.partial(_stage1_kernel, k=k, r=r), grid=(b, s // SLABS, J2),
        in_specs=[pl.BlockSpec((r, k), lambda bi, si, j: (0, 0)), blk(k)],
        out_specs=blk(r),
        out_shape=jax.ShapeDtypeStruct((b, s, J2, r * SUBLANES, LANES), F32),
        compiler_params=_cparams(3), name="dft_stage1")(w, x5)


def _slab_cat(a_ref, kk, n_slabs):
    return jnp.concatenate([jnp.concatenate([a_ref[cb, :, 0, kk].reshape(N2, LANES),
                                             a_ref[cb, :, 1, kk].reshape(N2, LANES)], axis=0)
                            for cb in range(n_slabs)], axis=1).astype(BF16)


def _stage2_filter_kernel(m_ref, a_ref, o_ref):
    for kk in range(K1_BLOCK):
        o_ref[kk] = jnp.dot(m_ref[kk], _slab_cat(a_ref, kk, 2 * SLABS), preferred_element_type=F32)


def dft_stage2_filter(mat, a7):
    n1h = a7.shape[4]
    return pl.pallas_call(
        _stage2_filter_kernel, grid=(n1h // K1_BLOCK,),
        in_specs=[pl.BlockSpec((K1_BLOCK, 2 * N2, 2 * N2), lambda kb: (kb, 0, 0)),
                  pl.BlockSpec((None, 2 * SLABS, J2, 2, K1_BLOCK, SUBLANES, LANES), lambda kb: (0, 0, 0, 0, kb, 0, 0))],
        out_specs=pl.BlockSpec((K1_BLOCK, 2 * N2, 2 * HC), lambda kb: (kb, 0, 0)),
        out_shape=jax.ShapeDtypeStruct((n1h, 2 * N2, 2 * HC), F32),
        compiler_params=_cparams(1), name="dft_stage2_filter")(mat, a7)


def _stage2_conv_kernel(m_ref, mt_ref, h_ref, a_ref, o_ref):
    for kk in range(K1_BLOCK):
        z = jnp.dot(m_ref[kk], _slab_cat(a_ref, kk, SLABS), preferred_element_type=F32)
        zr, zi = z[:N2], z[N2:]
        hr, hi = h_ref[kk, :N2], h_ref[kk, N2:]
        y = jnp.concatenate([zr * hr - zi * hi, zr * hi + zi * hr], axis=0).astype(BF16)
        g = jnp.dot(mt_ref[kk], y, preferred_element_type=F32)
        for cb in range(SLABS):
            o_ref[cb, :, 0, kk] = g[:N2, cb * LANES:(cb + 1) * LANES].reshape(J2, SUBLANES, LANES)
            o_ref[cb, :, 1, kk] = g[N2:, cb * LANES:(cb + 1) * LANES].reshape(J2, SUBLANES, LANES)


def dft_stage2_conv(mat, mat_t, spec, order, a7):
    b, n1h = a7.shape[0], a7.shape[4]
    ablk = pl.BlockSpec((None, SLABS, J2, 2, K1_BLOCK, SUBLANES, LANES), lambda kb, bi: (bi, 0, 0, 0, kb, 0, 0))
    return pl.pallas_call(
        _stage2_conv_kernel, grid=(n1h // K1_BLOCK, b),
        in_specs=[pl.BlockSpec((K1_BLOCK, 2 * N2, 2 * N2), lambda kb, bi: (kb, 0, 0)),
                  pl.BlockSpec((K1_BLOCK, 2 * N2, 2 * N2), lambda kb, bi: (kb, 0, 0)),
                  pl.BlockSpec((K1_BLOCK, 2 * N2, HC), lambda kb, bi: (kb, 0, order)),
                  ablk],
        out_specs=ablk,
        out_shape=jax.ShapeDtypeStruct(a7.shape, F32),
        compiler_params=_cparams(2), name="dft_stage2_conv")(mat, mat_t, spec, a7)


def _stage1_inv_kernel(w_ref, g_ref, z_ref, gate_ref, d_ref, o_ref, *, n1h, scale):
    w, d = w_ref[...], d_ref[...]
    for m in range(SUBLANES):
        y = jnp.dot(w, _rows_of(g_ref, m, 2 * n1h).astype(BF16), preferred_element_type=F32) * scale
        _store_rows(o_ref, m, _rows_of(gate_ref, m, n1h) * (y + d * _rows_of(z_ref, m, n1h)))


def dft_stage1_inv(w_inv, g5, z5, gate5, d_vec, *, scale):
    b = g5.shape[0]
    n1h = w_inv.shape[0]
    blk = lambda rows: pl.BlockSpec((None, SLABS, None, rows * SUBLANES, LANES), lambda bi, j: (bi, 0, j, 0, 0))
    return pl.pallas_call(
        functools.partial(_stage1_inv_kernel, n1h=n1h, scale=scale),
        grid=(b, J2),
        in_specs=[pl.BlockSpec((n1h, 2 * n1h), lambda bi, j: (0, 0)), blk(2 * n1h), blk(n1h), blk(n1h),
                  pl.BlockSpec((1, HC), lambda bi, j: (0, 0))],
        out_specs=blk(n1h),
        out_shape=jax.ShapeDtypeStruct(z5.shape, F32),
        compiler_params=_cparams(2), name="dft_stage1_inv")(w_inv, g5, z5, gate5, d_vec)


def _silu(g):
    return g / (1.0 + jnp.exp(-g))


def _band_mask(t, halo, i, nblk):
    qi = lax.broadcasted_iota(jnp.int32, (t, t + 2 * halo), 0)
    kj = lax.broadcasted_iota(jnp.int32, (t, t + 2 * halo), 1)
    rel = kj - halo - qi
    return ((jnp.abs(rel) <= halo) & ((kj >= halo) | (i > 0)) & ((kj < halo + t) | (i < nblk - 1)))


def _head_pair_operands(win):
    x = win.astype(F32)
    swapped = pltpu.roll(x, B_HEAD_DIM, 1)
    low = lax.broadcasted_iota(jnp.int32, x.shape, 1) < B_HEAD_DIM
    place = lambda cond, val: jnp.where(cond, val, 0.0).astype(BF16)
    return ((place(low, x), place(~low, swapped)), (place(low, swapped), place(~low, x)))


def _band_attn_kernel(sink_ref, q_ref, kp_ref, kc_ref, kn_ref, vp_ref, vc_ref, vn_ref, gate_ref, o_ref, *, nsub):
    t, hd = B_BLOCK, B_HEAD_DIM
    kpads = _head_pair_operands(jnp.concatenate([kp_ref[...], kc_ref[...], kn_ref[...]], axis=0))
    vpads = _head_pair_operands(jnp.concatenate([vp_ref[...], vc_ref[...], vn_ref[...]], axis=0))
    n_blocks = pl.num_programs(1) * nsub
    heads = [(pair, par) for pair in range(B_HEADS // 2) for par in range(2)]
    for u in range(nsub):
        valid = _band_mask(t, t, pl.program_id(1) * nsub + u, n_blocks)
        rows = slice(u * t, (u + 1) * t)
        win = slice(u * t, (u + 3) * t)
        scores = [lax.dot_general(q_ref[rows, pair * LANES:(pair + 1) * LANES], kpads[(2 * pair) // B_GROUP][par][win],
                                  (((1,), (1,)), ((), ())), preferred_element_type=F32) for pair, par in heads]
        probs, denoms = [], []
        for (pair, par), s in zip(heads, scores):
            s = jnp.where(valid, s * (hd ** -0.5), NEG_INF)
            sink = sink_ref[2 * pair + par]
            m = jnp.maximum(jnp.max(s, axis=-1, keepdims=True), sink)
            p = jnp.exp(s - m)
            denoms.append(jnp.sum(p, axis=-1, keepdims=True) + jnp.exp(sink - m))
            probs.append(p.astype(BF16))
        outs = [jnp.dot(p, vpads[(2 * pair) // B_GROUP][par][win], preferred_element_type=F32) / d
                for (pair, par), p, d in zip(heads, probs, denoms)]
        for pair in range(B_HEADS // 2):
            lanes = slice(pair * LANES, (pair + 1) * LANES)
            o_ref[rows, lanes] = ((outs[2 * pair] + outs[2 * pair + 1]) * _silu(gate_ref[rows, lanes])).astype(BF16)


def band_attn(qk, v_att, at_gate, sink, batch, seq_len, *, nsub=4):
    t = B_BLOCK
    tq = t * nsub
    nblk = seq_len // t
    qk3 = qk.reshape(batch, seq_len, qk.shape[1])
    v3 = v_att.reshape(batch, seq_len, LANES)
    g3 = at_gate.reshape(batch, seq_len, at_gate.shape[1])
    k_col = B_HEADS * B_HEAD_DIM // LANES
    halo = lambda col, f: pl.BlockSpec((None, t, LANES), lambda b, i: (b, f(i), col))
    cur = lambda col: pl.BlockSpec((None, tq, LANES), lambda b, i: (b, i, col))
    prev = lambda i: jnp.maximum(i * nsub - 1, 0)
    nxt = lambda i: jnp.minimum((i + 1) * nsub, nblk - 1)
    wide = pl.BlockSpec((None, tq, B_HEADS * B_HEAD_DIM), lambda b, i: (b, i, 0))
    out = pl.pallas_call(
        functools.partial(_band_attn_kernel, nsub=nsub), grid=(batch, seq_len // tq),
        in_specs=[pl.BlockSpec(memory_space=pltpu.SMEM), wide,
                  halo(k_col, prev), cur(k_col), halo(k_col, nxt), halo(0, prev), cur(0), halo(0, nxt), wide],
        out_specs=wide,
        out_shape=jax.ShapeDtypeStruct((batch, seq_len, B_HEADS * B_HEAD_DIM), BF16),
        compiler_params=_cparams(2), name="band_attn")(sink, qk3, qk3, qk3, qk3, v3, v3, v3, g3)
    return out.reshape(batch * seq_len, -1)


def _dil_attn_kernel(q_ref, kp_ref, kc_ref, kn_ref, vp_ref, vc_ref, vn_ref, o_ref, lse_ref, *, nsub):
    t, hd, halo = 2 * C_RADIUS, C_HEAD_DIM, C_RADIUS
    n_blocks = pl.num_programs(1) * nsub
    lane = lax.broadcasted_iota(jnp.int32, (t, LANES), 1)
    for u in range(nsub):
        valid = _band_mask(t, halo, pl.program_id(1) * nsub + u, n_blocks)
        rows = slice(u * t, (u + 1) * t)
        win = slice(u * t, (u + 1) * t + 2 * halo)
        cols = [slice(h * hd, (h + 1) * hd) for h in range(C_HEADS)]
        window = lambda p_ref, c_ref, n_ref, sl: jnp.concatenate([p_ref[:, sl], c_ref[:, sl], n_ref[:, sl]], axis=0)[win]
        scores = [lax.dot_general(q_ref[rows, sl], window(kp_ref, kc_ref, kn_ref, sl), (((1,), (1,)), ((), ())),
                                  preferred_element_type=F32) for sl in cols]
        probs, denoms = [], []
        lse_tile = jnp.zeros((t, LANES), F32)
        for h, s in enumerate(scores):
            s = jnp.where(valid, s * (hd ** -0.5), NEG_INF)
            m = jnp.max(s, axis=-1, keepdims=True)
            p = jnp.exp(s - m)
            denom = jnp.sum(p, axis=-1, keepdims=True)
            lse_tile = jnp.where(lane == h, m + jnp.log(denom), lse_tile)
            probs.append(p.astype(BF16))
            denoms.append(denom)
        for sl, p, denom in zip(cols, probs, denoms):
            o = jnp.dot(p, window(vp_ref, vc_ref, vn_ref, sl), preferred_element_type=F32) / denom
            o_ref[rows, sl] = o.astype(BF16)
        lse_ref[rows, :] = lse_tile


def dil_attn(qk, v, n_seq, ls):
    w = C_HEADS * C_HEAD_DIM
    t = 2 * C_RADIUS
    nsub = next(n for n in (4, 2, 1) if ls % (n * t) == 0)
    tq = t * nsub
    n_halo = ls // C_RADIUS
    qk3 = qk.reshape(n_seq, ls, qk.shape[1])
    v3 = v.reshape(n_seq, ls, w)
    prev = lambda i: jnp.maximum(i * (tq // C_RADIUS) - 1, 0)
    nxt = lambda i: jnp.minimum((i + 1) * (tq // C_RADIUS), n_halo - 1)
    big = lambda col: pl.BlockSpec((None, tq, w), lambda b, i: (b, i, col))
    halo = lambda col, f: pl.BlockSpec((None, C_RADIUS, w), lambda b, i: (b, f(i), col))
    o, lse = pl.pallas_call(
        functools.partial(_dil_attn_kernel, nsub=nsub), grid=(n_seq, ls // tq),
        in_specs=[big(0), halo(1, prev), big(1), halo(1, nxt), halo(0, prev), big(0), halo(0, nxt)],
        out_specs=(big(0), pl.BlockSpec((None, tq, LANES), lambda b, i: (b, i, 0))),
        out_shape=(jax.ShapeDtypeStruct((n_seq, ls, w), BF16),
                   jax.ShapeDtypeStruct((n_seq, ls, LANES), F32)),
        compiler_params=_cparams(2), name="dil_attn")(qk3, qk3, qk3, qk3, v3, v3, v3)
    return o.reshape(n_seq * ls, w), lse.reshape(n_seq * ls, LANES)


def _out_proj_even_kernel(hy_ref, at_ref, x_ref, w_ref, o_ref, *, nb):
    mixed = jnp.concatenate([_load_time_tiles(hy_ref, nb).astype(BF16), at_ref[...]], axis=1)
    o_ref[...] = x_ref[...] + jnp.dot(mixed, w_ref[...], preferred_element_type=F32)


def out_proj_even(hy, at, x2d, w_bf, *, tm=512):
    m, d = x2d.shape
    batch, n1h = hy.shape[0], hy.shape[3]
    tps = n1h * N2 // tm
    nb = tm // N2
    half = pl.BlockSpec((tm, HC), lambda b, i: (b * tps + i, 0))
    full = pl.BlockSpec((tm, d), lambda b, i: (b * tps + i, 0))
    return pl.pallas_call(
        functools.partial(_out_proj_even_kernel, nb=nb), grid=(batch, tps),
        in_specs=[pl.BlockSpec((None, SLABS, J2, nb, SUBLANES, LANES), lambda b, i: (b, 0, 0, i, 0, 0)), half, full,
                  pl.BlockSpec((d, d), lambda b, i: (0, 0))],
        out_specs=full, out_shape=jax.ShapeDtypeStruct((m, d), F32),
        compiler_params=_cparams(2), name="out_proj_even")(hy, at, x2d, w_bf)


def _merge_out_kernel(*refs, final):
    o_refs, l_refs = refs[0:3], refs[3:6]
    gate_ref, x_ref, w_ref = refs[6:9]
    out_ref = refs[-1]
    ls = [r[...] for r in l_refs]
    mx = jnp.maximum(jnp.maximum(ls[0], ls[1]), ls[2])
    es = [jnp.exp(l - mx) for l in ls]
    den = es[0] + es[1] + es[2]
    alphas = [e / den for e in es]
    hd = C_HEAD_DIM
    parts = []
    for h in range(C_HEADS):
        sl = slice(h * hd, (h + 1) * hd)
        parts.append(sum(alphas[g][:, h:h + 1] * o_refs[g][:, sl].astype(F32) for g in range(3)))
    y = (jnp.concatenate(parts, axis=1) * _silu(gate_ref[...])).astype(BF16)
    r = x_ref[...] + jnp.dot(y, w_ref[...], preferred_element_type=F32)
    if final:
        ms = jnp.mean(r * r, axis=-1, keepdims=True)
        r = r * lax.rsqrt(ms + NORM_EPS) * refs[9][...]
    out_ref[...] = r


def merge_out(outs, lses, pp, x2d, w_bf, final_g=None, *, tm=512):
    m, d = x2d.shape
    full = pl.BlockSpec((tm, d), lambda i: (i, 0))
    lspec = pl.BlockSpec((tm, LANES), lambda i: (i, 0))
    gate_col = pp.shape[1] // d - 1
    in_specs = [full] * 3 + [lspec] * 3 + [pl.BlockSpec((tm, d), lambda i: (i, gate_col)), full,
                                           pl.BlockSpec((d, d), lambda i: (0, 0))]
    args = list(outs) + list(lses) + [pp, x2d, w_bf]
    if final_g is not None:
        in_specs.append(pl.BlockSpec((1, d), lambda i: (0, 0)))
        args.append(final_g.reshape(1, d))
    return pl.pallas_call(
        functools.partial(_merge_out_kernel, final=final_g is not None), grid=(m // tm,),
        in_specs=in_specs, out_specs=full, out_shape=jax.ShapeDtypeStruct((m, d), F32),
        compiler_params=_cparams(1), name="merge_out")(*args)


def _even_w_in(w_in):
    return jnp.concatenate([w_in[:, 2048:2688], w_in[:, :2048], w_in[:, 2816:], w_in[:, 2688:2816]], axis=1).astype(BF16)


def _odd_w_in(w_in):
    w = C_HEADS * C_HEAD_DIM
    blk = lambda g, t: w_in[:, (3 * g + t) * w:(3 * g + t + 1) * w]
    per = [[blk(g, 0), blk(g, 1), blk(g, 2)] for g in range(len(C_PATTERNS))]
    per[0].append(w_in[:, 9 * w:])
    return [jnp.concatenate(cols, axis=1).astype(BF16) for cols in per]


def hyena_spectrum(seq_len, tabs, fw1, fb1, ff1, fw2, fb2, ff2, fw3):
    n1h = seq_len // N2
    _, w1_full, _, mat, _ = tabs
    filt = hyena_filter(seq_len, fw1, fb1, ff1, fw2, fb2, ff2, fw3)
    a = dft_stage1(w1_full, filt.reshape(1, 2 * SLABS, J2, 2 * n1h * SUBLANES, LANES))
    return dft_stage2_filter(mat, a.reshape(1, 2 * SLABS, J2, 2, n1h, SUBLANES, LANES))


def hyena_conv(z, gate, d_vec, spec, order, tabs):
    batch, n1h = z.shape[0], z.shape[3]
    w1_half, _, w1_inv, mat, mat_t = tabs
    rows = lambda a, r: a.reshape(batch, SLABS, J2, r * SUBLANES, LANES)
    z5 = rows(z, n1h)
    a = dft_stage1(w1_half, z5)
    g = dft_stage2_conv(mat, mat_t, spec, order, a.reshape(batch, SLABS, J2, 2, n1h, SUBLANES, LANES))
    out = dft_stage1_inv(w1_inv, rows(g, 2 * n1h), z5, rows(gate, n1h), d_vec.reshape(1, HC), scale=1.0 / (n1h * N2))
    return out.reshape(z.shape)


def hybrid_layer(x2d, batch, seq_len, norm_g, w_in_bf, conv_w, conv_b, hyena_d, sink, w_out_bf, spec, tabs, rope):
    rope_t, half = rope
    qk, x1, x2s, v, at_gate, v_att = hybrid_proj(x2d, norm_g, w_in_bf, conv_w, conv_b, batch, seq_len, rope_t, half)
    z = hyena_conv(v, x1, hyena_d[0], spec, 0, tabs)
    z = hyena_conv(z, x2s, hyena_d[1], spec, 1, tabs)
    at = band_attn(qk, v_att, at_gate, sink, batch, seq_len)
    return out_proj_even(z, at, x2d, w_out_bf)


def _to_strided(a2d, batch, seq_len, dilation):
    if dilation == 1:
        return a2d
    w = a2d.shape[1]
    return a2d.reshape(batch, seq_len // dilation, dilation, w).swapaxes(1, 2).reshape(batch * seq_len, w)


def _from_strided(a2d, batch, seq_len, dilation):
    if dilation == 1:
        return a2d
    w = a2d.shape[1]
    return a2d.reshape(batch, dilation, seq_len // dilation, w).swapaxes(1, 2).reshape(batch * seq_len, w)


def dilated_layer(x2d, batch, seq_len, norm_g, w_in_bfs, w_out_bf, ropes, final_g):
    w = C_HEADS * C_HEAD_DIM
    outs, lses, gate = [], [], None
    for gi, (_, dilation) in enumerate(C_PATTERNS):
        rope_t, half = ropes[dilation]
        xs = _to_strided(x2d, batch, seq_len, dilation)
        groups = ((2 * w, BF16), (w, BF16)) + (((w, F32),) if gi == 0 else ())
        res = norm_proj(xs, norm_g, w_in_bfs[gi], seq_len, rope_t, half, groups)
        if gi == 0:
            gate = res[2]
        o, lse = dil_attn(res[0], res[1], batch * dilation, seq_len // dilation)
        outs.append(_from_strided(o, batch, seq_len, dilation))
        lses.append(_from_strided(lse, batch, seq_len, dilation))
    return merge_out(outs, lses, gate, x2d, w_out_bf, final_g)


def kernel(x_prompt, x_sample, a_norm, a_w_in, a_conv_w, a_conv_b, a_filt_w1, a_filt_b1, a_filt_f1, a_filt_w2, a_filt_b2, a_filt_f2, a_filt_w3, a_hyena_d, a_sink, a_w_out, c_norm, c_w_in, c_w_out, final_norm):
    depth = a_norm.shape[0] + c_norm.shape[0]
    xs = [x_prompt, x_sample]
    shapes = [(x.shape[0], x.shape[1]) for x in xs]
    acts = [x.reshape(-1, D_MODEL) for x in xs]
    seq_lens = sorted({s[1] for s in shapes})
    tabs = {sl: dft_tables(sl // N2) for sl in seq_lens}
    rope_even = {sl: rope_tables(sl, B_HEAD_DIM) for sl in seq_lens}
    rope_odd = {}
    for sl in seq_lens:
        tables, half = rope_tables(sl, C_HEAD_DIM)
        rope_odd[sl] = {d: (tuple(_to_strided(t, 1, sl, d) for t in tables), half) for _, d in C_PATTERNS}
    for layer in range(depth):
        i = layer // 2
        if layer % 2 == 0:
            w_in_bf = _even_w_in(a_w_in[i])
            w_out_bf = a_w_out[i].astype(BF16)
            specs = {sl: hyena_spectrum(sl, tabs[sl], a_filt_w1[i], a_filt_b1[i], a_filt_f1[i], a_filt_w2[i],
                                        a_filt_b2[i], a_filt_f2[i], a_filt_w3[i]) for sl in seq_lens}
            acts = [hybrid_layer(x2d, b, sl, a_norm[i], w_in_bf, a_conv_w[i], a_conv_b[i], a_hyena_d[i], a_sink[i],
                                 w_out_bf, specs[sl], tabs[sl], rope_even[sl])
                    for x2d, (b, sl) in zip(acts, shapes)]
        else:
            w_in_bfs = _odd_w_in(c_w_in[i])
            w_out_bf = c_w_out[i].astype(BF16)
            final_g = final_norm if layer == depth - 1 else None
            acts = [dilated_layer(x2d, b, sl, c_norm[i], w_in_bfs, w_out_bf, rope_odd[sl], final_g)
                    for x2d, (b, sl) in zip(acts, shapes)]
    assert depth % 2 == 0
    return tuple(a.reshape(b, sl, D_MODEL) for a, (b, sl) in zip(acts, shapes))
```

```python
import functools
import math

import numpy as np
import jax
import jax.numpy as jnp
from jax import lax
from jax.experimental import pallas as pl
from jax.experimental.pallas import tpu as pltpu

F32 = jnp.float32
BF16 = jnp.bfloat16

D_MODEL = 1024
HC = 512
FILTER_BANDS = 16
DECAY_TARGET = 1e-2
FAST_DECAY_PCT = 0.3
SLOW_DECAY_PCT = 1.5
DECAY_SHIFT = 0.05
B_HEAD_DIM = 64
B_HEADS = 8
B_KV_HEADS = 2
B_GROUP = 4
B_BLOCK = 128
C_PATTERNS = ((128, 1), (512, 4), (2048, 16))
C_HEADS = 8
C_HEAD_DIM = 128
C_RADIUS = 64
ROPE_THETA = 500000.0
NORM_EPS = 1e-6
NEG_INF = -1e30

LANES = 128
SUBLANES = 8
N2 = 128
J2 = N2 // SUBLANES
SLABS = HC // LANES
K1_BLOCK = 8
VMEM_LIMIT = 56 * 1024 * 1024


def _cparams(n_axes):
    return pltpu.CompilerParams(dimension_semantics=("arbitrary",) * n_axes,
                                vmem_limit_bytes=VMEM_LIMIT)


def _rope_chunk(blk, cos, sa, sb, half):
    return (blk * cos + pltpu.roll(blk, LANES - half, 1) * sa + pltpu.roll(blk, half, 1) * sb)


def _norm_proj_kernel(x_ref, g_ref, w_ref, cos_ref, sa_ref, sb_ref, *outs, half, groups, row_chunk):
    g = g_ref[...]
    for r0 in range(0, x_ref.shape[0], row_chunk):
        rows = slice(r0, r0 + row_chunk)
        acc = jnp.dot(_rms_bf16(x_ref[rows, :], g), w_ref[...], preferred_element_type=F32)
        col = 0
        for gi, (o_ref, (width, dtype)) in enumerate(zip(outs, groups)):
            if gi == 0:
                cos, sa, sb = cos_ref[rows, :], sa_ref[rows, :], sb_ref[rows, :]
                for ci in range(width // LANES):
                    src = slice(col + ci * LANES, col + (ci + 1) * LANES)
                    o_ref[rows, ci * LANES:(ci + 1) * LANES] = _rope_chunk(acc[:, src], cos, sa, sb, half).astype(dtype)
            else:
                o_ref[rows, :] = acc[:, col:col + width].astype(dtype)
            col += width


def norm_proj(x2d, g, w_bf, seq_len, rope, half, groups, *, tm=512, row_chunk=256):
    m, d = x2d.shape
    n = w_bf.shape[1]
    assert m % tm == 0 and seq_len % tm == 0 and sum(wd for wd, _ in groups) == n
    tiles_per_seq = seq_len // tm
    in_specs = [pl.BlockSpec((tm, d), lambda i: (i, 0)),
                pl.BlockSpec((1, d), lambda i: (0, 0)),
                pl.BlockSpec((d, n), lambda i: (0, 0))]
    in_specs += [pl.BlockSpec((tm, LANES), lambda i: (i % tiles_per_seq, 0))] * 3
    return pl.pallas_call(
        functools.partial(_norm_proj_kernel, half=half, groups=tuple(groups), row_chunk=row_chunk),
        grid=(m // tm,), in_specs=in_specs,
        out_specs=tuple(pl.BlockSpec((tm, wd), lambda i: (i, 0)) for wd, _ in groups),
        out_shape=tuple(jax.ShapeDtypeStruct((m, wd), dt) for wd, dt in groups),
        compiler_params=_cparams(1), name="norm_proj")(x2d, g.reshape(1, d), w_bf, *rope)


def _by_residue(ref, d):
    per = ref.shape[0] // d
    return jnp.concatenate([ref[pl.ds(r, per, stride=d), :] for r in range(d)], axis=0)


def _norm_proj_strided_kernel(*refs, half, groups, d, row_chunk):
    n_slab = D_MODEL // LANES
    x_slabs = refs[:n_slab]
    g_ref, w_ref, cos_ref, sa_ref, sb_ref = refs[n_slab:n_slab + 5]
    outs = refs[n_slab + 5:]
    tm = x_slabs[0].shape[0]
    per = tm // d
    x = jnp.concatenate([_by_residue(xs, d) for xs in x_slabs], axis=1)
    cos_t, sa_t, sb_t = _by_residue(cos_ref, d), _by_residue(sa_ref, d), _by_residue(sb_ref, d)
    g = g_ref[...]
    for r0 in range(0, tm, row_chunk):
        rows = slice(r0, r0 + row_chunk)
        acc = jnp.dot(_rms_bf16(x[rows], g), w_ref[...], preferred_element_type=F32)
        col = 0
        for gi, (o_ref, (width, dtype)) in enumerate(zip(outs, groups)):
            if gi == 0:
                res = jnp.concatenate([_rope_chunk(acc[:, col + ci * LANES:col + (ci + 1) * LANES],
                                                   cos_t[rows], sa_t[rows], sb_t[rows], half)
                                       for ci in range(width // LANES)], axis=1).astype(dtype)
            else:
                res = acc[:, col:col + width].astype(dtype)
            for k in range(row_chunk // per):
                o_ref[r0 // per + k] = res[k * per:(k + 1) * per]
            col += width


def norm_proj_strided(x2d, g, w_bf, batch, seq_len, d, rope, half, groups, *, tm=512, row_chunk=256):
    m, dm = x2d.shape
    n = w_bf.shape[1]
    tps = seq_len // tm
    per = tm // d
    assert seq_len % tm == 0 and row_chunk % per == 0 and sum(wd for wd, _ in groups) == n
    in_specs = [pl.BlockSpec((tm, LANES), functools.partial(lambda c, t: (t, c), c)) for c in range(dm // LANES)]
    in_specs += [pl.BlockSpec((1, dm), lambda t: (0, 0)), pl.BlockSpec((dm, n), lambda t: (0, 0))]
    in_specs += [pl.BlockSpec((tm, LANES), lambda t: (t % tps, 0))] * 3
    return pl.pallas_call(
        functools.partial(_norm_proj_strided_kernel, half=half, groups=tuple(groups), d=d, row_chunk=row_chunk),
        grid=(m // tm,), in_specs=in_specs,
        out_specs=tuple(pl.BlockSpec((None, d, per, wd), lambda t: (t // tps, 0, t % tps, 0)) for wd, _ in groups),
        out_shape=tuple(jax.ShapeDtypeStruct((batch, d, seq_len // d, wd), dt) for wd, dt in groups),
        compiler_params=_cparams(1), name="norm_proj_strided")(*([x2d] * (dm // LANES)), g.reshape(1, dm), w_bf, *rope)


def _rms_bf16(x, g):
    ms = jnp.mean(x * x, axis=-1, keepdims=True)
    return (x * lax.rsqrt(ms + NORM_EPS) * g).astype(BF16)


_QK_W, _HY_W, _ATT_V_W = 640, 3 * HC, 128
_HY0 = _QK_W
_HYG0 = _HY0 + _HY_W
_ATG0 = _HYG0 + HC
_VATT0 = _ATG0 + HC


def _hybrid_proj_kernel(x_ref, xp_ref, xn_ref, g_ref, w_ref, cos_ref, sa_ref, sb_ref, cw_ref, cb_ref,
                        qk_ref, x1_ref, x2_ref, v_ref, atg_ref, vatt_ref, u_scr, *, half, tiles_per_seq, row_chunk):
    tm = x_ref.shape[0]
    i = pl.program_id(0) % tiles_per_seq
    g = g_ref[...]
    halo = jnp.concatenate([xp_ref[...], xn_ref[...]], axis=0)
    acc_h = jnp.dot(_rms_bf16(halo, g), w_ref[:, _HY0:_HYG0], preferred_element_type=F32)
    prev_row = jnp.where(i > 0, acc_h[SUBLANES - 1:SUBLANES], 0.0)
    next_row = jnp.where(i < tiles_per_seq - 1, acc_h[SUBLANES:SUBLANES + 1], 0.0)
    for r0 in range(0, tm, row_chunk):
        rows = slice(r0, r0 + row_chunk)
        acc = jnp.dot(_rms_bf16(x_ref[rows, :], g), w_ref[...], preferred_element_type=F32)
        cos, sa, sb = cos_ref[rows, :], sa_ref[rows, :], sb_ref[rows, :]
        for ci in range(_QK_W // LANES):
            sl = slice(ci * LANES, (ci + 1) * LANES)
            qk_ref[rows, sl] = _rope_chunk(acc[:, sl], cos, sa, sb, half).astype(BF16)
        u_scr[rows, :] = acc[:, _HY0:_ATG0]
        atg_ref[rows, :] = acc[:, _ATG0:_VATT0]
        vatt_ref[rows, :] = acc[:, _VATT0:].astype(BF16)
    u = u_scr[:, :_HY_W]
    row = lax.broadcasted_iota(jnp.int32, u.shape, 0)
    um1 = jnp.where(row == 0, prev_row, pltpu.roll(u, 1, 0))
    up1 = jnp.where(row == tm - 1, next_row, pltpu.roll(u, tm - 1, 0))
    w = cw_ref[...]
    y = um1 * w[0:1, :] + u * w[1:2, :] + up1 * w[2:3, :] + cb_ref[...]
    _store_time_tiles(x1_ref, y[:, :HC])
    _store_time_tiles(x2_ref, y[:, HC:2 * HC] * _silu(u_scr[:, _HY_W:]))
    _store_time_tiles(v_ref, y[:, 2 * HC:])


def hybrid_proj(x2d, g, w_bf, conv_w, conv_b, batch, seq_len, rope, half, *, tm=512, row_chunk=256):
    m, d = x2d.shape
    n = w_bf.shape[1]
    tiles_per_seq = seq_len // tm
    sub = tm // SUBLANES
    last8 = m // SUBLANES - 1
    hy_shape = jax.ShapeDtypeStruct((batch, SLABS, J2, seq_len // N2, SUBLANES, LANES), F32)
    hy_spec = pl.BlockSpec((None, SLABS, J2, tm // N2, SUBLANES, LANES),
                           lambda t: (t // tiles_per_seq, 0, 0, t % tiles_per_seq, 0, 0))
    row_blk = lambda wd: pl.BlockSpec((tm, wd), lambda t: (t, 0))
    const = lambda shape: pl.BlockSpec(shape, lambda t: (0, 0))
    return pl.pallas_call(
        functools.partial(_hybrid_proj_kernel, half=half, tiles_per_seq=tiles_per_seq, row_chunk=row_chunk),
        grid=(m // tm,),
        in_specs=[row_blk(d),
                  pl.BlockSpec((SUBLANES, d), lambda t: (jnp.maximum(t * sub - 1, 0), 0)),
                  pl.BlockSpec((SUBLANES, d), lambda t: (jnp.minimum((t + 1) * sub, last8), 0)),
                  const((1, d)), const((d, n))]
                 + [pl.BlockSpec((tm, LANES), lambda t: (t % tiles_per_seq, 0))] * 3
                 + [const((3, _HY_W)), const((1, _HY_W))],
        out_specs=(row_blk(_QK_W), hy_spec, hy_spec, hy_spec, row_blk(HC), row_blk(_ATT_V_W)),
        out_shape=(jax.ShapeDtypeStruct((m, _QK_W), BF16), hy_shape, hy_shape, hy_shape,
                   jax.ShapeDtypeStruct((m, HC), F32), jax.ShapeDtypeStruct((m, _ATT_V_W), BF16)),
        scratch_shapes=[pltpu.VMEM((tm, _HY_W + HC), F32)],
        compiler_params=_cparams(1), name="hybrid_proj")(x2d, x2d, x2d, g.reshape(1, d), w_bf, *rope,
                                                         conv_w, conv_b.reshape(1, -1))


def rope_tables(seq_len, head_dim):
    rot = head_dim // 4
    half = rot // 2
    inv = jnp.power(ROPE_THETA, -2.0 * jnp.arange(half, dtype=F32) / rot)
    ang = jnp.arange(seq_len).astype(F32)[:, None] * inv[None, :]
    cos, sin = jnp.cos(ang), jnp.sin(ang)
    lane = np.arange(LANES) % head_dim
    idx = lane % half
    in_rot = jnp.asarray(lane < rot)[None, :]
    first = jnp.asarray(lane < half)[None, :]
    cos_t = jnp.where(in_rot, cos[:, idx], 1.0)
    sa = jnp.where(first, -sin[:, idx], 0.0)
    sb = jnp.where(in_rot & ~first, sin[:, idx], 0.0)
    return (cos_t, sa, sb), half


def _store_time_tiles(ref, val):
    for s in range(SLABS):
        for i in range(val.shape[0] // N2):
            ref[s, :, i] = val[i * N2:(i + 1) * N2, s * LANES:(s + 1) * LANES].reshape(J2, SUBLANES, LANES)


def _load_time_tiles(ref, nb):
    return jnp.concatenate([jnp.concatenate([ref[s, :, i].reshape(N2, LANES) for s in range(SLABS)], axis=1)
                            for i in range(nb)], axis=0)


def _filter_kernel(band_ref, w1_ref, b1_ref, f1_ref, w2_ref, b2_ref, f2_ref, w3_ref, dl_ref, o_ref,
                   *, seq_len, tr):
    n = pl.program_id(0) * tr + lax.broadcasted_iota(jnp.int32, (tr, 1), 0)
    j = jnp.where(n < seq_len, n, 2 * seq_len - n).astype(F32)
    t = j * (1.0 / (seq_len - 1))
    wpos = (2.0 * math.pi) * j / seq_len
    lane = lax.broadcasted_iota(jnp.int32, (tr, LANES), 1)
    arg = band_ref[...] * wpos
    feats = jnp.where(lane == 0, t,
                      jnp.where(lane <= FILTER_BANDS, jnp.cos(arg),
                                jnp.where(lane <= 2 * FILTER_BANDS, -jnp.sin(arg), 0.0)))
    h = jnp.dot(feats.astype(BF16), w1_ref[...], preferred_element_type=F32) + b1_ref[...]
    h = jnp.sin(f1_ref[...] * h)
    h = jnp.dot(h.astype(BF16), w2_ref[...], preferred_element_type=F32) + b2_ref[...]
    h = jnp.sin(f2_ref[...] * h)
    h = jnp.dot(h.astype(BF16), w3_ref[...], preferred_element_type=F32)
    win = jnp.exp(-t * dl_ref[...]) + DECAY_SHIFT
    win2 = jnp.concatenate([win, win], axis=1)
    fwd = h[:, :2 * HC] * win2
    bwd = h[:, 2 * HC:] * win2
    res = (jnp.where(n < seq_len, fwd, 0.0)
           + jnp.where(n == 0, bwd, 0.0) - jnp.where(n > seq_len, bwd, 0.0))
    for order in range(2):
        _store_time_tiles(o_ref.at[order * SLABS:(order + 1) * SLABS], res[:, order * HC:(order + 1) * HC])


def hyena_filter(seq_len, w1, b1, f1, w2, b2, f2, w3, *, tr=512):
    hid = w1.shape[1]
    pad = LANES - hid
    w1p = jnp.pad(w1, ((0, LANES - w1.shape[0]), (0, pad))).astype(BF16)
    w2p = jnp.pad(w2, ((0, pad), (0, pad))).astype(BF16)
    w3r = w3.reshape(hid, 2, 2, HC).transpose(0, 2, 1, 3).reshape(hid, 4 * HC)
    w3p = jnp.pad(w3r, ((0, pad), (0, 0))).astype(BF16)
    vec = lambda v: jnp.pad(v, (0, pad)).reshape(1, LANES)
    bands = jnp.linspace(1e-4, FILTER_BANDS - 1, FILTER_BANDS, dtype=F32)
    band_l = jnp.concatenate([jnp.zeros((1,), F32), bands, bands,
                              jnp.zeros((LANES - 1 - 2 * FILTER_BANDS,), F32)]).reshape(1, LANES)
    max_decay = math.log(DECAY_TARGET) / FAST_DECAY_PCT
    min_decay = math.log(DECAY_TARGET) / SLOW_DECAY_PCT
    deltas = jnp.abs(jnp.linspace(min_decay, max_decay, HC, dtype=F32)).reshape(1, HC)
    full = lambda a: pl.BlockSpec(a.shape, lambda i: (0, 0))
    args = [band_l, w1p, vec(b1), vec(f1), w2p, vec(b2), vec(f2), w3p, deltas]
    return pl.pallas_call(
        functools.partial(_filter_kernel, seq_len=seq_len, tr=tr),
        grid=(2 * seq_len // tr,),
        in_specs=[full(a) for a in args],
        out_specs=pl.BlockSpec((2 * SLABS, J2, tr // N2, SUBLANES, LANES), lambda i: (0, 0, i, 0, 0)),
        out_shape=jax.ShapeDtypeStruct((2 * SLABS, J2, 2 * seq_len // N2, SUBLANES, LANES), F32),
        compiler_params=_cparams(1), name="hyena_filter")(*args)


def dft_tables(n1h):
    n1 = 2 * n1h
    n = n1 * N2
    k1 = jnp.arange(n1h, dtype=jnp.int32)
    m1 = (jnp.arange(n1, dtype=jnp.int32)[None, :] * (2 * k1[:, None] + 1)) % (2 * n1)
    a1 = m1.astype(F32) * (math.pi / n1)
    c1, s1 = jnp.cos(a1), jnp.sin(a1)
    w1_full = jnp.concatenate([c1, -s1], axis=0).astype(BF16)
    w1_half = w1_full[:, :n1h]
    w1_inv = jnp.concatenate([c1[:, :n1h].T, -s1[:, :n1h].T], axis=1).astype(BF16)
    k2 = jnp.arange(N2, dtype=jnp.int32)
    n2 = jnp.arange(N2, dtype=jnp.int32)
    f = 2 * (k2[None, :, None] * n1 + k1[:, None, None]) + 1
    m2 = (n2[None, None, :] * f) % (2 * n)
    th = m2.astype(F32) * (math.pi / n)
    c, s = jnp.cos(th), jnp.sin(th)
    mat = jnp.concatenate([jnp.concatenate([c, s], axis=2),
                           jnp.concatenate([-s, c], axis=2)], axis=1)
    return w1_half, w1_full, w1_inv, mat.astype(BF16), jnp.swapaxes(mat, 1, 2).astype(BF16)


def _rows_of(ref2, m, n_rows):
    return jnp.concatenate([ref2[cb, pl.ds(m, n_rows, stride=SUBLANES), :] for cb in range(SLABS)], axis=1)


def _store_rows(ref2, m, val):
    for cb in range(SLABS):
        ref2[cb, pl.ds(m, val.shape[0], stride=SUBLANES), :] = val[:, cb * LANES:(cb + 1) * LANES]


def _stage1_kernel(w_ref, x_ref, o_ref, *, k, r):
    w = w_ref[...]
    for m in range(SUBLANES):
        _store_rows(o_ref, m, jnp.dot(w, _rows_of(x_ref, m, k).astype(BF16), preferred_element_type=F32))


def dft_stage1(w, x5):
    b, s = x5.shape[:2]
    r, k = w.shape
    assert s % SLABS == 0 and x5.shape[3] == k * SUBLANES
    blk = lambda rows: pl.BlockSpec((None, SLABS, None, rows * SUBLANES, LANES), lambda bi, si, j: (bi, si, j, 0, 0))
    return pl.pallas_call(
        functools.partial(_stage1_kernel, k=k, r=r), grid=(b, s // SLABS, J2),
        in_specs=[pl.BlockSpec((r, k), lambda bi, si, j: (0, 0)), blk(k)],
        out_specs=blk(r),
        out_shape=jax.ShapeDtypeStruct((b, s, J2, r * SUBLANES, LANES), F32),
        compiler_params=_cparams(3), name="dft_stage1")(w, x5)


def _slab_cat(a_ref, kk, n_slabs):
    return jnp.concatenate([jnp.concatenate([a_ref[cb, :, 0, kk].reshape(N2, LANES),
                                             a_ref[cb, :, 1, kk].reshape(N2, LANES)], axis=0)
                            for cb in range(n_slabs)], axis=1).astype(BF16)


def _stage2_filter_kernel(m_ref, a_ref, o_ref):
    for kk in range(K1_BLOCK):
        o_ref[kk] = jnp.dot(m_ref[kk], _slab_cat(a_ref, kk, 2 * SLABS), preferred_element_type=F32)


def dft_stage2_filter(mat, a7):
    n1h = a7.shape[4]
    return pl.pallas_call(
        _stage2_filter_kernel, grid=(n1h // K1_BLOCK,),
        in_specs=[pl.BlockSpec((K1_BLOCK, 2 * N2, 2 * N2), lambda kb: (kb, 0, 0)),
                  pl.BlockSpec((None, 2 * SLABS, J2, 2, K1_BLOCK, SUBLANES, LANES), lambda kb: (0, 0, 0, 0, kb, 0, 0))],
        out_specs=pl.BlockSpec((K1_BLOCK, 2 * N2, 2 * HC), lambda kb: (kb, 0, 0)),
        out_shape=jax.ShapeDtypeStruct((n1h, 2 * N2, 2 * HC), F32),
        compiler_params=_cparams(1), name="dft_stage2_filter")(mat, a7)


def _stage2_conv_kernel(m_ref, mt_ref, h_ref, a_ref, o_ref):
    for kk in range(K1_BLOCK):
        z = jnp.dot(m_ref[kk], _slab_cat(a_ref, kk, SLABS), preferred_element_type=F32)
        zr, zi = z[:N2], z[N2:]
        hr, hi = h_ref[kk, :N2], h_ref[kk, N2:]
        y = jnp.concatenate([zr * hr - zi * hi, zr * hi + zi * hr], axis=0).astype(BF16)
        g = jnp.dot(mt_ref[kk], y, preferred_element_type=F32)
        for cb in range(SLABS):
            o_ref[cb, :, 0, kk] = g[:N2, cb * LANES:(cb + 1) * LANES].reshape(J2, SUBLANES, LANES)
            o_ref[cb, :, 1, kk] = g[N2:, cb * LANES:(cb + 1) * LANES].reshape(J2, SUBLANES, LANES)


def dft_stage2_conv(mat, mat_t, spec, order, a7):
    b, n1h = a7.shape[0], a7.shape[4]
    ablk = pl.BlockSpec((None, SLABS, J2, 2, K1_BLOCK, SUBLANES, LANES), lambda kb, bi: (bi, 0, 0, 0, kb, 0, 0))
    return pl.pallas_call(
        _stage2_conv_kernel, grid=(n1h // K1_BLOCK, b),
        in_specs=[pl.BlockSpec((K1_BLOCK, 2 * N2, 2 * N2), lambda kb, bi: (kb, 0, 0)),
                  pl.BlockSpec((K1_BLOCK, 2 * N2, 2 * N2), lambda kb, bi: (kb, 0, 0)),
                  pl.BlockSpec((K1_BLOCK, 2 * N2, HC), lambda kb, bi: (kb, 0, order)),
                  ablk],
        out_specs=ablk,
        out_shape=jax.ShapeDtypeStruct(a7.shape, F32),
        compiler_params=_cparams(2), name="dft_stage2_conv")(mat, mat_t, spec, a7)


def _stage1_inv_kernel(w_ref, g_ref, z_ref, gate_ref, d_ref, o_ref, *, n1h, scale):
    w, d = w_ref[...], d_ref[...]
    for m in range(SUBLANES):
        y = jnp.dot(w, _rows_of(g_ref, m, 2 * n1h).astype(BF16), preferred_element_type=F32) * scale
        _store_rows(o_ref, m, _rows_of(gate_ref, m, n1h) * (y + d * _rows_of(z_ref, m, n1h)))


def dft_stage1_inv(w_inv, g5, z5, gate5, d_vec, *, scale):
    b = g5.shape[0]
    n1h = w_inv.shape[0]
    blk = lambda rows: pl.BlockSpec((None, SLABS, None, rows * SUBLANES, LANES), lambda bi, j: (bi, 0, j, 0, 0))
    return pl.pallas_call(
        functools.partial(_stage1_inv_kernel, n1h=n1h, scale=scale),
        grid=(b, J2),
        in_specs=[pl.BlockSpec((n1h, 2 * n1h), lambda bi, j: (0, 0)), blk(2 * n1h), blk(n1h), blk(n1h),
                  pl.BlockSpec((1, HC), lambda bi, j: (0, 0))],
        out_specs=blk(n1h),
        out_shape=jax.ShapeDtypeStruct(z5.shape, F32),
        compiler_params=_cparams(2), name="dft_stage1_inv")(w_inv, g5, z5, gate5, d_vec)


def _silu(g):
    return g / (1.0 + jnp.exp(-g))


def _band_mask(t, halo, i, nblk):
    qi = lax.broadcasted_iota(jnp.int32, (t, t + 2 * halo), 0)
    kj = lax.broadcasted_iota(jnp.int32, (t, t + 2 * halo), 1)
    rel = kj - halo - qi
    return ((jnp.abs(rel) <= halo) & ((kj >= halo) | (i > 0)) & ((kj < halo + t) | (i < nblk - 1)))


def _head_pair_operands(win):
    x = win.astype(F32)
    swapped = pltpu.roll(x, B_HEAD_DIM, 1)
    low = lax.broadcasted_iota(jnp.int32, x.shape, 1) < B_HEAD_DIM
    place = lambda cond, val: jnp.where(cond, val, 0.0).astype(BF16)
    return ((place(low, x), place(~low, swapped)), (place(low, swapped), place(~low, x)))


def _band_attn_kernel(sink_ref, q_ref, kp_ref, kc_ref, kn_ref, vp_ref, vc_ref, vn_ref, gate_ref, o_ref, *, nsub):
    t, hd = B_BLOCK, B_HEAD_DIM
    kpads = _head_pair_operands(jnp.concatenate([kp_ref[...], kc_ref[...], kn_ref[...]], axis=0))
    vpads = _head_pair_operands(jnp.concatenate([vp_ref[...], vc_ref[...], vn_ref[...]], axis=0))
    n_blocks = pl.num_programs(1) * nsub
    heads = [(pair, par) for pair in range(B_HEADS // 2) for par in range(2)]
    for u in range(nsub):
        valid = _band_mask(t, t, pl.program_id(1) * nsub + u, n_blocks)
        rows = slice(u * t, (u + 1) * t)
        win = slice(u * t, (u + 3) * t)
        scores = [lax.dot_general(q_ref[rows, pair * LANES:(pair + 1) * LANES], kpads[(2 * pair) // B_GROUP][par][win],
                                  (((1,), (1,)), ((), ())), preferred_element_type=F32) for pair, par in heads]
        probs, denoms = [], []
        for (pair, par), s in zip(heads, scores):
            s = jnp.where(valid, s * (hd ** -0.5), NEG_INF)
            sink = sink_ref[2 * pair + par]
            m = jnp.maximum(jnp.max(s, axis=-1, keepdims=True), sink)
            p = jnp.exp(s - m)
            denoms.append(jnp.sum(p, axis=-1, keepdims=True) + jnp.exp(sink - m))
            probs.append(p.astype(BF16))
        outs = [jnp.dot(p, vpads[(2 * pair) // B_GROUP][par][win], preferred_element_type=F32) / d
                for (pair, par), p, d in zip(heads, probs, denoms)]
        for pair in range(B_HEADS // 2):
            lanes = slice(pair * LANES, (pair + 1) * LANES)
            o_ref[rows, lanes] = ((outs[2 * pair] + outs[2 * pair + 1]) * _silu(gate_ref[rows, lanes])).astype(BF16)


def band_attn(qk, v_att, at_gate, sink, batch, seq_len, *, nsub=4):
    t = B_BLOCK
    tq = t * nsub
    nblk = seq_len // t
    qk3 = qk.reshape(batch, seq_len, qk.shape[1])
    v3 = v_att.reshape(batch, seq_len, LANES)
    g3 = at_gate.reshape(batch, seq_len, at_gate.shape[1])
    k_col = B_HEADS * B_HEAD_DIM // LANES
    halo = lambda col, f: pl.BlockSpec((None, t, LANES), lambda b, i: (b, f(i), col))
    cur = lambda col: pl.BlockSpec((None, tq, LANES), lambda b, i: (b, i, col))
    prev = lambda i: jnp.maximum(i * nsub - 1, 0)
    nxt = lambda i: jnp.minimum((i + 1) * nsub, nblk - 1)
    wide = pl.BlockSpec((None, tq, B_HEADS * B_HEAD_DIM), lambda b, i: (b, i, 0))
    out = pl.pallas_call(
        functools.partial(_band_attn_kernel, nsub=nsub), grid=(batch, seq_len // tq),
        in_specs=[pl.BlockSpec(memory_space=pltpu.SMEM), wide,
                  halo(k_col, prev), cur(k_col), halo(k_col, nxt), halo(0, prev), cur(0), halo(0, nxt), wide],
        out_specs=wide,
        out_shape=jax.ShapeDtypeStruct((batch, seq_len, B_HEADS * B_HEAD_DIM), BF16),
        compiler_params=_cparams(2), name="band_attn")(sink, qk3, qk3, qk3, qk3, v3, v3, v3, g3)
    return out.reshape(batch * seq_len, -1)


def _dil_attn_kernel(q_ref, kp_ref, kc_ref, kn_ref, vp_ref, vc_ref, vn_ref, o_ref, lse_ref, *, nsub):
    t, hd, halo = 2 * C_RADIUS, C_HEAD_DIM, C_RADIUS
    n_blocks = pl.num_programs(1) * nsub
    lane = lax.broadcasted_iota(jnp.int32, (t, LANES), 1)
    for u in range(nsub):
        valid = _band_mask(t, halo, pl.program_id(1) * nsub + u, n_blocks)
        rows = slice(u * t, (u + 1) * t)
        win = slice(u * t, (u + 1) * t + 2 * halo)
        cols = [slice(h * hd, (h + 1) * hd) for h in range(C_HEADS)]
        window = lambda p_ref, c_ref, n_ref, sl: jnp.concatenate([p_ref[:, sl], c_ref[:, sl], n_ref[:, sl]], axis=0)[win]
        scores = [lax.dot_general(q_ref[rows, sl], window(kp_ref, kc_ref, kn_ref, sl), (((1,), (1,)), ((), ())),
                                  preferred_element_type=F32) for sl in cols]
        probs, denoms = [], []
        lse_tile = jnp.zeros((t, LANES), F32)
        for h, s in enumerate(scores):
            s = jnp.where(valid, s * (hd ** -0.5), NEG_INF)
            m = jnp.max(s, axis=-1, keepdims=True)
            p = jnp.exp(s - m)
            denom = jnp.sum(p, axis=-1, keepdims=True)
            lse_tile = jnp.where(lane == h, m + jnp.log(denom), lse_tile)
            probs.append(p.astype(BF16))
            denoms.append(denom)
        for sl, p, denom in zip(cols, probs, denoms):
            o = jnp.dot(p, window(vp_ref, vc_ref, vn_ref, sl), preferred_element_type=F32) / denom
            o_ref[rows, sl] = o.astype(BF16)
        lse_ref[rows, :] = lse_tile


def dil_attn(qk, v, n_seq, ls):
    w = C_HEADS * C_HEAD_DIM
    t = 2 * C_RADIUS
    nsub = next(n for n in (4, 2, 1) if ls % (n * t) == 0)
    tq = t * nsub
    n_halo = ls // C_RADIUS
    qk3 = qk.reshape(n_seq, ls, qk.shape[1])
    v3 = v.reshape(n_seq, ls, w)
    prev = lambda i: jnp.maximum(i * (tq // C_RADIUS) - 1, 0)
    nxt = lambda i: jnp.minimum((i + 1) * (tq // C_RADIUS), n_halo - 1)
    big = lambda col: pl.BlockSpec((None, tq, w), lambda b, i: (b, i, col))
    halo = lambda col, f: pl.BlockSpec((None, C_RADIUS, w), lambda b, i: (b, f(i), col))
    o, lse = pl.pallas_call(
        functools.partial(_dil_attn_kernel, nsub=nsub), grid=(n_seq, ls // tq),
        in_specs=[big(0), halo(1, prev), big(1), halo(1, nxt), halo(0, prev), big(0), halo(0, nxt)],
        out_specs=(big(0), pl.BlockSpec((None, tq, LANES), lambda b, i: (b, i, 0))),
        out_shape=(jax.ShapeDtypeStruct((n_seq, ls, w), BF16),
                   jax.ShapeDtypeStruct((n_seq, ls, LANES), F32)),
        compiler_params=_cparams(2), name="dil_attn")(qk3, qk3, qk3, qk3, v3, v3, v3)
    return o.reshape(n_seq * ls, w), lse.reshape(n_seq * ls, LANES)


def _out_proj_even_kernel(hy_ref, at_ref, x_ref, w_ref, o_ref, *, nb):
    mixed = jnp.concatenate([_load_time_tiles(hy_ref, nb).astype(BF16), at_ref[...]], axis=1)
    o_ref[...] = x_ref[...] + jnp.dot(mixed, w_ref[...], preferred_element_type=F32)


def out_proj_even(hy, at, x2d, w_bf, *, tm=512):
    m, d = x2d.shape
    batch, n1h = hy.shape[0], hy.shape[3]
    tps = n1h * N2 // tm
    nb = tm // N2
    half = pl.BlockSpec((tm, HC), lambda b, i: (b * tps + i, 0))
    full = pl.BlockSpec((tm, d), lambda b, i: (b * tps + i, 0))
    return pl.pallas_call(
        functools.partial(_out_proj_even_kernel, nb=nb), grid=(batch, tps),
        in_specs=[pl.BlockSpec((None, SLABS, J2, nb, SUBLANES, LANES), lambda b, i: (b, 0, 0, i, 0, 0)), half, full,
                  pl.BlockSpec((d, d), lambda b, i: (0, 0))],
        out_specs=full, out_shape=jax.ShapeDtypeStruct((m, d), F32),
        compiler_params=_cparams(2), name="out_proj_even")(hy, at, x2d, w_bf)


def _merge_out_kernel(*refs, final, dilations):
    nh = C_HEADS
    o_nat, lse_nat = refs[0], refs[1]
    pos = 2
    strided = []
    for _ in dilations:
        strided.append((refs[pos:pos + nh], refs[pos + nh]))
        pos += nh + 1
    gate_ref, x_ref, w_ref = refs[pos:pos + 3]
    pos += 3
    g_ref = refs[pos] if final else None
    out_ref, o_scr, l_scr = refs[-3:]
    for gi, (d, (o_slabs, lse_ref)) in enumerate(zip(dilations, strided)):
        per = o_nat.shape[0] // d
        for r in range(d):
            l_scr[gi, pl.ds(r, per, stride=d), :] = lse_ref[r]
            for h in range(nh):
                o_scr[gi, h, pl.ds(r, per, stride=d), :] = o_slabs[h][r].astype(F32)
    ls = [lse_nat[...]] + [l_scr[gi] for gi in range(len(dilations))]
    mx = functools.reduce(jnp.maximum, ls)
    es = [jnp.exp(l - mx) for l in ls]
    den = functools.reduce(lambda a, b: a + b, es)
    alphas = [e / den for e in es]
    hd = C_HEAD_DIM
    parts = []
    for h in range(nh):
        acc = alphas[0][:, h:h + 1] * o_nat[:, h * hd:(h + 1) * hd].astype(F32)
        for gi in range(len(dilations)):
            acc = acc + alphas[gi + 1][:, h:h + 1] * o_scr[gi, h]
        parts.append(acc)
    y = (jnp.concatenate(parts, axis=1) * _silu(gate_ref[...])).astype(BF16)
    r = x_ref[...] + jnp.dot(y, w_ref[...], preferred_element_type=F32)
    if final:
        ms = jnp.mean(r * r, axis=-1, keepdims=True)
        r = r * lax.rsqrt(ms + NORM_EPS) * g_ref[...]
    out_ref[...] = r


def merge_out(o_nat, lse_nat, strided, gate, x2d, w_bf, batch, seq_len, final_g=None, *, tm=512):
    m, dm = x2d.shape
    tps = seq_len // tm
    nat = lambda wd: pl.BlockSpec((tm, wd), lambda b, i: (b * tps + i, 0))
    in_specs, args = [nat(dm), nat(LANES)], [o_nat, lse_nat]
    for d, o, lse in strided:
        per = tm // d
        in_specs += [pl.BlockSpec((None, d, per, LANES), functools.partial(lambda h, b, i: (b, 0, i, h), h))
                     for h in range(C_HEADS)]
        in_specs.append(pl.BlockSpec((None, d, per, LANES), lambda b, i: (b, 0, i, 0)))
        args += [o] * C_HEADS + [lse]
    in_specs += [nat(dm), nat(dm), pl.BlockSpec((dm, dm), lambda b, i: (0, 0))]
    args += [gate, x2d, w_bf]
    if final_g is not None:
        in_specs.append(pl.BlockSpec((1, dm), lambda b, i: (0, 0)))
        args.append(final_g.reshape(1, dm))
    n_str = len(strided)
    return pl.pallas_call(
        functools.partial(_merge_out_kernel, final=final_g is not None, dilations=tuple(d for d, _, _ in strided)),
        grid=(batch, tps), in_specs=in_specs, out_specs=nat(dm), out_shape=jax.ShapeDtypeStruct((m, dm), F32),
        scratch_shapes=[pltpu.VMEM((n_str, C_HEADS, tm, LANES), F32), pltpu.VMEM((n_str, tm, LANES), F32)],
        compiler_params=_cparams(2), name="merge_out")(*args)


def _even_w_in(w_in):
    return jnp.concatenate([w_in[:, 2048:2688], w_in[:, :2048], w_in[:, 2816:], w_in[:, 2688:2816]], axis=1).astype(BF16)


def _odd_w_in(w_in):
    w = C_HEADS * C_HEAD_DIM
    blk = lambda g, t: w_in[:, (3 * g + t) * w:(3 * g + t + 1) * w]
    per = [[blk(g, 0), blk(g, 1), blk(g, 2)] for g in range(len(C_PATTERNS))]
    per[0].append(w_in[:, 9 * w:])
    return [jnp.concatenate(cols, axis=1).astype(BF16) for cols in per]


def hyena_spectrum(seq_len, tabs, fw1, fb1, ff1, fw2, fb2, ff2, fw3):
    n1h = seq_len // N2
    _, w1_full, _, mat, _ = tabs
    filt = hyena_filter(seq_len, fw1, fb1, ff1, fw2, fb2, ff2, fw3)
    a = dft_stage1(w1_full, filt.reshape(1, 2 * SLABS, J2, 2 * n1h * SUBLANES, LANES))
    return dft_stage2_filter(mat, a.reshape(1, 2 * SLABS, J2, 2, n1h, SUBLANES, LANES))


def hyena_conv(z, gate, d_vec, spec, order, tabs):
    batch, n1h = z.shape[0], z.shape[3]
    w1_half, _, w1_inv, mat, mat_t = tabs
    rows = lambda a, r: a.reshape(batch, SLABS, J2, r * SUBLANES, LANES)
    z5 = rows(z, n1h)
    a = dft_stage1(w1_half, z5)
    g = dft_stage2_conv(mat, mat_t, spec, order, a.reshape(batch, SLABS, J2, 2, n1h, SUBLANES, LANES))
    out = dft_stage1_inv(w1_inv, rows(g, 2 * n1h), z5, rows(gate, n1h), d_vec.reshape(1, HC), scale=1.0 / (n1h * N2))
    return out.reshape(z.shape)


def hybrid_layer(x2d, batch, seq_len, norm_g, w_in_bf, conv_w, conv_b, hyena_d, sink, w_out_bf, spec, tabs, rope):
    rope_t, half = rope
    qk, x1, x2s, v, at_gate, v_att = hybrid_proj(x2d, norm_g, w_in_bf, conv_w, conv_b, batch, seq_len, rope_t, half)
    z = hyena_conv(v, x1, hyena_d[0], spec, 0, tabs)
    z = hyena_conv(z, x2s, hyena_d[1], spec, 1, tabs)
    at = band_attn(qk, v_att, at_gate, sink, batch, seq_len)
    return out_proj_even(z, at, x2d, w_out_bf)


def dilated_layer(x2d, batch, seq_len, norm_g, w_in_bfs, w_out_bf, rope, final_g):
    w = C_HEADS * C_HEAD_DIM
    m = batch * seq_len
    rope_t, half = rope
    o_nat = lse_nat = gate = None
    strided = []
    for gi, (_, d) in enumerate(C_PATTERNS):
        ls = seq_len // d
        if d == 1:
            qk, v, gate = norm_proj(x2d, norm_g, w_in_bfs[gi], seq_len, rope_t, half,
                                    ((2 * w, BF16), (w, BF16), (w, F32)))
            o_nat, lse_nat = dil_attn(qk, v, batch, ls)
        else:
            qk, v = norm_proj_strided(x2d, norm_g, w_in_bfs[gi], batch, seq_len, d, rope_t, half,
                                      ((2 * w, BF16), (w, BF16)))
            o, lse = dil_attn(qk.reshape(m, 2 * w), v.reshape(m, w), batch * d, ls)
            strided.append((d, o.reshape(batch, d, ls, w), lse.reshape(batch, d, ls, LANES)))
    return merge_out(o_nat, lse_nat, strided, gate, x2d, w_out_bf, batch, seq_len, final_g)


def kernel(x_prompt, x_sample, a_norm, a_w_in, a_conv_w, a_conv_b, a_filt_w1, a_filt_b1, a_filt_f1, a_filt_w2, a_filt_b2, a_filt_f2, a_filt_w3, a_hyena_d, a_sink, a_w_out, c_norm, c_w_in, c_w_out, final_norm):
    depth = a_norm.shape[0] + c_norm.shape[0]
    xs = [x_prompt, x_sample]
    shapes = [(x.shape[0], x.shape[1]) for x in xs]
    acts = [x.reshape(-1, D_MODEL) for x in xs]
    seq_lens = sorted({s[1] for s in shapes})
    tabs = {sl: dft_tables(sl // N2) for sl in seq_lens}
    rope_even = {sl: rope_tables(sl, B_HEAD_DIM) for sl in seq_lens}
    rope_odd = {sl: rope_tables(sl, C_HEAD_DIM) for sl in seq_lens}
    for layer in range(depth):
        i = layer // 2
        if layer % 2 == 0:
            w_in_bf = _even_w_in(a_w_in[i])
            w_out_bf = a_w_out[i].astype(BF16)
            specs = {sl: hyena_spectrum(sl, tabs[sl], a_filt_w1[i], a_filt_b1[i], a_filt_f1[i], a_filt_w2[i],
                                        a_filt_b2[i], a_filt_f2[i], a_filt_w3[i]) for sl in seq_lens}
            acts = [hybrid_layer(x2d, b, sl, a_norm[i], w_in_bf, a_conv_w[i], a_conv_b[i], a_hyena_d[i], a_sink[i],
                                 w_out_bf, specs[sl], tabs[sl], rope_even[sl])
                    for x2d, (b, sl) in zip(acts, shapes)]
        else:
            w_in_bfs = _odd_w_in(c_w_in[i])
            w_out_bf = c_w_out[i].astype(BF16)
            final_g = final_norm if layer == depth - 1 else None
            acts = [dilated_layer(x2d, b, sl, c_norm[i], w_in_bfs, w_out_bf, rope_odd[sl], final_g)
                    for x2d, (b, sl) in zip(acts, shapes)]
    assert depth % 2 == 0
    return tuple(a.reshape(b, sl, D_MODEL) for a, (b, sl) in zip(acts, shapes))
```

```python
import functools
import math

import numpy as np
import jax
import jax.numpy as jnp
from jax import lax
from jax.experimental import pallas as pl
from jax.experimental.pallas import tpu as pltpu

F32 = jnp.float32
BF16 = jnp.bfloat16

D_MODEL = 1024
HC = 512
FILTER_BANDS = 16
DECAY_TARGET = 1e-2
FAST_DECAY_PCT = 0.3
SLOW_DECAY_PCT = 1.5
DECAY_SHIFT = 0.05
B_HEAD_DIM = 64
B_HEADS = 8
B_KV_HEADS = 2
B_GROUP = 4
B_BLOCK = 128
C_PATTERNS = ((128, 1), (512, 4), (2048, 16))
C_HEADS = 8
C_HEAD_DIM = 128
C_RADIUS = 64
ROPE_THETA = 500000.0
NORM_EPS = 1e-6
NEG_INF = -1e30

LANES = 128
SUBLANES = 8
N2 = 128
J2 = N2 // SUBLANES
SLABS = HC // LANES
K1_BLOCK = 8
VMEM_LIMIT = 56 * 1024 * 1024


def _cparams(n_axes):
    return pltpu.CompilerParams(dimension_semantics=("arbitrary",) * n_axes,
                                vmem_limit_bytes=VMEM_LIMIT)


def _rope_chunk(blk, cos, sa, sb, half):
    return (blk * cos + pltpu.roll(blk, LANES - half, 1) * sa + pltpu.roll(blk, half, 1) * sb)


def _norm_proj_kernel(x_ref, g_ref, w_ref, cos_ref, sa_ref, sb_ref, *outs, half, groups, row_chunk):
    g = g_ref[...]
    for r0 in range(0, x_ref.shape[0], row_chunk):
        rows = slice(r0, r0 + row_chunk)
        acc = jnp.dot(_rms_bf16(x_ref[rows, :], g), w_ref[...], preferred_element_type=F32)
        col = 0
        for gi, (o_ref, (width, dtype)) in enumerate(zip(outs, groups)):
            if gi == 0:
                cos, sa, sb = cos_ref[rows, :], sa_ref[rows, :], sb_ref[rows, :]
                for ci in range(width // LANES):
                    src = slice(col + ci * LANES, col + (ci + 1) * LANES)
                    o_ref[rows, ci * LANES:(ci + 1) * LANES] = _rope_chunk(acc[:, src], cos, sa, sb, half).astype(dtype)
            else:
                o_ref[rows, :] = acc[:, col:col + width].astype(dtype)
            col += width


def norm_proj(x2d, g, w_bf, seq_len, rope, half, groups, *, tm=512, row_chunk=256):
    m, d = x2d.shape
    n = w_bf.shape[1]
    assert m % tm == 0 and seq_len % tm == 0 and sum(wd for wd, _ in groups) == n
    tiles_per_seq = seq_len // tm
    in_specs = [pl.BlockSpec((tm, d), lambda i: (i, 0)),
                pl.BlockSpec((1, d), lambda i: (0, 0)),
                pl.BlockSpec((d, n), lambda i: (0, 0))]
    in_specs += [pl.BlockSpec((tm, LANES), lambda i: (i % tiles_per_seq, 0))] * 3
    return pl.pallas_call(
        functools.partial(_norm_proj_kernel, half=half, groups=tuple(groups), row_chunk=row_chunk),
        grid=(m // tm,), in_specs=in_specs,
        out_specs=tuple(pl.BlockSpec((tm, wd), lambda i: (i, 0)) for wd, _ in groups),
        out_shape=tuple(jax.ShapeDtypeStruct((m, wd), dt) for wd, dt in groups),
        compiler_params=_cparams(1), name="norm_proj")(x2d, g.reshape(1, d), w_bf, *rope)


def _by_residue(ref, d):
    per = ref.shape[0] // d
    return jnp.concatenate([ref[pl.ds(r, per, stride=d), :] for r in range(d)], axis=0)


def _norm_proj_strided_kernel(*refs, half, groups, d, row_chunk):
    n_slab = D_MODEL // LANES
    x_slabs = refs[:n_slab]
    g_ref, w_ref, cos_ref, sa_ref, sb_ref = refs[n_slab:n_slab + 5]
    outs = refs[n_slab + 5:]
    tm = x_slabs[0].shape[0]
    per = tm // d
    x = jnp.concatenate([_by_residue(xs, d) for xs in x_slabs], axis=1)
    cos_t, sa_t, sb_t = _by_residue(cos_ref, d), _by_residue(sa_ref, d), _by_residue(sb_ref, d)
    g = g_ref[...]
    for r0 in range(0, tm, row_chunk):
        rows = slice(r0, r0 + row_chunk)
        acc = jnp.dot(_rms_bf16(x[rows], g), w_ref[...], preferred_element_type=F32)
        col = 0
        for gi, (o_ref, (width, dtype)) in enumerate(zip(outs, groups)):
            if gi == 0:
                res = jnp.concatenate([_rope_chunk(acc[:, col + ci * LANES:col + (ci + 1) * LANES],
                                                   cos_t[rows], sa_t[rows], sb_t[rows], half)
                                       for ci in range(width // LANES)], axis=1).astype(dtype)
            else:
                res = acc[:, col:col + width].astype(dtype)
            for k in range(row_chunk // per):
                o_ref[r0 // per + k] = res[k * per:(k + 1) * per]
            col += width


def norm_proj_strided(x2d, g, w_bf, batch, seq_len, d, rope, half, groups, *, tm=512, row_chunk=256):
    m, dm = x2d.shape
    n = w_bf.shape[1]
    tps = seq_len // tm
    per = tm // d
    assert seq_len % tm == 0 and row_chunk % per == 0 and sum(wd for wd, _ in groups) == n
    in_specs = [pl.BlockSpec((tm, LANES), functools.partial(lambda c, t: (t, c), c)) for c in range(dm // LANES)]
    in_specs += [pl.BlockSpec((1, dm), lambda t: (0, 0)), pl.BlockSpec((dm, n), lambda t: (0, 0))]
    in_specs += [pl.BlockSpec((tm, LANES), lambda t: (t % tps, 0))] * 3
    return pl.pallas_call(
        functools.partial(_norm_proj_strided_kernel, half=half, groups=tuple(groups), d=d, row_chunk=row_chunk),
        grid=(m // tm,), in_specs=in_specs,
        out_specs=tuple(pl.BlockSpec((None, d, per, wd), lambda t: (t // tps, 0, t % tps, 0)) for wd, _ in groups),
        out_shape=tuple(jax.ShapeDtypeStruct((batch, d, seq_len // d, wd), dt) for wd, dt in groups),
        compiler_params=_cparams(1), name="norm_proj_strided")(*([x2d] * (dm // LANES)), g.reshape(1, dm), w_bf, *rope)


def _rms_bf16(x, g):
    ms = jnp.mean(x * x, axis=-1, keepdims=True)
    return (x * lax.rsqrt(ms + NORM_EPS) * g).astype(BF16)


_QK_W, _HY_W, _ATT_V_W = 640, 3 * HC, 128
_HY0 = _QK_W
_HYG0 = _HY0 + _HY_W
_ATG0 = _HYG0 + HC
_VATT0 = _ATG0 + HC


def _hybrid_proj_kernel(x_ref, xp_ref, xn_ref, g_ref, w_ref, cos_ref, sa_ref, sb_ref, cw_ref, cb_ref,
                        qk_ref, x1_ref, x2_ref, v_ref, atg_ref, vatt_ref, u_scr, *, half, tiles_per_seq, row_chunk):
    tm = x_ref.shape[0]
    i = pl.program_id(0) % tiles_per_seq
    g = g_ref[...]
    halo = jnp.concatenate([xp_ref[...], xn_ref[...]], axis=0)
    acc_h = jnp.dot(_rms_bf16(halo, g), w_ref[:, _HY0:_HYG0], preferred_element_type=F32)
    prev_row = jnp.where(i > 0, acc_h[SUBLANES - 1:SUBLANES], 0.0)
    next_row = jnp.where(i < tiles_per_seq - 1, acc_h[SUBLANES:SUBLANES + 1], 0.0)
    for r0 in range(0, tm, row_chunk):
        rows = slice(r0, r0 + row_chunk)
        acc = jnp.dot(_rms_bf16(x_ref[rows, :], g), w_ref[...], preferred_element_type=F32)
        cos, sa, sb = cos_ref[rows, :], sa_ref[rows, :], sb_ref[rows, :]
        for ci in range(_QK_W // LANES):
            sl = slice(ci * LANES, (ci + 1) * LANES)
            qk_ref[rows, sl] = _rope_chunk(acc[:, sl], cos, sa, sb, half).astype(BF16)
        u_scr[rows, :] = acc[:, _HY0:_ATG0]
        atg_ref[rows, :] = acc[:, _ATG0:_VATT0]
        vatt_ref[rows, :] = acc[:, _VATT0:].astype(BF16)
    u = u_scr[:, :_HY_W]
    row = lax.broadcasted_iota(jnp.int32, u.shape, 0)
    um1 = jnp.where(row == 0, prev_row, pltpu.roll(u, 1, 0))
    up1 = jnp.where(row == tm - 1, next_row, pltpu.roll(u, tm - 1, 0))
    w = cw_ref[...]
    y = um1 * w[0:1, :] + u * w[1:2, :] + up1 * w[2:3, :] + cb_ref[...]
    _store_time_tiles(x1_ref, y[:, :HC])
    _store_time_tiles(x2_ref, y[:, HC:2 * HC] * _silu(u_scr[:, _HY_W:]))
    _store_time_tiles(v_ref, y[:, 2 * HC:])


def hybrid_proj(x2d, g, w_bf, conv_w, conv_b, batch, seq_len, rope, half, *, tm=512, row_chunk=256):
    m, d = x2d.shape
    n = w_bf.shape[1]
    tiles_per_seq = seq_len // tm
    sub = tm // SUBLANES
    last8 = m // SUBLANES - 1
    hy_shape = jax.ShapeDtypeStruct((batch, SLABS, J2, seq_len // N2, SUBLANES, LANES), F32)
    hy_spec = pl.BlockSpec((None, SLABS, J2, tm // N2, SUBLANES, LANES),
                           lambda t: (t // tiles_per_seq, 0, 0, t % tiles_per_seq, 0, 0))
    row_blk = lambda wd: pl.BlockSpec((tm, wd), lambda t: (t, 0))
    const = lambda shape: pl.BlockSpec(shape, lambda t: (0, 0))
    return pl.pallas_call(
        functools.partial(_hybrid_proj_kernel, half=half, tiles_per_seq=tiles_per_seq, row_chunk=row_chunk),
        grid=(m // tm,),
        in_specs=[row_blk(d),
                  pl.BlockSpec((SUBLANES, d), lambda t: (jnp.maximum(t * sub - 1, 0), 0)),
                  pl.BlockSpec((SUBLANES, d), lambda t: (jnp.minimum((t + 1) * sub, last8), 0)),
                  const((1, d)), const((d, n))]
                 + [pl.BlockSpec((tm, LANES), lambda t: (t % tiles_per_seq, 0))] * 3
                 + [const((3, _HY_W)), const((1, _HY_W))],
        out_specs=(row_blk(_QK_W), hy_spec, hy_spec, hy_spec, row_blk(HC), row_blk(_ATT_V_W)),
        out_shape=(jax.ShapeDtypeStruct((m, _QK_W), BF16), hy_shape, hy_shape, hy_shape,
                   jax.ShapeDtypeStruct((m, HC), F32), jax.ShapeDtypeStruct((m, _ATT_V_W), BF16)),
        scratch_shapes=[pltpu.VMEM((tm, _HY_W + HC), F32)],
        compiler_params=_cparams(1), name="hybrid_proj")(x2d, x2d, x2d, g.reshape(1, d), w_bf, *rope,
                                                         conv_w, conv_b.reshape(1, -1))


def rope_tables(seq_len, head_dim):
    rot = head_dim // 4
    half = rot // 2
    inv = jnp.power(ROPE_THETA, -2.0 * jnp.arange(half, dtype=F32) / rot)
    ang = jnp.arange(seq_len).astype(F32)[:, None] * inv[None, :]
    cos, sin = jnp.cos(ang), jnp.sin(ang)
    lane = np.arange(LANES) % head_dim
    idx = lane % half
    in_rot = jnp.asarray(lane < rot)[None, :]
    first = jnp.asarray(lane < half)[None, :]
    cos_t = jnp.where(in_rot, cos[:, idx], 1.0)
    sa = jnp.where(first, -sin[:, idx], 0.0)
    sb = jnp.where(in_rot & ~first, sin[:, idx], 0.0)
    return (cos_t, sa, sb), half


def _store_time_tiles(ref, val):
    for s in range(SLABS):
        for i in range(val.shape[0] // N2):
            ref[s, :, i] = val[i * N2:(i + 1) * N2, s * LANES:(s + 1) * LANES].reshape(J2, SUBLANES, LANES)


def _load_time_tiles(ref, nb):
    return jnp.concatenate([jnp.concatenate([ref[s, :, i].reshape(N2, LANES) for s in range(SLABS)], axis=1)
                            for i in range(nb)], axis=0)


def _filter_kernel(band_ref, w1_ref, b1_ref, f1_ref, w2_ref, b2_ref, f2_ref, w3_ref, dl_ref, o_ref,
                   *, seq_len, tr):
    n = pl.program_id(0) * tr + lax.broadcasted_iota(jnp.int32, (tr, 1), 0)
    j = jnp.where(n < seq_len, n, 2 * seq_len - n).astype(F32)
    t = j * (1.0 / (seq_len - 1))
    wpos = (2.0 * math.pi) * j / seq_len
    lane = lax.broadcasted_iota(jnp.int32, (tr, LANES), 1)
    arg = band_ref[...] * wpos
    feats = jnp.where(lane == 0, t,
                      jnp.where(lane <= FILTER_BANDS, jnp.cos(arg),
                                jnp.where(lane <= 2 * FILTER_BANDS, -jnp.sin(arg), 0.0)))
    h = jnp.dot(feats.astype(BF16), w1_ref[...], preferred_element_type=F32) + b1_ref[...]
    h = jnp.sin(f1_ref[...] * h)
    h = jnp.dot(h.astype(BF16), w2_ref[...], preferred_element_type=F32) + b2_ref[...]
    h = jnp.sin(f2_ref[...] * h)
    h = jnp.dot(h.astype(BF16), w3_ref[...], preferred_element_type=F32)
    win = jnp.exp(-t * dl_ref[...]) + DECAY_SHIFT
    win2 = jnp.concatenate([win, win], axis=1)
    fwd = h[:, :2 * HC] * win2
    bwd = h[:, 2 * HC:] * win2
    res = (jnp.where(n < seq_len, fwd, 0.0)
           + jnp.where(n == 0, bwd, 0.0) - jnp.where(n > seq_len, bwd, 0.0))
    for order in range(2):
        _store_time_tiles(o_ref.at[order * SLABS:(order + 1) * SLABS], res[:, order * HC:(order + 1) * HC])


def hyena_filter(seq_len, w1, b1, f1, w2, b2, f2, w3, *, tr=512):
    hid = w1.shape[1]
    pad = LANES - hid
    w1p = jnp.pad(w1, ((0, LANES - w1.shape[0]), (0, pad))).astype(BF16)
    w2p = jnp.pad(w2, ((0, pad), (0, pad))).astype(BF16)
    w3r = w3.reshape(hid, 2, 2, HC).transpose(0, 2, 1, 3).reshape(hid, 4 * HC)
    w3p = jnp.pad(w3r, ((0, pad), (0, 0))).astype(BF16)
    vec = lambda v: jnp.pad(v, (0, pad)).reshape(1, LANES)
    bands = jnp.linspace(1e-4, FILTER_BANDS - 1, FILTER_BANDS, dtype=F32)
    band_l = jnp.concatenate([jnp.zeros((1,), F32), bands, bands,
                              jnp.zeros((LANES - 1 - 2 * FILTER_BANDS,), F32)]).reshape(1, LANES)
    max_decay = math.log(DECAY_TARGET) / FAST_DECAY_PCT
    min_decay = math.log(DECAY_TARGET) / SLOW_DECAY_PCT
    deltas = jnp.abs(jnp.linspace(min_decay, max_decay, HC, dtype=F32)).reshape(1, HC)
    full = lambda a: pl.BlockSpec(a.shape, lambda i: (0, 0))
    args = [band_l, w1p, vec(b1), vec(f1), w2p, vec(b2), vec(f2), w3p, deltas]
    return pl.pallas_call(
        functools.partial(_filter_kernel, seq_len=seq_len, tr=tr),
        grid=(2 * seq_len // tr,),
        in_specs=[full(a) for a in args],
        out_specs=pl.BlockSpec((2 * SLABS, J2, tr // N2, SUBLANES, LANES), lambda i: (0, 0, i, 0, 0)),
        out_shape=jax.ShapeDtypeStruct((2 * SLABS, J2, 2 * seq_len // N2, SUBLANES, LANES), F32),
        compiler_params=_cparams(1), name="hyena_filter")(*args)


def dft_tables(n1h):
    n1 = 2 * n1h
    n = n1 * N2
    k1 = jnp.arange(n1h, dtype=jnp.int32)
    m1 = (jnp.arange(n1, dtype=jnp.int32)[None, :] * (2 * k1[:, None] + 1)) % (2 * n1)
    a1 = m1.astype(F32) * (math.pi / n1)
    c1, s1 = jnp.cos(a1), jnp.sin(a1)
    w1_full = jnp.concatenate([c1, -s1], axis=0).astype(BF16)
    w1_half = w1_full[:, :n1h]
    w1_inv = jnp.concatenate([c1[:, :n1h].T, -s1[:, :n1h].T], axis=1).astype(BF16)
    k2 = jnp.arange(N2, dtype=jnp.int32)
    n2 = jnp.arange(N2, dtype=jnp.int32)
    f = 2 * (k2[None, :, None] * n1 + k1[:, None, None]) + 1
    m2 = (n2[None, None, :] * f) % (2 * n)
    th = m2.astype(F32) * (math.pi / n)
    c, s = jnp.cos(th), jnp.sin(th)
    mat = jnp.concatenate([jnp.concatenate([c, s], axis=2),
                           jnp.concatenate([-s, c], axis=2)], axis=1)
    return w1_half, w1_full, w1_inv, mat.astype(BF16), jnp.swapaxes(mat, 1, 2).astype(BF16)


def _rows_of(ref2, m, n_rows):
    return jnp.concatenate([ref2[cb, pl.ds(m, n_rows, stride=SUBLANES), :] for cb in range(SLABS)], axis=1)


def _store_rows(ref2, m, val):
    for cb in range(SLABS):
        ref2[cb, pl.ds(m, val.shape[0], stride=SUBLANES), :] = val[:, cb * LANES:(cb + 1) * LANES]


def _stage1_kernel(w_ref, x_ref, o_ref, *, k, r):
    w = w_ref[...]
    for m in range(SUBLANES):
        _store_rows(o_ref, m, jnp.dot(w, _rows_of(x_ref, m, k).astype(BF16), preferred_element_type=F32))


def dft_stage1(w, x5):
    b, s = x5.shape[:2]
    r, k = w.shape
    assert s % SLABS == 0 and x5.shape[3] == k * SUBLANES
    blk = lambda rows: pl.BlockSpec((None, SLABS, None, rows * SUBLANES, LANES), lambda bi, si, j: (bi, si, j, 0, 0))
    return pl.pallas_call(
        functools.partial(_stage1_kernel, k=k, r=r), grid=(b, s // SLABS, J2),
        in_specs=[pl.BlockSpec((r, k), lambda bi, si, j: (0, 0)), blk(k)],
        out_specs=blk(r),
        out_shape=jax.ShapeDtypeStruct((b, s, J2, r * SUBLANES, LANES), F32),
        compiler_params=_cparams(3), name="dft_stage1")(w, x5)


def _slab_cat(a_ref, kk, n_slabs):
    return jnp.concatenate([jnp.concatenate([a_ref[cb, :, 0, kk].reshape(N2, LANES),
                                             a_ref[cb, :, 1, kk].reshape(N2, LANES)], axis=0)
                            for cb in range(n_slabs)], axis=1).astype(BF16)


def _stage2_filter_kernel(m_ref, a_ref, o_ref):
    for kk in range(K1_BLOCK):
        o_ref[kk] = jnp.dot(m_ref[kk], _slab_cat(a_ref, kk, 2 * SLABS), preferred_element_type=F32)


def dft_stage2_filter(mat, a7):
    n1h = a7.shape[4]
    return pl.pallas_call(
        _stage2_filter_kernel, grid=(n1h // K1_BLOCK,),
        in_specs=[pl.BlockSpec((K1_BLOCK, 2 * N2, 2 * N2), lambda kb: (kb, 0, 0)),
                  pl.BlockSpec((None, 2 * SLABS, J2, 2, K1_BLOCK, SUBLANES, LANES), lambda kb: (0, 0, 0, 0, kb, 0, 0))],
        out_specs=pl.BlockSpec((K1_BLOCK, 2 * N2, 2 * HC), lambda kb: (kb, 0, 0)),
        out_shape=jax.ShapeDtypeStruct((n1h, 2 * N2, 2 * HC), F32),
        compiler_params=_cparams(1), name="dft_stage2_filter")(mat, a7)


def _stage2_conv_kernel(m_ref, mt_ref, h_ref, a_ref, o_ref):
    for kk in range(K1_BLOCK):
        z = jnp.dot(m_ref[kk], _slab_cat(a_ref, kk, SLABS), preferred_element_type=F32)
        zr, zi = z[:N2], z[N2:]
        hr, hi = h_ref[kk, :N2], h_ref[kk, N2:]
        y = jnp.concatenate([zr * hr - zi * hi, zr * hi + zi * hr], axis=0).astype(BF16)
        g = jnp.dot(mt_ref[kk], y, preferred_element_type=F32)
        for cb in range(SLABS):
            o_ref[cb, :, 0, kk] = g[:N2, cb * LANES:(cb + 1) * LANES].reshape(J2, SUBLANES, LANES)
            o_ref[cb, :, 1, kk] = g[N2:, cb * LANES:(cb + 1) * LANES].reshape(J2, SUBLANES, LANES)


def dft_stage2_conv(mat, mat_t, spec, order, a7):
    b, n1h = a7.shape[0], a7.shape[4]
    ablk = pl.BlockSpec((None, SLABS, J2, 2, K1_BLOCK, SUBLANES, LANES), lambda kb, bi: (bi, 0, 0, 0, kb, 0, 0))
    return pl.pallas_call(
        _stage2_conv_kernel, grid=(n1h // K1_BLOCK, b),
        in_specs=[pl.BlockSpec((K1_BLOCK, 2 * N2, 2 * N2), lambda kb, bi: (kb, 0, 0)),
                  pl.BlockSpec((K1_BLOCK, 2 * N2, 2 * N2), lambda kb, bi: (kb, 0, 0)),
                  pl.BlockSpec((K1_BLOCK, 2 * N2, HC), lambda kb, bi: (kb, 0, order)),
                  ablk],
        out_specs=ablk,
        out_shape=jax.ShapeDtypeStruct(a7.shape, F32),
        compiler_params=_cparams(2), name="dft_stage2_conv")(mat, mat_t, spec, a7)


def _stage1_inv_kernel(w_ref, g_ref, z_ref, gate_ref, d_ref, o_ref, *, n1h, scale):
    w, d = w_ref[...], d_ref[...]
    for m in range(SUBLANES):
        y = jnp.dot(w, _rows_of(g_ref, m, 2 * n1h).astype(BF16), preferred_element_type=F32) * scale
        _store_rows(o_ref, m, _rows_of(gate_ref, m, n1h) * (y + d * _rows_of(z_ref, m, n1h)))


def dft_stage1_inv(w_inv, g5, z5, gate5, d_vec, *, scale):
    b = g5.shape[0]
    n1h = w_inv.shape[0]
    blk = lambda rows: pl.BlockSpec((None, SLABS, None, rows * SUBLANES, LANES), lambda bi, j: (bi, 0, j, 0, 0))
    return pl.pallas_call(
        functools.partial(_stage1_inv_kernel, n1h=n1h, scale=scale),
        grid=(b, J2),
        in_specs=[pl.BlockSpec((n1h, 2 * n1h), lambda bi, j: (0, 0)), blk(2 * n1h), blk(n1h), blk(n1h),
                  pl.BlockSpec((1, HC), lambda bi, j: (0, 0))],
        out_specs=blk(n1h),
        out_shape=jax.ShapeDtypeStruct(z5.shape, F32),
        compiler_params=_cparams(2), name="dft_stage1_inv")(w_inv, g5, z5, gate5, d_vec)


CONV_SCRATCH_BYTES = 16 * 1024 * 1024
CONV_J_BLOCK = 4


def _conv_fused_kernel(w1_ref, wi_ref, m_ref, mt_ref, h_ref, z_ref, gate_ref, d_ref, o_ref, a_scr,
                       *, n1h, sg, nb, scale):
    t = pl.program_id(2)
    n_a = J2 // CONV_J_BLOCK
    cat = lambda pieces: jnp.concatenate(pieces, axis=1) if len(pieces) > 1 else pieces[0]
    strided = lambda ref, idx, m, rows: cat([ref[(cb,) + idx + (pl.ds(m, rows, stride=SUBLANES), slice(None))]
                                             for cb in range(sg)])

    @pl.when(t < n_a)
    def _():
        w = w1_ref[...]
        for jj in range(CONV_J_BLOCK):
            j = t * CONV_J_BLOCK + jj
            for m in range(SUBLANES):
                res = jnp.dot(w, strided(z_ref, (jj,), m, n1h).astype(BF16), preferred_element_type=F32)
                for cb in range(sg):
                    a_scr[cb, j, pl.ds(m, 2 * n1h, stride=SUBLANES), :] = res[:, cb * LANES:(cb + 1) * LANES]

    @pl.when((t >= n_a) & (t < n_a + nb))
    def _():
        kb = t - n_a
        for kk in range(K1_BLOCK):
            re0 = pl.multiple_of((kb * K1_BLOCK + kk) * SUBLANES, SUBLANES)
            im0 = pl.multiple_of((n1h + kb * K1_BLOCK + kk) * SUBLANES, SUBLANES)
            tile = lambda cb, r0: a_scr[cb, :, pl.ds(r0, SUBLANES), :].reshape(N2, LANES)
            xs = cat([jnp.concatenate([tile(cb, re0), tile(cb, im0)], axis=0) for cb in range(sg)]).astype(BF16)
            z = jnp.dot(m_ref[kk], xs, preferred_element_type=F32)
            zr, zi = z[:N2], z[N2:]
            hr, hi = h_ref[kk, :N2], h_ref[kk, N2:]
            y = jnp.concatenate([zr * hr - zi * hi, zr * hi + zi * hr], axis=0).astype(BF16)
            g = jnp.dot(mt_ref[kk], y, preferred_element_type=F32)
            for cb in range(sg):
                lanes = slice(cb * LANES, (cb + 1) * LANES)
                a_scr[cb, :, pl.ds(re0, SUBLANES), :] = g[:N2, lanes].reshape(J2, SUBLANES, LANES)
                a_scr[cb, :, pl.ds(im0, SUBLANES), :] = g[N2:, lanes].reshape(J2, SUBLANES, LANES)

    @pl.when(t >= n_a + nb)
    def _():
        w, d = wi_ref[...], d_ref[...]
        for jj in range(CONV_J_BLOCK):
            j = (t - n_a - nb) * CONV_J_BLOCK + jj
            for m in range(SUBLANES):
                gm = cat([a_scr[cb, j, pl.ds(m, 2 * n1h, stride=SUBLANES), :] for cb in range(sg)]).astype(BF16)
                y = jnp.dot(w, gm, preferred_element_type=F32) * scale
                res = strided(gate_ref, (jj,), m, n1h) * (y + d * strided(z_ref, (jj,), m, n1h))
                for cb in range(sg):
                    o_ref[cb, jj, pl.ds(m, n1h, stride=SUBLANES), :] = res[:, cb * LANES:(cb + 1) * LANES]


def hyena_conv_fused(z, gate, d_vec, spec, order, tabs):
    batch, n1h = z.shape[0], z.shape[3]
    w1_half, _, w1_inv, mat, mat_t = tabs
    sg = max(1, min(SLABS, CONV_SCRATCH_BYTES // (J2 * 2 * n1h * SUBLANES * LANES * 4)))
    nb = n1h // K1_BLOCK
    n_a = J2 // CONV_J_BLOCK
    rows = lambda a: a.reshape(batch, SLABS, J2, n1h * SUBLANES, LANES)
    phase_c = lambda t: jnp.maximum(t - n_a - nb, 0)
    z_idx = lambda t: jnp.where(t < n_a + nb, jnp.minimum(t, n_a - 1), t - n_a - nb)
    tile = lambda f: pl.BlockSpec((None, sg, CONV_J_BLOCK, n1h * SUBLANES, LANES), lambda b, s, t: (b, s, f(t), 0, 0))
    kblk = lambda t: jnp.clip(t - n_a, 0, nb - 1)
    const = lambda a: pl.BlockSpec(a.shape, lambda b, s, t: (0, 0))
    mspec = pl.BlockSpec((K1_BLOCK, 2 * N2, 2 * N2), lambda b, s, t: (kblk(t), 0, 0))
    out = pl.pallas_call(
        functools.partial(_conv_fused_kernel, n1h=n1h, sg=sg, nb=nb, scale=1.0 / (n1h * N2)),
        grid=(batch, SLABS // sg, 2 * n_a + nb),
        in_specs=[const(w1_half), const(w1_inv), mspec, mspec,
                  pl.BlockSpec((K1_BLOCK, 2 * N2, sg * LANES),
                               lambda b, s, t: (kblk(t), 0, order * (SLABS // sg) + s)),
                  tile(z_idx), tile(phase_c),
                  pl.BlockSpec((1, sg * LANES), lambda b, s, t: (0, s))],
        out_specs=tile(phase_c),
        out_shape=jax.ShapeDtypeStruct((batch, SLABS, J2, n1h * SUBLANES, LANES), F32),
        scratch_shapes=[pltpu.VMEM((sg, J2, 2 * n1h * SUBLANES, LANES), F32)],
        compiler_params=_cparams(3), name="hyena_conv")(w1_half, w1_inv, mat, mat_t, spec, rows(z), rows(gate),
                                                        d_vec.reshape(1, HC))
    return out.reshape(z.shape)


def _silu(g):
    return g / (1.0 + jnp.exp(-g))


def _band_mask(t, halo, i, nblk):
    qi = lax.broadcasted_iota(jnp.int32, (t, t + 2 * halo), 0)
    kj = lax.broadcasted_iota(jnp.int32, (t, t + 2 * halo), 1)
    rel = kj - halo - qi
    return ((jnp.abs(rel) <= halo) & ((kj >= halo) | (i > 0)) & ((kj < halo + t) | (i < nblk - 1)))


def _head_pair_operands(win):
    x = win.astype(F32)
    swapped = pltpu.roll(x, B_HEAD_DIM, 1)
    low = lax.broadcasted_iota(jnp.int32, x.shape, 1) < B_HEAD_DIM
    place = lambda cond, val: jnp.where(cond, val, 0.0).astype(BF16)
    return ((place(low, x), place(~low, swapped)), (place(low, swapped), place(~low, x)))


def _band_attn_kernel(sink_ref, q_ref, kp_ref, kc_ref, kn_ref, vp_ref, vc_ref, vn_ref, gate_ref, o_ref, *, nsub):
    t, hd = B_BLOCK, B_HEAD_DIM
    kpads = _head_pair_operands(jnp.concatenate([kp_ref[...], kc_ref[...], kn_ref[...]], axis=0))
    vpads = _head_pair_operands(jnp.concatenate([vp_ref[...], vc_ref[...], vn_ref[...]], axis=0))
    n_blocks = pl.num_programs(1) * nsub
    heads = [(pair, par) for pair in range(B_HEADS // 2) for par in range(2)]
    for u in range(nsub):
        valid = _band_mask(t, t, pl.program_id(1) * nsub + u, n_blocks)
        rows = slice(u * t, (u + 1) * t)
        win = slice(u * t, (u + 3) * t)
        scores = [lax.dot_general(q_ref[rows, pair * LANES:(pair + 1) * LANES], kpads[(2 * pair) // B_GROUP][par][win],
                                  (((1,), (1,)), ((), ())), preferred_element_type=F32) for pair, par in heads]
        probs, denoms = [], []
        for (pair, par), s in zip(heads, scores):
            s = jnp.where(valid, s * (hd ** -0.5), NEG_INF)
            sink = sink_ref[2 * pair + par]
            m = jnp.maximum(jnp.max(s, axis=-1, keepdims=True), sink)
            p = jnp.exp(s - m)
            denoms.append(jnp.sum(p, axis=-1, keepdims=True) + jnp.exp(sink - m))
            probs.append(p.astype(BF16))
        outs = [jnp.dot(p, vpads[(2 * pair) // B_GROUP][par][win], preferred_element_type=F32) / d
                for (pair, par), p, d in zip(heads, probs, denoms)]
        for pair in range(B_HEADS // 2):
            lanes = slice(pair * LANES, (pair + 1) * LANES)
            o_ref[rows, lanes] = ((outs[2 * pair] + outs[2 * pair + 1]) * _silu(gate_ref[rows, lanes])).astype(BF16)


def band_attn(qk, v_att, at_gate, sink, batch, seq_len, *, nsub=4):
    t = B_BLOCK
    tq = t * nsub
    nblk = seq_len // t
    qk3 = qk.reshape(batch, seq_len, qk.shape[1])
    v3 = v_att.reshape(batch, seq_len, LANES)
    g3 = at_gate.reshape(batch, seq_len, at_gate.shape[1])
    k_col = B_HEADS * B_HEAD_DIM // LANES
    halo = lambda col, f: pl.BlockSpec((None, t, LANES), lambda b, i: (b, f(i), col))
    cur = lambda col: pl.BlockSpec((None, tq, LANES), lambda b, i: (b, i, col))
    prev = lambda i: jnp.maximum(i * nsub - 1, 0)
    nxt = lambda i: jnp.minimum((i + 1) * nsub, nblk - 1)
    wide = pl.BlockSpec((None, tq, B_HEADS * B_HEAD_DIM), lambda b, i: (b, i, 0))
    out = pl.pallas_call(
        functools.partial(_band_attn_kernel, nsub=nsub), grid=(batch, seq_len // tq),
        in_specs=[pl.BlockSpec(memory_space=pltpu.SMEM), wide,
                  halo(k_col, prev), cur(k_col), halo(k_col, nxt), halo(0, prev), cur(0), halo(0, nxt), wide],
        out_specs=wide,
        out_shape=jax.ShapeDtypeStruct((batch, seq_len, B_HEADS * B_HEAD_DIM), BF16),
        compiler_params=_cparams(2), name="band_attn")(sink, qk3, qk3, qk3, qk3, v3, v3, v3, g3)
    return out.reshape(batch * seq_len, -1)


def _dil_attn_kernel(q_ref, kp_ref, kc_ref, kn_ref, vp_ref, vc_ref, vn_ref, o_ref, lse_ref, *, nsub):
    t, hd, halo = 2 * C_RADIUS, C_HEAD_DIM, C_RADIUS
    n_blocks = pl.num_programs(1) * nsub
    lane = lax.broadcasted_iota(jnp.int32, (t, LANES), 1)
    for u in range(nsub):
        valid = _band_mask(t, halo, pl.program_id(1) * nsub + u, n_blocks)
        rows = slice(u * t, (u + 1) * t)
        win = slice(u * t, (u + 1) * t + 2 * halo)
        cols = [slice(h * hd, (h + 1) * hd) for h in range(C_HEADS)]
        window = lambda p_ref, c_ref, n_ref, sl: jnp.concatenate([p_ref[:, sl], c_ref[:, sl], n_ref[:, sl]], axis=0)[win]
        scores = [lax.dot_general(q_ref[rows, sl], window(kp_ref, kc_ref, kn_ref, sl), (((1,), (1,)), ((), ())),
                                  preferred_element_type=F32) for sl in cols]
        probs, denoms = [], []
        lse_tile = jnp.zeros((t, LANES), F32)
        for h, s in enumerate(scores):
            s = jnp.where(valid, s * (hd ** -0.5), NEG_INF)
            m = jnp.max(s, axis=-1, keepdims=True)
            p = jnp.exp(s - m)
            denom = jnp.sum(p, axis=-1, keepdims=True)
            lse_tile = jnp.where(lane == h, m + jnp.log(denom), lse_tile)
            probs.append(p.astype(BF16))
            denoms.append(denom)
        for sl, p, denom in zip(cols, probs, denoms):
            o = jnp.dot(p, window(vp_ref, vc_ref, vn_ref, sl), preferred_element_type=F32) / denom
            o_ref[rows, sl] = o.astype(BF16)
        lse_ref[rows, :] = lse_tile


def dil_attn(qk, v, n_seq, ls):
    w = C_HEADS * C_HEAD_DIM
    t = 2 * C_RADIUS
    nsub = next(n for n in (4, 2, 1) if ls % (n * t) == 0)
    tq = t * nsub
    n_halo = ls // C_RADIUS
    qk3 = qk.reshape(n_seq, ls, qk.shape[1])
    v3 = v.reshape(n_seq, ls, w)
    prev = lambda i: jnp.maximum(i * (tq // C_RADIUS) - 1, 0)
    nxt = lambda i: jnp.minimum((i + 1) * (tq // C_RADIUS), n_halo - 1)
    big = lambda col: pl.BlockSpec((None, tq, w), lambda b, i: (b, i, col))
    halo = lambda col, f: pl.BlockSpec((None, C_RADIUS, w), lambda b, i: (b, f(i), col))
    o, lse = pl.pallas_call(
        functools.partial(_dil_attn_kernel, nsub=nsub), grid=(n_seq, ls // tq),
        in_specs=[big(0), halo(1, prev), big(1), halo(1, nxt), halo(0, prev), big(0), halo(0, nxt)],
        out_specs=(big(0), pl.BlockSpec((None, tq, LANES), lambda b, i: (b, i, 0))),
        out_shape=(jax.ShapeDtypeStruct((n_seq, ls, w), BF16),
                   jax.ShapeDtypeStruct((n_seq, ls, LANES), F32)),
        compiler_params=_cparams(2), name="dil_attn")(qk3, qk3, qk3, qk3, v3, v3, v3)
    return o.reshape(n_seq * ls, w), lse.reshape(n_seq * ls, LANES)


def _out_proj_even_kernel(hy_ref, at_ref, x_ref, w_ref, o_ref, *, nb):
    mixed = jnp.concatenate([_load_time_tiles(hy_ref, nb).astype(BF16), at_ref[...]], axis=1)
    o_ref[...] = x_ref[...] + jnp.dot(mixed, w_ref[...], preferred_element_type=F32)


def out_proj_even(hy, at, x2d, w_bf, *, tm=512):
    m, d = x2d.shape
    batch, n1h = hy.shape[0], hy.shape[3]
    tps = n1h * N2 // tm
    nb = tm // N2
    half = pl.BlockSpec((tm, HC), lambda b, i: (b * tps + i, 0))
    full = pl.BlockSpec((tm, d), lambda b, i: (b * tps + i, 0))
    return pl.pallas_call(
        functools.partial(_out_proj_even_kernel, nb=nb), grid=(batch, tps),
        in_specs=[pl.BlockSpec((None, SLABS, J2, nb, SUBLANES, LANES), lambda b, i: (b, 0, 0, i, 0, 0)), half, full,
                  pl.BlockSpec((d, d), lambda b, i: (0, 0))],
        out_specs=full, out_shape=jax.ShapeDtypeStruct((m, d), F32),
        compiler_params=_cparams(2), name="out_proj_even")(hy, at, x2d, w_bf)


def _merge_out_kernel(*refs, final, dilations):
    nh = C_HEADS
    o_nat, lse_nat = refs[0], refs[1]
    pos = 2
    strided = []
    for _ in dilations:
        strided.append((refs[pos:pos + nh], refs[pos + nh]))
        pos += nh + 1
    gate_ref, x_ref, w_ref = refs[pos:pos + 3]
    pos += 3
    g_ref = refs[pos] if final else None
    out_ref, o_scr, l_scr = refs[-3:]
    for gi, (d, (o_slabs, lse_ref)) in enumerate(zip(dilations, strided)):
        per = o_nat.shape[0] // d
        for r in range(d):
            l_scr[gi, pl.ds(r, per, stride=d), :] = lse_ref[r]
            for h in range(nh):
                o_scr[gi, h, pl.ds(r, per, stride=d), :] = o_slabs[h][r].astype(F32)
    ls = [lse_nat[...]] + [l_scr[gi] for gi in range(len(dilations))]
    mx = functools.reduce(jnp.maximum, ls)
    es = [jnp.exp(l - mx) for l in ls]
    den = functools.reduce(lambda a, b: a + b, es)
    alphas = [e / den for e in es]
    hd = C_HEAD_DIM
    parts = []
    for h in range(nh):
        acc = alphas[0][:, h:h + 1] * o_nat[:, h * hd:(h + 1) * hd].astype(F32)
        for gi in range(len(dilations)):
            acc = acc + alphas[gi + 1][:, h:h + 1] * o_scr[gi, h]
        parts.append(acc)
    y = (jnp.concatenate(parts, axis=1) * _silu(gate_ref[...])).astype(BF16)
    r = x_ref[...] + jnp.dot(y, w_ref[...], preferred_element_type=F32)
    if final:
        ms = jnp.mean(r * r, axis=-1, keepdims=True)
        r = r * lax.rsqrt(ms + NORM_EPS) * g_ref[...]
    out_ref[...] = r


def merge_out(o_nat, lse_nat, strided, gate, x2d, w_bf, batch, seq_len, final_g=None, *, tm=512):
    m, dm = x2d.shape
    tps = seq_len // tm
    nat = lambda wd: pl.BlockSpec((tm, wd), lambda b, i: (b * tps + i, 0))
    in_specs, args = [nat(dm), nat(LANES)], [o_nat, lse_nat]
    for d, o, lse in strided:
        per = tm // d
        in_specs += [pl.BlockSpec((None, d, per, LANES), functools.partial(lambda h, b, i: (b, 0, i, h), h))
                     for h in range(C_HEADS)]
        in_specs.append(pl.BlockSpec((None, d, per, LANES), lambda b, i: (b, 0, i, 0)))
        args += [o] * C_HEADS + [lse]
    in_specs += [nat(dm), nat(dm), pl.BlockSpec((dm, dm), lambda b, i: (0, 0))]
    args += [gate, x2d, w_bf]
    if final_g is not None:
        in_specs.append(pl.BlockSpec((1, dm), lambda b, i: (0, 0)))
        args.append(final_g.reshape(1, dm))
    n_str = len(strided)
    return pl.pallas_call(
        functools.partial(_merge_out_kernel, final=final_g is not None, dilations=tuple(d for d, _, _ in strided)),
        grid=(batch, tps), in_specs=in_specs, out_specs=nat(dm), out_shape=jax.ShapeDtypeStruct((m, dm), F32),
        scratch_shapes=[pltpu.VMEM((n_str, C_HEADS, tm, LANES), F32), pltpu.VMEM((n_str, tm, LANES), F32)],
        compiler_params=_cparams(2), name="merge_out")(*args)


def _even_w_in(w_in):
    return jnp.concatenate([w_in[:, 2048:2688], w_in[:, :2048], w_in[:, 2816:], w_in[:, 2688:2816]], axis=1).astype(BF16)


def _odd_w_in(w_in):
    w = C_HEADS * C_HEAD_DIM
    blk = lambda g, t: w_in[:, (3 * g + t) * w:(3 * g + t + 1) * w]
    per = [[blk(g, 0), blk(g, 1), blk(g, 2)] for g in range(len(C_PATTERNS))]
    per[0].append(w_in[:, 9 * w:])
    return [jnp.concatenate(cols, axis=1).astype(BF16) for cols in per]


def hyena_spectrum(seq_len, tabs, fw1, fb1, ff1, fw2, fb2, ff2, fw3):
    n1h = seq_len // N2
    _, w1_full, _, mat, _ = tabs
    filt = hyena_filter(seq_len, fw1, fb1, ff1, fw2, fb2, ff2, fw3)
    a = dft_stage1(w1_full, filt.reshape(1, 2 * SLABS, J2, 2 * n1h * SUBLANES, LANES))
    return dft_stage2_filter(mat, a.reshape(1, 2 * SLABS, J2, 2, n1h, SUBLANES, LANES))


def hyena_conv(z, gate, d_vec, spec, order, tabs):
    batch, n1h = z.shape[0], z.shape[3]
    w1_half, _, w1_inv, mat, mat_t = tabs
    rows = lambda a, r: a.reshape(batch, SLABS, J2, r * SUBLANES, LANES)
    z5 = rows(z, n1h)
    a = dft_stage1(w1_half, z5)
    g = dft_stage2_conv(mat, mat_t, spec, order, a.reshape(batch, SLABS, J2, 2, n1h, SUBLANES, LANES))
    out = dft_stage1_inv(w1_inv, rows(g, 2 * n1h), z5, rows(gate, n1h), d_vec.reshape(1, HC), scale=1.0 / (n1h * N2))
    return out.reshape(z.shape)


def hybrid_layer(x2d, batch, seq_len, norm_g, w_in_bf, conv_w, conv_b, hyena_d, sink, w_out_bf, spec, tabs, rope):
    rope_t, half = rope
    qk, x1, x2s, v, at_gate, v_att = hybrid_proj(x2d, norm_g, w_in_bf, conv_w, conv_b, batch, seq_len, rope_t, half)
    z = hyena_conv_fused(v, x1, hyena_d[0], spec, 0, tabs)
    z = hyena_conv_fused(z, x2s, hyena_d[1], spec, 1, tabs)
    at = band_attn(qk, v_att, at_gate, sink, batch, seq_len)
    return out_proj_even(z, at, x2d, w_out_bf)


def dilated_layer(x2d, batch, seq_len, norm_g, w_in_bfs, w_out_bf, rope, final_g):
    w = C_HEADS * C_HEAD_DIM
    m = batch * seq_len
    rope_t, half = rope
    o_nat = lse_nat = gate = None
    strided = []
    for gi, (_, d) in enumerate(C_PATTERNS):
        ls = seq_len // d
        if d == 1:
            qk, v, gate = norm_proj(x2d, norm_g, w_in_bfs[gi], seq_len, rope_t, half,
                                    ((2 * w, BF16), (w, BF16), (w, F32)))
            o_nat, lse_nat = dil_attn(qk, v, batch, ls)
        else:
            qk, v = norm_proj_strided(x2d, norm_g, w_in_bfs[gi], batch, seq_len, d, rope_t, half,
                                      ((2 * w, BF16), (w, BF16)))
            o, lse = dil_attn(qk.reshape(m, 2 * w), v.reshape(m, w), batch * d, ls)
            strided.append((d, o.reshape(batch, d, ls, w), lse.reshape(batch, d, ls, LANES)))
    return merge_out(o_nat, lse_nat, strided, gate, x2d, w_out_bf, batch, seq_len, final_g)


def kernel(x_prompt, x_sample, a_norm, a_w_in, a_conv_w, a_conv_b, a_filt_w1, a_filt_b1, a_filt_f1, a_filt_w2, a_filt_b2, a_filt_f2, a_filt_w3, a_hyena_d, a_sink, a_w_out, c_norm, c_w_in, c_w_out, final_norm):
    depth = a_norm.shape[0] + c_norm.shape[0]
    xs = [x_prompt, x_sample]
    shapes = [(x.shape[0], x.shape[1]) for x in xs]
    acts = [x.reshape(-1, D_MODEL) for x in xs]
    seq_lens = sorted({s[1] for s in shapes})
    tabs = {sl: dft_tables(sl // N2) for sl in seq_lens}
    rope_even = {sl: rope_tables(sl, B_HEAD_DIM) for sl in seq_lens}
    rope_odd = {sl: rope_tables(sl, C_HEAD_DIM) for sl in seq_lens}
    for layer in range(depth):
        i = layer // 2
        if layer % 2 == 0:
            w_in_bf = _even_w_in(a_w_in[i])
            w_out_bf = a_w_out[i].astype(BF16)
            specs = {sl: hyena_spectrum(sl, tabs[sl], a_filt_w1[i], a_filt_b1[i], a_filt_f1[i], a_filt_w2[i],
                                        a_filt_b2[i], a_filt_f2[i], a_filt_w3[i]) for sl in seq_lens}
            acts = [hybrid_layer(x2d, b, sl, a_norm[i], w_in_bf, a_conv_w[i], a_conv_b[i], a_hyena_d[i], a_sink[i],
                                 w_out_bf, specs[sl], tabs[sl], rope_even[sl])
                    for x2d, (b, sl) in zip(acts, shapes)]
        else:
            w_in_bfs = _odd_w_in(c_w_in[i])
            w_out_bf = c_w_out[i].astype(BF16)
            final_g = final_norm if layer == depth - 1 else None
            acts = [dilated_layer(x2d, b, sl, c_norm[i], w_in_bfs, w_out_bf, rope_odd[sl], final_g)
                    for x2d, (b, sl) in zip(acts, shapes)]
    assert depth % 2 == 0
    return tuple(a.reshape(b, sl, D_MODEL) for a, (b, sl) in zip(acts, shapes))
```

```python
import functools
import math

import numpy as np
import jax
import jax.numpy as jnp
from jax import lax
from jax.experimental import pallas as pl
from jax.experimental.pallas import tpu as pltpu

F32 = jnp.float32
BF16 = jnp.bfloat16

D_MODEL = 1024
HC = 512
FILTER_BANDS = 16
DECAY_TARGET = 1e-2
FAST_DECAY_PCT = 0.3
SLOW_DECAY_PCT = 1.5
DECAY_SHIFT = 0.05
B_HEAD_DIM = 64
B_HEADS = 8
B_KV_HEADS = 2
B_GROUP = 4
B_BLOCK = 128
C_PATTERNS = ((128, 1), (512, 4), (2048, 16))
C_HEADS = 8
C_HEAD_DIM = 128
C_RADIUS = 64
ROPE_THETA = 500000.0
NORM_EPS = 1e-6
NEG_INF = -1e30
LOG2E = math.log2(math.e)

LANES = 128
SUBLANES = 8
N2 = 128
J2 = N2 // SUBLANES
SLABS = HC // LANES
K1_BLOCK = 8
VMEM_LIMIT = 56 * 1024 * 1024


def _cparams(n_axes):
    return pltpu.CompilerParams(dimension_semantics=("arbitrary",) * n_axes,
                                vmem_limit_bytes=VMEM_LIMIT)


def _rope_chunk(blk, cos, sa, sb, half):
    return (blk * cos + pltpu.roll(blk, LANES - half, 1) * sa + pltpu.roll(blk, half, 1) * sb)


def _norm_proj_kernel(x_ref, g_ref, w_ref, cos_ref, sa_ref, sb_ref, *outs, half, groups, row_chunk):
    g = g_ref[...]
    chunks = [slice(r0, r0 + row_chunk) for r0 in range(0, x_ref.shape[0], row_chunk)]
    accs = [jnp.dot(_rms_bf16(x_ref[rows, :], g), w_ref[...], preferred_element_type=F32) for rows in chunks]
    for rows, acc in zip(chunks, accs):
        col = 0
        for gi, (o_ref, (width, dtype)) in enumerate(zip(outs, groups)):
            if gi == 0:
                cos, sa, sb = cos_ref[rows, :], sa_ref[rows, :], sb_ref[rows, :]
                for ci in range(width // LANES):
                    src = slice(col + ci * LANES, col + (ci + 1) * LANES)
                    o_ref[rows, ci * LANES:(ci + 1) * LANES] = _rope_chunk(acc[:, src], cos, sa, sb, half).astype(dtype)
            else:
                o_ref[rows, :] = acc[:, col:col + width].astype(dtype)
            col += width


def norm_proj(x2d, g, w_bf, seq_len, rope, half, groups, *, tm=512, row_chunk=256):
    m, d = x2d.shape
    n = w_bf.shape[1]
    assert m % tm == 0 and seq_len % tm == 0 and sum(wd for wd, _ in groups) == n
    tiles_per_seq = seq_len // tm
    in_specs = [pl.BlockSpec((tm, d), lambda i: (i, 0)),
                pl.BlockSpec((1, d), lambda i: (0, 0)),
                pl.BlockSpec((d, n), lambda i: (0, 0))]
    in_specs += [pl.BlockSpec((tm, LANES), lambda i: (i % tiles_per_seq, 0))] * 3
    return pl.pallas_call(
        functools.partial(_norm_proj_kernel, half=half, groups=tuple(groups), row_chunk=row_chunk),
        grid=(m // tm,), in_specs=in_specs,
        out_specs=tuple(pl.BlockSpec((tm, wd), lambda i: (i, 0)) for wd, _ in groups),
        out_shape=tuple(jax.ShapeDtypeStruct((m, wd), dt) for wd, dt in groups),
        compiler_params=_cparams(1), name="norm_proj")(x2d, g.reshape(1, d), w_bf, *rope)


def _by_residue(ref, d):
    per = ref.shape[0] // d
    return jnp.concatenate([ref[pl.ds(r, per, stride=d), :] for r in range(d)], axis=0)


def _norm_proj_strided_kernel(*refs, half, groups, d, row_chunk):
    n_slab = D_MODEL // LANES
    x_slabs = refs[:n_slab]
    g_ref, w_ref, cos_ref, sa_ref, sb_ref = refs[n_slab:n_slab + 5]
    outs = refs[n_slab + 5:]
    tm = x_slabs[0].shape[0]
    per = tm // d
    x = jnp.concatenate([_by_residue(xs, d) for xs in x_slabs], axis=1)
    cos_t, sa_t, sb_t = _by_residue(cos_ref, d), _by_residue(sa_ref, d), _by_residue(sb_ref, d)
    g = g_ref[...]
    chunks = [slice(r0, r0 + row_chunk) for r0 in range(0, tm, row_chunk)]
    accs = [jnp.dot(_rms_bf16(x[rows], g), w_ref[...], preferred_element_type=F32) for rows in chunks]
    for rows, acc in zip(chunks, accs):
        r0 = rows.start
        col = 0
        for gi, (o_ref, (width, dtype)) in enumerate(zip(outs, groups)):
            if gi == 0:
                res = jnp.concatenate([_rope_chunk(acc[:, col + ci * LANES:col + (ci + 1) * LANES],
                                                   cos_t[rows], sa_t[rows], sb_t[rows], half)
                                       for ci in range(width // LANES)], axis=1).astype(dtype)
            else:
                res = acc[:, col:col + width].astype(dtype)
            for k in range(row_chunk // per):
                o_ref[r0 // per + k] = res[k * per:(k + 1) * per]
            col += width


def norm_proj_strided(x2d, g, w_bf, batch, seq_len, d, rope, half, groups, *, tm=512, row_chunk=256):
    m, dm = x2d.shape
    n = w_bf.shape[1]
    tps = seq_len // tm
    per = tm // d
    assert seq_len % tm == 0 and row_chunk % per == 0 and sum(wd for wd, _ in groups) == n
    in_specs = [pl.BlockSpec((tm, LANES), functools.partial(lambda c, t: (t, c), c)) for c in range(dm // LANES)]
    in_specs += [pl.BlockSpec((1, dm), lambda t: (0, 0)), pl.BlockSpec((dm, n), lambda t: (0, 0))]
    in_specs += [pl.BlockSpec((tm, LANES), lambda t: (t % tps, 0))] * 3
    return pl.pallas_call(
        functools.partial(_norm_proj_strided_kernel, half=half, groups=tuple(groups), d=d, row_chunk=row_chunk),
        grid=(m // tm,), in_specs=in_specs,
        out_specs=tuple(pl.BlockSpec((None, d, per, wd), lambda t: (t // tps, 0, t % tps, 0)) for wd, _ in groups),
        out_shape=tuple(jax.ShapeDtypeStruct((batch, d, seq_len // d, wd), dt) for wd, dt in groups),
        compiler_params=_cparams(1), name="norm_proj_strided")(*([x2d] * (dm // LANES)), g.reshape(1, dm), w_bf, *rope)


def _rms_bf16(x, g):
    ms = jnp.mean(x * x, axis=-1, keepdims=True)
    return (x * lax.rsqrt(ms + NORM_EPS) * g).astype(BF16)


_QK_W, _HY_W, _ATT_V_W = 640, 3 * HC, 128
_HY0 = _QK_W
_HYG0 = _HY0 + _HY_W
_ATG0 = _HYG0 + HC
_VATT0 = _ATG0 + HC


def _hybrid_proj_kernel(x_ref, xp_ref, xn_ref, g_ref, w_ref, cos_ref, sa_ref, sb_ref, cw_ref, cb_ref,
                        qk_ref, x1_ref, x2_ref, v_ref, atg_ref, vatt_ref, u_scr, *, half, tiles_per_seq, row_chunk):
    tm = x_ref.shape[0]
    i = pl.program_id(0) % tiles_per_seq
    g = g_ref[...]
    halo = jnp.concatenate([xp_ref[...], xn_ref[...]], axis=0)
    acc_h = jnp.dot(_rms_bf16(halo, g), w_ref[:, _HY0:_HYG0], preferred_element_type=F32)
    prev_row = jnp.where(i > 0, acc_h[SUBLANES - 1:SUBLANES], 0.0)
    next_row = jnp.where(i < tiles_per_seq - 1, acc_h[SUBLANES:SUBLANES + 1], 0.0)
    chunks = [slice(r0, r0 + row_chunk) for r0 in range(0, tm, row_chunk)]
    accs = [jnp.dot(_rms_bf16(x_ref[rows, :], g), w_ref[...], preferred_element_type=F32) for rows in chunks]
    for rows, acc in zip(chunks, accs):
        cos, sa, sb = cos_ref[rows, :], sa_ref[rows, :], sb_ref[rows, :]
        for ci in range(_QK_W // LANES):
            sl = slice(ci * LANES, (ci + 1) * LANES)
            qk_ref[rows, sl] = _rope_chunk(acc[:, sl], cos, sa, sb, half).astype(BF16)
        u_scr[rows, :] = acc[:, _HY0:_ATG0]
        atg_ref[rows, :] = acc[:, _ATG0:_VATT0]
        vatt_ref[rows, :] = acc[:, _VATT0:].astype(BF16)
    u = u_scr[:, :_HY_W]
    row = lax.broadcasted_iota(jnp.int32, u.shape, 0)
    um1 = jnp.where(row == 0, prev_row, pltpu.roll(u, 1, 0))
    up1 = jnp.where(row == tm - 1, next_row, pltpu.roll(u, tm - 1, 0))
    w = cw_ref[...]
    y = um1 * w[0:1, :] + u * w[1:2, :] + up1 * w[2:3, :] + cb_ref[...]
    _store_time_tiles(x1_ref, y[:, :HC])
    _store_time_tiles(x2_ref, y[:, HC:2 * HC] * _silu(u_scr[:, _HY_W:]))
    _store_time_tiles(v_ref, y[:, 2 * HC:])


def hybrid_proj(x2d, g, w_bf, conv_w, conv_b, batch, seq_len, rope, half, *, tm=512, row_chunk=256):
    m, d = x2d.shape
    n = w_bf.shape[1]
    tiles_per_seq = seq_len // tm
    sub = tm // SUBLANES
    last8 = m // SUBLANES - 1
    hy_shape = jax.ShapeDtypeStruct((batch, SLABS, J2, seq_len // N2, SUBLANES, LANES), F32)
    hy_spec = pl.BlockSpec((None, SLABS, J2, tm // N2, SUBLANES, LANES),
                           lambda t: (t // tiles_per_seq, 0, 0, t % tiles_per_seq, 0, 0))
    row_blk = lambda wd: pl.BlockSpec((tm, wd), lambda t: (t, 0))
    const = lambda shape: pl.BlockSpec(shape, lambda t: (0, 0))
    return pl.pallas_call(
        functools.partial(_hybrid_proj_kernel, half=half, tiles_per_seq=tiles_per_seq, row_chunk=row_chunk),
        grid=(m // tm,),
        in_specs=[row_blk(d),
                  pl.BlockSpec((SUBLANES, d), lambda t: (jnp.maximum(t * sub - 1, 0), 0)),
                  pl.BlockSpec((SUBLANES, d), lambda t: (jnp.minimum((t + 1) * sub, last8), 0)),
                  const((1, d)), const((d, n))]
                 + [pl.BlockSpec((tm, LANES), lambda t: (t % tiles_per_seq, 0))] * 3
                 + [const((3, _HY_W)), const((1, _HY_W))],
        out_specs=(row_blk(_QK_W), hy_spec, hy_spec, hy_spec, row_blk(HC), row_blk(_ATT_V_W)),
        out_shape=(jax.ShapeDtypeStruct((m, _QK_W), BF16), hy_shape, hy_shape, hy_shape,
                   jax.ShapeDtypeStruct((m, HC), F32), jax.ShapeDtypeStruct((m, _ATT_V_W), BF16)),
        scratch_shapes=[pltpu.VMEM((tm, _HY_W + HC), F32)],
        compiler_params=_cparams(1), name="hybrid_proj")(x2d, x2d, x2d, g.reshape(1, d), w_bf, *rope,
                                                         conv_w, conv_b.reshape(1, -1))


def rope_tables(seq_len, head_dim):
    rot = head_dim // 4
    half = rot // 2
    inv = jnp.power(ROPE_THETA, -2.0 * jnp.arange(half, dtype=F32) / rot)
    ang = jnp.arange(seq_len).astype(F32)[:, None] * inv[None, :]
    cos, sin = jnp.cos(ang), jnp.sin(ang)
    lane = np.arange(LANES) % head_dim
    idx = lane % half
    in_rot = jnp.asarray(lane < rot)[None, :]
    first = jnp.asarray(lane < half)[None, :]
    cos_t = jnp.where(in_rot, cos[:, idx], 1.0)
    sa = jnp.where(first, -sin[:, idx], 0.0)
    sb = jnp.where(in_rot & ~first, sin[:, idx], 0.0)
    return (cos_t, sa, sb), half


def _store_time_tiles(ref, val):
    for s in range(SLABS):
        for i in range(val.shape[0] // N2):
            ref[s, :, i] = val[i * N2:(i + 1) * N2, s * LANES:(s + 1) * LANES].reshape(J2, SUBLANES, LANES)


def _load_time_tiles(ref, nb):
    return jnp.concatenate([jnp.concatenate([ref[s, :, i].reshape(N2, LANES) for s in range(SLABS)], axis=1)
                            for i in range(nb)], axis=0)


def _filter_kernel(band_ref, w1_ref, b1_ref, f1_ref, w2_ref, b2_ref, f2_ref, w3_ref, dl_ref, o_ref,
                   *, seq_len, tr):
    n = pl.program_id(0) * tr + lax.broadcasted_iota(jnp.int32, (tr, 1), 0)
    j = jnp.where(n < seq_len, n, 2 * seq_len - n).astype(F32)
    t = j * (1.0 / (seq_len - 1))
    wpos = (2.0 * math.pi) * j / seq_len
    lane = lax.broadcasted_iota(jnp.int32, (tr, LANES), 1)
    arg = band_ref[...] * wpos
    feats = jnp.where(lane == 0, t,
                      jnp.where(lane <= FILTER_BANDS, jnp.cos(arg),
                                jnp.where(lane <= 2 * FILTER_BANDS, -jnp.sin(arg), 0.0)))
    h = jnp.dot(feats.astype(BF16), w1_ref[...], preferred_element_type=F32) + b1_ref[...]
    h = jnp.sin(f1_ref[...] * h)
    h = jnp.dot(h.astype(BF16), w2_ref[...], preferred_element_type=F32) + b2_ref[...]
    h = jnp.sin(f2_ref[...] * h)
    h = jnp.dot(h.astype(BF16), w3_ref[...], preferred_element_type=F32)
    win = jnp.exp(-t * dl_ref[...]) + DECAY_SHIFT
    win2 = jnp.concatenate([win, win], axis=1)
    fwd = h[:, :2 * HC] * win2
    bwd = h[:, 2 * HC:] * win2
    res = (jnp.where(n < seq_len, fwd, 0.0)
           + jnp.where(n == 0, bwd, 0.0) - jnp.where(n > seq_len, bwd, 0.0))
    for order in range(2):
        _store_time_tiles(o_ref.at[order * SLABS:(order + 1) * SLABS], res[:, order * HC:(order + 1) * HC])


def hyena_filter(seq_len, w1, b1, f1, w2, b2, f2, w3, *, tr=512):
    hid = w1.shape[1]
    pad = LANES - hid
    w1p = jnp.pad(w1, ((0, LANES - w1.shape[0]), (0, pad))).astype(BF16)
    w2p = jnp.pad(w2, ((0, pad), (0, pad))).astype(BF16)
    w3r = w3.reshape(hid, 2, 2, HC).transpose(0, 2, 1, 3).reshape(hid, 4 * HC)
    w3p = jnp.pad(w3r, ((0, pad), (0, 0))).astype(BF16)
    vec = lambda v: jnp.pad(v, (0, pad)).reshape(1, LANES)
    bands = jnp.linspace(1e-4, FILTER_BANDS - 1, FILTER_BANDS, dtype=F32)
    band_l = jnp.concatenate([jnp.zeros((1,), F32), bands, bands,
                              jnp.zeros((LANES - 1 - 2 * FILTER_BANDS,), F32)]).reshape(1, LANES)
    max_decay = math.log(DECAY_TARGET) / FAST_DECAY_PCT
    min_decay = math.log(DECAY_TARGET) / SLOW_DECAY_PCT
    deltas = jnp.abs(jnp.linspace(min_decay, max_decay, HC, dtype=F32)).reshape(1, HC)
    full = lambda a: pl.BlockSpec(a.shape, lambda i: (0, 0))
    args = [band_l, w1p, vec(b1), vec(f1), w2p, vec(b2), vec(f2), w3p, deltas]
    return pl.pallas_call(
        functools.partial(_filter_kernel, seq_len=seq_len, tr=tr),
        grid=(2 * seq_len // tr,),
        in_specs=[full(a) for a in args],
        out_specs=pl.BlockSpec((2 * SLABS, J2, tr // N2, SUBLANES, LANES), lambda i: (0, 0, i, 0, 0)),
        out_shape=jax.ShapeDtypeStruct((2 * SLABS, J2, 2 * seq_len // N2, SUBLANES, LANES), F32),
        compiler_params=_cparams(1), name="hyena_filter")(*args)


def dft_tables(n1h):
    n1 = 2 * n1h
    n = n1 * N2
    k1 = jnp.arange(n1h, dtype=jnp.int32)
    m1 = (jnp.arange(n1, dtype=jnp.int32)[None, :] * (2 * k1[:, None] + 1)) % (2 * n1)
    a1 = m1.astype(F32) * (math.pi / n1)
    c1, s1 = jnp.cos(a1), jnp.sin(a1)
    w1_full = jnp.concatenate([c1, -s1], axis=0).astype(BF16)
    w1_half = w1_full[:, :n1h]
    w1_inv = jnp.concatenate([c1[:, :n1h].T, -s1[:, :n1h].T], axis=1).astype(BF16)
    k2 = jnp.arange(N2, dtype=jnp.int32)
    n2 = jnp.arange(N2, dtype=jnp.int32)
    f = 2 * (k2[None, :, None] * n1 + k1[:, None, None]) + 1
    m2 = (n2[None, None, :] * f) % (2 * n)
    th = m2.astype(F32) * (math.pi / n)
    c, s = jnp.cos(th), jnp.sin(th)
    mat = jnp.concatenate([jnp.concatenate([c, s], axis=2),
                           jnp.concatenate([-s, c], axis=2)], axis=1)
    return w1_half, w1_full, w1_inv, mat.astype(BF16), jnp.swapaxes(mat, 1, 2).astype(BF16)


def _rows_of(ref2, m, n_rows):
    return jnp.concatenate([ref2[cb, pl.ds(m, n_rows, stride=SUBLANES), :] for cb in range(SLABS)], axis=1)


def _store_rows(ref2, m, val):
    for cb in range(SLABS):
        ref2[cb, pl.ds(m, val.shape[0], stride=SUBLANES), :] = val[:, cb * LANES:(cb + 1) * LANES]


def _stage1_kernel(w_ref, x_ref, o_ref, *, k, r):
    w = w_ref[...]
    for m in range(SUBLANES):
        _store_rows(o_ref, m, jnp.dot(w, _rows_of(x_ref, m, k).astype(BF16), preferred_element_type=F32))


def dft_stage1(w, x5):
    b, s = x5.shape[:2]
    r, k = w.shape
    assert s % SLABS == 0 and x5.shape[3] == k * SUBLANES
    blk = lambda rows: pl.BlockSpec((None, SLABS, None, rows * SUBLANES, LANES), lambda bi, si, j: (bi, si, j, 0, 0))
    return pl.pallas_call(
        functools.partial(_stage1_kernel, k=k, r=r), grid=(b, s // SLABS, J2),
        in_specs=[pl.BlockSpec((r, k), lambda bi, si, j: (0, 0)), blk(k)],
        out_specs=blk(r),
        out_shape=jax.ShapeDtypeStruct((b, s, J2, r * SUBLANES, LANES), F32),
        compiler_params=_cparams(3), name="dft_stage1")(w, x5)


def _slab_cat(a_ref, kk, n_slabs):
    return jnp.concatenate([jnp.concatenate([a_ref[cb, :, 0, kk].reshape(N2, LANES),
                                             a_ref[cb, :, 1, kk].reshape(N2, LANES)], axis=0)
                            for cb in range(n_slabs)], axis=1).astype(BF16)


def _stage2_filter_kernel(m_ref, a_ref, o_ref):
    for kk in range(K1_BLOCK):
        o_ref[kk] = jnp.dot(m_ref[kk], _slab_cat(a_ref, kk, 2 * SLABS), preferred_element_type=F32)


def dft_stage2_filter(mat, a7):
    n1h = a7.shape[4]
    return pl.pallas_call(
        _stage2_filter_kernel, grid=(n1h // K1_BLOCK,),
        in_specs=[pl.BlockSpec((K1_BLOCK, 2 * N2, 2 * N2), lambda kb: (kb, 0, 0)),
                  pl.BlockSpec((None, 2 * SLABS, J2, 2, K1_BLOCK, SUBLANES, LANES), lambda kb: (0, 0, 0, 0, kb, 0, 0))],
        out_specs=pl.BlockSpec((K1_BLOCK, 2 * N2, 2 * HC), lambda kb: (kb, 0, 0)),
        out_shape=jax.ShapeDtypeStruct((n1h, 2 * N2, 2 * HC), F32),
        compiler_params=_cparams(1), name="dft_stage2_filter")(mat, a7)


def _stage2_conv_kernel(m_ref, mt_ref, h_ref, a_ref, o_ref):
    for kk in range(K1_BLOCK):
        z = jnp.dot(m_ref[kk], _slab_cat(a_ref, kk, SLABS), preferred_element_type=F32)
        zr, zi = z[:N2], z[N2:]
        hr, hi = h_ref[kk, :N2], h_ref[kk, N2:]
        y = jnp.concatenate([zr * hr - zi * hi, zr * hi + zi * hr], axis=0).astype(BF16)
        g = jnp.dot(mt_ref[kk], y, preferred_element_type=F32)
        for cb in range(SLABS):
            o_ref[cb, :, 0, kk] = g[:N2, cb * LANES:(cb + 1) * LANES].reshape(J2, SUBLANES, LANES)
            o_ref[cb, :, 1, kk] = g[N2:, cb * LANES:(cb + 1) * LANES].reshape(J2, SUBLANES, LANES)


def dft_stage2_conv(mat, mat_t, spec, order, a7):
    b, n1h = a7.shape[0], a7.shape[4]
    ablk = pl.BlockSpec((None, SLABS, J2, 2, K1_BLOCK, SUBLANES, LANES), lambda kb, bi: (bi, 0, 0, 0, kb, 0, 0))
    return pl.pallas_call(
        _stage2_conv_kernel, grid=(n1h // K1_BLOCK, b),
        in_specs=[pl.BlockSpec((K1_BLOCK, 2 * N2, 2 * N2), lambda kb, bi: (kb, 0, 0)),
                  pl.BlockSpec((K1_BLOCK, 2 * N2, 2 * N2), lambda kb, bi: (kb, 0, 0)),
                  pl.BlockSpec((K1_BLOCK, 2 * N2, HC), lambda kb, bi: (kb, 0, order)),
                  ablk],
        out_specs=ablk,
        out_shape=jax.ShapeDtypeStruct(a7.shape, F32),
        compiler_params=_cparams(2), name="dft_stage2_conv")(mat, mat_t, spec, a7)


def _stage1_inv_kernel(w_ref, g_ref, z_ref, gate_ref, d_ref, o_ref, *, n1h, scale):
    w, d = w_ref[...], d_ref[...]
    for m in range(SUBLANES):
        y = jnp.dot(w, _rows_of(g_ref, m, 2 * n1h).astype(BF16), preferred_element_type=F32) * scale
        _store_rows(o_ref, m, _rows_of(gate_ref, m, n1h) * (y + d * _rows_of(z_ref, m, n1h)))


def dft_stage1_inv(w_inv, g5, z5, gate5, d_vec, *, scale):
    b = g5.shape[0]
    n1h = w_inv.shape[0]
    blk = lambda rows: pl.BlockSpec((None, SLABS, None, rows * SUBLANES, LANES), lambda bi, j: (bi, 0, j, 0, 0))
    return pl.pallas_call(
        functools.partial(_stage1_inv_kernel, n1h=n1h, scale=scale),
        grid=(b, J2),
        in_specs=[pl.BlockSpec((n1h, 2 * n1h), lambda bi, j: (0, 0)), blk(2 * n1h), blk(n1h), blk(n1h),
                  pl.BlockSpec((1, HC), lambda bi, j: (0, 0))],
        out_specs=blk(n1h),
        out_shape=jax.ShapeDtypeStruct(z5.shape, F32),
        compiler_params=_cparams(2), name="dft_stage1_inv")(w_inv, g5, z5, gate5, d_vec)


CONV_SCRATCH_BYTES = 16 * 1024 * 1024
CONV_J_BLOCK = 4


def _conv_fused_kernel(w1_ref, wi_ref, m_ref, mt_ref, h_ref, z_ref, gate_ref, d_ref, o_ref, a_scr,
                       *, n1h, sg, nb, scale):
    t = pl.program_id(2)
    n_a = J2 // CONV_J_BLOCK
    cat = lambda pieces: jnp.concatenate(pieces, axis=1) if len(pieces) > 1 else pieces[0]
    strided = lambda ref, idx, m, rows: cat([ref[(cb,) + idx + (pl.ds(m, rows, stride=SUBLANES), slice(None))]
                                             for cb in range(sg)])

    @pl.when(t < n_a)
    def _():
        w = w1_ref[...]
        for jj in range(CONV_J_BLOCK):
            j = t * CONV_J_BLOCK + jj
            for m in range(SUBLANES):
                res = jnp.dot(w, strided(z_ref, (jj,), m, n1h).astype(BF16), preferred_element_type=F32)
                for cb in range(sg):
                    a_scr[cb, j, pl.ds(m, 2 * n1h, stride=SUBLANES), :] = res[:, cb * LANES:(cb + 1) * LANES]

    @pl.when((t >= n_a) & (t < n_a + nb))
    def _():
        kb = t - n_a
        re0 = [pl.multiple_of((kb * K1_BLOCK + kk) * SUBLANES, SUBLANES) for kk in range(K1_BLOCK)]
        im0 = [pl.multiple_of((n1h + kb * K1_BLOCK + kk) * SUBLANES, SUBLANES) for kk in range(K1_BLOCK)]
        tile = lambda cb, r0: a_scr[cb, :, pl.ds(r0, SUBLANES), :].reshape(N2, LANES)
        zs = [jnp.dot(m_ref[kk], cat([jnp.concatenate([tile(cb, re0[kk]), tile(cb, im0[kk])], axis=0)
                                      for cb in range(sg)]).astype(BF16), preferred_element_type=F32)
              for kk in range(K1_BLOCK)]
        ys = []
        for kk, z in enumerate(zs):
            zr, zi = z[:N2], z[N2:]
            hr, hi = h_ref[kk, :N2], h_ref[kk, N2:]
            ys.append(jnp.concatenate([zr * hr - zi * hi, zr * hi + zi * hr], axis=0).astype(BF16))
        gs = [jnp.dot(mt_ref[kk], y, preferred_element_type=F32) for kk, y in enumerate(ys)]
        for kk, g in enumerate(gs):
            for cb in range(sg):
                lanes = slice(cb * LANES, (cb + 1) * LANES)
                a_scr[cb, :, pl.ds(re0[kk], SUBLANES), :] = g[:N2, lanes].reshape(J2, SUBLANES, LANES)
                a_scr[cb, :, pl.ds(im0[kk], SUBLANES), :] = g[N2:, lanes].reshape(J2, SUBLANES, LANES)

    @pl.when(t >= n_a + nb)
    def _():
        w, d = wi_ref[...], d_ref[...]
        for jj in range(CONV_J_BLOCK):
            j = (t - n_a - nb) * CONV_J_BLOCK + jj
            for m in range(SUBLANES):
                gm = cat([a_scr[cb, j, pl.ds(m, 2 * n1h, stride=SUBLANES), :] for cb in range(sg)]).astype(BF16)
                y = jnp.dot(w, gm, preferred_element_type=F32) * scale
                res = strided(gate_ref, (jj,), m, n1h) * (y + d * strided(z_ref, (jj,), m, n1h))
                for cb in range(sg):
                    o_ref[cb, jj, pl.ds(m, n1h, stride=SUBLANES), :] = res[:, cb * LANES:(cb + 1) * LANES]


def hyena_conv_fused(z, gate, d_vec, spec, order, tabs):
    batch, n1h = z.shape[0], z.shape[3]
    w1_half, _, w1_inv, mat, mat_t = tabs
    sg = max(1, min(SLABS, CONV_SCRATCH_BYTES // (J2 * 2 * n1h * SUBLANES * LANES * 4)))
    nb = n1h // K1_BLOCK
    n_a = J2 // CONV_J_BLOCK
    rows = lambda a: a.reshape(batch, SLABS, J2, n1h * SUBLANES, LANES)
    phase_c = lambda t: jnp.maximum(t - n_a - nb, 0)
    z_idx = lambda t: jnp.where(t < n_a + nb, jnp.minimum(t, n_a - 1), t - n_a - nb)
    tile = lambda f: pl.BlockSpec((None, sg, CONV_J_BLOCK, n1h * SUBLANES, LANES), lambda b, s, t: (b, s, f(t), 0, 0))
    kblk = lambda t: jnp.clip(t - n_a, 0, nb - 1)
    const = lambda a: pl.BlockSpec(a.shape, lambda b, s, t: (0, 0))
    mspec = pl.BlockSpec((K1_BLOCK, 2 * N2, 2 * N2), lambda b, s, t: (kblk(t), 0, 0))
    out = pl.pallas_call(
        functools.partial(_conv_fused_kernel, n1h=n1h, sg=sg, nb=nb, scale=1.0 / (n1h * N2)),
        grid=(batch, SLABS // sg, 2 * n_a + nb),
        in_specs=[const(w1_half), const(w1_inv), mspec, mspec,
                  pl.BlockSpec((K1_BLOCK, 2 * N2, sg * LANES),
                               lambda b, s, t: (kblk(t), 0, order * (SLABS // sg) + s)),
                  tile(z_idx), tile(phase_c),
                  pl.BlockSpec((1, sg * LANES), lambda b, s, t: (0, s))],
        out_specs=tile(phase_c),
        out_shape=jax.ShapeDtypeStruct((batch, SLABS, J2, n1h * SUBLANES, LANES), F32),
        scratch_shapes=[pltpu.VMEM((sg, J2, 2 * n1h * SUBLANES, LANES), F32)],
        compiler_params=_cparams(3), name="hyena_conv")(w1_half, w1_inv, mat, mat_t, spec, rows(z), rows(gate),
                                                        d_vec.reshape(1, HC))
    return out.reshape(z.shape)


def _silu(g):
    return g / (1.0 + jnp.exp(-g))


def _band_mask(t, halo, i, nblk):
    qi = lax.broadcasted_iota(jnp.int32, (t, t + 2 * halo), 0)
    kj = lax.broadcasted_iota(jnp.int32, (t, t + 2 * halo), 1)
    rel = kj - halo - qi
    return ((jnp.abs(rel) <= halo) & ((kj >= halo) | (i > 0)) & ((kj < halo + t) | (i < nblk - 1)))


def _head_pair_operands(win):
    x = win.astype(F32)
    swapped = pltpu.roll(x, B_HEAD_DIM, 1)
    low = lax.broadcasted_iota(jnp.int32, x.shape, 1) < B_HEAD_DIM
    place = lambda cond, val: jnp.where(cond, val, 0.0).astype(BF16)
    return ((place(low, x), place(~low, swapped)), (place(low, swapped), place(~low, x)))


def _band_attn_kernel(sink_ref, q_ref, kp_ref, kc_ref, kn_ref, vp_ref, vc_ref, vn_ref, gate_ref, o_ref, *, nsub):
    t, hd = B_BLOCK, B_HEAD_DIM
    kpads = _head_pair_operands(jnp.concatenate([kp_ref[...], kc_ref[...], kn_ref[...]], axis=0))
    vpads = _head_pair_operands(jnp.concatenate([vp_ref[...], vc_ref[...], vn_ref[...]], axis=0))
    n_blocks = pl.num_programs(1) * nsub
    heads = [(pair, par) for pair in range(B_HEADS // 2) for par in range(2)]
    for u in range(nsub):
        valid = _band_mask(t, t, pl.program_id(1) * nsub + u, n_blocks)
        rows = slice(u * t, (u + 1) * t)
        win = slice(u * t, (u + 3) * t)
        scores = [lax.dot_general(q_ref[rows, pair * LANES:(pair + 1) * LANES], kpads[(2 * pair) // B_GROUP][par][win],
                                  (((1,), (1,)), ((), ())), preferred_element_type=F32) for pair, par in heads]
        probs, denoms = [], []
        scale = hd ** -0.5
        for (pair, par), s in zip(heads, scores):
            s = jnp.where(valid, s, NEG_INF)
            sink = sink_ref[2 * pair + par]
            m_raw = jnp.maximum(jnp.max(s, axis=-1, keepdims=True), sink * (1.0 / scale))
            p = jnp.exp2((s - m_raw) * (scale * LOG2E))
            denoms.append(jnp.sum(p, axis=-1, keepdims=True) + jnp.exp2((sink * (1.0 / scale) - m_raw) * (scale * LOG2E)))
            probs.append(p.astype(BF16))
        outs = [jnp.dot(p, vpads[(2 * pair) // B_GROUP][par][win], preferred_element_type=F32) / d
                for (pair, par), p, d in zip(heads, probs, denoms)]
        for pair in range(B_HEADS // 2):
            lanes = slice(pair * LANES, (pair + 1) * LANES)
            o_ref[rows, lanes] = ((outs[2 * pair] + outs[2 * pair + 1]) * _silu(gate_ref[rows, lanes])).astype(BF16)


def band_attn(qk, v_att, at_gate, sink, batch, seq_len, *, nsub=4):
    t = B_BLOCK
    tq = t * nsub
    nblk = seq_len // t
    qk3 = qk.reshape(batch, seq_len, qk.shape[1])
    v3 = v_att.reshape(batch, seq_len, LANES)
    g3 = at_gate.reshape(batch, seq_len, at_gate.shape[1])
    k_col = B_HEADS * B_HEAD_DIM // LANES
    halo = lambda col, f: pl.BlockSpec((None, t, LANES), lambda b, i: (b, f(i), col))
    cur = lambda col: pl.BlockSpec((None, tq, LANES), lambda b, i: (b, i, col))
    prev = lambda i: jnp.maximum(i * nsub - 1, 0)
    nxt = lambda i: jnp.minimum((i + 1) * nsub, nblk - 1)
    wide = pl.BlockSpec((None, tq, B_HEADS * B_HEAD_DIM), lambda b, i: (b, i, 0))
    out = pl.pallas_call(
        functools.partial(_band_attn_kernel, nsub=nsub), grid=(batch, seq_len // tq),
        in_specs=[pl.BlockSpec(memory_space=pltpu.SMEM), wide,
                  halo(k_col, prev), cur(k_col), halo(k_col, nxt), halo(0, prev), cur(0), halo(0, nxt), wide],
        out_specs=wide,
        out_shape=jax.ShapeDtypeStruct((batch, seq_len, B_HEADS * B_HEAD_DIM), BF16),
        compiler_params=_cparams(2), name="band_attn")(sink, qk3, qk3, qk3, qk3, v3, v3, v3, g3)
    return out.reshape(batch * seq_len, -1)


def _dil_attn_kernel(q_ref, kp_ref, kc_ref, kn_ref, vp_ref, vc_ref, vn_ref, o_ref, lse_ref, *, nsub):
    t, hd, halo = 2 * C_RADIUS, C_HEAD_DIM, C_RADIUS
    n_blocks = pl.num_programs(1) * nsub
    lane = lax.broadcasted_iota(jnp.int32, (t, LANES), 1)
    for u in range(nsub):
        valid = _band_mask(t, halo, pl.program_id(1) * nsub + u, n_blocks)
        rows = slice(u * t, (u + 1) * t)
        win = slice(u * t, (u + 1) * t + 2 * halo)
        cols = [slice(h * hd, (h + 1) * hd) for h in range(C_HEADS)]
        window = lambda p_ref, c_ref, n_ref, sl: jnp.concatenate([p_ref[:, sl], c_ref[:, sl], n_ref[:, sl]], axis=0)[win]
        scores = [lax.dot_general(q_ref[rows, sl], window(kp_ref, kc_ref, kn_ref, sl), (((1,), (1,)), ((), ())),
                                  preferred_element_type=F32) for sl in cols]
        probs, denoms = [], []
        lse_tile = jnp.zeros((t, LANES), F32)
        scale = hd ** -0.5
        for h, s in enumerate(scores):
            s = jnp.where(valid, s, NEG_INF)
            m_raw = jnp.max(s, axis=-1, keepdims=True)
            p = jnp.exp2((s - m_raw) * (scale * LOG2E))
            denom = jnp.sum(p, axis=-1, keepdims=True)
            lse_tile = jnp.where(lane == h, m_raw * scale + jnp.log(denom), lse_tile)
            probs.append(p.astype(BF16))
            denoms.append(denom)
        for sl, p, denom in zip(cols, probs, denoms):
            o = jnp.dot(p, window(vp_ref, vc_ref, vn_ref, sl), preferred_element_type=F32) / denom
            o_ref[rows, sl] = o.astype(BF16)
        lse_ref[rows, :] = lse_tile


def dil_attn(qk, v, n_seq, ls):
    w = C_HEADS * C_HEAD_DIM
    t = 2 * C_RADIUS
    nsub = next(n for n in (4, 2, 1) if ls % (n * t) == 0)
    tq = t * nsub
    n_halo = ls // C_RADIUS
    qk3 = qk.reshape(n_seq, ls, qk.shape[1])
    v3 = v.reshape(n_seq, ls, w)
    prev = lambda i: jnp.maximum(i * (tq // C_RADIUS) - 1, 0)
    nxt = lambda i: jnp.minimum((i + 1) * (tq // C_RADIUS), n_halo - 1)
    big = lambda col: pl.BlockSpec((None, tq, w), lambda b, i: (b, i, col))
    halo = lambda col, f: pl.BlockSpec((None, C_RADIUS, w), lambda b, i: (b, f(i), col))
    o, lse = pl.pallas_call(
        functools.partial(_dil_attn_kernel, nsub=nsub), grid=(n_seq, ls // tq),
        in_specs=[big(0), halo(1, prev), big(1), halo(1, nxt), halo(0, prev), big(0), halo(0, nxt)],
        out_specs=(big(0), pl.BlockSpec((None, tq, LANES), lambda b, i: (b, i, 0))),
        out_shape=(jax.ShapeDtypeStruct((n_seq, ls, w), BF16),
                   jax.ShapeDtypeStruct((n_seq, ls, LANES), F32)),
        compiler_params=_cparams(2), name="dil_attn")(qk3, qk3, qk3, qk3, v3, v3, v3)
    return o.reshape(n_seq * ls, w), lse.reshape(n_seq * ls, LANES)


def _out_proj_even_kernel(hy_ref, at_ref, x_ref, w_ref, o_ref, *, nb):
    mixed = jnp.concatenate([_load_time_tiles(hy_ref, nb).astype(BF16), at_ref[...]], axis=1)
    o_ref[...] = x_ref[...] + jnp.dot(mixed, w_ref[...], preferred_element_type=F32)


def out_proj_even(hy, at, x2d, w_bf, *, tm=512):
    m, d = x2d.shape
    batch, n1h = hy.shape[0], hy.shape[3]
    tps = n1h * N2 // tm
    nb = tm // N2
    half = pl.BlockSpec((tm, HC), lambda b, i: (b * tps + i, 0))
    full = pl.BlockSpec((tm, d), lambda b, i: (b * tps + i, 0))
    return pl.pallas_call(
        functools.partial(_out_proj_even_kernel, nb=nb), grid=(batch, tps),
        in_specs=[pl.BlockSpec((None, SLABS, J2, nb, SUBLANES, LANES), lambda b, i: (b, 0, 0, i, 0, 0)), half, full,
                  pl.BlockSpec((d, d), lambda b, i: (0, 0))],
        out_specs=full, out_shape=jax.ShapeDtypeStruct((m, d), F32),
        compiler_params=_cparams(2), name="out_proj_even")(hy, at, x2d, w_bf)


def _merge_out_kernel(*refs, final, dilations):
    nh = C_HEADS
    o_nat, lse_nat = refs[0], refs[1]
    pos = 2
    strided = []
    for _ in dilations:
        strided.append((refs[pos:pos + nh], refs[pos + nh]))
        pos += nh + 1
    gate_ref, x_ref, w_ref = refs[pos:pos + 3]
    pos += 3
    g_ref = refs[pos] if final else None
    out_ref, o_scr, l_scr = refs[-3:]
    for gi, (d, (o_slabs, lse_ref)) in enumerate(zip(dilations, strided)):
        per = o_nat.shape[0] // d
        for r in range(d):
            l_scr[gi, pl.ds(r, per, stride=d), :] = lse_ref[r]
            for h in range(nh):
                o_scr[gi, h, pl.ds(r, per, stride=d), :] = o_slabs[h][r].astype(F32)
    ls = [lse_nat[...]] + [l_scr[gi] for gi in range(len(dilations))]
    mx = functools.reduce(jnp.maximum, ls)
    es = [jnp.exp(l - mx) for l in ls]
    den = functools.reduce(lambda a, b: a + b, es)
    alphas = [e / den for e in es]
    hd = C_HEAD_DIM
    parts = []
    for h in range(nh):
        acc = alphas[0][:, h:h + 1] * o_nat[:, h * hd:(h + 1) * hd].astype(F32)
        for gi in range(len(dilations)):
            acc = acc + alphas[gi + 1][:, h:h + 1] * o_scr[gi, h]
        parts.append(acc)
    y = (jnp.concatenate(parts, axis=1) * _silu(gate_ref[...])).astype(BF16)
    r = x_ref[...] + jnp.dot(y, w_ref[...], preferred_element_type=F32)
    if final:
        ms = jnp.mean(r * r, axis=-1, keepdims=True)
        r = r * lax.rsqrt(ms + NORM_EPS) * g_ref[...]
    out_ref[...] = r


def merge_out(o_nat, lse_nat, strided, gate, x2d, w_bf, batch, seq_len, final_g=None, *, tm=512):
    m, dm = x2d.shape
    tps = seq_len // tm
    nat = lambda wd: pl.BlockSpec((tm, wd), lambda b, i: (b * tps + i, 0))
    in_specs, args = [nat(dm), nat(LANES)], [o_nat, lse_nat]
    for d, o, lse in strided:
        per = tm // d
        in_specs += [pl.BlockSpec((None, d, per, LANES), functools.partial(lambda h, b, i: (b, 0, i, h), h))
                     for h in range(C_HEADS)]
        in_specs.append(pl.BlockSpec((None, d, per, LANES), lambda b, i: (b, 0, i, 0)))
        args += [o] * C_HEADS + [lse]
    in_specs += [nat(dm), nat(dm), pl.BlockSpec((dm, dm), lambda b, i: (0, 0))]
    args += [gate, x2d, w_bf]
    if final_g is not None:
        in_specs.append(pl.BlockSpec((1, dm), lambda b, i: (0, 0)))
        args.append(final_g.reshape(1, dm))
    n_str = len(strided)
    return pl.pallas_call(
        functools.partial(_merge_out_kernel, final=final_g is not None, dilations=tuple(d for d, _, _ in strided)),
        grid=(batch, tps), in_specs=in_specs, out_specs=nat(dm), out_shape=jax.ShapeDtypeStruct((m, dm), F32),
        scratch_shapes=[pltpu.VMEM((n_str, C_HEADS, tm, LANES), F32), pltpu.VMEM((n_str, tm, LANES), F32)],
        compiler_params=_cparams(2), name="merge_out")(*args)


def _even_w_in(w_in):
    return jnp.concatenate([w_in[:, 2048:2688], w_in[:, :2048], w_in[:, 2816:], w_in[:, 2688:2816]], axis=1).astype(BF16)


def _odd_w_in(w_in):
    w = C_HEADS * C_HEAD_DIM
    blk = lambda g, t: w_in[:, (3 * g + t) * w:(3 * g + t + 1) * w]
    per = [[blk(g, 0), blk(g, 1), blk(g, 2)] for g in range(len(C_PATTERNS))]
    per[0].append(w_in[:, 9 * w:])
    return [jnp.concatenate(cols, axis=1).astype(BF16) for cols in per]


def hyena_spectrum(seq_len, tabs, fw1, fb1, ff1, fw2, fb2, ff2, fw3):
    n1h = seq_len // N2
    _, w1_full, _, mat, _ = tabs
    filt = hyena_filter(seq_len, fw1, fb1, ff1, fw2, fb2, ff2, fw3)
    a = dft_stage1(w1_full, filt.reshape(1, 2 * SLABS, J2, 2 * n1h * SUBLANES, LANES))
    return dft_stage2_filter(mat, a.reshape(1, 2 * SLABS, J2, 2, n1h, SUBLANES, LANES))


def hyena_conv(z, gate, d_vec, spec, order, tabs):
    batch, n1h = z.shape[0], z.shape[3]
    w1_half, _, w1_inv, mat, mat_t = tabs
    rows = lambda a, r: a.reshape(batch, SLABS, J2, r * SUBLANES, LANES)
    z5 = rows(z, n1h)
    a = dft_stage1(w1_half, z5)
    g = dft_stage2_conv(mat, mat_t, spec, order, a.reshape(batch, SLABS, J2, 2, n1h, SUBLANES, LANES))
    out = dft_stage1_inv(w1_inv, rows(g, 2 * n1h), z5, rows(gate, n1h), d_vec.reshape(1, HC), scale=1.0 / (n1h * N2))
    return out.reshape(z.shape)


def hybrid_layer(x2d, batch, seq_len, norm_g, w_in_bf, conv_w, conv_b, hyena_d, sink, w_out_bf, spec, tabs, rope):
    rope_t, half = rope
    qk, x1, x2s, v, at_gate, v_att = hybrid_proj(x2d, norm_g, w_in_bf, conv_w, conv_b, batch, seq_len, rope_t, half)
    z = hyena_conv_fused(v, x1, hyena_d[0], spec, 0, tabs)
    z = hyena_conv_fused(z, x2s, hyena_d[1], spec, 1, tabs)
    at = band_attn(qk, v_att, at_gate, sink, batch, seq_len)
    return out_proj_even(z, at, x2d, w_out_bf)


def dilated_layer(x2d, batch, seq_len, norm_g, w_in_bfs, w_out_bf, rope, final_g):
    w = C_HEADS * C_HEAD_DIM
    m = batch * seq_len
    rope_t, half = rope
    o_nat = lse_nat = gate = None
    strided = []
    for gi, (_, d) in enumerate(C_PATTERNS):
        ls = seq_len // d
        if d == 1:
            qk, v, gate = norm_proj(x2d, norm_g, w_in_bfs[gi], seq_len, rope_t, half,
                                    ((2 * w, BF16), (w, BF16), (w, F32)))
            o_nat, lse_nat = dil_attn(qk, v, batch, ls)
        else:
            qk, v = norm_proj_strided(x2d, norm_g, w_in_bfs[gi], batch, seq_len, d, rope_t, half,
                                      ((2 * w, BF16), (w, BF16)))
            o, lse = dil_attn(qk.reshape(m, 2 * w), v.reshape(m, w), batch * d, ls)
            strided.append((d, o.reshape(batch, d, ls, w), lse.reshape(batch, d, ls, LANES)))
    return merge_out(o_nat, lse_nat, strided, gate, x2d, w_out_bf, batch, seq_len, final_g)


def kernel(x_prompt, x_sample, a_norm, a_w_in, a_conv_w, a_conv_b, a_filt_w1, a_filt_b1, a_filt_f1, a_filt_w2, a_filt_b2, a_filt_f2, a_filt_w3, a_hyena_d, a_sink, a_w_out, c_norm, c_w_in, c_w_out, final_norm):
    depth = a_norm.shape[0] + c_norm.shape[0]
    xs = [x_prompt, x_sample]
    shapes = [(x.shape[0], x.shape[1]) for x in xs]
    acts = [x.reshape(-1, D_MODEL) for x in xs]
    seq_lens = sorted({s[1] for s in shapes})
    tabs = {sl: dft_tables(sl // N2) for sl in seq_lens}
    rope_even = {sl: rope_tables(sl, B_HEAD_DIM) for sl in seq_lens}
    rope_odd = {sl: rope_tables(sl, C_HEAD_DIM) for sl in seq_lens}
    for layer in range(depth):
        i = layer // 2
        if layer % 2 == 0:
            w_in_bf = _even_w_in(a_w_in[i])
            w_out_bf = a_w_out[i].astype(BF16)
            specs = {sl: hyena_spectrum(sl, tabs[sl], a_filt_w1[i], a_filt_b1[i], a_filt_f1[i], a_filt_w2[i],
                                        a_filt_b2[i], a_filt_f2[i], a_filt_w3[i]) for sl in seq_lens}
            acts = [hybrid_layer(x2d, b, sl, a_norm[i], w_in_bf, a_conv_w[i], a_conv_b[i], a_hyena_d[i], a_sink[i],
                                 w_out_bf, specs[sl], tabs[sl], rope_even[sl])
                    for x2d, (b, sl) in zip(acts, shapes)]
        else:
            w_in_bfs = _odd_w_in(c_w_in[i])
            w_out_bf = c_w_out[i].astype(BF16)
            final_g = final_norm if layer == depth - 1 else None
            acts = [dilated_layer(x2d, b, sl, c_norm[i], w_in_bfs, w_out_bf, rope_odd[sl], final_g)
                    for x2d, (b, sl) in zip(acts, shapes)]
    assert depth % 2 == 0
    return tuple(a.reshape(b, sl, D_MODEL) for a, (b, sl) in zip(acts, shapes))
```

```python
import functools
import math

import numpy as np
import jax
import jax.numpy as jnp
from jax import lax
from jax.experimental import pallas as pl
from jax.experimental.pallas import tpu as pltpu

F32 = jnp.float32
BF16 = jnp.bfloat16

D_MODEL = 1024
HC = 512
FILTER_BANDS = 16
DECAY_TARGET = 1e-2
FAST_DECAY_PCT = 0.3
SLOW_DECAY_PCT = 1.5
DECAY_SHIFT = 0.05
B_HEAD_DIM = 64
B_HEADS = 8
B_KV_HEADS = 2
B_GROUP = 4
B_BLOCK = 128
C_PATTERNS = ((128, 1), (512, 4), (2048, 16))
C_HEADS = 8
C_HEAD_DIM = 128
C_RADIUS = 64
ROPE_THETA = 500000.0
NORM_EPS = 1e-6
NEG_INF = -1e30
LOG2E = math.log2(math.e)

LANES = 128
SUBLANES = 8
N2 = 128
J2 = N2 // SUBLANES
SLABS = HC // LANES
K1_BLOCK = 8
ATTN_SUB = 128
VMEM_LIMIT = 56 * 1024 * 1024


def _cparams(n_axes):
    return pltpu.CompilerParams(dimension_semantics=("arbitrary",) * n_axes,
                                vmem_limit_bytes=VMEM_LIMIT)


def _rope_chunk(blk, cos, sa, sb, half):
    return (blk * cos + pltpu.roll(blk, LANES - half, 1) * sa + pltpu.roll(blk, half, 1) * sb)


def _norm_proj_kernel(x_ref, g_ref, w_ref, cos_ref, sa_ref, sb_ref, *outs, half, groups, row_chunk):
    g = g_ref[...]
    chunks = [slice(r0, r0 + row_chunk) for r0 in range(0, x_ref.shape[0], row_chunk)]
    accs = [jnp.dot(_rms_bf16(x_ref[rows, :], g), w_ref[...], preferred_element_type=F32) for rows in chunks]
    for rows, acc in zip(chunks, accs):
        col = 0
        for gi, (o_ref, (width, dtype)) in enumerate(zip(outs, groups)):
            if gi == 0:
                cos, sa, sb = cos_ref[rows, :], sa_ref[rows, :], sb_ref[rows, :]
                for ci in range(width // LANES):
                    src = slice(col + ci * LANES, col + (ci + 1) * LANES)
                    o_ref[rows, ci * LANES:(ci + 1) * LANES] = _rope_chunk(acc[:, src], cos, sa, sb, half).astype(dtype)
            else:
                o_ref[rows, :] = acc[:, col:col + width].astype(dtype)
            col += width


def norm_proj(x2d, g, w_bf, seq_len, rope, half, groups, *, tm=512, row_chunk=256):
    m, d = x2d.shape
    n = w_bf.shape[1]
    assert m % tm == 0 and seq_len % tm == 0 and sum(wd for wd, _ in groups) == n
    tiles_per_seq = seq_len // tm
    in_specs = [pl.BlockSpec((tm, d), lambda i: (i, 0)),
                pl.BlockSpec((1, d), lambda i: (0, 0)),
                pl.BlockSpec((d, n), lambda i: (0, 0))]
    in_specs += [pl.BlockSpec((tm, LANES), lambda i: (i % tiles_per_seq, 0))] * 3
    return pl.pallas_call(
        functools.partial(_norm_proj_kernel, half=half, groups=tuple(groups), row_chunk=row_chunk),
        grid=(m // tm,), in_specs=in_specs,
        out_specs=tuple(pl.BlockSpec((tm, wd), lambda i: (i, 0)) for wd, _ in groups),
        out_shape=tuple(jax.ShapeDtypeStruct((m, wd), dt) for wd, dt in groups),
        compiler_params=_cparams(1), name="norm_proj")(x2d, g.reshape(1, d), w_bf, *rope)


def _by_residue(ref, d):
    per = ref.shape[0] // d
    return jnp.concatenate([ref[pl.ds(r, per, stride=d), :] for r in range(d)], axis=0)


def _norm_proj_strided_kernel(*refs, half, groups, d, row_chunk):
    n_slab = D_MODEL // LANES
    x_slabs = refs[:n_slab]
    g_ref, w_ref, cos_ref, sa_ref, sb_ref = refs[n_slab:n_slab + 5]
    outs = refs[n_slab + 5:]
    tm = x_slabs[0].shape[0]
    per = tm // d
    x = jnp.concatenate([_by_residue(xs, d) for xs in x_slabs], axis=1)
    cos_t, sa_t, sb_t = _by_residue(cos_ref, d), _by_residue(sa_ref, d), _by_residue(sb_ref, d)
    g = g_ref[...]
    chunks = [slice(r0, r0 + row_chunk) for r0 in range(0, tm, row_chunk)]
    accs = [jnp.dot(_rms_bf16(x[rows], g), w_ref[...], preferred_element_type=F32) for rows in chunks]
    for rows, acc in zip(chunks, accs):
        r0 = rows.start
        col = 0
        for gi, (o_ref, (width, dtype)) in enumerate(zip(outs, groups)):
            if gi == 0:
                res = jnp.concatenate([_rope_chunk(acc[:, col + ci * LANES:col + (ci + 1) * LANES],
                                                   cos_t[rows], sa_t[rows], sb_t[rows], half)
                                       for ci in range(width // LANES)], axis=1).astype(dtype)
            else:
                res = acc[:, col:col + width].astype(dtype)
            for k in range(row_chunk // per):
                o_ref[r0 // per + k] = res[k * per:(k + 1) * per]
            col += width


def norm_proj_strided(x2d, g, w_bf, batch, seq_len, d, rope, half, groups, *, tm=512, row_chunk=256):
    m, dm = x2d.shape
    n = w_bf.shape[1]
    tps = seq_len // tm
    per = tm // d
    assert seq_len % tm == 0 and row_chunk % per == 0 and sum(wd for wd, _ in groups) == n
    in_specs = [pl.BlockSpec((tm, LANES), functools.partial(lambda c, t: (t, c), c)) for c in range(dm // LANES)]
    in_specs += [pl.BlockSpec((1, dm), lambda t: (0, 0)), pl.BlockSpec((dm, n), lambda t: (0, 0))]
    in_specs += [pl.BlockSpec((tm, LANES), lambda t: (t % tps, 0))] * 3
    return pl.pallas_call(
        functools.partial(_norm_proj_strided_kernel, half=half, groups=tuple(groups), d=d, row_chunk=row_chunk),
        grid=(m // tm,), in_specs=in_specs,
        out_specs=tuple(pl.BlockSpec((None, d, per, wd), lambda t: (t // tps, 0, t % tps, 0)) for wd, _ in groups),
        out_shape=tuple(jax.ShapeDtypeStruct((batch, d, seq_len // d, wd), dt) for wd, dt in groups),
        compiler_params=_cparams(1), name="norm_proj_strided")(*([x2d] * (dm // LANES)), g.reshape(1, dm), w_bf, *rope)


def _rms_bf16(x, g):
    ms = jnp.mean(x * x, axis=-1, keepdims=True)
    return (x * lax.rsqrt(ms + NORM_EPS) * g).astype(BF16)


_QK_W, _HY_W, _ATT_V_W = 640, 3 * HC, 128
_HY0 = _QK_W
_HYG0 = _HY0 + _HY_W
_ATG0 = _HYG0 + HC
_VATT0 = _ATG0 + HC


def _hybrid_proj_kernel(x_ref, xp_ref, xn_ref, g_ref, w_ref, cos_ref, sa_ref, sb_ref, cw_ref, cb_ref,
                        qk_ref, x1_ref, x2_ref, v_ref, atg_ref, vatt_ref, u_scr, *, half, tiles_per_seq, row_chunk):
    tm = x_ref.shape[0]
    i = pl.program_id(0) % tiles_per_seq
    g = g_ref[...]
    halo = jnp.concatenate([xp_ref[...], xn_ref[...]], axis=0)
    acc_h = jnp.dot(_rms_bf16(halo, g), w_ref[:, _HY0:_HYG0], preferred_element_type=F32)
    prev_row = jnp.where(i > 0, acc_h[SUBLANES - 1:SUBLANES], 0.0)
    next_row = jnp.where(i < tiles_per_seq - 1, acc_h[SUBLANES:SUBLANES + 1], 0.0)
    chunks = [slice(r0, r0 + row_chunk) for r0 in range(0, tm, row_chunk)]
    accs = [jnp.dot(_rms_bf16(x_ref[rows, :], g), w_ref[...], preferred_element_type=F32) for rows in chunks]
    for rows, acc in zip(chunks, accs):
        cos, sa, sb = cos_ref[rows, :], sa_ref[rows, :], sb_ref[rows, :]
        for ci in range(_QK_W // LANES):
            sl = slice(ci * LANES, (ci + 1) * LANES)
            qk_ref[rows, sl] = _rope_chunk(acc[:, sl], cos, sa, sb, half).astype(BF16)
        u_scr[rows, :] = acc[:, _HY0:_ATG0]
        atg_ref[rows, :] = acc[:, _ATG0:_VATT0]
        vatt_ref[rows, :] = acc[:, _VATT0:].astype(BF16)
    u = u_scr[:, :_HY_W]
    row = lax.broadcasted_iota(jnp.int32, u.shape, 0)
    um1 = jnp.where(row == 0, prev_row, pltpu.roll(u, 1, 0))
    up1 = jnp.where(row == tm - 1, next_row, pltpu.roll(u, tm - 1, 0))
    w = cw_ref[...]
    y = um1 * w[0:1, :] + u * w[1:2, :] + up1 * w[2:3, :] + cb_ref[...]
    _store_time_tiles(x1_ref, y[:, :HC])
    _store_time_tiles(x2_ref, y[:, HC:2 * HC] * _silu(u_scr[:, _HY_W:]))
    _store_time_tiles(v_ref, y[:, 2 * HC:])


def hybrid_proj(x2d, g, w_bf, conv_w, conv_b, batch, seq_len, rope, half, *, tm=512, row_chunk=256):
    m, d = x2d.shape
    n = w_bf.shape[1]
    tiles_per_seq = seq_len // tm
    sub = tm // SUBLANES
    last8 = m // SUBLANES - 1
    hy_shape = jax.ShapeDtypeStruct((batch, SLABS, J2, seq_len // N2, SUBLANES, LANES), F32)
    hy_spec = pl.BlockSpec((None, SLABS, J2, tm // N2, SUBLANES, LANES),
                           lambda t: (t // tiles_per_seq, 0, 0, t % tiles_per_seq, 0, 0))
    row_blk = lambda wd: pl.BlockSpec((tm, wd), lambda t: (t, 0))
    const = lambda shape: pl.BlockSpec(shape, lambda t: (0, 0))
    return pl.pallas_call(
        functools.partial(_hybrid_proj_kernel, half=half, tiles_per_seq=tiles_per_seq, row_chunk=row_chunk),
        grid=(m // tm,),
        in_specs=[row_blk(d),
                  pl.BlockSpec((SUBLANES, d), lambda t: (jnp.maximum(t * sub - 1, 0), 0)),
                  pl.BlockSpec((SUBLANES, d), lambda t: (jnp.minimum((t + 1) * sub, last8), 0)),
                  const((1, d)), const((d, n))]
                 + [pl.BlockSpec((tm, LANES), lambda t: (t % tiles_per_seq, 0))] * 3
                 + [const((3, _HY_W)), const((1, _HY_W))],
        out_specs=(row_blk(_QK_W), hy_spec, hy_spec, hy_spec, row_blk(HC), row_blk(_ATT_V_W)),
        out_shape=(jax.ShapeDtypeStruct((m, _QK_W), BF16), hy_shape, hy_shape, hy_shape,
                   jax.ShapeDtypeStruct((m, HC), F32), jax.ShapeDtypeStruct((m, _ATT_V_W), BF16)),
        scratch_shapes=[pltpu.VMEM((tm, _HY_W + HC), F32)],
        compiler_params=_cparams(1), name="hybrid_proj")(x2d, x2d, x2d, g.reshape(1, d), w_bf, *rope,
                                                         conv_w, conv_b.reshape(1, -1))


def rope_tables(seq_len, head_dim):
    rot = head_dim // 4
    half = rot // 2
    inv = jnp.power(ROPE_THETA, -2.0 * jnp.arange(half, dtype=F32) / rot)
    ang = jnp.arange(seq_len).astype(F32)[:, None] * inv[None, :]
    cos, sin = jnp.cos(ang), jnp.sin(ang)
    lane = np.arange(LANES) % head_dim
    idx = lane % half
    in_rot = jnp.asarray(lane < rot)[None, :]
    first = jnp.asarray(lane < half)[None, :]
    cos_t = jnp.where(in_rot, cos[:, idx], 1.0)
    sa = jnp.where(first, -sin[:, idx], 0.0)
    sb = jnp.where(in_rot & ~first, sin[:, idx], 0.0)
    return (cos_t, sa, sb), half


def _store_time_tiles(ref, val):
    for s in range(SLABS):
        for i in range(val.shape[0] // N2):
            ref[s, :, i] = val[i * N2:(i + 1) * N2, s * LANES:(s + 1) * LANES].reshape(J2, SUBLANES, LANES)


def _load_time_tiles(ref, nb):
    return jnp.concatenate([jnp.concatenate([ref[s, :, i].reshape(N2, LANES) for s in range(SLABS)], axis=1)
                            for i in range(nb)], axis=0)


def _filter_kernel(band_ref, w1_ref, b1_ref, f1_ref, w2_ref, b2_ref, f2_ref, w3_ref, dl_ref, o_ref,
                   *, seq_len, tr):
    j = (pl.program_id(0) * tr + lax.broadcasted_iota(jnp.int32, (tr, 1), 0)).astype(F32)
    t = j * (1.0 / (seq_len - 1))
    wpos = (2.0 * math.pi) * j / seq_len
    lane = lax.broadcasted_iota(jnp.int32, (tr, LANES), 1)
    arg = band_ref[...] * wpos
    feats = jnp.where(lane == 0, t,
                      jnp.where(lane <= FILTER_BANDS, jnp.cos(arg),
                                jnp.where(lane <= 2 * FILTER_BANDS, -jnp.sin(arg), 0.0)))
    h = jnp.dot(feats.astype(BF16), w1_ref[...], preferred_element_type=F32) + b1_ref[...]
    h = jnp.sin(f1_ref[...] * h)
    h = jnp.dot(h.astype(BF16), w2_ref[...], preferred_element_type=F32) + b2_ref[...]
    h = jnp.sin(f2_ref[...] * h)
    h = jnp.dot(h.astype(BF16), w3_ref[...], preferred_element_type=F32)
    win = jnp.exp(-t * dl_ref[...]) + DECAY_SHIFT
    for grp in range(4):
        _store_time_tiles(o_ref.at[grp * SLABS:(grp + 1) * SLABS], h[:, grp * HC:(grp + 1) * HC] * win)


def hyena_filter(seq_len, w1, b1, f1, w2, b2, f2, w3, *, tr=512):
    hid = w1.shape[1]
    pad = LANES - hid
    w1p = jnp.pad(w1, ((0, LANES - w1.shape[0]), (0, pad))).astype(BF16)
    w2p = jnp.pad(w2, ((0, pad), (0, pad))).astype(BF16)
    w3r = w3.reshape(hid, 2, 2, HC).transpose(0, 2, 1, 3).reshape(hid, 4 * HC)
    w3p = jnp.pad(w3r, ((0, pad), (0, 0))).astype(BF16)
    vec = lambda v: jnp.pad(v, (0, pad)).reshape(1, LANES)
    bands = jnp.linspace(1e-4, FILTER_BANDS - 1, FILTER_BANDS, dtype=F32)
    band_l = jnp.concatenate([jnp.zeros((1,), F32), bands, bands,
                              jnp.zeros((LANES - 1 - 2 * FILTER_BANDS,), F32)]).reshape(1, LANES)
    max_decay = math.log(DECAY_TARGET) / FAST_DECAY_PCT
    min_decay = math.log(DECAY_TARGET) / SLOW_DECAY_PCT
    deltas = jnp.abs(jnp.linspace(min_decay, max_decay, HC, dtype=F32)).reshape(1, HC)
    full = lambda a: pl.BlockSpec(a.shape, lambda i: (0, 0))
    args = [band_l, w1p, vec(b1), vec(f1), w2p, vec(b2), vec(f2), w3p, deltas]
    return pl.pallas_call(
        functools.partial(_filter_kernel, seq_len=seq_len, tr=tr),
        grid=(seq_len // tr,),
        in_specs=[full(a) for a in args],
        out_specs=pl.BlockSpec((4 * SLABS, J2, tr // N2, SUBLANES, LANES), lambda i: (0, 0, i, 0, 0)),
        out_shape=jax.ShapeDtypeStruct((4 * SLABS, J2, seq_len // N2, SUBLANES, LANES), F32),
        compiler_params=_cparams(1), name="hyena_filter")(*args)


def dft_tables(n1h):
    n1 = 2 * n1h
    n = n1 * N2
    k1 = jnp.arange(n1h, dtype=jnp.int32)
    m1 = (jnp.arange(n1, dtype=jnp.int32)[None, :] * (2 * k1[:, None] + 1)) % (2 * n1)
    a1 = m1.astype(F32) * (math.pi / n1)
    c1, s1 = jnp.cos(a1)[:, :n1h], jnp.sin(a1)[:, :n1h]
    w1_half = jnp.concatenate([c1, -s1], axis=0).astype(BF16)
    w1_inv = jnp.concatenate([c1.T, -s1.T], axis=1).astype(BF16)
    k2 = jnp.arange(N2, dtype=jnp.int32)
    n2 = jnp.arange(N2, dtype=jnp.int32)
    f = 2 * (k2[None, :, None] * n1 + k1[:, None, None]) + 1
    m2 = (n2[None, None, :] * f) % (2 * n)
    th = m2.astype(F32) * (math.pi / n)
    c, s = jnp.cos(th), jnp.sin(th)
    mat = jnp.concatenate([jnp.concatenate([c, s], axis=2),
                           jnp.concatenate([-s, c], axis=2)], axis=1)
    return w1_half, w1_inv, mat.astype(BF16), jnp.swapaxes(mat, 1, 2).astype(BF16)


def _rows_of(ref2, m, n_rows):
    return jnp.concatenate([ref2[cb, pl.ds(m, n_rows, stride=SUBLANES), :] for cb in range(SLABS)], axis=1)


def _store_rows(ref2, m, val):
    for cb in range(SLABS):
        ref2[cb, pl.ds(m, val.shape[0], stride=SUBLANES), :] = val[:, cb * LANES:(cb + 1) * LANES]


def _stage1_kernel(w_ref, x_ref, o_ref, *, k, r):
    w = w_ref[...]
    for m in range(SUBLANES):
        _store_rows(o_ref, m, jnp.dot(w, _rows_of(x_ref, m, k).astype(BF16), preferred_element_type=F32))


def dft_stage1(w, x5):
    b, s = x5.shape[:2]
    r, k = w.shape
    assert s % SLABS == 0 and x5.shape[3] == k * SUBLANES
    blk = lambda rows: pl.BlockSpec((None, SLABS, None, rows * SUBLANES, LANES), lambda bi, si, j: (bi, si, j, 0, 0))
    return pl.pallas_call(
        functools.partial(_stage1_kernel, k=k, r=r), grid=(b, s // SLABS, J2),
        in_specs=[pl.BlockSpec((r, k), lambda bi, si, j: (0, 0)), blk(k)],
        out_specs=blk(r),
        out_shape=jax.ShapeDtypeStruct((b, s, J2, r * SUBLANES, LANES), F32),
        compiler_params=_cparams(3), name="dft_stage1")(w, x5)


def _slab_cat(a_ref, kk, n_slabs):
    return jnp.concatenate([jnp.concatenate([a_ref[cb, :, 0, kk].reshape(N2, LANES),
                                             a_ref[cb, :, 1, kk].reshape(N2, LANES)], axis=0)
                            for cb in range(n_slabs)], axis=1).astype(BF16)


def _stage2_filter_kernel(m_ref, af_ref, ab_ref, o_ref):
    for kk in range(K1_BLOCK):
        zf = jnp.dot(m_ref[kk], _slab_cat(af_ref, kk, SLABS), preferred_element_type=F32)
        zb = jnp.dot(m_ref[kk], _slab_cat(ab_ref, kk, SLABS), preferred_element_type=F32)
        o_ref[kk] = jnp.concatenate([zf[:N2] + zb[:N2], zf[N2:] - zb[N2:]], axis=0)


def dft_stage2_filter(mat, a7):
    n1h = a7.shape[4]
    grp = lambda first: pl.BlockSpec((None, SLABS, J2, 2, K1_BLOCK, SUBLANES, LANES),
                                     lambda kb, o: (0, first + o, 0, 0, kb, 0, 0))
    return pl.pallas_call(
        _stage2_filter_kernel, grid=(n1h // K1_BLOCK, 2),
        in_specs=[pl.BlockSpec((K1_BLOCK, 2 * N2, 2 * N2), lambda kb, o: (kb, 0, 0)), grp(0), grp(2)],
        out_specs=pl.BlockSpec((K1_BLOCK, 2 * N2, HC), lambda kb, o: (kb, 0, o)),
        out_shape=jax.ShapeDtypeStruct((n1h, 2 * N2, 2 * HC), F32),
        compiler_params=_cparams(2), name="dft_stage2_filter")(mat, a7, a7)


CONV_SCRATCH_BYTES = 16 * 1024 * 1024
CONV_J_BLOCK = 4


def _conv_fused_kernel(w1_ref, wi_ref, m_ref, mt_ref, h_ref, z_ref, gate_ref, d_ref, o_ref, a_scr,
                       *, n1h, sg, nb, scale):
    t = pl.program_id(2)
    n_a = J2 // CONV_J_BLOCK
    cat = lambda pieces: jnp.concatenate(pieces, axis=1) if len(pieces) > 1 else pieces[0]
    strided = lambda ref, idx, m, rows: cat([ref[(cb,) + idx + (pl.ds(m, rows, stride=SUBLANES), slice(None))]
                                             for cb in range(sg)])

    @pl.when(t < n_a)
    def _():
        w = w1_ref[...]
        for jj in range(CONV_J_BLOCK):
            j = t * CONV_J_BLOCK + jj
            for m in range(SUBLANES):
                res = jnp.dot(w, strided(z_ref, (jj,), m, n1h).astype(BF16), preferred_element_type=F32)
                for cb in range(sg):
                    a_scr[cb, j, pl.ds(m, 2 * n1h, stride=SUBLANES), :] = res[:, cb * LANES:(cb + 1) * LANES]

    @pl.when((t >= n_a) & (t < n_a + nb))
    def _():
        kb = t - n_a
        re0 = [pl.multiple_of((kb * K1_BLOCK + kk) * SUBLANES, SUBLANES) for kk in range(K1_BLOCK)]
        im0 = [pl.multiple_of((n1h + kb * K1_BLOCK + kk) * SUBLANES, SUBLANES) for kk in range(K1_BLOCK)]
        tile = lambda cb, r0: a_scr[cb, :, pl.ds(r0, SUBLANES), :].reshape(N2, LANES)
        zs = [jnp.dot(m_ref[kk], cat([jnp.concatenate([tile(cb, re0[kk]), tile(cb, im0[kk])], axis=0)
                                      for cb in range(sg)]).astype(BF16), preferred_element_type=F32)
              for kk in range(K1_BLOCK)]
        ys = []
        for kk, z in enumerate(zs):
            zr, zi = z[:N2], z[N2:]
            hr, hi = h_ref[kk, :N2], h_ref[kk, N2:]
            ys.append(jnp.concatenate([zr * hr - zi * hi, zr * hi + zi * hr], axis=0).astype(BF16))
        gs = [jnp.dot(mt_ref[kk], y, preferred_element_type=F32) for kk, y in enumerate(ys)]
        for kk, g in enumerate(gs):
            for cb in range(sg):
                lanes = slice(cb * LANES, (cb + 1) * LANES)
                a_scr[cb, :, pl.ds(re0[kk], SUBLANES), :] = g[:N2, lanes].reshape(J2, SUBLANES, LANES)
                a_scr[cb, :, pl.ds(im0[kk], SUBLANES), :] = g[N2:, lanes].reshape(J2, SUBLANES, LANES)

    @pl.when(t >= n_a + nb)
    def _():
        w, d = wi_ref[...], d_ref[...]
        for jj in range(CONV_J_BLOCK):
            j = (t - n_a - nb) * CONV_J_BLOCK + jj
            for m in range(SUBLANES):
                gm = cat([a_scr[cb, j, pl.ds(m, 2 * n1h, stride=SUBLANES), :] for cb in range(sg)]).astype(BF16)
                y = jnp.dot(w, gm, preferred_element_type=F32) * scale
                res = strided(gate_ref, (jj,), m, n1h) * (y + d * strided(z_ref, (jj,), m, n1h))
                for cb in range(sg):
                    o_ref[cb, jj, pl.ds(m, n1h, stride=SUBLANES), :] = res[:, cb * LANES:(cb + 1) * LANES]


def hyena_conv_fused(z, gate, d_vec, spec, order, tabs):
    batch, n1h = z.shape[0], z.shape[3]
    w1_half, w1_inv, mat, mat_t = tabs
    sg = max(1, min(SLABS, CONV_SCRATCH_BYTES // (J2 * 2 * n1h * SUBLANES * LANES * 4)))
    nb = n1h // K1_BLOCK
    n_a = J2 // CONV_J_BLOCK
    rows = lambda a: a.reshape(batch, SLABS, J2, n1h * SUBLANES, LANES)
    phase_c = lambda t: jnp.maximum(t - n_a - nb, 0)
    z_idx = lambda t: jnp.where(t < n_a + nb, jnp.minimum(t, n_a - 1), t - n_a - nb)
    tile = lambda f: pl.BlockSpec((None, sg, CONV_J_BLOCK, n1h * SUBLANES, LANES), lambda b, s, t: (b, s, f(t), 0, 0))
    kblk = lambda t: jnp.clip(t - n_a, 0, nb - 1)
    const = lambda a: pl.BlockSpec(a.shape, lambda b, s, t: (0, 0))
    mspec = pl.BlockSpec((K1_BLOCK, 2 * N2, 2 * N2), lambda b, s, t: (kblk(t), 0, 0))
    out = pl.pallas_call(
        functools.partial(_conv_fused_kernel, n1h=n1h, sg=sg, nb=nb, scale=1.0 / (n1h * N2)),
        grid=(batch, SLABS // sg, 2 * n_a + nb),
        in_specs=[const(w1_half), const(w1_inv), mspec, mspec,
                  pl.BlockSpec((K1_BLOCK, 2 * N2, sg * LANES),
                               lambda b, s, t: (kblk(t), 0, order * (SLABS // sg) + s)),
                  tile(z_idx), tile(phase_c),
                  pl.BlockSpec((1, sg * LANES), lambda b, s, t: (0, s))],
        out_specs=tile(phase_c),
        out_shape=jax.ShapeDtypeStruct((batch, SLABS, J2, n1h * SUBLANES, LANES), F32),
        scratch_shapes=[pltpu.VMEM((sg, J2, 2 * n1h * SUBLANES, LANES), F32)],
        compiler_params=_cparams(3), name="hyena_conv")(w1_half, w1_inv, mat, mat_t, spec, rows(z), rows(gate),
                                                        d_vec.reshape(1, HC))
    return out.reshape(z.shape)


def _silu(g):
    return g / (1.0 + jnp.exp(-g))


def _band_mask(t, halo, q0, seq_len):
    qi = lax.broadcasted_iota(jnp.int32, (t, t + 2 * halo), 0)
    kj = lax.broadcasted_iota(jnp.int32, (t, t + 2 * halo), 1)
    kpos = q0 - halo + kj
    return (jnp.abs(kj - halo - qi) <= halo) & (kpos >= 0) & (kpos < seq_len)


def _head_pair_operands(win):
    x = win.astype(F32)
    swapped = pltpu.roll(x, B_HEAD_DIM, 1)
    low = lax.broadcasted_iota(jnp.int32, x.shape, 1) < B_HEAD_DIM
    place = lambda cond, val: jnp.where(cond, val, 0.0).astype(BF16)
    return ((place(low, x), place(~low, swapped)), (place(low, swapped), place(~low, x)))


def _band_attn_kernel(sink_ref, q_ref, kp_ref, kc_ref, kn_ref, vp_ref, vc_ref, vn_ref, gate_ref, o_ref, *, seq_len):
    t, hd, halo = ATTN_SUB, B_HEAD_DIM, B_BLOCK
    kpads = _head_pair_operands(jnp.concatenate([kp_ref[...], kc_ref[...], kn_ref[...]], axis=0))
    vpads = _head_pair_operands(jnp.concatenate([vp_ref[...], vc_ref[...], vn_ref[...]], axis=0))
    tq = q_ref.shape[0]
    heads = [(pair, par) for pair in range(B_HEADS // 2) for par in range(2)]
    for u in range(tq // t):
        valid = _band_mask(t, halo, pl.program_id(1) * tq + u * t, seq_len)
        rows = slice(u * t, (u + 1) * t)
        win = slice(u * t, (u + 1) * t + 2 * halo)
        scores = [lax.dot_general(q_ref[rows, pair * LANES:(pair + 1) * LANES], kpads[(2 * pair) // B_GROUP][par][win],
                                  (((1,), (1,)), ((), ())), preferred_element_type=F32) for pair, par in heads]
        probs, denoms = [], []
        scale = hd ** -0.5
        for (pair, par), s in zip(heads, scores):
            s = jnp.where(valid, s, NEG_INF)
            sink = sink_ref[2 * pair + par]
            m_raw = jnp.maximum(jnp.max(s, axis=-1, keepdims=True), sink * (1.0 / scale))
            p = jnp.exp2((s - m_raw) * (scale * LOG2E))
            denoms.append(jnp.sum(p, axis=-1, keepdims=True) + jnp.exp2((sink * (1.0 / scale) - m_raw) * (scale * LOG2E)))
            probs.append(p.astype(BF16))
        outs = [jnp.dot(p, vpads[(2 * pair) // B_GROUP][par][win], preferred_element_type=F32) / d
                for (pair, par), p, d in zip(heads, probs, denoms)]
        for pair in range(B_HEADS // 2):
            lanes = slice(pair * LANES, (pair + 1) * LANES)
            o_ref[rows, lanes] = ((outs[2 * pair] + outs[2 * pair + 1]) * _silu(gate_ref[rows, lanes])).astype(BF16)


def band_attn(qk, v_att, at_gate, sink, batch, seq_len, *, nsub=4):
    t = B_BLOCK
    tq = t * nsub
    nblk = seq_len // t
    qk3 = qk.reshape(batch, seq_len, qk.shape[1])
    v3 = v_att.reshape(batch, seq_len, LANES)
    g3 = at_gate.reshape(batch, seq_len, at_gate.shape[1])
    k_col = B_HEADS * B_HEAD_DIM // LANES
    halo = lambda col, f: pl.BlockSpec((None, t, LANES), lambda b, i: (b, f(i), col))
    cur = lambda col: pl.BlockSpec((None, tq, LANES), lambda b, i: (b, i, col))
    prev = lambda i: jnp.maximum(i * nsub - 1, 0)
    nxt = lambda i: jnp.minimum((i + 1) * nsub, nblk - 1)
    wide = pl.BlockSpec((None, tq, B_HEADS * B_HEAD_DIM), lambda b, i: (b, i, 0))
    out = pl.pallas_call(
        functools.partial(_band_attn_kernel, seq_len=seq_len), grid=(batch, seq_len // tq),
        in_specs=[pl.BlockSpec(memory_space=pltpu.SMEM), wide,
                  halo(k_col, prev), cur(k_col), halo(k_col, nxt), halo(0, prev), cur(0), halo(0, nxt), wide],
        out_specs=wide,
        out_shape=jax.ShapeDtypeStruct((batch, seq_len, B_HEADS * B_HEAD_DIM), BF16),
        compiler_params=_cparams(2), name="band_attn")(sink, qk3, qk3, qk3, qk3, v3, v3, v3, g3)
    return out.reshape(batch * seq_len, -1)


def _dil_attn_kernel(q_ref, kp_ref, kc_ref, kn_ref, vp_ref, vc_ref, vn_ref, o_ref, lse_ref, *, seq_len):
    t, hd, halo = ATTN_SUB, C_HEAD_DIM, C_RADIUS
    tq = q_ref.shape[0]
    lane = lax.broadcasted_iota(jnp.int32, (t, LANES), 1)
    for u in range(tq // t):
        valid = _band_mask(t, halo, pl.program_id(1) * tq + u * t, seq_len)
        rows = slice(u * t, (u + 1) * t)
        win = slice(u * t, (u + 1) * t + 2 * halo)
        cols = [slice(h * hd, (h + 1) * hd) for h in range(C_HEADS)]
        window = lambda p_ref, c_ref, n_ref, sl: jnp.concatenate([p_ref[:, sl], c_ref[:, sl], n_ref[:, sl]], axis=0)[win]
        scores = [lax.dot_general(q_ref[rows, sl], window(kp_ref, kc_ref, kn_ref, sl), (((1,), (1,)), ((), ())),
                                  preferred_element_type=F32) for sl in cols]
        probs, denoms = [], []
        lse_tile = jnp.zeros((t, LANES), F32)
        scale = hd ** -0.5
        for h, s in enumerate(scores):
            s = jnp.where(valid, s, NEG_INF)
            m_raw = jnp.max(s, axis=-1, keepdims=True)
            p = jnp.exp2((s - m_raw) * (scale * LOG2E))
            denom = jnp.sum(p, axis=-1, keepdims=True)
            lse_tile = jnp.where(lane == h, m_raw * scale + jnp.log(denom), lse_tile)
            probs.append(p.astype(BF16))
            denoms.append(denom)
        for sl, p, denom in zip(cols, probs, denoms):
            o = jnp.dot(p, window(vp_ref, vc_ref, vn_ref, sl), preferred_element_type=F32) / denom
            o_ref[rows, sl] = o.astype(BF16)
        lse_ref[rows, :] = lse_tile


def dil_attn(qk, v, n_seq, ls):
    w = C_HEADS * C_HEAD_DIM
    t = 2 * C_RADIUS
    nsub = next(n for n in (4, 2, 1) if ls % (n * t) == 0)
    tq = t * nsub
    n_halo = ls // C_RADIUS
    qk3 = qk.reshape(n_seq, ls, qk.shape[1])
    v3 = v.reshape(n_seq, ls, w)
    prev = lambda i: jnp.maximum(i * (tq // C_RADIUS) - 1, 0)
    nxt = lambda i: jnp.minimum((i + 1) * (tq // C_RADIUS), n_halo - 1)
    big = lambda col: pl.BlockSpec((None, tq, w), lambda b, i: (b, i, col))
    halo = lambda col, f: pl.BlockSpec((None, C_RADIUS, w), lambda b, i: (b, f(i), col))
    o, lse = pl.pallas_call(
        functools.partial(_dil_attn_kernel, seq_len=ls), grid=(n_seq, ls // tq),
        in_specs=[big(0), halo(1, prev), big(1), halo(1, nxt), halo(0, prev), big(0), halo(0, nxt)],
        out_specs=(big(0), pl.BlockSpec((None, tq, LANES), lambda b, i: (b, i, 0))),
        out_shape=(jax.ShapeDtypeStruct((n_seq, ls, w), BF16),
                   jax.ShapeDtypeStruct((n_seq, ls, LANES), F32)),
        compiler_params=_cparams(2), name="dil_attn")(qk3, qk3, qk3, qk3, v3, v3, v3)
    return o.reshape(n_seq * ls, w), lse.reshape(n_seq * ls, LANES)


def _out_proj_even_kernel(hy_ref, at_ref, x_ref, w_ref, o_ref, *, nb):
    mixed = jnp.concatenate([_load_time_tiles(hy_ref, nb).astype(BF16), at_ref[...]], axis=1)
    o_ref[...] = x_ref[...] + jnp.dot(mixed, w_ref[...], preferred_element_type=F32)


def out_proj_even(hy, at, x2d, w_bf, *, tm=512):
    m, d = x2d.shape
    batch, n1h = hy.shape[0], hy.shape[3]
    tps = n1h * N2 // tm
    nb = tm // N2
    half = pl.BlockSpec((tm, HC), lambda b, i: (b * tps + i, 0))
    full = pl.BlockSpec((tm, d), lambda b, i: (b * tps + i, 0))
    return pl.pallas_call(
        functools.partial(_out_proj_even_kernel, nb=nb), grid=(batch, tps),
        in_specs=[pl.BlockSpec((None, SLABS, J2, nb, SUBLANES, LANES), lambda b, i: (b, 0, 0, i, 0, 0)), half, full,
                  pl.BlockSpec((d, d), lambda b, i: (0, 0))],
        out_specs=full, out_shape=jax.ShapeDtypeStruct((m, d), F32),
        compiler_params=_cparams(2), name="out_proj_even")(hy, at, x2d, w_bf)


def _merge_out_kernel(*refs, final, dilations):
    nh = C_HEADS
    o_nat, lse_nat = refs[0], refs[1]
    pos = 2
    strided = []
    for _ in dilations:
        strided.append((refs[pos:pos + nh], refs[pos + nh]))
        pos += nh + 1
    gate_ref, x_ref, w_ref = refs[pos:pos + 3]
    pos += 3
    g_ref = refs[pos] if final else None
    out_ref, o_scr, l_scr = refs[-3:]
    for gi, (d, (o_slabs, lse_ref)) in enumerate(zip(dilations, strided)):
        per = o_nat.shape[0] // d
        for r in range(d):
            l_scr[gi, pl.ds(r, per, stride=d), :] = lse_ref[r]
            for h in range(nh):
                o_scr[gi, h, pl.ds(r, per, stride=d), :] = o_slabs[h][r].astype(F32)
    ls = [lse_nat[...]] + [l_scr[gi] for gi in range(len(dilations))]
    mx = functools.reduce(jnp.maximum, ls)
    es = [jnp.exp(l - mx) for l in ls]
    den = functools.reduce(lambda a, b: a + b, es)
    alphas = [e / den for e in es]
    hd = C_HEAD_DIM
    parts = []
    for h in range(nh):
        acc = alphas[0][:, h:h + 1] * o_nat[:, h * hd:(h + 1) * hd].astype(F32)
        for gi in range(len(dilations)):
            acc = acc + alphas[gi + 1][:, h:h + 1] * o_scr[gi, h]
        parts.append(acc)
    y = (jnp.concatenate(parts, axis=1) * _silu(gate_ref[...])).astype(BF16)
    r = x_ref[...] + jnp.dot(y, w_ref[...], preferred_element_type=F32)
    if final:
        ms = jnp.mean(r * r, axis=-1, keepdims=True)
        r = r * lax.rsqrt(ms + NORM_EPS) * g_ref[...]
    out_ref[...] = r


def merge_out(o_nat, lse_nat, strided, gate, x2d, w_bf, batch, seq_len, final_g=None, *, tm=512):
    m, dm = x2d.shape
    tps = seq_len // tm
    nat = lambda wd: pl.BlockSpec((tm, wd), lambda b, i: (b * tps + i, 0))
    in_specs, args = [nat(dm), nat(LANES)], [o_nat, lse_nat]
    for d, o, lse in strided:
        per = tm // d
        in_specs += [pl.BlockSpec((None, d, per, LANES), functools.partial(lambda h, b, i: (b, 0, i, h), h))
                     for h in range(C_HEADS)]
        in_specs.append(pl.BlockSpec((None, d, per, LANES), lambda b, i: (b, 0, i, 0)))
        args += [o] * C_HEADS + [lse]
    in_specs += [nat(dm), nat(dm), pl.BlockSpec((dm, dm), lambda b, i: (0, 0))]
    args += [gate, x2d, w_bf]
    if final_g is not None:
        in_specs.append(pl.BlockSpec((1, dm), lambda b, i: (0, 0)))
        args.append(final_g.reshape(1, dm))
    n_str = len(strided)
    return pl.pallas_call(
        functools.partial(_merge_out_kernel, final=final_g is not None, dilations=tuple(d for d, _, _ in strided)),
        grid=(batch, tps), in_specs=in_specs, out_specs=nat(dm), out_shape=jax.ShapeDtypeStruct((m, dm), F32),
        scratch_shapes=[pltpu.VMEM((n_str, C_HEADS, tm, LANES), F32), pltpu.VMEM((n_str, tm, LANES), F32)],
        compiler_params=_cparams(2), name="merge_out")(*args)


def _even_w_in(w_in):
    return jnp.concatenate([w_in[:, 2048:2688], w_in[:, :2048], w_in[:, 2816:], w_in[:, 2688:2816]], axis=1).astype(BF16)


def _odd_w_in(w_in):
    w = C_HEADS * C_HEAD_DIM
    blk = lambda g, t: w_in[:, (3 * g + t) * w:(3 * g + t + 1) * w]
    per = [[blk(g, 0), blk(g, 1), blk(g, 2)] for g in range(len(C_PATTERNS))]
    per[0].append(w_in[:, 9 * w:])
    return [jnp.concatenate(cols, axis=1).astype(BF16) for cols in per]


def hyena_spectrum(seq_len, tabs, fw1, fb1, ff1, fw2, fb2, ff2, fw3):
    n1h = seq_len // N2
    w1_half, _, mat, _ = tabs
    filt = hyena_filter(seq_len, fw1, fb1, ff1, fw2, fb2, ff2, fw3)
    a = dft_stage1(w1_half, filt.reshape(1, 4 * SLABS, J2, n1h * SUBLANES, LANES))
    return dft_stage2_filter(mat, a.reshape(1, 4 * SLABS, J2, 2, n1h, SUBLANES, LANES))


def hybrid_layer(x2d, batch, seq_len, norm_g, w_in_bf, conv_w, conv_b, hyena_d, sink, w_out_bf, spec, tabs, rope):
    rope_t, half = rope
    qk, x1, x2s, v, at_gate, v_att = hybrid_proj(x2d, norm_g, w_in_bf, conv_w, conv_b, batch, seq_len, rope_t, half)
    z = hyena_conv_fused(v, x1, hyena_d[0], spec, 0, tabs)
    z = hyena_conv_fused(z, x2s, hyena_d[1], spec, 1, tabs)
    at = band_attn(qk, v_att, at_gate, sink, batch, seq_len)
    return out_proj_even(z, at, x2d, w_out_bf)


def dilated_layer(x2d, batch, seq_len, norm_g, w_in_bfs, w_out_bf, rope, final_g):
    w = C_HEADS * C_HEAD_DIM
    m = batch * seq_len
    rope_t, half = rope
    o_nat = lse_nat = gate = None
    strided = []
    for gi, (_, d) in enumerate(C_PATTERNS):
        ls = seq_len // d
        if d == 1:
            qk, v, gate = norm_proj(x2d, norm_g, w_in_bfs[gi], seq_len, rope_t, half,
                                    ((2 * w, BF16), (w, BF16), (w, F32)))
            o_nat, lse_nat = dil_attn(qk, v, batch, ls)
        else:
            qk, v = norm_proj_strided(x2d, norm_g, w_in_bfs[gi], batch, seq_len, d, rope_t, half,
                                      ((2 * w, BF16), (w, BF16)))
            o, lse = dil_attn(qk.reshape(m, 2 * w), v.reshape(m, w), batch * d, ls)
            strided.append((d, o.reshape(batch, d, ls, w), lse.reshape(batch, d, ls, LANES)))
    return merge_out(o_nat, lse_nat, strided, gate, x2d, w_out_bf, batch, seq_len, final_g)


def kernel(x_prompt, x_sample, a_norm, a_w_in, a_conv_w, a_conv_b, a_filt_w1, a_filt_b1, a_filt_f1, a_filt_w2, a_filt_b2, a_filt_f2, a_filt_w3, a_hyena_d, a_sink, a_w_out, c_norm, c_w_in, c_w_out, final_norm):
    depth = a_norm.shape[0] + c_norm.shape[0]
    xs = [x_prompt, x_sample]
    shapes = [(x.shape[0], x.shape[1]) for x in xs]
    acts = [x.reshape(-1, D_MODEL) for x in xs]
    seq_lens = sorted({s[1] for s in shapes})
    tabs = {sl: dft_tables(sl // N2) for sl in seq_lens}
    rope_even = {sl: rope_tables(sl, B_HEAD_DIM) for sl in seq_lens}
    rope_odd = {sl: rope_tables(sl, C_HEAD_DIM) for sl in seq_lens}
    for layer in range(depth):
        i = layer // 2
        if layer % 2 == 0:
            w_in_bf = _even_w_in(a_w_in[i])
            w_out_bf = a_w_out[i].astype(BF16)
            specs = {sl: hyena_spectrum(sl, tabs[sl], a_filt_w1[i], a_filt_b1[i], a_filt_f1[i], a_filt_w2[i],
                                        a_filt_b2[i], a_filt_f2[i], a_filt_w3[i]) for sl in seq_lens}
            acts = [hybrid_layer(x2d, b, sl, a_norm[i], w_in_bf, a_conv_w[i], a_conv_b[i], a_hyena_d[i], a_sink[i],
                                 w_out_bf, specs[sl], tabs[sl], rope_even[sl])
                    for x2d, (b, sl) in zip(acts, shapes)]
        else:
            w_in_bfs = _odd_w_in(c_w_in[i])
            w_out_bf = c_w_out[i].astype(BF16)
            final_g = final_norm if layer == depth - 1 else None
            acts = [dilated_layer(x2d, b, sl, c_norm[i], w_in_bfs, w_out_bf, rope_odd[sl], final_g)
                    for x2d, (b, sl) in zip(acts, shapes)]
    assert depth % 2 == 0
    return tuple(a.reshape(b, sl, D_MODEL) for a, (b, sl) in zip(acts, shapes))
```

```python
import functools
import math

import numpy as np
import jax
import jax.numpy as jnp
from jax import lax
from jax.experimental import pallas as pl
from jax.experimental.pallas import tpu as pltpu

F32 = jnp.float32
BF16 = jnp.bfloat16

D_MODEL = 1024
HC = 512
FILTER_BANDS = 16
DECAY_TARGET = 1e-2
FAST_DECAY_PCT = 0.3
SLOW_DECAY_PCT = 1.5
DECAY_SHIFT = 0.05
B_HEAD_DIM = 64
B_HEADS = 8
B_KV_HEADS = 2
B_GROUP = 4
B_BLOCK = 128
C_PATTERNS = ((128, 1), (512, 4), (2048, 16))
C_HEADS = 8
C_HEAD_DIM = 128
C_RADIUS = 64
ROPE_THETA = 500000.0
NORM_EPS = 1e-6
NEG_INF = -1e30
LOG2E = math.log2(math.e)

LANES = 128
SUBLANES = 8
N2 = 128
J2 = N2 // SUBLANES
SLABS = HC // LANES
K1_BLOCK = 8
ATTN_SUB = 128
VMEM_LIMIT = 56 * 1024 * 1024


def _cparams(n_axes):
    return pltpu.CompilerParams(dimension_semantics=("arbitrary",) * n_axes,
                                vmem_limit_bytes=VMEM_LIMIT)


def _rope_chunk(blk, cos, sa, sb, half):
    return (blk * cos + pltpu.roll(blk, LANES - half, 1) * sa + pltpu.roll(blk, half, 1) * sb)


def _norm_proj_kernel(x_ref, g_ref, w_ref, cos_ref, sa_ref, sb_ref, *outs, half, groups, row_chunk):
    g = g_ref[...]
    chunks = [slice(r0, r0 + row_chunk) for r0 in range(0, x_ref.shape[0], row_chunk)]
    accs = [jnp.dot(_rms_bf16(x_ref[rows, :], g), w_ref[...], preferred_element_type=F32) for rows in chunks]
    for rows, acc in zip(chunks, accs):
        col = 0
        for gi, (o_ref, (width, dtype)) in enumerate(zip(outs, groups)):
            if gi == 0:
                cos, sa, sb = cos_ref[rows, :], sa_ref[rows, :], sb_ref[rows, :]
                for ci in range(width // LANES):
                    src = slice(col + ci * LANES, col + (ci + 1) * LANES)
                    o_ref[rows, ci * LANES:(ci + 1) * LANES] = _rope_chunk(acc[:, src], cos, sa, sb, half).astype(dtype)
            else:
                o_ref[rows, :] = acc[:, col:col + width].astype(dtype)
            col += width


def norm_proj(x2d, g, w_bf, seq_len, rope, half, groups, *, tm=512, row_chunk=256):
    m, d = x2d.shape
    n = w_bf.shape[1]
    assert m % tm == 0 and seq_len % tm == 0 and sum(wd for wd, _ in groups) == n
    tiles_per_seq = seq_len // tm
    in_specs = [pl.BlockSpec((tm, d), lambda i: (i, 0)),
                pl.BlockSpec((1, d), lambda i: (0, 0)),
                pl.BlockSpec((d, n), lambda i: (0, 0))]
    in_specs += [pl.BlockSpec((tm, LANES), lambda i: (i % tiles_per_seq, 0))] * 3
    return pl.pallas_call(
        functools.partial(_norm_proj_kernel, half=half, groups=tuple(groups), row_chunk=row_chunk),
        grid=(m // tm,), in_specs=in_specs,
        out_specs=tuple(pl.BlockSpec((tm, wd), lambda i: (i, 0)) for wd, _ in groups),
        out_shape=tuple(jax.ShapeDtypeStruct((m, wd), dt) for wd, dt in groups),
        compiler_params=_cparams(1), name="norm_proj")(x2d, g.reshape(1, d), w_bf, *rope)


def _by_residue(ref, d, residues):
    per = ref.shape[0] // d
    return jnp.concatenate([ref[pl.ds(r, per, stride=d), :] for r in residues], axis=0)


def _norm_proj_strided_kernel(*refs, half, groups, d, row_chunk):
    n_slab = D_MODEL // LANES
    x_slabs = refs[:n_slab]
    g_ref, w_ref, cos_ref, sa_ref, sb_ref = refs[n_slab:n_slab + 5]
    outs = refs[n_slab + 5:]
    tm = x_slabs[0].shape[0]
    per = tm // d
    g = g_ref[...]
    chunks = [range(r0, r0 + row_chunk // per) for r0 in range(0, d, row_chunk // per)]
    accs = [jnp.dot(_rms_bf16(jnp.concatenate([_by_residue(xs, d, res) for xs in x_slabs], axis=1), g), w_ref[...],
                    preferred_element_type=F32) for res in chunks]
    for res, acc in zip(chunks, accs):
        col = 0
        for gi, (o_ref, (width, dtype)) in enumerate(zip(outs, groups)):
            if gi == 0:
                cos, sa, sb = (_by_residue(t_ref, d, res) for t_ref in (cos_ref, sa_ref, sb_ref))
                val = jnp.concatenate([_rope_chunk(acc[:, col + ci * LANES:col + (ci + 1) * LANES], cos, sa, sb, half)
                                       for ci in range(width // LANES)], axis=1).astype(dtype)
            else:
                val = acc[:, col:col + width].astype(dtype)
            for k, r in enumerate(res):
                o_ref[r] = val[k * per:(k + 1) * per]
            col += width


def norm_proj_strided(x2d, g, w_bf, batch, seq_len, d, rope, half, groups, *, tm=512, row_chunk=256):
    m, dm = x2d.shape
    n = w_bf.shape[1]
    tps = seq_len // tm
    per = tm // d
    assert seq_len % tm == 0 and row_chunk % per == 0 and sum(wd for wd, _ in groups) == n
    in_specs = [pl.BlockSpec((tm, LANES), functools.partial(lambda c, t: (t, c), c)) for c in range(dm // LANES)]
    in_specs += [pl.BlockSpec((1, dm), lambda t: (0, 0)), pl.BlockSpec((dm, n), lambda t: (0, 0))]
    in_specs += [pl.BlockSpec((tm, LANES), lambda t: (t % tps, 0))] * 3
    return pl.pallas_call(
        functools.partial(_norm_proj_strided_kernel, half=half, groups=tuple(groups), d=d, row_chunk=row_chunk),
        grid=(m // tm,), in_specs=in_specs,
        out_specs=tuple(pl.BlockSpec((None, d, per, wd), lambda t: (t // tps, 0, t % tps, 0)) for wd, _ in groups),
        out_shape=tuple(jax.ShapeDtypeStruct((batch, d, seq_len // d, wd), dt) for wd, dt in groups),
        compiler_params=_cparams(1), name="norm_proj_strided")(*([x2d] * (dm // LANES)), g.reshape(1, dm), w_bf, *rope)


def _rms_bf16(x, g):
    ms = jnp.mean(x * x, axis=-1, keepdims=True)
    return (x * lax.rsqrt(ms + NORM_EPS) * g).astype(BF16)


_QK_W, _HY_W, _ATT_V_W = 640, 3 * HC, 128
_HY0 = _QK_W
_HYG0 = _HY0 + _HY_W
_ATG0 = _HYG0 + HC
_VATT0 = _ATG0 + HC


def _hybrid_proj_kernel(x_ref, xp_ref, xn_ref, g_ref, w_ref, cos_ref, sa_ref, sb_ref, cw_ref, cb_ref,
                        qk_ref, x1_ref, x2_ref, v_ref, atg_ref, vatt_ref, u_scr, *, half, tiles_per_seq, row_chunk):
    tm = x_ref.shape[0]
    i = pl.program_id(0) % tiles_per_seq
    g = g_ref[...]
    halo = jnp.concatenate([xp_ref[...], xn_ref[...]], axis=0)
    acc_h = jnp.dot(_rms_bf16(halo, g), w_ref[:, _HY0:_HYG0], preferred_element_type=F32)
    prev_row = jnp.where(i > 0, acc_h[SUBLANES - 1:SUBLANES], 0.0)
    next_row = jnp.where(i < tiles_per_seq - 1, acc_h[SUBLANES:SUBLANES + 1], 0.0)
    chunks = [slice(r0, r0 + row_chunk) for r0 in range(0, tm, row_chunk)]
    accs = [jnp.dot(_rms_bf16(x_ref[rows, :], g), w_ref[...], preferred_element_type=F32) for rows in chunks]
    for rows, acc in zip(chunks, accs):
        cos, sa, sb = cos_ref[rows, :], sa_ref[rows, :], sb_ref[rows, :]
        for ci in range(_QK_W // LANES):
            sl = slice(ci * LANES, (ci + 1) * LANES)
            qk_ref[rows, sl] = _rope_chunk(acc[:, sl], cos, sa, sb, half).astype(BF16)
        u_scr[rows, :] = acc[:, _HY0:_ATG0]
        atg_ref[rows, :] = acc[:, _ATG0:_VATT0]
        vatt_ref[rows, :] = acc[:, _VATT0:].astype(BF16)
    u = u_scr[:, :_HY_W]
    row = lax.broadcasted_iota(jnp.int32, u.shape, 0)
    um1 = jnp.where(row == 0, prev_row, pltpu.roll(u, 1, 0))
    up1 = jnp.where(row == tm - 1, next_row, pltpu.roll(u, tm - 1, 0))
    w = cw_ref[...]
    y = um1 * w[0:1, :] + u * w[1:2, :] + up1 * w[2:3, :] + cb_ref[...]
    _store_time_tiles(x1_ref, y[:, :HC])
    _store_time_tiles(x2_ref, y[:, HC:2 * HC] * _silu(u_scr[:, _HY_W:]))
    _store_time_tiles(v_ref, y[:, 2 * HC:])


def hybrid_proj(x2d, g, w_bf, conv_w, conv_b, batch, seq_len, rope, half, *, tm=512, row_chunk=256):
    m, d = x2d.shape
    n = w_bf.shape[1]
    tiles_per_seq = seq_len // tm
    sub = tm // SUBLANES
    last8 = m // SUBLANES - 1
    hy_shape = jax.ShapeDtypeStruct((batch, SLABS, J2, seq_len // N2, SUBLANES, LANES), F32)
    hy_spec = pl.BlockSpec((None, SLABS, J2, tm // N2, SUBLANES, LANES),
                           lambda t: (t // tiles_per_seq, 0, 0, t % tiles_per_seq, 0, 0))
    row_blk = lambda wd: pl.BlockSpec((tm, wd), lambda t: (t, 0))
    const = lambda shape: pl.BlockSpec(shape, lambda t: (0, 0))
    return pl.pallas_call(
        functools.partial(_hybrid_proj_kernel, half=half, tiles_per_seq=tiles_per_seq, row_chunk=row_chunk),
        grid=(m // tm,),
        in_specs=[row_blk(d),
                  pl.BlockSpec((SUBLANES, d), lambda t: (jnp.maximum(t * sub - 1, 0), 0)),
                  pl.BlockSpec((SUBLANES, d), lambda t: (jnp.minimum((t + 1) * sub, last8), 0)),
                  const((1, d)), const((d, n))]
                 + [pl.BlockSpec((tm, LANES), lambda t: (t % tiles_per_seq, 0))] * 3
                 + [const((3, _HY_W)), const((1, _HY_W))],
        out_specs=(row_blk(_QK_W), hy_spec, hy_spec, hy_spec, row_blk(HC), row_blk(_ATT_V_W)),
        out_shape=(jax.ShapeDtypeStruct((m, _QK_W), BF16), hy_shape, hy_shape, hy_shape,
                   jax.ShapeDtypeStruct((m, HC), F32), jax.ShapeDtypeStruct((m, _ATT_V_W), BF16)),
        scratch_shapes=[pltpu.VMEM((tm, _HY_W + HC), F32)],
        compiler_params=_cparams(1), name="hybrid_proj")(x2d, x2d, x2d, g.reshape(1, d), w_bf, *rope,
                                                         conv_w, conv_b.reshape(1, -1))


def rope_tables(seq_len, head_dim):
    rot = head_dim // 4
    half = rot // 2
    inv = jnp.power(ROPE_THETA, -2.0 * jnp.arange(half, dtype=F32) / rot)
    ang = jnp.arange(seq_len).astype(F32)[:, None] * inv[None, :]
    cos, sin = jnp.cos(ang), jnp.sin(ang)
    lane = np.arange(LANES) % head_dim
    idx = lane % half
    in_rot = jnp.asarray(lane < rot)[None, :]
    first = jnp.asarray(lane < half)[None, :]
    cos_t = jnp.where(in_rot, cos[:, idx], 1.0)
    sa = jnp.where(first, -sin[:, idx], 0.0)
    sb = jnp.where(in_rot & ~first, sin[:, idx], 0.0)
    return (cos_t, sa, sb), half


def _store_time_tiles(ref, val):
    for s in range(SLABS):
        for i in range(val.shape[0] // N2):
            ref[s, :, i] = val[i * N2:(i + 1) * N2, s * LANES:(s + 1) * LANES].reshape(J2, SUBLANES, LANES)


def _load_time_tiles(ref, nb):
    return jnp.concatenate([jnp.concatenate([ref[s, :, i].reshape(N2, LANES) for s in range(SLABS)], axis=1)
                            for i in range(nb)], axis=0)


def _filter_kernel(band_ref, w1_ref, b1_ref, f1_ref, w2_ref, b2_ref, f2_ref, w3_ref, dl_ref, o_ref,
                   *, seq_len, tr):
    j = (pl.program_id(0) * tr + lax.broadcasted_iota(jnp.int32, (tr, 1), 0)).astype(F32)
    t = j * (1.0 / (seq_len - 1))
    wpos = (2.0 * math.pi) * j / seq_len
    lane = lax.broadcasted_iota(jnp.int32, (tr, LANES), 1)
    arg = band_ref[...] * wpos
    feats = jnp.where(lane == 0, t,
                      jnp.where(lane <= FILTER_BANDS, jnp.cos(arg),
                                jnp.where(lane <= 2 * FILTER_BANDS, -jnp.sin(arg), 0.0)))
    h = jnp.dot(feats.astype(BF16), w1_ref[...], preferred_element_type=F32) + b1_ref[...]
    h = jnp.sin(f1_ref[...] * h)
    h = jnp.dot(h.astype(BF16), w2_ref[...], preferred_element_type=F32) + b2_ref[...]
    h = jnp.sin(f2_ref[...] * h)
    h = jnp.dot(h.astype(BF16), w3_ref[...], preferred_element_type=F32)
    win = jnp.exp(-t * dl_ref[...]) + DECAY_SHIFT
    for grp in range(4):
        _store_time_tiles(o_ref.at[grp * SLABS:(grp + 1) * SLABS], h[:, grp * HC:(grp + 1) * HC] * win)


def hyena_filter(seq_len, w1, b1, f1, w2, b2, f2, w3, *, tr=512):
    hid = w1.shape[1]
    pad = LANES - hid
    w1p = jnp.pad(w1, ((0, LANES - w1.shape[0]), (0, pad))).astype(BF16)
    w2p = jnp.pad(w2, ((0, pad), (0, pad))).astype(BF16)
    w3r = w3.reshape(hid, 2, 2, HC).transpose(0, 2, 1, 3).reshape(hid, 4 * HC)
    w3p = jnp.pad(w3r, ((0, pad), (0, 0))).astype(BF16)
    vec = lambda v: jnp.pad(v, (0, pad)).reshape(1, LANES)
    bands = jnp.linspace(1e-4, FILTER_BANDS - 1, FILTER_BANDS, dtype=F32)
    band_l = jnp.concatenate([jnp.zeros((1,), F32), bands, bands,
                              jnp.zeros((LANES - 1 - 2 * FILTER_BANDS,), F32)]).reshape(1, LANES)
    max_decay = math.log(DECAY_TARGET) / FAST_DECAY_PCT
    min_decay = math.log(DECAY_TARGET) / SLOW_DECAY_PCT
    deltas = jnp.abs(jnp.linspace(min_decay, max_decay, HC, dtype=F32)).reshape(1, HC)
    full = lambda a: pl.BlockSpec(a.shape, lambda i: (0, 0))
    args = [band_l, w1p, vec(b1), vec(f1), w2p, vec(b2), vec(f2), w3p, deltas]
    return pl.pallas_call(
        functools.partial(_filter_kernel, seq_len=seq_len, tr=tr),
        grid=(seq_len // tr,),
        in_specs=[full(a) for a in args],
        out_specs=pl.BlockSpec((4 * SLABS, J2, tr // N2, SUBLANES, LANES), lambda i: (0, 0, i, 0, 0)),
        out_shape=jax.ShapeDtypeStruct((4 * SLABS, J2, seq_len // N2, SUBLANES, LANES), F32),
        compiler_params=_cparams(1), name="hyena_filter")(*args)


def dft_tables(n1h):
    n1 = 2 * n1h
    n = n1 * N2
    k1 = jnp.arange(n1h, dtype=jnp.int32)
    m1 = (jnp.arange(n1, dtype=jnp.int32)[None, :] * (2 * k1[:, None] + 1)) % (2 * n1)
    a1 = m1.astype(F32) * (math.pi / n1)
    c1, s1 = jnp.cos(a1)[:, :n1h], jnp.sin(a1)[:, :n1h]
    w1_half = jnp.concatenate([c1, -s1], axis=0).astype(BF16)
    w1_inv = jnp.concatenate([c1.T, -s1.T], axis=1).astype(BF16)
    k2 = jnp.arange(N2, dtype=jnp.int32)
    n2 = jnp.arange(N2, dtype=jnp.int32)
    f = 2 * (k2[None, :, None] * n1 + k1[:, None, None]) + 1
    m2 = (n2[None, None, :] * f) % (2 * n)
    th = m2.astype(F32) * (math.pi / n)
    c, s = jnp.cos(th), jnp.sin(th)
    mat = jnp.concatenate([jnp.concatenate([c, s], axis=2),
                           jnp.concatenate([-s, c], axis=2)], axis=1)
    return w1_half, w1_inv, mat.astype(BF16), jnp.swapaxes(mat, 1, 2).astype(BF16)


def _rows_of(ref2, m, n_rows):
    return jnp.concatenate([ref2[cb, pl.ds(m, n_rows, stride=SUBLANES), :] for cb in range(SLABS)], axis=1)


def _store_rows(ref2, m, val):
    for cb in range(SLABS):
        ref2[cb, pl.ds(m, val.shape[0], stride=SUBLANES), :] = val[:, cb * LANES:(cb + 1) * LANES]


def _stage1_kernel(w_ref, x_ref, o_ref, *, k, r):
    w = w_ref[...]
    for m in range(SUBLANES):
        _store_rows(o_ref, m, jnp.dot(w, _rows_of(x_ref, m, k).astype(BF16), preferred_element_type=F32))


def dft_stage1(w, x5):
    b, s = x5.shape[:2]
    r, k = w.shape
    assert s % SLABS == 0 and x5.shape[3] == k * SUBLANES
    blk = lambda rows: pl.BlockSpec((None, SLABS, None, rows * SUBLANES, LANES), lambda bi, si, j: (bi, si, j, 0, 0))
    return pl.pallas_call(
        functools.partial(_stage1_kernel, k=k, r=r), grid=(b, s // SLABS, J2),
        in_specs=[pl.BlockSpec((r, k), lambda bi, si, j: (0, 0)), blk(k)],
        out_specs=blk(r),
        out_shape=jax.ShapeDtypeStruct((b, s, J2, r * SUBLANES, LANES), F32),
        compiler_params=_cparams(3), name="dft_stage1")(w, x5)


def _slab_cat(a_ref, kk, n_slabs):
    return jnp.concatenate([jnp.concatenate([a_ref[cb, :, 0, kk].reshape(N2, LANES),
                                             a_ref[cb, :, 1, kk].reshape(N2, LANES)], axis=0)
                            for cb in range(n_slabs)], axis=1).astype(BF16)


def _stage2_filter_kernel(m_ref, af_ref, ab_ref, o_ref):
    for kk in range(K1_BLOCK):
        zf = jnp.dot(m_ref[kk], _slab_cat(af_ref, kk, SLABS), preferred_element_type=F32)
        zb = jnp.dot(m_ref[kk], _slab_cat(ab_ref, kk, SLABS), preferred_element_type=F32)
        o_ref[kk] = jnp.concatenate([zf[:N2] + zb[:N2], zf[N2:] - zb[N2:]], axis=0)


def dft_stage2_filter(mat, a7):
    n1h = a7.shape[4]
    grp = lambda first: pl.BlockSpec((None, SLABS, J2, 2, K1_BLOCK, SUBLANES, LANES),
                                     lambda kb, o: (0, first + o, 0, 0, kb, 0, 0))
    return pl.pallas_call(
        _stage2_filter_kernel, grid=(n1h // K1_BLOCK, 2),
        in_specs=[pl.BlockSpec((K1_BLOCK, 2 * N2, 2 * N2), lambda kb, o: (kb, 0, 0)), grp(0), grp(2)],
        out_specs=pl.BlockSpec((K1_BLOCK, 2 * N2, HC), lambda kb, o: (kb, 0, o)),
        out_shape=jax.ShapeDtypeStruct((n1h, 2 * N2, 2 * HC), F32),
        compiler_params=_cparams(2), name="dft_stage2_filter")(mat, a7, a7)


CONV_SCRATCH_BYTES = 16 * 1024 * 1024
CONV_J_BLOCK = 4


def _conv_fused_kernel(w1_ref, wi_ref, m_ref, mt_ref, h_ref, z_ref, gate_ref, d_ref, o_ref, a_scr,
                       *, n1h, sg, nb, scale):
    t = pl.program_id(2)
    n_a = J2 // CONV_J_BLOCK
    cat = lambda pieces: jnp.concatenate(pieces, axis=1) if len(pieces) > 1 else pieces[0]
    strided = lambda ref, idx, m, rows: cat([ref[(cb,) + idx + (pl.ds(m, rows, stride=SUBLANES), slice(None))]
                                             for cb in range(sg)])

    @pl.when(t < n_a)
    def _():
        w = w1_ref[...]
        for jj in range(CONV_J_BLOCK):
            j = t * CONV_J_BLOCK + jj
            for m in range(SUBLANES):
                res = jnp.dot(w, strided(z_ref, (jj,), m, n1h).astype(BF16), preferred_element_type=F32)
                for cb in range(sg):
                    a_scr[cb, j, pl.ds(m, 2 * n1h, stride=SUBLANES), :] = res[:, cb * LANES:(cb + 1) * LANES]

    @pl.when((t >= n_a) & (t < n_a + nb))
    def _():
        kb = t - n_a
        re0 = [pl.multiple_of((kb * K1_BLOCK + kk) * SUBLANES, SUBLANES) for kk in range(K1_BLOCK)]
        im0 = [pl.multiple_of((n1h + kb * K1_BLOCK + kk) * SUBLANES, SUBLANES) for kk in range(K1_BLOCK)]
        tile = lambda cb, r0: a_scr[cb, :, pl.ds(r0, SUBLANES), :].reshape(N2, LANES)
        zs = [jnp.dot(m_ref[kk], cat([jnp.concatenate([tile(cb, re0[kk]), tile(cb, im0[kk])], axis=0)
                                      for cb in range(sg)]).astype(BF16), preferred_element_type=F32)
              for kk in range(K1_BLOCK)]
        ys = []
        for kk, z in enumerate(zs):
            zr, zi = z[:N2], z[N2:]
            hr, hi = h_ref[kk, :N2], h_ref[kk, N2:]
            ys.append(jnp.concatenate([zr * hr - zi * hi, zr * hi + zi * hr], axis=0).astype(BF16))
        gs = [jnp.dot(mt_ref[kk], y, preferred_element_type=F32) for kk, y in enumerate(ys)]
        for kk, g in enumerate(gs):
            for cb in range(sg):
                lanes = slice(cb * LANES, (cb + 1) * LANES)
                a_scr[cb, :, pl.ds(re0[kk], SUBLANES), :] = g[:N2, lanes].reshape(J2, SUBLANES, LANES)
                a_scr[cb, :, pl.ds(im0[kk], SUBLANES), :] = g[N2:, lanes].reshape(J2, SUBLANES, LANES)

    @pl.when(t >= n_a + nb)
    def _():
        w, d = wi_ref[...], d_ref[...]
        for jj in range(CONV_J_BLOCK):
            j = (t - n_a - nb) * CONV_J_BLOCK + jj
            for m in range(SUBLANES):
                gm = cat([a_scr[cb, j, pl.ds(m, 2 * n1h, stride=SUBLANES), :] for cb in range(sg)]).astype(BF16)
                y = jnp.dot(w, gm, preferred_element_type=F32) * scale
                res = strided(gate_ref, (jj,), m, n1h) * (y + d * strided(z_ref, (jj,), m, n1h))
                for cb in range(sg):
                    o_ref[cb, jj, pl.ds(m, n1h, stride=SUBLANES), :] = res[:, cb * LANES:(cb + 1) * LANES]


def hyena_conv_fused(z, gate, d_vec, spec, order, tabs):
    batch, n1h = z.shape[0], z.shape[3]
    w1_half, w1_inv, mat, mat_t = tabs
    sg = max(1, min(SLABS, CONV_SCRATCH_BYTES // (J2 * 2 * n1h * SUBLANES * LANES * 4)))
    nb = n1h // K1_BLOCK
    n_a = J2 // CONV_J_BLOCK
    rows = lambda a: a.reshape(batch, SLABS, J2, n1h * SUBLANES, LANES)
    phase_c = lambda t: jnp.maximum(t - n_a - nb, 0)
    z_idx = lambda t: jnp.where(t < n_a + nb, jnp.minimum(t, n_a - 1), t - n_a - nb)
    tile = lambda f: pl.BlockSpec((None, sg, CONV_J_BLOCK, n1h * SUBLANES, LANES), lambda b, s, t: (b, s, f(t), 0, 0))
    kblk = lambda t: jnp.clip(t - n_a, 0, nb - 1)
    const = lambda a: pl.BlockSpec(a.shape, lambda b, s, t: (0, 0))
    mspec = pl.BlockSpec((K1_BLOCK, 2 * N2, 2 * N2), lambda b, s, t: (kblk(t), 0, 0))
    out = pl.pallas_call(
        functools.partial(_conv_fused_kernel, n1h=n1h, sg=sg, nb=nb, scale=1.0 / (n1h * N2)),
        grid=(batch, SLABS // sg, 2 * n_a + nb),
        in_specs=[const(w1_half), const(w1_inv), mspec, mspec,
                  pl.BlockSpec((K1_BLOCK, 2 * N2, sg * LANES),
                               lambda b, s, t: (kblk(t), 0, order * (SLABS // sg) + s)),
                  tile(z_idx), tile(phase_c),
                  pl.BlockSpec((1, sg * LANES), lambda b, s, t: (0, s))],
        out_specs=tile(phase_c),
        out_shape=jax.ShapeDtypeStruct((batch, SLABS, J2, n1h * SUBLANES, LANES), F32),
        scratch_shapes=[pltpu.VMEM((sg, J2, 2 * n1h * SUBLANES, LANES), F32)],
        compiler_params=_cparams(3), name="hyena_conv")(w1_half, w1_inv, mat, mat_t, spec, rows(z), rows(gate),
                                                        d_vec.reshape(1, HC))
    return out.reshape(z.shape)


def _silu(g):
    return g / (1.0 + jnp.exp(-g))


def _band_mask(t, halo, q0, seq_len):
    qi = lax.broadcasted_iota(jnp.int32, (t, t + 2 * halo), 0)
    kj = lax.broadcasted_iota(jnp.int32, (t, t + 2 * halo), 1)
    kpos = q0 - halo + kj
    return (jnp.abs(kj - halo - qi) <= halo) & (kpos >= 0) & (kpos < seq_len)


def _head_pair_operands(win):
    x = win.astype(F32)
    swapped = pltpu.roll(x, B_HEAD_DIM, 1)
    low = lax.broadcasted_iota(jnp.int32, x.shape, 1) < B_HEAD_DIM
    place = lambda cond, val: jnp.where(cond, val, 0.0).astype(BF16)
    return ((place(low, x), place(~low, swapped)), (place(low, swapped), place(~low, x)))


def _band_attn_kernel(sink_ref, q_ref, kp_ref, kc_ref, kn_ref, vp_ref, vc_ref, vn_ref, gate_ref, o_ref, *, seq_len):
    t, hd, halo = ATTN_SUB, B_HEAD_DIM, B_BLOCK
    kpads = _head_pair_operands(jnp.concatenate([kp_ref[...], kc_ref[...], kn_ref[...]], axis=0))
    vpads = _head_pair_operands(jnp.concatenate([vp_ref[...], vc_ref[...], vn_ref[...]], axis=0))
    tq = q_ref.shape[0]
    heads = [(pair, par) for pair in range(B_HEADS // 2) for par in range(2)]
    for u in range(tq // t):
        valid = _band_mask(t, halo, pl.program_id(1) * tq + u * t, seq_len)
        rows = slice(u * t, (u + 1) * t)
        win = slice(u * t, (u + 1) * t + 2 * halo)
        scores = [lax.dot_general(q_ref[rows, pair * LANES:(pair + 1) * LANES], kpads[(2 * pair) // B_GROUP][par][win],
                                  (((1,), (1,)), ((), ())), preferred_element_type=F32) for pair, par in heads]
        probs, denoms = [], []
        scale = hd ** -0.5
        for (pair, par), s in zip(heads, scores):
            s = jnp.where(valid, s, NEG_INF)
            sink = sink_ref[2 * pair + par]
            m_raw = jnp.maximum(jnp.max(s, axis=-1, keepdims=True), sink * (1.0 / scale))
            p = jnp.exp2((s - m_raw) * (scale * LOG2E))
            denoms.append(jnp.sum(p, axis=-1, keepdims=True) + jnp.exp2((sink * (1.0 / scale) - m_raw) * (scale * LOG2E)))
            probs.append(p.astype(BF16))
        outs = [jnp.dot(p, vpads[(2 * pair) // B_GROUP][par][win], preferred_element_type=F32) / d
                for (pair, par), p, d in zip(heads, probs, denoms)]
        for pair in range(B_HEADS // 2):
            lanes = slice(pair * LANES, (pair + 1) * LANES)
            o_ref[rows, lanes] = ((outs[2 * pair] + outs[2 * pair + 1]) * _silu(gate_ref[rows, lanes])).astype(BF16)


def band_attn(qk, v_att, at_gate, sink, batch, seq_len, *, nsub=4):
    t = B_BLOCK
    tq = t * nsub
    nblk = seq_len // t
    qk3 = qk.reshape(batch, seq_len, qk.shape[1])
    v3 = v_att.reshape(batch, seq_len, LANES)
    g3 = at_gate.reshape(batch, seq_len, at_gate.shape[1])
    k_col = B_HEADS * B_HEAD_DIM // LANES
    halo = lambda col, f: pl.BlockSpec((None, t, LANES), lambda b, i: (b, f(i), col))
    cur = lambda col: pl.BlockSpec((None, tq, LANES), lambda b, i: (b, i, col))
    prev = lambda i: jnp.maximum(i * nsub - 1, 0)
    nxt = lambda i: jnp.minimum((i + 1) * nsub, nblk - 1)
    wide = pl.BlockSpec((None, tq, B_HEADS * B_HEAD_DIM), lambda b, i: (b, i, 0))
    out = pl.pallas_call(
        functools.partial(_band_attn_kernel, seq_len=seq_len), grid=(batch, seq_len // tq),
        in_specs=[pl.BlockSpec(memory_space=pltpu.SMEM), wide,
                  halo(k_col, prev), cur(k_col), halo(k_col, nxt), halo(0, prev), cur(0), halo(0, nxt), wide],
        out_specs=wide,
        out_shape=jax.ShapeDtypeStruct((batch, seq_len, B_HEADS * B_HEAD_DIM), BF16),
        compiler_params=_cparams(2), name="band_attn")(sink, qk3, qk3, qk3, qk3, v3, v3, v3, g3)
    return out.reshape(batch * seq_len, -1)


def _dil_attn_kernel(q_ref, kp_ref, kc_ref, kn_ref, vp_ref, vc_ref, vn_ref, o_ref, lse_ref, *, seq_len):
    t, hd, halo = ATTN_SUB, C_HEAD_DIM, C_RADIUS
    tq = q_ref.shape[0]
    nsub = tq // t
    lane = lax.broadcasted_iota(jnp.int32, (t, LANES), 1)
    cols = [slice(h * hd, (h + 1) * hd) for h in range(C_HEADS)]
    scale = hd ** -0.5
    rows = lambda u: slice(u * t, (u + 1) * t)
    window = lambda u, p_ref, c_ref, n_ref, sl: jnp.concatenate(
        [p_ref[:, sl], c_ref[:, sl], n_ref[:, sl]], axis=0)[u * t:(u + 1) * t + 2 * halo]

    def qk_stage(u):
        return [lax.dot_general(q_ref[rows(u), sl], window(u, kp_ref, kc_ref, kn_ref, sl), (((1,), (1,)), ((), ())),
                                preferred_element_type=F32) for sl in cols]

    def softmax_stage(u, scores):
        valid = _band_mask(t, halo, pl.program_id(1) * tq + u * t, seq_len)
        probs, denoms = [], []
        lse_tile = jnp.zeros((t, LANES), F32)
        for h, s in enumerate(scores):
            s = jnp.where(valid, s, NEG_INF)
            m_raw = jnp.max(s, axis=-1, keepdims=True)
            p = jnp.exp2((s - m_raw) * (scale * LOG2E))
            denom = jnp.sum(p, axis=-1, keepdims=True)
            lse_tile = jnp.where(lane == h, m_raw * scale + jnp.log(denom), lse_tile)
            probs.append(p.astype(BF16))
            denoms.append(denom)
        lse_ref[rows(u), :] = lse_tile
        return probs, denoms

    def pv_stage(u, probs, denoms):
        for sl, p, denom in zip(cols, probs, denoms):
            o = jnp.dot(p, window(u, vp_ref, vc_ref, vn_ref, sl), preferred_element_type=F32) / denom
            o_ref[rows(u), sl] = o.astype(BF16)

    for u in range(nsub):
        pv_stage(u, *softmax_stage(u, qk_stage(u)))


def dil_attn(qk, v, n_seq, ls):
    w = C_HEADS * C_HEAD_DIM
    t = 2 * C_RADIUS
    nsub = next(n for n in (8, 4, 2, 1) if ls % (n * t) == 0)
    tq = t * nsub
    n_halo = ls // C_RADIUS
    qk3 = qk.reshape(n_seq, ls, qk.shape[1])
    v3 = v.reshape(n_seq, ls, w)
    prev = lambda i: jnp.maximum(i * (tq // C_RADIUS) - 1, 0)
    nxt = lambda i: jnp.minimum((i + 1) * (tq // C_RADIUS), n_halo - 1)
    big = lambda col: pl.BlockSpec((None, tq, w), lambda b, i: (b, i, col))
    halo = lambda col, f: pl.BlockSpec((None, C_RADIUS, w), lambda b, i: (b, f(i), col))
    o, lse = pl.pallas_call(
        functools.partial(_dil_attn_kernel, seq_len=ls), grid=(n_seq, ls // tq),
        in_specs=[big(0), halo(1, prev), big(1), halo(1, nxt), halo(0, prev), big(0), halo(0, nxt)],
        out_specs=(big(0), pl.BlockSpec((None, tq, LANES), lambda b, i: (b, i, 0))),
        out_shape=(jax.ShapeDtypeStruct((n_seq, ls, w), BF16),
                   jax.ShapeDtypeStruct((n_seq, ls, LANES), F32)),
        compiler_params=_cparams(2), name="dil_attn")(qk3, qk3, qk3, qk3, v3, v3, v3)
    return o.reshape(n_seq * ls, w), lse.reshape(n_seq * ls, LANES)


def _out_proj_even_kernel(hy_ref, at_ref, x_ref, w_ref, o_ref, *, nb):
    mixed = jnp.concatenate([_load_time_tiles(hy_ref, nb).astype(BF16), at_ref[...]], axis=1)
    o_ref[...] = x_ref[...] + jnp.dot(mixed, w_ref[...], preferred_element_type=F32)


def out_proj_even(hy, at, x2d, w_bf, *, tm=512):
    m, d = x2d.shape
    batch, n1h = hy.shape[0], hy.shape[3]
    tps = n1h * N2 // tm
    nb = tm // N2
    half = pl.BlockSpec((tm, HC), lambda b, i: (b * tps + i, 0))
    full = pl.BlockSpec((tm, d), lambda b, i: (b * tps + i, 0))
    return pl.pallas_call(
        functools.partial(_out_proj_even_kernel, nb=nb), grid=(batch, tps),
        in_specs=[pl.BlockSpec((None, SLABS, J2, nb, SUBLANES, LANES), lambda b, i: (b, 0, 0, i, 0, 0)), half, full,
                  pl.BlockSpec((d, d), lambda b, i: (0, 0))],
        out_specs=full, out_shape=jax.ShapeDtypeStruct((m, d), F32),
        compiler_params=_cparams(2), name="out_proj_even")(hy, at, x2d, w_bf)


def _merge_out_kernel(*refs, final, dilations):
    nh = C_HEADS
    o_nat, lse_nat = refs[0], refs[1]
    pos = 2
    strided = []
    for _ in dilations:
        strided.append((refs[pos:pos + nh], refs[pos + nh]))
        pos += nh + 1
    gate_ref, x_ref, w_ref = refs[pos:pos + 3]
    pos += 3
    g_ref = refs[pos] if final else None
    out_ref, o_scr, l_scr = refs[-3:]
    for gi, (d, (o_slabs, lse_ref)) in enumerate(zip(dilations, strided)):
        per = o_nat.shape[0] // d
        for r in range(d):
            l_scr[gi, pl.ds(r, per, stride=d), :] = lse_ref[r]
            for h in range(nh):
                o_scr[gi, h, pl.ds(r, per, stride=d), :] = o_slabs[h][r].astype(F32)
    ls = [lse_nat[...]] + [l_scr[gi] for gi in range(len(dilations))]
    mx = functools.reduce(jnp.maximum, ls)
    es = [jnp.exp(l - mx) for l in ls]
    den = functools.reduce(lambda a, b: a + b, es)
    alphas = [e / den for e in es]
    hd = C_HEAD_DIM
    parts = []
    for h in range(nh):
        acc = alphas[0][:, h:h + 1] * o_nat[:, h * hd:(h + 1) * hd].astype(F32)
        for gi in range(len(dilations)):
            acc = acc + alphas[gi + 1][:, h:h + 1] * o_scr[gi, h]
        parts.append(acc)
    y = (jnp.concatenate(parts, axis=1) * _silu(gate_ref[...])).astype(BF16)
    r = x_ref[...] + jnp.dot(y, w_ref[...], preferred_element_type=F32)
    if final:
        ms = jnp.mean(r * r, axis=-1, keepdims=True)
        r = r * lax.rsqrt(ms + NORM_EPS) * g_ref[...]
    out_ref[...] = r


def merge_out(o_nat, lse_nat, strided, gate, x2d, w_bf, batch, seq_len, final_g=None, *, tm=512):
    m, dm = x2d.shape
    tps = seq_len // tm
    nat = lambda wd: pl.BlockSpec((tm, wd), lambda b, i: (b * tps + i, 0))
    in_specs, args = [nat(dm), nat(LANES)], [o_nat, lse_nat]
    for d, o, lse in strided:
        per = tm // d
        in_specs += [pl.BlockSpec((None, d, per, LANES), functools.partial(lambda h, b, i: (b, 0, i, h), h))
                     for h in range(C_HEADS)]
        in_specs.append(pl.BlockSpec((None, d, per, LANES), lambda b, i: (b, 0, i, 0)))
        args += [o] * C_HEADS + [lse]
    in_specs += [nat(dm), nat(dm), pl.BlockSpec((dm, dm), lambda b, i: (0, 0))]
    args += [gate, x2d, w_bf]
    if final_g is not None:
        in_specs.append(pl.BlockSpec((1, dm), lambda b, i: (0, 0)))
        args.append(final_g.reshape(1, dm))
    n_str = len(strided)
    return pl.pallas_call(
        functools.partial(_merge_out_kernel, final=final_g is not None, dilations=tuple(d for d, _, _ in strided)),
        grid=(batch, tps), in_specs=in_specs, out_specs=nat(dm), out_shape=jax.ShapeDtypeStruct((m, dm), F32),
        scratch_shapes=[pltpu.VMEM((n_str, C_HEADS, tm, LANES), F32), pltpu.VMEM((n_str, tm, LANES), F32)],
        compiler_params=_cparams(2), name="merge_out")(*args)


def _even_w_in(w_in):
    return jnp.concatenate([w_in[:, 2048:2688], w_in[:, :2048], w_in[:, 2816:], w_in[:, 2688:2816]], axis=1).astype(BF16)


def _odd_w_in(w_in):
    w = C_HEADS * C_HEAD_DIM
    blk = lambda g, t: w_in[:, (3 * g + t) * w:(3 * g + t + 1) * w]
    per = [[blk(g, 0), blk(g, 1), blk(g, 2)] for g in range(len(C_PATTERNS))]
    per[0].append(w_in[:, 9 * w:])
    return [jnp.concatenate(cols, axis=1).astype(BF16) for cols in per]


def hyena_spectrum(seq_len, tabs, fw1, fb1, ff1, fw2, fb2, ff2, fw3):
    n1h = seq_len // N2
    w1_half, _, mat, _ = tabs
    filt = hyena_filter(seq_len, fw1, fb1, ff1, fw2, fb2, ff2, fw3)
    a = dft_stage1(w1_half, filt.reshape(1, 4 * SLABS, J2, n1h * SUBLANES, LANES))
    return dft_stage2_filter(mat, a.reshape(1, 4 * SLABS, J2, 2, n1h, SUBLANES, LANES))


def hybrid_layer(x2d, batch, seq_len, norm_g, w_in_bf, conv_w, conv_b, hyena_d, sink, w_out_bf, spec, tabs, rope):
    rope_t, half = rope
    qk, x1, x2s, v, at_gate, v_att = hybrid_proj(x2d, norm_g, w_in_bf, conv_w, conv_b, batch, seq_len, rope_t, half)
    z = hyena_conv_fused(v, x1, hyena_d[0], spec, 0, tabs)
    z = hyena_conv_fused(z, x2s, hyena_d[1], spec, 1, tabs)
    at = band_attn(qk, v_att, at_gate, sink, batch, seq_len)
    return out_proj_even(z, at, x2d, w_out_bf)


def dilated_layer(x2d, batch, seq_len, norm_g, w_in_bfs, w_out_bf, rope, final_g):
    w = C_HEADS * C_HEAD_DIM
    m = batch * seq_len
    rope_t, half = rope
    o_nat = lse_nat = gate = None
    strided = []
    for gi, (_, d) in enumerate(C_PATTERNS):
        ls = seq_len // d
        if d == 1:
            qk, v, gate = norm_proj(x2d, norm_g, w_in_bfs[gi], seq_len, rope_t, half,
                                    ((2 * w, BF16), (w, BF16), (w, F32)))
            o_nat, lse_nat = dil_attn(qk, v, batch, ls)
        else:
            qk, v = norm_proj_strided(x2d, norm_g, w_in_bfs[gi], batch, seq_len, d, rope_t, half,
                                      ((2 * w, BF16), (w, BF16)))
            o, lse = dil_attn(qk.reshape(m, 2 * w), v.reshape(m, w), batch * d, ls)
            strided.append((d, o.reshape(batch, d, ls, w), lse.reshape(batch, d, ls, LANES)))
    return merge_out(o_nat, lse_nat, strided, gate, x2d, w_out_bf, batch, seq_len, final_g)


def kernel(x_prompt, x_sample, a_norm, a_w_in, a_conv_w, a_conv_b, a_filt_w1, a_filt_b1, a_filt_f1, a_filt_w2, a_filt_b2, a_filt_f2, a_filt_w3, a_hyena_d, a_sink, a_w_out, c_norm, c_w_in, c_w_out, final_norm):
    depth = a_norm.shape[0] + c_norm.shape[0]
    xs = [x_prompt, x_sample]
    shapes = [(x.shape[0], x.shape[1]) for x in xs]
    acts = [x.reshape(-1, D_MODEL) for x in xs]
    seq_lens = sorted({s[1] for s in shapes})
    tabs = {sl: dft_tables(sl // N2) for sl in seq_lens}
    rope_even = {sl: rope_tables(sl, B_HEAD_DIM) for sl in seq_lens}
    rope_odd = {sl: rope_tables(sl, C_HEAD_DIM) for sl in seq_lens}
    for layer in range(depth):
        i = layer // 2
        if layer % 2 == 0:
            w_in_bf = _even_w_in(a_w_in[i])
            w_out_bf = a_w_out[i].astype(BF16)
            specs = {sl: hyena_spectrum(sl, tabs[sl], a_filt_w1[i], a_filt_b1[i], a_filt_f1[i], a_filt_w2[i],
                                        a_filt_b2[i], a_filt_f2[i], a_filt_w3[i]) for sl in seq_lens}
            acts = [hybrid_layer(x2d, b, sl, a_norm[i], w_in_bf, a_conv_w[i], a_conv_b[i], a_hyena_d[i], a_sink[i],
                                 w_out_bf, specs[sl], tabs[sl], rope_even[sl])
                    for x2d, (b, sl) in zip(acts, shapes)]
        else:
            w_in_bfs = _odd_w_in(c_w_in[i])
            w_out_bf = c_w_out[i].astype(BF16)
            final_g = final_norm if layer == depth - 1 else None
            acts = [dilated_layer(x2d, b, sl, c_norm[i], w_in_bfs, w_out_bf, rope_odd[sl], final_g)
                    for x2d, (b, sl) in zip(acts, shapes)]
    assert depth % 2 == 0
    return tuple(a.reshape(b, sl, D_MODEL) for a, (b, sl) in zip(acts, shapes))
```

```python
import functools
import math

import numpy as np
import jax
import jax.numpy as jnp
from jax import lax
from jax.experimental import pallas as pl
from jax.experimental.pallas import tpu as pltpu

F32 = jnp.float32
BF16 = jnp.bfloat16

D_MODEL = 1024
HC = 512
FILTER_BANDS = 16
DECAY_TARGET = 1e-2
FAST_DECAY_PCT = 0.3
SLOW_DECAY_PCT = 1.5
DECAY_SHIFT = 0.05
B_HEAD_DIM = 64
B_HEADS = 8
B_GROUP = 4
B_BLOCK = 128
C_PATTERNS = ((128, 1), (512, 4), (2048, 16))
C_HEADS = 8
C_HEAD_DIM = 128
C_RADIUS = 64
ROPE_THETA = 500000.0
NORM_EPS = 1e-6
NEG_INF = -1e30
LOG2E = math.log2(math.e)

LANES = 128
SUBLANES = 8
N2 = 128
J2 = N2 // SUBLANES
SLABS = HC // LANES
K1_BLOCK = 8
ATTN_SUB = 128
VMEM_LIMIT = 56 * 1024 * 1024


def _cparams(n_axes):
    return pltpu.CompilerParams(dimension_semantics=("arbitrary",) * n_axes,
                                vmem_limit_bytes=VMEM_LIMIT)


def _rope_chunk(blk, cos, sa, sb, half):
    return (blk * cos + pltpu.roll(blk, LANES - half, 1) * sa + pltpu.roll(blk, half, 1) * sb)


def _two_deep(chunks, matmul, epilogue):
    prev = None
    for chunk in chunks:
        acc = matmul(chunk)
        if prev is not None:
            epilogue(*prev)
        prev = (chunk, acc)
    epilogue(*prev)


def _norm_proj_kernel(x_ref, g_ref, w_ref, cos_ref, sa_ref, sb_ref, *outs, half, groups, row_chunk):
    g = g_ref[...]
    chunks = [slice(r0, r0 + row_chunk) for r0 in range(0, x_ref.shape[0], row_chunk)]
    matmul = lambda rows: jnp.dot(_rms_bf16(x_ref[rows, :], g), w_ref[...], preferred_element_type=F32)

    def epilogue(rows, acc):
        col = 0
        for gi, (o_ref, (width, dtype)) in enumerate(zip(outs, groups)):
            if gi == 0:
                cos, sa, sb = cos_ref[rows, :], sa_ref[rows, :], sb_ref[rows, :]
                for ci in range(width // LANES):
                    src = slice(col + ci * LANES, col + (ci + 1) * LANES)
                    o_ref[rows, ci * LANES:(ci + 1) * LANES] = _rope_chunk(acc[:, src], cos, sa, sb, half).astype(dtype)
            else:
                o_ref[rows, :] = acc[:, col:col + width].astype(dtype)
            col += width

    _two_deep(chunks, matmul, epilogue)


def _resident(shape):
    return pl.BlockSpec(shape, lambda *_: (0,) * len(shape), pipeline_mode=pl.Buffered(1))


def norm_proj(x2d, g, w_bf, seq_len, rope, half, groups, *, tm=1024, row_chunk=256):
    m, d = x2d.shape
    n = w_bf.shape[1]
    assert m % tm == 0 and seq_len % tm == 0 and sum(wd for wd, _ in groups) == n
    tiles_per_seq = seq_len // tm
    in_specs = [pl.BlockSpec((tm, d), lambda i: (i, 0)), _resident((1, d)), _resident((d, n))]
    in_specs += [pl.BlockSpec((tm, LANES), lambda i: (i % tiles_per_seq, 0))] * 3
    return pl.pallas_call(
        functools.partial(_norm_proj_kernel, half=half, groups=tuple(groups), row_chunk=row_chunk),
        grid=(m // tm,), in_specs=in_specs,
        out_specs=tuple(pl.BlockSpec((tm, wd), lambda i: (i, 0)) for wd, _ in groups),
        out_shape=tuple(jax.ShapeDtypeStruct((m, wd), dt) for wd, dt in groups),
        compiler_params=_cparams(1), name="norm_proj")(x2d, g.reshape(1, d), w_bf, *rope)


def _by_residue(ref, d, residues):
    per = ref.shape[0] // d
    return jnp.concatenate([ref[pl.ds(r, per, stride=d), :] for r in residues], axis=0)


def _norm_proj_strided_kernel(*refs, half, groups, d, row_chunk):
    n_slab = D_MODEL // LANES
    x_slabs = refs[:n_slab]
    g_ref, w_ref, cos_ref, sa_ref, sb_ref = refs[n_slab:n_slab + 5]
    outs = refs[n_slab + 5:]
    tm = x_slabs[0].shape[0]
    per = tm // d
    g = g_ref[...]
    chunks = [range(r0, r0 + row_chunk // per) for r0 in range(0, d, row_chunk // per)]
    matmul = lambda res: jnp.dot(_rms_bf16(jnp.concatenate([_by_residue(xs, d, res) for xs in x_slabs], axis=1), g),
                                 w_ref[...], preferred_element_type=F32)

    def epilogue(res, acc):
        col = 0
        for gi, (o_ref, (width, dtype)) in enumerate(zip(outs, groups)):
            if gi == 0:
                cos, sa, sb = (_by_residue(t_ref, d, res) for t_ref in (cos_ref, sa_ref, sb_ref))
                val = jnp.concatenate([_rope_chunk(acc[:, col + ci * LANES:col + (ci + 1) * LANES], cos, sa, sb, half)
                                       for ci in range(width // LANES)], axis=1).astype(dtype)
            else:
                val = acc[:, col:col + width].astype(dtype)
            for k, r in enumerate(res):
                o_ref[r] = val[k * per:(k + 1) * per]
            col += width

    _two_deep(chunks, matmul, epilogue)


def norm_proj_strided(x2d, g, w_bf, batch, seq_len, d, rope, half, groups, *, tm=1024, row_chunk=256):
    m, dm = x2d.shape
    n = w_bf.shape[1]
    tps = seq_len // tm
    per = tm // d
    assert seq_len % tm == 0 and row_chunk % per == 0 and sum(wd for wd, _ in groups) == n
    in_specs = [pl.BlockSpec((tm, LANES), functools.partial(lambda c, t: (t, c), c)) for c in range(dm // LANES)]
    in_specs += [_resident((1, dm)), _resident((dm, n))]
    in_specs += [pl.BlockSpec((tm, LANES), lambda t: (t % tps, 0))] * 3
    return pl.pallas_call(
        functools.partial(_norm_proj_strided_kernel, half=half, groups=tuple(groups), d=d, row_chunk=row_chunk),
        grid=(m // tm,), in_specs=in_specs,
        out_specs=tuple(pl.BlockSpec((None, d, per, wd), lambda t: (t // tps, 0, t % tps, 0)) for wd, _ in groups),
        out_shape=tuple(jax.ShapeDtypeStruct((batch, d, seq_len // d, wd), dt) for wd, dt in groups),
        compiler_params=_cparams(1), name="norm_proj_strided")(*([x2d] * (dm // LANES)), g.reshape(1, dm), w_bf, *rope)


def _rms_bf16(x, g):
    ms = jnp.mean(x * x, axis=-1, keepdims=True)
    return (x * lax.rsqrt(ms + NORM_EPS) * g).astype(BF16)


_QK_W, _HY_W, _ATT_V_W = 640, 3 * HC, 128
_HY0 = _QK_W
_HYG0 = _HY0 + _HY_W
_ATG0 = _HYG0 + HC
_VATT0 = _ATG0 + HC


def _hybrid_proj_kernel(x_ref, xp_ref, xn_ref, g_ref, w_ref, cos_ref, sa_ref, sb_ref, cw_ref, cb_ref,
                        qk_ref, x1_ref, x2_ref, v_ref, atg_ref, vatt_ref, u_scr, *, half, tiles_per_seq, row_chunk):
    tm = x_ref.shape[0]
    i = pl.program_id(0) % tiles_per_seq
    g = g_ref[...]
    halo = jnp.concatenate([xp_ref[...], xn_ref[...]], axis=0)
    acc_h = jnp.dot(_rms_bf16(halo, g), w_ref[:, _HY0:_HYG0], preferred_element_type=F32)
    prev_row = jnp.where(i > 0, acc_h[SUBLANES - 1:SUBLANES], 0.0)
    next_row = jnp.where(i < tiles_per_seq - 1, acc_h[SUBLANES:SUBLANES + 1], 0.0)
    chunks = [slice(r0, r0 + row_chunk) for r0 in range(0, tm, row_chunk)]
    accs = [jnp.dot(_rms_bf16(x_ref[rows, :], g), w_ref[...], preferred_element_type=F32) for rows in chunks]
    for rows, acc in zip(chunks, accs):
        cos, sa, sb = cos_ref[rows, :], sa_ref[rows, :], sb_ref[rows, :]
        for ci in range(_QK_W // LANES):
            sl = slice(ci * LANES, (ci + 1) * LANES)
            qk_ref[rows, sl] = _rope_chunk(acc[:, sl], cos, sa, sb, half).astype(BF16)
        u_scr[rows, :] = acc[:, _HY0:_ATG0]
        atg_ref[rows, :] = acc[:, _ATG0:_VATT0]
        vatt_ref[rows, :] = acc[:, _VATT0:].astype(BF16)
    u = u_scr[:, :_HY_W]
    row = lax.broadcasted_iota(jnp.int32, u.shape, 0)
    um1 = jnp.where(row == 0, prev_row, pltpu.roll(u, 1, 0))
    up1 = jnp.where(row == tm - 1, next_row, pltpu.roll(u, tm - 1, 0))
    w = cw_ref[...]
    y = um1 * w[0:1, :] + u * w[1:2, :] + up1 * w[2:3, :] + cb_ref[...]
    _store_time_tiles(x1_ref, y[:, :HC])
    _store_time_tiles(x2_ref, y[:, HC:2 * HC] * _silu(u_scr[:, _HY_W:]))
    _store_time_tiles(v_ref, y[:, 2 * HC:])


def hybrid_proj(x2d, g, w_bf, conv_w, conv_b, batch, seq_len, rope, half, *, tm=512, row_chunk=256):
    m, d = x2d.shape
    n = w_bf.shape[1]
    tiles_per_seq = seq_len // tm
    sub = tm // SUBLANES
    last8 = m // SUBLANES - 1
    hy_shape = jax.ShapeDtypeStruct((batch, SLABS, J2, seq_len // N2, SUBLANES, LANES), F32)
    hy_spec = pl.BlockSpec((None, SLABS, J2, tm // N2, SUBLANES, LANES),
                           lambda t: (t // tiles_per_seq, 0, 0, t % tiles_per_seq, 0, 0))
    row_blk = lambda wd: pl.BlockSpec((tm, wd), lambda t: (t, 0))
    return pl.pallas_call(
        functools.partial(_hybrid_proj_kernel, half=half, tiles_per_seq=tiles_per_seq, row_chunk=row_chunk),
        grid=(m // tm,),
        in_specs=[row_blk(d),
                  pl.BlockSpec((SUBLANES, d), lambda t: (jnp.maximum(t * sub - 1, 0), 0)),
                  pl.BlockSpec((SUBLANES, d), lambda t: (jnp.minimum((t + 1) * sub, last8), 0)),
                  _resident((1, d)), _resident((d, n))]
                 + [pl.BlockSpec((tm, LANES), lambda t: (t % tiles_per_seq, 0))] * 3
                 + [_resident((3, _HY_W)), _resident((1, _HY_W))],
        out_specs=(row_blk(_QK_W), hy_spec, hy_spec, hy_spec, row_blk(HC), row_blk(_ATT_V_W)),
        out_shape=(jax.ShapeDtypeStruct((m, _QK_W), BF16), hy_shape, hy_shape, hy_shape,
                   jax.ShapeDtypeStruct((m, HC), F32), jax.ShapeDtypeStruct((m, _ATT_V_W), BF16)),
        scratch_shapes=[pltpu.VMEM((tm, _HY_W + HC), F32)],
        compiler_params=_cparams(1), name="hybrid_proj")(x2d, x2d, x2d, g.reshape(1, d), w_bf, *rope,
                                                         conv_w, conv_b.reshape(1, -1))


def rope_tables(seq_len, head_dim):
    rot = head_dim // 4
    half = rot // 2
    inv = jnp.power(ROPE_THETA, -2.0 * jnp.arange(half, dtype=F32) / rot)
    ang = jnp.arange(seq_len).astype(F32)[:, None] * inv[None, :]
    cos, sin = jnp.cos(ang), jnp.sin(ang)
    lane = np.arange(LANES) % head_dim
    idx = lane % half
    in_rot = jnp.asarray(lane < rot)[None, :]
    first = jnp.asarray(lane < half)[None, :]
    cos_t = jnp.where(in_rot, cos[:, idx], 1.0)
    sa = jnp.where(first, -sin[:, idx], 0.0)
    sb = jnp.where(in_rot & ~first, sin[:, idx], 0.0)
    return (cos_t, sa, sb), half


def _store_time_tiles(ref, val):
    for s in range(SLABS):
        for i in range(val.shape[0] // N2):
            ref[s, :, i] = val[i * N2:(i + 1) * N2, s * LANES:(s + 1) * LANES].reshape(J2, SUBLANES, LANES)


def _load_time_tiles(ref, nb):
    return jnp.concatenate([jnp.concatenate([ref[s, :, i].reshape(N2, LANES) for s in range(SLABS)], axis=1)
                            for i in range(nb)], axis=0)


def _filter_kernel(band_ref, w1_ref, b1_ref, f1_ref, w2_ref, b2_ref, f2_ref, w3_ref, dl_ref, o_ref,
                   *, seq_len, tr):
    j = (pl.program_id(0) * tr + lax.broadcasted_iota(jnp.int32, (tr, 1), 0)).astype(F32)
    t = j * (1.0 / (seq_len - 1))
    wpos = (2.0 * math.pi) * j / seq_len
    lane = lax.broadcasted_iota(jnp.int32, (tr, LANES), 1)
    arg = band_ref[...] * wpos
    feats = jnp.where(lane == 0, t,
                      jnp.where(lane <= FILTER_BANDS, jnp.cos(arg),
                                jnp.where(lane <= 2 * FILTER_BANDS, -jnp.sin(arg), 0.0)))
    h = jnp.dot(feats.astype(BF16), w1_ref[...], preferred_element_type=F32) + b1_ref[...]
    h = jnp.sin(f1_ref[...] * h)
    h = jnp.dot(h.astype(BF16), w2_ref[...], preferred_element_type=F32) + b2_ref[...]
    h = jnp.sin(f2_ref[...] * h)
    h = jnp.dot(h.astype(BF16), w3_ref[...], preferred_element_type=F32)
    win = jnp.exp(-t * dl_ref[...]) + DECAY_SHIFT
    for grp in range(4):
        _store_time_tiles(o_ref.at[grp * SLABS:(grp + 1) * SLABS], h[:, grp * HC:(grp + 1) * HC] * win)


def hyena_filter(seq_len, w1, b1, f1, w2, b2, f2, w3, *, tr=512):
    hid = w1.shape[1]
    pad = LANES - hid
    w1p = jnp.pad(w1, ((0, LANES - w1.shape[0]), (0, pad))).astype(BF16)
    w2p = jnp.pad(w2, ((0, pad), (0, pad))).astype(BF16)
    w3r = w3.reshape(hid, 2, 2, HC).transpose(0, 2, 1, 3).reshape(hid, 4 * HC)
    w3p = jnp.pad(w3r, ((0, pad), (0, 0))).astype(BF16)
    vec = lambda v: jnp.pad(v, (0, pad)).reshape(1, LANES)
    bands = jnp.linspace(1e-4, FILTER_BANDS - 1, FILTER_BANDS, dtype=F32)
    band_l = jnp.concatenate([jnp.zeros((1,), F32), bands, bands,
                              jnp.zeros((LANES - 1 - 2 * FILTER_BANDS,), F32)]).reshape(1, LANES)
    max_decay = math.log(DECAY_TARGET) / FAST_DECAY_PCT
    min_decay = math.log(DECAY_TARGET) / SLOW_DECAY_PCT
    deltas = jnp.abs(jnp.linspace(min_decay, max_decay, HC, dtype=F32)).reshape(1, HC)
    full = lambda a: pl.BlockSpec(a.shape, lambda i: (0, 0))
    args = [band_l, w1p, vec(b1), vec(f1), w2p, vec(b2), vec(f2), w3p, deltas]
    return pl.pallas_call(
        functools.partial(_filter_kernel, seq_len=seq_len, tr=tr),
        grid=(seq_len // tr,),
        in_specs=[full(a) for a in args],
        out_specs=pl.BlockSpec((4 * SLABS, J2, tr // N2, SUBLANES, LANES), lambda i: (0, 0, i, 0, 0)),
        out_shape=jax.ShapeDtypeStruct((4 * SLABS, J2, seq_len // N2, SUBLANES, LANES), F32),
        compiler_params=_cparams(1), name="hyena_filter")(*args)


def dft_tables(n1h):
    n1 = 2 * n1h
    n = n1 * N2
    k1 = jnp.arange(n1h, dtype=jnp.int32)
    m1 = (jnp.arange(n1, dtype=jnp.int32)[None, :] * (2 * k1[:, None] + 1)) % (2 * n1)
    a1 = m1.astype(F32) * (math.pi / n1)
    c1, s1 = jnp.cos(a1)[:, :n1h], jnp.sin(a1)[:, :n1h]
    w1_half = jnp.concatenate([c1, -s1], axis=0).astype(BF16)
    w1_inv = jnp.concatenate([c1.T, -s1.T], axis=1).astype(BF16)
    k2 = jnp.arange(N2, dtype=jnp.int32)
    n2 = jnp.arange(N2, dtype=jnp.int32)
    f = 2 * (k2[None, :, None] * n1 + k1[:, None, None]) + 1
    m2 = (n2[None, None, :] * f) % (2 * n)
    th = m2.astype(F32) * (math.pi / n)
    c, s = jnp.cos(th), jnp.sin(th)
    mat = jnp.concatenate([jnp.concatenate([c, s], axis=2),
                           jnp.concatenate([-s, c], axis=2)], axis=1)
    return w1_half, w1_inv, mat.astype(BF16), jnp.swapaxes(mat, 1, 2).astype(BF16)


def _rows_of(ref2, m, n_rows):
    return jnp.concatenate([ref2[cb, pl.ds(m, n_rows, stride=SUBLANES), :] for cb in range(SLABS)], axis=1)


def _store_rows(ref2, m, val):
    for cb in range(SLABS):
        ref2[cb, pl.ds(m, val.shape[0], stride=SUBLANES), :] = val[:, cb * LANES:(cb + 1) * LANES]


def _stage1_kernel(w_ref, x_ref, o_ref, *, k, r):
    w = w_ref[...]
    for m in range(SUBLANES):
        _store_rows(o_ref, m, jnp.dot(w, _rows_of(x_ref, m, k).astype(BF16), preferred_element_type=F32))


def dft_stage1(w, x5):
    b, s = x5.shape[:2]
    r, k = w.shape
    assert s % SLABS == 0 and x5.shape[3] == k * SUBLANES
    blk = lambda rows: pl.BlockSpec((None, SLABS, None, rows * SUBLANES, LANES), lambda bi, si, j: (bi, si, j, 0, 0))
    return pl.pallas_call(
        functools.partial(_stage1_kernel, k=k, r=r), grid=(b, s // SLABS, J2),
        in_specs=[pl.BlockSpec((r, k), lambda bi, si, j: (0, 0)), blk(k)],
        out_specs=blk(r),
        out_shape=jax.ShapeDtypeStruct((b, s, J2, r * SUBLANES, LANES), F32),
        compiler_params=_cparams(3), name="dft_stage1")(w, x5)


def _slab_cat(a_ref, kk, n_slabs):
    return jnp.concatenate([jnp.concatenate([a_ref[cb, :, 0, kk].reshape(N2, LANES),
                                             a_ref[cb, :, 1, kk].reshape(N2, LANES)], axis=0)
                            for cb in range(n_slabs)], axis=1).astype(BF16)


def _stage2_filter_kernel(m_ref, af_ref, ab_ref, o_ref):
    for kk in range(K1_BLOCK):
        zf = jnp.dot(m_ref[kk], _slab_cat(af_ref, kk, SLABS), preferred_element_type=F32)
        zb = jnp.dot(m_ref[kk], _slab_cat(ab_ref, kk, SLABS), preferred_element_type=F32)
        o_ref[kk] = jnp.concatenate([zf[:N2] + zb[:N2], zf[N2:] - zb[N2:]], axis=0)


def dft_stage2_filter(mat, a7):
    n1h = a7.shape[4]
    grp = lambda first: pl.BlockSpec((None, SLABS, J2, 2, K1_BLOCK, SUBLANES, LANES),
                                     lambda kb, o: (0, first + o, 0, 0, kb, 0, 0))
    return pl.pallas_call(
        _stage2_filter_kernel, grid=(n1h // K1_BLOCK, 2),
        in_specs=[pl.BlockSpec((K1_BLOCK, 2 * N2, 2 * N2), lambda kb, o: (kb, 0, 0)), grp(0), grp(2)],
        out_specs=pl.BlockSpec((K1_BLOCK, 2 * N2, HC), lambda kb, o: (kb, 0, o)),
        out_shape=jax.ShapeDtypeStruct((n1h, 2 * N2, 2 * HC), F32),
        compiler_params=_cparams(2), name="dft_stage2_filter")(mat, a7, a7)


CONV_SCRATCH_BYTES = 16 * 1024 * 1024
CONV_J_BLOCK = 4


def _conv_fused_kernel(w1_ref, wi_ref, m_ref, mt_ref, h_ref, z_ref, gate_ref, d_ref, o_ref, a_scr,
                       *, n1h, sg, nb, scale):
    t = pl.program_id(2)
    n_a = J2 // CONV_J_BLOCK
    cat = lambda pieces: jnp.concatenate(pieces, axis=1) if len(pieces) > 1 else pieces[0]
    strided = lambda ref, idx, m, rows: cat([ref[(cb,) + idx + (pl.ds(m, rows, stride=SUBLANES), slice(None))]
                                             for cb in range(sg)])

    @pl.when(t < n_a)
    def _():
        w = w1_ref[...]
        for jj in range(CONV_J_BLOCK):
            j = t * CONV_J_BLOCK + jj
            for m in range(SUBLANES):
                res = jnp.dot(w, strided(z_ref, (jj,), m, n1h).astype(BF16), preferred_element_type=F32)
                for cb in range(sg):
                    a_scr[cb, j, pl.ds(m, 2 * n1h, stride=SUBLANES), :] = res[:, cb * LANES:(cb + 1) * LANES]

    @pl.when((t >= n_a) & (t < n_a + nb))
    def _():
        kb = t - n_a
        re0 = [pl.multiple_of((kb * K1_BLOCK + kk) * SUBLANES, SUBLANES) for kk in range(K1_BLOCK)]
        im0 = [pl.multiple_of((n1h + kb * K1_BLOCK + kk) * SUBLANES, SUBLANES) for kk in range(K1_BLOCK)]
        tile = lambda cb, r0: a_scr[cb, :, pl.ds(r0, SUBLANES), :].reshape(N2, LANES)
        zs = [jnp.dot(m_ref[kk], cat([jnp.concatenate([tile(cb, re0[kk]), tile(cb, im0[kk])], axis=0)
                                      for cb in range(sg)]).astype(BF16), preferred_element_type=F32)
              for kk in range(K1_BLOCK)]
        ys = []
        for kk, z in enumerate(zs):
            zr, zi = z[:N2], z[N2:]
            hr, hi = h_ref[kk, :N2], h_ref[kk, N2:]
            ys.append(jnp.concatenate([zr * hr - zi * hi, zr * hi + zi * hr], axis=0).astype(BF16))
        gs = [jnp.dot(mt_ref[kk], y, preferred_element_type=F32) for kk, y in enumerate(ys)]
        for kk, g in enumerate(gs):
            for cb in range(sg):
                lanes = slice(cb * LANES, (cb + 1) * LANES)
                a_scr[cb, :, pl.ds(re0[kk], SUBLANES), :] = g[:N2, lanes].reshape(J2, SUBLANES, LANES)
                a_scr[cb, :, pl.ds(im0[kk], SUBLANES), :] = g[N2:, lanes].reshape(J2, SUBLANES, LANES)

    @pl.when(t >= n_a + nb)
    def _():
        w, d = wi_ref[...], d_ref[...]
        for jj in range(CONV_J_BLOCK):
            j = (t - n_a - nb) * CONV_J_BLOCK + jj
            for m in range(SUBLANES):
                gm = cat([a_scr[cb, j, pl.ds(m, 2 * n1h, stride=SUBLANES), :] for cb in range(sg)]).astype(BF16)
                y = jnp.dot(w, gm, preferred_element_type=F32) * scale
                res = strided(gate_ref, (jj,), m, n1h) * (y + d * strided(z_ref, (jj,), m, n1h))
                for cb in range(sg):
                    o_ref[cb, jj, pl.ds(m, n1h, stride=SUBLANES), :] = res[:, cb * LANES:(cb + 1) * LANES]


def hyena_conv_fused(z, gate, d_vec, spec, order, tabs):
    batch, n1h = z.shape[0], z.shape[3]
    w1_half, w1_inv, mat, mat_t = tabs
    sg = max(1, min(SLABS, CONV_SCRATCH_BYTES // (J2 * 2 * n1h * SUBLANES * LANES * 4)))
    nb = n1h // K1_BLOCK
    n_a = J2 // CONV_J_BLOCK
    rows = lambda a: a.reshape(batch, SLABS, J2, n1h * SUBLANES, LANES)
    phase_c = lambda t: jnp.maximum(t - n_a - nb, 0)
    z_idx = lambda t: jnp.where(t < n_a + nb, jnp.minimum(t, n_a - 1), t - n_a - nb)
    tile = lambda f: pl.BlockSpec((None, sg, CONV_J_BLOCK, n1h * SUBLANES, LANES), lambda b, s, t: (b, s, f(t), 0, 0))
    kblk = lambda t: jnp.clip(t - n_a, 0, nb - 1)
    const = lambda a: pl.BlockSpec(a.shape, lambda b, s, t: (0, 0))
    mspec = pl.BlockSpec((K1_BLOCK, 2 * N2, 2 * N2), lambda b, s, t: (kblk(t), 0, 0))
    out = pl.pallas_call(
        functools.partial(_conv_fused_kernel, n1h=n1h, sg=sg, nb=nb, scale=1.0 / (n1h * N2)),
        grid=(batch, SLABS // sg, 2 * n_a + nb),
        in_specs=[const(w1_half), const(w1_inv), mspec, mspec,
                  pl.BlockSpec((K1_BLOCK, 2 * N2, sg * LANES),
                               lambda b, s, t: (kblk(t), 0, order * (SLABS // sg) + s)),
                  tile(z_idx), tile(phase_c),
                  pl.BlockSpec((1, sg * LANES), lambda b, s, t: (0, s))],
        out_specs=tile(phase_c),
        out_shape=jax.ShapeDtypeStruct((batch, SLABS, J2, n1h * SUBLANES, LANES), F32),
        scratch_shapes=[pltpu.VMEM((sg, J2, 2 * n1h * SUBLANES, LANES), F32)],
        compiler_params=_cparams(3), name="hyena_conv")(w1_half, w1_inv, mat, mat_t, spec, rows(z), rows(gate),
                                                        d_vec.reshape(1, HC))
    return out.reshape(z.shape)


def _silu(g):
    return g / (1.0 + jnp.exp(-g))


def _band_mask(t, halo, q0, seq_len):
    qi = lax.broadcasted_iota(jnp.int32, (t, t + 2 * halo), 0)
    kj = lax.broadcasted_iota(jnp.int32, (t, t + 2 * halo), 1)
    kpos = q0 - halo + kj
    return (jnp.abs(kj - halo - qi) <= halo) & (kpos >= 0) & (kpos < seq_len)


def _head_pair_operands(win):
    x = win.astype(F32)
    swapped = pltpu.roll(x, B_HEAD_DIM, 1)
    low = lax.broadcasted_iota(jnp.int32, x.shape, 1) < B_HEAD_DIM
    place = lambda cond, val: jnp.where(cond, val, 0.0).astype(BF16)
    return ((place(low, x), place(~low, swapped)), (place(low, swapped), place(~low, x)))


def _band_attn_kernel(sink_ref, q_ref, kp_ref, kc_ref, kn_ref, vp_ref, vc_ref, vn_ref, gate_ref, o_ref, *, seq_len):
    t, hd, halo = ATTN_SUB, B_HEAD_DIM, B_BLOCK
    kpads = _head_pair_operands(jnp.concatenate([kp_ref[...], kc_ref[...], kn_ref[...]], axis=0))
    vpads = _head_pair_operands(jnp.concatenate([vp_ref[...], vc_ref[...], vn_ref[...]], axis=0))
    tq = q_ref.shape[0]
    heads = [(pair, par) for pair in range(B_HEADS // 2) for par in range(2)]
    for u in range(tq // t):
        valid = _band_mask(t, halo, pl.program_id(1) * tq + u * t, seq_len)
        rows = slice(u * t, (u + 1) * t)
        win = slice(u * t, (u + 1) * t + 2 * halo)
        scores = [lax.dot_general(q_ref[rows, pair * LANES:(pair + 1) * LANES], kpads[(2 * pair) // B_GROUP][par][win],
                                  (((1,), (1,)), ((), ())), preferred_element_type=F32) for pair, par in heads]
        probs, denoms = [], []
        scale = hd ** -0.5
        for (pair, par), s in zip(heads, scores):
            s = jnp.where(valid, s, NEG_INF)
            sink = sink_ref[2 * pair + par]
            m_raw = jnp.maximum(jnp.max(s, axis=-1, keepdims=True), sink * (1.0 / scale))
            p = jnp.exp2((s - m_raw) * (scale * LOG2E))
            denoms.append(jnp.sum(p, axis=-1, keepdims=True) + jnp.exp2((sink * (1.0 / scale) - m_raw) * (scale * LOG2E)))
            probs.append(p.astype(BF16))
        outs = [jnp.dot(p, vpads[(2 * pair) // B_GROUP][par][win], preferred_element_type=F32) / d
                for (pair, par), p, d in zip(heads, probs, denoms)]
        for pair in range(B_HEADS // 2):
            lanes = slice(pair * LANES, (pair + 1) * LANES)
            o_ref[rows, lanes] = ((outs[2 * pair] + outs[2 * pair + 1]) * _silu(gate_ref[rows, lanes])).astype(BF16)


def band_attn(qk, v_att, at_gate, sink, batch, seq_len, *, nsub=4):
    t = B_BLOCK
    tq = t * nsub
    nblk = seq_len // t
    qk3 = qk.reshape(batch, seq_len, qk.shape[1])
    v3 = v_att.reshape(batch, seq_len, LANES)
    g3 = at_gate.reshape(batch, seq_len, at_gate.shape[1])
    k_col = B_HEADS * B_HEAD_DIM // LANES
    halo = lambda col, f: pl.BlockSpec((None, t, LANES), lambda b, i: (b, f(i), col))
    cur = lambda col: pl.BlockSpec((None, tq, LANES), lambda b, i: (b, i, col))
    prev = lambda i: jnp.maximum(i * nsub - 1, 0)
    nxt = lambda i: jnp.minimum((i + 1) * nsub, nblk - 1)
    wide = pl.BlockSpec((None, tq, B_HEADS * B_HEAD_DIM), lambda b, i: (b, i, 0))
    out = pl.pallas_call(
        functools.partial(_band_attn_kernel, seq_len=seq_len), grid=(batch, seq_len // tq),
        in_specs=[pl.BlockSpec(memory_space=pltpu.SMEM), wide,
                  halo(k_col, prev), cur(k_col), halo(k_col, nxt), halo(0, prev), cur(0), halo(0, nxt), wide],
        out_specs=wide,
        out_shape=jax.ShapeDtypeStruct((batch, seq_len, B_HEADS * B_HEAD_DIM), BF16),
        compiler_params=_cparams(2), name="band_attn")(sink, qk3, qk3, qk3, qk3, v3, v3, v3, g3)
    return out.reshape(batch * seq_len, -1)


def _dil_attn_kernel(q_ref, kp_ref, kc_ref, kn_ref, vp_ref, vc_ref, vn_ref, o_ref, lse_ref, *, seq_len):
    t, hd, halo = ATTN_SUB, C_HEAD_DIM, C_RADIUS
    tq = q_ref.shape[0]
    nsub = tq // t
    lane = lax.broadcasted_iota(jnp.int32, (t, LANES), 1)
    cols = [slice(h * hd, (h + 1) * hd) for h in range(C_HEADS)]
    scale = hd ** -0.5
    rows = lambda u: slice(u * t, (u + 1) * t)
    window = lambda u, p_ref, c_ref, n_ref, sl: jnp.concatenate(
        [p_ref[:, sl], c_ref[:, sl], n_ref[:, sl]], axis=0)[u * t:(u + 1) * t + 2 * halo]

    def qk_stage(u):
        return [lax.dot_general(q_ref[rows(u), sl], window(u, kp_ref, kc_ref, kn_ref, sl), (((1,), (1,)), ((), ())),
                                preferred_element_type=F32) for sl in cols]

    def softmax_stage(u, scores):
        valid = _band_mask(t, halo, pl.program_id(1) * tq + u * t, seq_len)
        probs, denoms = [], []
        lse_tile = jnp.zeros((t, LANES), F32)
        for h, s in enumerate(scores):
            s = jnp.where(valid, s, NEG_INF)
            m_raw = jnp.max(s, axis=-1, keepdims=True)
            p = jnp.exp2((s - m_raw) * (scale * LOG2E))
            denom = jnp.sum(p, axis=-1, keepdims=True)
            lse_tile = jnp.where(lane == h, m_raw * scale + jnp.log(denom), lse_tile)
            probs.append(p.astype(BF16))
            denoms.append(denom)
        lse_ref[rows(u), :] = lse_tile
        return probs, denoms

    def pv_stage(u, probs, denoms):
        for sl, p, denom in zip(cols, probs, denoms):
            o = jnp.dot(p, window(u, vp_ref, vc_ref, vn_ref, sl), preferred_element_type=F32) / denom
            o_ref[rows(u), sl] = o.astype(BF16)

    for u in range(nsub):
        pv_stage(u, *softmax_stage(u, qk_stage(u)))


def dil_attn(qk, v, n_seq, ls):
    w = C_HEADS * C_HEAD_DIM
    t = 2 * C_RADIUS
    nsub = next(n for n in (8, 4, 2, 1) if ls % (n * t) == 0)
    tq = t * nsub
    n_halo = ls // C_RADIUS
    qk3 = qk.reshape(n_seq, ls, qk.shape[1])
    v3 = v.reshape(n_seq, ls, w)
    prev = lambda i: jnp.maximum(i * (tq // C_RADIUS) - 1, 0)
    nxt = lambda i: jnp.minimum((i + 1) * (tq // C_RADIUS), n_halo - 1)
    big = lambda col: pl.BlockSpec((None, tq, w), lambda b, i: (b, i, col))
    halo = lambda col, f: pl.BlockSpec((None, C_RADIUS, w), lambda b, i: (b, f(i), col))
    o, lse = pl.pallas_call(
        functools.partial(_dil_attn_kernel, seq_len=ls), grid=(n_seq, ls // tq),
        in_specs=[big(0), halo(1, prev), big(1), halo(1, nxt), halo(0, prev), big(0), halo(0, nxt)],
        out_specs=(big(0), pl.BlockSpec((None, tq, LANES), lambda b, i: (b, i, 0))),
        out_shape=(jax.ShapeDtypeStruct((n_seq, ls, w), BF16),
                   jax.ShapeDtypeStruct((n_seq, ls, LANES), F32)),
        compiler_params=_cparams(2), name="dil_attn")(qk3, qk3, qk3, qk3, v3, v3, v3)
    return o.reshape(n_seq * ls, w), lse.reshape(n_seq * ls, LANES)


def _out_proj_even_kernel(hy_ref, at_ref, x_ref, w_ref, o_ref, *, nb):
    mixed = jnp.concatenate([_load_time_tiles(hy_ref, nb).astype(BF16), at_ref[...]], axis=1)
    o_ref[...] = x_ref[...] + jnp.dot(mixed, w_ref[...], preferred_element_type=F32)


def out_proj_even(hy, at, x2d, w_bf, *, tm=1024):
    m, d = x2d.shape
    batch, n1h = hy.shape[0], hy.shape[3]
    tps = n1h * N2 // tm
    nb = tm // N2
    half = pl.BlockSpec((tm, HC), lambda b, i: (b * tps + i, 0))
    full = pl.BlockSpec((tm, d), lambda b, i: (b * tps + i, 0))
    return pl.pallas_call(
        functools.partial(_out_proj_even_kernel, nb=nb), grid=(batch, tps),
        in_specs=[pl.BlockSpec((None, SLABS, J2, nb, SUBLANES, LANES), lambda b, i: (b, 0, 0, i, 0, 0)), half, full,
                  pl.BlockSpec((d, d), lambda b, i: (0, 0))],
        out_specs=full, out_shape=jax.ShapeDtypeStruct((m, d), F32),
        compiler_params=_cparams(2), name="out_proj_even")(hy, at, x2d, w_bf)


def _merge_out_kernel(*refs, final, dilations):
    nh = C_HEADS
    o_nat, lse_nat = refs[0], refs[1]
    pos = 2
    strided = []
    for _ in dilations:
        strided.append((refs[pos:pos + nh], refs[pos + nh]))
        pos += nh + 1
    gate_ref, x_ref, w_ref = refs[pos:pos + 3]
    pos += 3
    g_ref = refs[pos] if final else None
    out_ref, o_scr, l_scr = refs[-3:]
    for gi, (d, (o_slabs, lse_ref)) in enumerate(zip(dilations, strided)):
        per = o_nat.shape[0] // d
        for r in range(d):
            l_scr[gi, pl.ds(r, per, stride=d), :] = lse_ref[r]
            for h in range(nh):
                o_scr[gi, h, pl.ds(r, per, stride=d), :] = o_slabs[h][r].astype(F32)
    ls = [lse_nat[...]] + [l_scr[gi] for gi in range(len(dilations))]
    mx = functools.reduce(jnp.maximum, ls)
    es = [jnp.exp(l - mx) for l in ls]
    den = functools.reduce(lambda a, b: a + b, es)
    alphas = [e / den for e in es]
    hd = C_HEAD_DIM
    parts = []
    for h in range(nh):
        acc = alphas[0][:, h:h + 1] * o_nat[:, h * hd:(h + 1) * hd].astype(F32)
        for gi in range(len(dilations)):
            acc = acc + alphas[gi + 1][:, h:h + 1] * o_scr[gi, h]
        parts.append(acc)
    y = (jnp.concatenate(parts, axis=1) * _silu(gate_ref[...])).astype(BF16)
    r = x_ref[...] + jnp.dot(y, w_ref[...], preferred_element_type=F32)
    if final:
        ms = jnp.mean(r * r, axis=-1, keepdims=True)
        r = r * lax.rsqrt(ms + NORM_EPS) * g_ref[...]
    out_ref[...] = r


def merge_out(o_nat, lse_nat, strided, gate, x2d, w_bf, batch, seq_len, final_g=None, *, tm=1024):
    m, dm = x2d.shape
    tps = seq_len // tm
    nat = lambda wd: pl.BlockSpec((tm, wd), lambda b, i: (b * tps + i, 0))
    in_specs, args = [nat(dm), nat(LANES)], [o_nat, lse_nat]
    for d, o, lse in strided:
        per = tm // d
        in_specs += [pl.BlockSpec((None, d, per, LANES), functools.partial(lambda h, b, i: (b, 0, i, h), h))
                     for h in range(C_HEADS)]
        in_specs.append(pl.BlockSpec((None, d, per, LANES), lambda b, i: (b, 0, i, 0)))
        args += [o] * C_HEADS + [lse]
    in_specs += [nat(dm), nat(dm), pl.BlockSpec((dm, dm), lambda b, i: (0, 0))]
    args += [gate, x2d, w_bf]
    if final_g is not None:
        in_specs.append(pl.BlockSpec((1, dm), lambda b, i: (0, 0)))
        args.append(final_g.reshape(1, dm))
    n_str = len(strided)
    return pl.pallas_call(
        functools.partial(_merge_out_kernel, final=final_g is not None, dilations=tuple(d for d, _, _ in strided)),
        grid=(batch, tps), in_specs=in_specs, out_specs=nat(dm), out_shape=jax.ShapeDtypeStruct((m, dm), F32),
        scratch_shapes=[pltpu.VMEM((n_str, C_HEADS, tm, LANES), F32), pltpu.VMEM((n_str, tm, LANES), F32)],
        compiler_params=_cparams(2), name="merge_out")(*args)


def _even_w_in(w_in):
    return jnp.concatenate([w_in[:, 2048:2688], w_in[:, :2048], w_in[:, 2816:], w_in[:, 2688:2816]], axis=1).astype(BF16)


def _odd_w_in(w_in):
    w = C_HEADS * C_HEAD_DIM
    blk = lambda g, t: w_in[:, (3 * g + t) * w:(3 * g + t + 1) * w]
    per = [[blk(g, 0), blk(g, 1), blk(g, 2)] for g in range(len(C_PATTERNS))]
    per[0].append(w_in[:, 9 * w:])
    return [jnp.concatenate(cols, axis=1).astype(BF16) for cols in per]


def hyena_spectrum(seq_len, tabs, fw1, fb1, ff1, fw2, fb2, ff2, fw3):
    n1h = seq_len // N2
    w1_half, _, mat, _ = tabs
    filt = hyena_filter(seq_len, fw1, fb1, ff1, fw2, fb2, ff2, fw3)
    a = dft_stage1(w1_half, filt.reshape(1, 4 * SLABS, J2, n1h * SUBLANES, LANES))
    return dft_stage2_filter(mat, a.reshape(1, 4 * SLABS, J2, 2, n1h, SUBLANES, LANES))


def hybrid_layer(x2d, batch, seq_len, norm_g, w_in_bf, conv_w, conv_b, hyena_d, sink, w_out_bf, spec, tabs, rope):
    rope_t, half = rope
    qk, x1, x2s, v, at_gate, v_att = hybrid_proj(x2d, norm_g, w_in_bf, conv_w, conv_b, batch, seq_len, rope_t, half)
    z = hyena_conv_fused(v, x1, hyena_d[0], spec, 0, tabs)
    z = hyena_conv_fused(z, x2s, hyena_d[1], spec, 1, tabs)
    at = band_attn(qk, v_att, at_gate, sink, batch, seq_len)
    return out_proj_even(z, at, x2d, w_out_bf)


def dilated_layer(x2d, batch, seq_len, norm_g, w_in_bfs, w_out_bf, rope, final_g):
    w = C_HEADS * C_HEAD_DIM
    m = batch * seq_len
    rope_t, half = rope
    o_nat = lse_nat = gate = None
    strided = []
    for gi, (_, d) in enumerate(C_PATTERNS):
        ls = seq_len // d
        if d == 1:
            qk, v, gate = norm_proj(x2d, norm_g, w_in_bfs[gi], seq_len, rope_t, half,
                                    ((2 * w, BF16), (w, BF16), (w, F32)))
            o_nat, lse_nat = dil_attn(qk, v, batch, ls)
        else:
            qk, v = norm_proj_strided(x2d, norm_g, w_in_bfs[gi], batch, seq_len, d, rope_t, half,
                                      ((2 * w, BF16), (w, BF16)))
            o, lse = dil_attn(qk.reshape(m, 2 * w), v.reshape(m, w), batch * d, ls)
            strided.append((d, o.reshape(batch, d, ls, w), lse.reshape(batch, d, ls, LANES)))
    return merge_out(o_nat, lse_nat, strided, gate, x2d, w_out_bf, batch, seq_len, final_g)


def kernel(x_prompt, x_sample, a_norm, a_w_in, a_conv_w, a_conv_b, a_filt_w1, a_filt_b1, a_filt_f1, a_filt_w2, a_filt_b2, a_filt_f2, a_filt_w3, a_hyena_d, a_sink, a_w_out, c_norm, c_w_in, c_w_out, final_norm):
    depth = a_norm.shape[0] + c_norm.shape[0]
    xs = [x_prompt, x_sample]
    shapes = [(x.shape[0], x.shape[1]) for x in xs]
    acts = [x.reshape(-1, D_MODEL) for x in xs]
    seq_lens = sorted({s[1] for s in shapes})
    tabs = {sl: dft_tables(sl // N2) for sl in seq_lens}
    rope_even = {sl: rope_tables(sl, B_HEAD_DIM) for sl in seq_lens}
    rope_odd = {sl: rope_tables(sl, C_HEAD_DIM) for sl in seq_lens}
    for layer in range(depth):
        i = layer // 2
        if layer % 2 == 0:
            w_in_bf = _even_w_in(a_w_in[i])
            w_out_bf = a_w_out[i].astype(BF16)
            specs = {sl: hyena_spectrum(sl, tabs[sl], a_filt_w1[i], a_filt_b1[i], a_filt_f1[i], a_filt_w2[i],
                                        a_filt_b2[i], a_filt_f2[i], a_filt_w3[i]) for sl in seq_lens}
            acts = [hybrid_layer(x2d, b, sl, a_norm[i], w_in_bf, a_conv_w[i], a_conv_b[i], a_hyena_d[i], a_sink[i],
                                 w_out_bf, specs[sl], tabs[sl], rope_even[sl])
                    for x2d, (b, sl) in zip(acts, shapes)]
        else:
            w_in_bfs = _odd_w_in(c_w_in[i])
            w_out_bf = c_w_out[i].astype(BF16)
            final_g = final_norm if layer == depth - 1 else None
            acts = [dilated_layer(x2d, b, sl, c_norm[i], w_in_bfs, w_out_bf, rope_odd[sl], final_g)
                    for x2d, (b, sl) in zip(acts, shapes)]
    assert depth % 2 == 0
    return tuple(a.reshape(b, sl, D_MODEL) for a, (b, sl) in zip(acts, shapes))
```

```python
import functools
import math

import numpy as np
import jax
import jax.numpy as jnp
from jax import lax
from jax.experimental import pallas as pl
from jax.experimental.pallas import tpu as pltpu

F32 = jnp.float32
BF16 = jnp.bfloat16

D_MODEL = 1024
HC = 512
FILTER_BANDS = 16
DECAY_TARGET = 1e-2
FAST_DECAY_PCT = 0.3
SLOW_DECAY_PCT = 1.5
DECAY_SHIFT = 0.05
B_HEAD_DIM = 64
B_HEADS = 8
B_GROUP = 4
B_BLOCK = 128
C_PATTERNS = ((128, 1), (512, 4), (2048, 16))
C_HEADS = 8
C_HEAD_DIM = 128
C_RADIUS = 64
ROPE_THETA = 500000.0
NORM_EPS = 1e-6
NEG_INF = -1e30
LOG2E = math.log2(math.e)

LANES = 128
SUBLANES = 8
N2 = 128
J2 = N2 // SUBLANES
SLABS = HC // LANES
K1_BLOCK = 8
ATTN_SUB = 128
VMEM_LIMIT = 56 * 1024 * 1024


def _cparams(n_axes):
    return pltpu.CompilerParams(dimension_semantics=("arbitrary",) * n_axes,
                                vmem_limit_bytes=VMEM_LIMIT)


def _rope_chunk(blk, cos, sa, sb, half):
    return (blk * cos + pltpu.roll(blk, LANES - half, 1) * sa + pltpu.roll(blk, half, 1) * sb)


def _two_deep(chunks, matmul, epilogue):
    prev = None
    for chunk in chunks:
        acc = matmul(chunk)
        if prev is not None:
            epilogue(*prev)
        prev = (chunk, acc)
    epilogue(*prev)


def _norm_proj_kernel(x_ref, g_ref, w_ref, cos_ref, sa_ref, sb_ref, *outs, half, groups, row_chunk):
    g = g_ref[...]
    chunks = [slice(r0, r0 + row_chunk) for r0 in range(0, x_ref.shape[0], row_chunk)]
    matmul = lambda rows: jnp.dot(_rms_bf16(x_ref[rows, :], g), w_ref[...], preferred_element_type=F32)

    def epilogue(rows, acc):
        col = 0
        for gi, (o_ref, (width, dtype)) in enumerate(zip(outs, groups)):
            if gi == 0:
                cos, sa, sb = cos_ref[rows, :], sa_ref[rows, :], sb_ref[rows, :]
                for ci in range(width // LANES):
                    src = slice(col + ci * LANES, col + (ci + 1) * LANES)
                    o_ref[rows, ci * LANES:(ci + 1) * LANES] = _rope_chunk(acc[:, src], cos, sa, sb, half).astype(dtype)
            else:
                o_ref[rows, :] = acc[:, col:col + width].astype(dtype)
            col += width

    _two_deep(chunks, matmul, epilogue)


def _resident(shape):
    return pl.BlockSpec(shape, lambda *_: (0,) * len(shape), pipeline_mode=pl.Buffered(1))


def norm_proj(x2d, g, w_bf, seq_len, rope, half, groups, *, tm=1024, row_chunk=256):
    m, d = x2d.shape
    n = w_bf.shape[1]
    assert m % tm == 0 and seq_len % tm == 0 and sum(wd for wd, _ in groups) == n
    tiles_per_seq = seq_len // tm
    in_specs = [pl.BlockSpec((tm, d), lambda i: (i, 0)), _resident((1, d)), _resident((d, n))]
    in_specs += [pl.BlockSpec((tm, LANES), lambda i: (i % tiles_per_seq, 0))] * 3
    return pl.pallas_call(
        functools.partial(_norm_proj_kernel, half=half, groups=tuple(groups), row_chunk=row_chunk),
        grid=(m // tm,), in_specs=in_specs,
        out_specs=tuple(pl.BlockSpec((tm, wd), lambda i: (i, 0)) for wd, _ in groups),
        out_shape=tuple(jax.ShapeDtypeStruct((m, wd), dt) for wd, dt in groups),
        compiler_params=_cparams(1), name="norm_proj")(x2d, g.reshape(1, d), w_bf, *rope)


def _by_residue(ref, d, residues):
    per = ref.shape[0] // d
    return jnp.concatenate([ref[pl.ds(r, per, stride=d), :] for r in residues], axis=0)


def _norm_proj_strided_kernel(*refs, half, groups, d, row_chunk):
    n_slab = D_MODEL // LANES
    x_slabs = refs[:n_slab]
    g_ref, w_ref, cos_ref, sa_ref, sb_ref = refs[n_slab:n_slab + 5]
    outs = refs[n_slab + 5:]
    tm = x_slabs[0].shape[0]
    per = tm // d
    g = g_ref[...]
    chunks = [range(r0, r0 + row_chunk // per) for r0 in range(0, d, row_chunk // per)]
    matmul = lambda res: jnp.dot(_rms_bf16(jnp.concatenate([_by_residue(xs, d, res) for xs in x_slabs], axis=1), g),
                                 w_ref[...], preferred_element_type=F32)

    def epilogue(res, acc):
        col = 0
        for gi, (o_ref, (width, dtype)) in enumerate(zip(outs, groups)):
            if gi == 0:
                cos, sa, sb = (_by_residue(t_ref, d, res) for t_ref in (cos_ref, sa_ref, sb_ref))
                val = jnp.concatenate([_rope_chunk(acc[:, col + ci * LANES:col + (ci + 1) * LANES], cos, sa, sb, half)
                                       for ci in range(width // LANES)], axis=1).astype(dtype)
            else:
                val = acc[:, col:col + width].astype(dtype)
            for k, r in enumerate(res):
                o_ref[r] = val[k * per:(k + 1) * per]
            col += width

    _two_deep(chunks, matmul, epilogue)


def norm_proj_strided(x2d, g, w_bf, batch, seq_len, d, rope, half, groups, *, tm=1024, row_chunk=256):
    m, dm = x2d.shape
    n = w_bf.shape[1]
    tps = seq_len // tm
    per = tm // d
    assert seq_len % tm == 0 and row_chunk % per == 0 and sum(wd for wd, _ in groups) == n
    in_specs = [pl.BlockSpec((tm, LANES), functools.partial(lambda c, t: (t, c), c)) for c in range(dm // LANES)]
    in_specs += [_resident((1, dm)), _resident((dm, n))]
    in_specs += [pl.BlockSpec((tm, LANES), lambda t: (t % tps, 0))] * 3
    return pl.pallas_call(
        functools.partial(_norm_proj_strided_kernel, half=half, groups=tuple(groups), d=d, row_chunk=row_chunk),
        grid=(m // tm,), in_specs=in_specs,
        out_specs=tuple(pl.BlockSpec((None, d, per, wd), lambda t: (t // tps, 0, t % tps, 0)) for wd, _ in groups),
        out_shape=tuple(jax.ShapeDtypeStruct((batch, d, seq_len // d, wd), dt) for wd, dt in groups),
        compiler_params=_cparams(1), name="norm_proj_strided")(*([x2d] * (dm // LANES)), g.reshape(1, dm), w_bf, *rope)


def _rms_bf16(x, g):
    ms = jnp.mean(x * x, axis=-1, keepdims=True)
    return (x * lax.rsqrt(ms + NORM_EPS) * g).astype(BF16)


_QK_W, _HY_W, _ATT_V_W = 640, 3 * HC, 128
_HY0 = _QK_W
_HYG0 = _HY0 + _HY_W
_ATG0 = _HYG0 + HC
_VATT0 = _ATG0 + HC


def _hybrid_proj_kernel(x_ref, xp_ref, xn_ref, g_ref, w_ref, cos_ref, sa_ref, sb_ref, cw_ref, cb_ref,
                        qk_ref, x1_ref, x2_ref, v_ref, atg_ref, vatt_ref, u_scr, *, half, tiles_per_seq, row_chunk):
    tm = x_ref.shape[0]
    i = pl.program_id(0) % tiles_per_seq
    g = g_ref[...]
    halo = jnp.concatenate([xp_ref[...], xn_ref[...]], axis=0)
    acc_h = jnp.dot(_rms_bf16(halo, g), w_ref[:, _HY0:_HYG0], preferred_element_type=F32)
    prev_row = jnp.where(i > 0, acc_h[SUBLANES - 1:SUBLANES], 0.0)
    next_row = jnp.where(i < tiles_per_seq - 1, acc_h[SUBLANES:SUBLANES + 1], 0.0)
    chunks = [slice(r0, r0 + row_chunk) for r0 in range(0, tm, row_chunk)]
    accs = [jnp.dot(_rms_bf16(x_ref[rows, :], g), w_ref[...], preferred_element_type=F32) for rows in chunks]
    for rows, acc in zip(chunks, accs):
        cos, sa, sb = cos_ref[rows, :], sa_ref[rows, :], sb_ref[rows, :]
        for ci in range(_QK_W // LANES):
            sl = slice(ci * LANES, (ci + 1) * LANES)
            qk_ref[rows, sl] = _rope_chunk(acc[:, sl], cos, sa, sb, half).astype(BF16)
        u_scr[rows, :] = acc[:, _HY0:_ATG0]
        atg_ref[rows, :] = acc[:, _ATG0:_VATT0]
        vatt_ref[rows, :] = acc[:, _VATT0:].astype(BF16)
    u = u_scr[:, :_HY_W]
    row = lax.broadcasted_iota(jnp.int32, u.shape, 0)
    um1 = jnp.where(row == 0, prev_row, pltpu.roll(u, 1, 0))
    up1 = jnp.where(row == tm - 1, next_row, pltpu.roll(u, tm - 1, 0))
    w = cw_ref[...]
    y = um1 * w[0:1, :] + u * w[1:2, :] + up1 * w[2:3, :] + cb_ref[...]
    _store_time_tiles(x1_ref, y[:, :HC])
    _store_time_tiles(x2_ref, y[:, HC:2 * HC] * _silu(u_scr[:, _HY_W:]))
    _store_time_tiles(v_ref, y[:, 2 * HC:])


def hybrid_proj(x2d, g, w_bf, conv_w, conv_b, batch, seq_len, rope, half, *, tm=512, row_chunk=256):
    m, d = x2d.shape
    n = w_bf.shape[1]
    tiles_per_seq = seq_len // tm
    sub = tm // SUBLANES
    last8 = m // SUBLANES - 1
    hy_shape = jax.ShapeDtypeStruct((batch, SLABS, J2, seq_len // N2, SUBLANES, LANES), F32)
    hy_spec = pl.BlockSpec((None, SLABS, J2, tm // N2, SUBLANES, LANES),
                           lambda t: (t // tiles_per_seq, 0, 0, t % tiles_per_seq, 0, 0))
    row_blk = lambda wd: pl.BlockSpec((tm, wd), lambda t: (t, 0))
    return pl.pallas_call(
        functools.partial(_hybrid_proj_kernel, half=half, tiles_per_seq=tiles_per_seq, row_chunk=row_chunk),
        grid=(m // tm,),
        in_specs=[row_blk(d),
                  pl.BlockSpec((SUBLANES, d), lambda t: (jnp.maximum(t * sub - 1, 0), 0)),
                  pl.BlockSpec((SUBLANES, d), lambda t: (jnp.minimum((t + 1) * sub, last8), 0)),
                  _resident((1, d)), _resident((d, n))]
                 + [pl.BlockSpec((tm, LANES), lambda t: (t % tiles_per_seq, 0))] * 3
                 + [_resident((3, _HY_W)), _resident((1, _HY_W))],
        out_specs=(row_blk(_QK_W), hy_spec, hy_spec, hy_spec, row_blk(HC), row_blk(_ATT_V_W)),
        out_shape=(jax.ShapeDtypeStruct((m, _QK_W), BF16), hy_shape, hy_shape, hy_shape,
                   jax.ShapeDtypeStruct((m, HC), F32), jax.ShapeDtypeStruct((m, _ATT_V_W), BF16)),
        scratch_shapes=[pltpu.VMEM((tm, _HY_W + HC), F32)],
        compiler_params=_cparams(1), name="hybrid_proj")(x2d, x2d, x2d, g.reshape(1, d), w_bf, *rope,
                                                         conv_w, conv_b.reshape(1, -1))


def rope_tables(seq_len, head_dim):
    rot = head_dim // 4
    half = rot // 2
    inv = jnp.power(ROPE_THETA, -2.0 * jnp.arange(half, dtype=F32) / rot)
    ang = jnp.arange(seq_len).astype(F32)[:, None] * inv[None, :]
    cos, sin = jnp.cos(ang), jnp.sin(ang)
    lane = np.arange(LANES) % head_dim
    idx = lane % half
    in_rot = jnp.asarray(lane < rot)[None, :]
    first = jnp.asarray(lane < half)[None, :]
    cos_t = jnp.where(in_rot, cos[:, idx], 1.0)
    sa = jnp.where(first, -sin[:, idx], 0.0)
    sb = jnp.where(in_rot & ~first, sin[:, idx], 0.0)
    return (cos_t, sa, sb), half


def _store_time_tiles(ref, val):
    for s in range(SLABS):
        for i in range(val.shape[0] // N2):
            ref[s, :, i] = val[i * N2:(i + 1) * N2, s * LANES:(s + 1) * LANES].reshape(J2, SUBLANES, LANES)


def _load_time_tiles(ref, nb):
    return jnp.concatenate([jnp.concatenate([ref[s, :, i].reshape(N2, LANES) for s in range(SLABS)], axis=1)
                            for i in range(nb)], axis=0)


def _filter_kernel(band_ref, w1_ref, b1_ref, f1_ref, w2_ref, b2_ref, f2_ref, w3_ref, dl_ref, o_ref,
                   *, seq_len, tr):
    j = (pl.program_id(0) * tr + lax.broadcasted_iota(jnp.int32, (tr, 1), 0)).astype(F32)
    t = j * (1.0 / (seq_len - 1))
    wpos = (2.0 * math.pi) * j / seq_len
    lane = lax.broadcasted_iota(jnp.int32, (tr, LANES), 1)
    arg = band_ref[...] * wpos
    feats = jnp.where(lane == 0, t,
                      jnp.where(lane <= FILTER_BANDS, jnp.cos(arg),
                                jnp.where(lane <= 2 * FILTER_BANDS, -jnp.sin(arg), 0.0)))
    h = jnp.dot(feats.astype(BF16), w1_ref[...], preferred_element_type=F32) + b1_ref[...]
    h = jnp.sin(f1_ref[...] * h)
    h = jnp.dot(h.astype(BF16), w2_ref[...], preferred_element_type=F32) + b2_ref[...]
    h = jnp.sin(f2_ref[...] * h)
    h = jnp.dot(h.astype(BF16), w3_ref[...], preferred_element_type=F32)
    win = jnp.exp(-t * dl_ref[...]) + DECAY_SHIFT
    for grp in range(4):
        _store_time_tiles(o_ref.at[grp * SLABS:(grp + 1) * SLABS], h[:, grp * HC:(grp + 1) * HC] * win)


def hyena_filter(seq_len, w1, b1, f1, w2, b2, f2, w3, *, tr=512):
    hid = w1.shape[1]
    pad = LANES - hid
    w1p = jnp.pad(w1, ((0, LANES - w1.shape[0]), (0, pad))).astype(BF16)
    w2p = jnp.pad(w2, ((0, pad), (0, pad))).astype(BF16)
    w3r = w3.reshape(hid, 2, 2, HC).transpose(0, 2, 1, 3).reshape(hid, 4 * HC)
    w3p = jnp.pad(w3r, ((0, pad), (0, 0))).astype(BF16)
    vec = lambda v: jnp.pad(v, (0, pad)).reshape(1, LANES)
    bands = jnp.linspace(1e-4, FILTER_BANDS - 1, FILTER_BANDS, dtype=F32)
    band_l = jnp.concatenate([jnp.zeros((1,), F32), bands, bands,
                              jnp.zeros((LANES - 1 - 2 * FILTER_BANDS,), F32)]).reshape(1, LANES)
    max_decay = math.log(DECAY_TARGET) / FAST_DECAY_PCT
    min_decay = math.log(DECAY_TARGET) / SLOW_DECAY_PCT
    deltas = jnp.abs(jnp.linspace(min_decay, max_decay, HC, dtype=F32)).reshape(1, HC)
    full = lambda a: pl.BlockSpec(a.shape, lambda i: (0, 0))
    args = [band_l, w1p, vec(b1), vec(f1), w2p, vec(b2), vec(f2), w3p, deltas]
    return pl.pallas_call(
        functools.partial(_filter_kernel, seq_len=seq_len, tr=tr),
        grid=(seq_len // tr,),
        in_specs=[full(a) for a in args],
        out_specs=pl.BlockSpec((4 * SLABS, J2, tr // N2, SUBLANES, LANES), lambda i: (0, 0, i, 0, 0)),
        out_shape=jax.ShapeDtypeStruct((4 * SLABS, J2, seq_len // N2, SUBLANES, LANES), F32),
        compiler_params=_cparams(1), name="hyena_filter")(*args)


def dft_tables(n1h):
    n1 = 2 * n1h
    n = n1 * N2
    k1 = jnp.arange(n1h, dtype=jnp.int32)
    m1 = (jnp.arange(n1, dtype=jnp.int32)[None, :] * (2 * k1[:, None] + 1)) % (2 * n1)
    a1 = m1.astype(F32) * (math.pi / n1)
    c1, s1 = jnp.cos(a1)[:, :n1h], jnp.sin(a1)[:, :n1h]
    w1_half = jnp.concatenate([c1, -s1], axis=0).astype(BF16)
    w1_inv = jnp.concatenate([c1.T, -s1.T], axis=1).astype(BF16)
    k2 = jnp.arange(N2, dtype=jnp.int32)
    n2 = jnp.arange(N2, dtype=jnp.int32)
    f = 2 * (k2[None, :, None] * n1 + k1[:, None, None]) + 1
    m2 = (n2[None, None, :] * f) % (2 * n)
    th = m2.astype(F32) * (math.pi / n)
    c, s = jnp.cos(th), jnp.sin(th)
    mat = jnp.concatenate([jnp.concatenate([c, s], axis=2),
                           jnp.concatenate([-s, c], axis=2)], axis=1)
    return w1_half, w1_inv, mat.astype(BF16), jnp.swapaxes(mat, 1, 2).astype(BF16)


def _rows_of(ref2, m, n_rows):
    return jnp.concatenate([ref2[cb, pl.ds(m, n_rows, stride=SUBLANES), :] for cb in range(SLABS)], axis=1)


def _store_rows(ref2, m, val):
    for cb in range(SLABS):
        ref2[cb, pl.ds(m, val.shape[0], stride=SUBLANES), :] = val[:, cb * LANES:(cb + 1) * LANES]


def _stage1_kernel(w_ref, x_ref, o_ref, *, k, r):
    w = w_ref[...]
    for m in range(SUBLANES):
        _store_rows(o_ref, m, jnp.dot(w, _rows_of(x_ref, m, k).astype(BF16), preferred_element_type=F32))


def dft_stage1(w, x5):
    b, s = x5.shape[:2]
    r, k = w.shape
    assert s % SLABS == 0 and x5.shape[3] == k * SUBLANES
    blk = lambda rows: pl.BlockSpec((None, SLABS, None, rows * SUBLANES, LANES), lambda bi, si, j: (bi, si, j, 0, 0))
    return pl.pallas_call(
        functools.partial(_stage1_kernel, k=k, r=r), grid=(b, s // SLABS, J2),
        in_specs=[pl.BlockSpec((r, k), lambda bi, si, j: (0, 0)), blk(k)],
        out_specs=blk(r),
        out_shape=jax.ShapeDtypeStruct((b, s, J2, r * SUBLANES, LANES), F32),
        compiler_params=_cparams(3), name="dft_stage1")(w, x5)


def _slab_cat(a_ref, kk, n_slabs):
    return jnp.concatenate([jnp.concatenate([a_ref[cb, :, 0, kk].reshape(N2, LANES),
                                             a_ref[cb, :, 1, kk].reshape(N2, LANES)], axis=0)
                            for cb in range(n_slabs)], axis=1).astype(BF16)


def _stage2_filter_kernel(m_ref, af_ref, ab_ref, o_ref):
    for kk in range(K1_BLOCK):
        zf = jnp.dot(m_ref[kk], _slab_cat(af_ref, kk, SLABS), preferred_element_type=F32)
        zb = jnp.dot(m_ref[kk], _slab_cat(ab_ref, kk, SLABS), preferred_element_type=F32)
        o_ref[kk] = jnp.concatenate([zf[:N2] + zb[:N2], zf[N2:] - zb[N2:]], axis=0)


def dft_stage2_filter(mat, a7):
    n1h = a7.shape[4]
    grp = lambda first: pl.BlockSpec((None, SLABS, J2, 2, K1_BLOCK, SUBLANES, LANES),
                                     lambda kb, o: (0, first + o, 0, 0, kb, 0, 0))
    return pl.pallas_call(
        _stage2_filter_kernel, grid=(n1h // K1_BLOCK, 2),
        in_specs=[pl.BlockSpec((K1_BLOCK, 2 * N2, 2 * N2), lambda kb, o: (kb, 0, 0)), grp(0), grp(2)],
        out_specs=pl.BlockSpec((K1_BLOCK, 2 * N2, HC), lambda kb, o: (kb, 0, o)),
        out_shape=jax.ShapeDtypeStruct((n1h, 2 * N2, 2 * HC), F32),
        compiler_params=_cparams(2), name="dft_stage2_filter")(mat, a7, a7)


CONV_SCRATCH_BYTES = 16 * 1024 * 1024
CONV_J_BLOCK = 4
EXPANDED_STAGE1_MAX_N1H = 32


def _conv_fused_kernel(w1_ref, wi_ref, m_ref, mt_ref, h_ref, z_ref, gate_ref, d_ref, o_ref, a_scr,
                       *, n1h, sg, nb, scale, expanded):
    t = pl.program_id(2)
    n_a = J2 // CONV_J_BLOCK
    cat = lambda pieces: jnp.concatenate(pieces, axis=1) if len(pieces) > 1 else pieces[0]
    strided = lambda ref, idx, m, rows: cat([ref[(cb,) + idx + (pl.ds(m, rows, stride=SUBLANES), slice(None))]
                                             for cb in range(sg)])
    whole = lambda ref, idx: cat([ref[(cb,) + idx] for cb in range(sg)])

    @pl.when(t < n_a)
    def _():
        w = w1_ref[...]
        for jj in range(CONV_J_BLOCK):
            j = t * CONV_J_BLOCK + jj
            if expanded:
                res = jnp.dot(w, whole(z_ref, (jj,)).astype(BF16), preferred_element_type=F32)
                for cb in range(sg):
                    a_scr[cb, j] = res[:, cb * LANES:(cb + 1) * LANES]
                continue
            for m in range(SUBLANES):
                res = jnp.dot(w, strided(z_ref, (jj,), m, n1h).astype(BF16), preferred_element_type=F32)
                for cb in range(sg):
                    a_scr[cb, j, pl.ds(m, 2 * n1h, stride=SUBLANES), :] = res[:, cb * LANES:(cb + 1) * LANES]

    @pl.when((t >= n_a) & (t < n_a + nb))
    def _():
        kb = t - n_a
        re0 = [pl.multiple_of((kb * K1_BLOCK + kk) * SUBLANES, SUBLANES) for kk in range(K1_BLOCK)]
        im0 = [pl.multiple_of((n1h + kb * K1_BLOCK + kk) * SUBLANES, SUBLANES) for kk in range(K1_BLOCK)]
        tile = lambda cb, r0: a_scr[cb, :, pl.ds(r0, SUBLANES), :].reshape(N2, LANES)
        zs = [jnp.dot(m_ref[kk], cat([jnp.concatenate([tile(cb, re0[kk]), tile(cb, im0[kk])], axis=0)
                                      for cb in range(sg)]).astype(BF16), preferred_element_type=F32)
              for kk in range(K1_BLOCK)]
        ys = []
        for kk, z in enumerate(zs):
            zr, zi = z[:N2], z[N2:]
            hr, hi = h_ref[kk, :N2], h_ref[kk, N2:]
            ys.append(jnp.concatenate([zr * hr - zi * hi, zr * hi + zi * hr], axis=0).astype(BF16))
        gs = [jnp.dot(mt_ref[kk], y, preferred_element_type=F32) for kk, y in enumerate(ys)]
        for kk, g in enumerate(gs):
            for cb in range(sg):
                lanes = slice(cb * LANES, (cb + 1) * LANES)
                a_scr[cb, :, pl.ds(re0[kk], SUBLANES), :] = g[:N2, lanes].reshape(J2, SUBLANES, LANES)
                a_scr[cb, :, pl.ds(im0[kk], SUBLANES), :] = g[N2:, lanes].reshape(J2, SUBLANES, LANES)

    @pl.when(t >= n_a + nb)
    def _():
        w, d = wi_ref[...], d_ref[...]
        for jj in range(CONV_J_BLOCK):
            j = (t - n_a - nb) * CONV_J_BLOCK + jj
            if expanded:
                y = jnp.dot(w, cat([a_scr[cb, j] for cb in range(sg)]).astype(BF16), preferred_element_type=F32) * scale
                res = whole(gate_ref, (jj,)) * (y + d * whole(z_ref, (jj,)))
                for cb in range(sg):
                    o_ref[cb, jj] = res[:, cb * LANES:(cb + 1) * LANES]
                continue
            for m in range(SUBLANES):
                gm = cat([a_scr[cb, j, pl.ds(m, 2 * n1h, stride=SUBLANES), :] for cb in range(sg)]).astype(BF16)
                y = jnp.dot(w, gm, preferred_element_type=F32) * scale
                res = strided(gate_ref, (jj,), m, n1h) * (y + d * strided(z_ref, (jj,), m, n1h))
                for cb in range(sg):
                    o_ref[cb, jj, pl.ds(m, n1h, stride=SUBLANES), :] = res[:, cb * LANES:(cb + 1) * LANES]


def hyena_conv_fused(z, gate, d_vec, spec, order, tabs):
    batch, n1h = z.shape[0], z.shape[3]
    w1_half, w1_inv, mat, mat_t = tabs
    expanded = n1h <= EXPANDED_STAGE1_MAX_N1H
    if expanded:
        eye = jnp.eye(SUBLANES, dtype=BF16)
        w1_half, w1_inv = jnp.kron(w1_half, eye), jnp.kron(w1_inv, eye)
    sg = max(1, min(SLABS, CONV_SCRATCH_BYTES // (J2 * 2 * n1h * SUBLANES * LANES * 4)))
    nb = n1h // K1_BLOCK
    n_a = J2 // CONV_J_BLOCK
    rows = lambda a: a.reshape(batch, SLABS, J2, n1h * SUBLANES, LANES)
    phase_c = lambda t: jnp.maximum(t - n_a - nb, 0)
    z_idx = lambda t: jnp.where(t < n_a + nb, jnp.minimum(t, n_a - 1), t - n_a - nb)
    tile = lambda f: pl.BlockSpec((None, sg, CONV_J_BLOCK, n1h * SUBLANES, LANES), lambda b, s, t: (b, s, f(t), 0, 0))
    kblk = lambda t: jnp.clip(t - n_a, 0, nb - 1)
    const = lambda a: pl.BlockSpec(a.shape, lambda b, s, t: (0, 0))
    mspec = pl.BlockSpec((K1_BLOCK, 2 * N2, 2 * N2), lambda b, s, t: (kblk(t), 0, 0))
    out = pl.pallas_call(
        functools.partial(_conv_fused_kernel, n1h=n1h, sg=sg, nb=nb, scale=1.0 / (n1h * N2), expanded=expanded),
        grid=(batch, SLABS // sg, 2 * n_a + nb),
        in_specs=[const(w1_half), const(w1_inv), mspec, mspec,
                  pl.BlockSpec((K1_BLOCK, 2 * N2, sg * LANES),
                               lambda b, s, t: (kblk(t), 0, order * (SLABS // sg) + s)),
                  tile(z_idx), tile(phase_c),
                  pl.BlockSpec((1, sg * LANES), lambda b, s, t: (0, s))],
        out_specs=tile(phase_c),
        out_shape=jax.ShapeDtypeStruct((batch, SLABS, J2, n1h * SUBLANES, LANES), F32),
        scratch_shapes=[pltpu.VMEM((sg, J2, 2 * n1h * SUBLANES, LANES), F32)],
        compiler_params=_cparams(3), name="hyena_conv")(w1_half, w1_inv, mat, mat_t, spec, rows(z), rows(gate),
                                                        d_vec.reshape(1, HC))
    return out.reshape(z.shape)


def _silu(g):
    return g / (1.0 + jnp.exp(-g))


def _band_mask(t, halo, q0, seq_len):
    qi = lax.broadcasted_iota(jnp.int32, (t, t + 2 * halo), 0)
    kj = lax.broadcasted_iota(jnp.int32, (t, t + 2 * halo), 1)
    kpos = q0 - halo + kj
    return (jnp.abs(kj - halo - qi) <= halo) & (kpos >= 0) & (kpos < seq_len)


def _head_pair_operands(win):
    x = win.astype(F32)
    swapped = pltpu.roll(x, B_HEAD_DIM, 1)
    low = lax.broadcasted_iota(jnp.int32, x.shape, 1) < B_HEAD_DIM
    place = lambda cond, val: jnp.where(cond, val, 0.0).astype(BF16)
    return ((place(low, x), place(~low, swapped)), (place(low, swapped), place(~low, x)))


def _band_attn_kernel(sink_ref, q_ref, kp_ref, kc_ref, kn_ref, vp_ref, vc_ref, vn_ref, gate_ref, o_ref, *, seq_len):
    t, hd, halo = ATTN_SUB, B_HEAD_DIM, B_BLOCK
    kpads = _head_pair_operands(jnp.concatenate([kp_ref[...], kc_ref[...], kn_ref[...]], axis=0))
    vpads = _head_pair_operands(jnp.concatenate([vp_ref[...], vc_ref[...], vn_ref[...]], axis=0))
    tq = q_ref.shape[0]
    heads = [(pair, par) for pair in range(B_HEADS // 2) for par in range(2)]
    for u in range(tq // t):
        valid = _band_mask(t, halo, pl.program_id(1) * tq + u * t, seq_len)
        rows = slice(u * t, (u + 1) * t)
        win = slice(u * t, (u + 1) * t + 2 * halo)
        scores = [lax.dot_general(q_ref[rows, pair * LANES:(pair + 1) * LANES], kpads[(2 * pair) // B_GROUP][par][win],
                                  (((1,), (1,)), ((), ())), preferred_element_type=F32) for pair, par in heads]
        probs, denoms = [], []
        scale = hd ** -0.5
        for (pair, par), s in zip(heads, scores):
            s = jnp.where(valid, s, NEG_INF)
            sink = sink_ref[2 * pair + par]
            m_raw = jnp.maximum(jnp.max(s, axis=-1, keepdims=True), sink * (1.0 / scale))
            p = jnp.exp2((s - m_raw) * (scale * LOG2E))
            denoms.append(jnp.sum(p, axis=-1, keepdims=True) + jnp.exp2((sink * (1.0 / scale) - m_raw) * (scale * LOG2E)))
            probs.append(p.astype(BF16))
        outs = [jnp.dot(p, vpads[(2 * pair) // B_GROUP][par][win], preferred_element_type=F32) / d
                for (pair, par), p, d in zip(heads, probs, denoms)]
        for pair in range(B_HEADS // 2):
            lanes = slice(pair * LANES, (pair + 1) * LANES)
            o_ref[rows, lanes] = ((outs[2 * pair] + outs[2 * pair + 1]) * _silu(gate_ref[rows, lanes])).astype(BF16)


def band_attn(qk, v_att, at_gate, sink, batch, seq_len, *, nsub=4):
    t = B_BLOCK
    tq = t * nsub
    nblk = seq_len // t
    qk3 = qk.reshape(batch, seq_len, qk.shape[1])
    v3 = v_att.reshape(batch, seq_len, LANES)
    g3 = at_gate.reshape(batch, seq_len, at_gate.shape[1])
    k_col = B_HEADS * B_HEAD_DIM // LANES
    halo = lambda col, f: pl.BlockSpec((None, t, LANES), lambda b, i: (b, f(i), col))
    cur = lambda col: pl.BlockSpec((None, tq, LANES), lambda b, i: (b, i, col))
    prev = lambda i: jnp.maximum(i * nsub - 1, 0)
    nxt = lambda i: jnp.minimum((i + 1) * nsub, nblk - 1)
    wide = pl.BlockSpec((None, tq, B_HEADS * B_HEAD_DIM), lambda b, i: (b, i, 0))
    out = pl.pallas_call(
        functools.partial(_band_attn_kernel, seq_len=seq_len), grid=(batch, seq_len // tq),
        in_specs=[pl.BlockSpec(memory_space=pltpu.SMEM), wide,
                  halo(k_col, prev), cur(k_col), halo(k_col, nxt), halo(0, prev), cur(0), halo(0, nxt), wide],
        out_specs=wide,
        out_shape=jax.ShapeDtypeStruct((batch, seq_len, B_HEADS * B_HEAD_DIM), BF16),
        compiler_params=_cparams(2), name="band_attn")(sink, qk3, qk3, qk3, qk3, v3, v3, v3, g3)
    return out.reshape(batch * seq_len, -1)


def _dil_attn_kernel(q_ref, kp_ref, kc_ref, kn_ref, vp_ref, vc_ref, vn_ref, o_ref, lse_ref, *, seq_len):
    t, hd, halo = ATTN_SUB, C_HEAD_DIM, C_RADIUS
    tq = q_ref.shape[0]
    nsub = tq // t
    lane = lax.broadcasted_iota(jnp.int32, (t, LANES), 1)
    cols = [slice(h * hd, (h + 1) * hd) for h in range(C_HEADS)]
    scale = hd ** -0.5
    rows = lambda u: slice(u * t, (u + 1) * t)
    window = lambda u, p_ref, c_ref, n_ref, sl: jnp.concatenate(
        [p_ref[:, sl], c_ref[:, sl], n_ref[:, sl]], axis=0)[u * t:(u + 1) * t + 2 * halo]

    def qk_stage(u):
        return [lax.dot_general(q_ref[rows(u), sl], window(u, kp_ref, kc_ref, kn_ref, sl), (((1,), (1,)), ((), ())),
                                preferred_element_type=F32) for sl in cols]

    def softmax_stage(u, scores):
        valid = _band_mask(t, halo, pl.program_id(1) * tq + u * t, seq_len)
        probs, denoms = [], []
        lse_tile = jnp.zeros((t, LANES), F32)
        for h, s in enumerate(scores):
            s = jnp.where(valid, s, NEG_INF)
            m_raw = jnp.max(s, axis=-1, keepdims=True)
            p = jnp.exp2((s - m_raw) * (scale * LOG2E))
            denom = jnp.sum(p, axis=-1, keepdims=True)
            lse_tile = jnp.where(lane == h, m_raw * scale + jnp.log(denom), lse_tile)
            probs.append(p.astype(BF16))
            denoms.append(denom)
        lse_ref[rows(u), :] = lse_tile
        return probs, denoms

    def pv_stage(u, probs, denoms):
        for sl, p, denom in zip(cols, probs, denoms):
            o = jnp.dot(p, window(u, vp_ref, vc_ref, vn_ref, sl), preferred_element_type=F32) / denom
            o_ref[rows(u), sl] = o.astype(BF16)

    for u in range(nsub):
        pv_stage(u, *softmax_stage(u, qk_stage(u)))


def dil_attn(qk, v, n_seq, ls):
    w = C_HEADS * C_HEAD_DIM
    t = 2 * C_RADIUS
    nsub = next(n for n in (8, 4, 2, 1) if ls % (n * t) == 0)
    tq = t * nsub
    n_halo = ls // C_RADIUS
    qk3 = qk.reshape(n_seq, ls, qk.shape[1])
    v3 = v.reshape(n_seq, ls, w)
    prev = lambda i: jnp.maximum(i * (tq // C_RADIUS) - 1, 0)
    nxt = lambda i: jnp.minimum((i + 1) * (tq // C_RADIUS), n_halo - 1)
    big = lambda col: pl.BlockSpec((None, tq, w), lambda b, i: (b, i, col))
    halo = lambda col, f: pl.BlockSpec((None, C_RADIUS, w), lambda b, i: (b, f(i), col))
    o, lse = pl.pallas_call(
        functools.partial(_dil_attn_kernel, seq_len=ls), grid=(n_seq, ls // tq),
        in_specs=[big(0), halo(1, prev), big(1), halo(1, nxt), halo(0, prev), big(0), halo(0, nxt)],
        out_specs=(big(0), pl.BlockSpec((None, tq, LANES), lambda b, i: (b, i, 0))),
        out_shape=(jax.ShapeDtypeStruct((n_seq, ls, w), BF16),
                   jax.ShapeDtypeStruct((n_seq, ls, LANES), F32)),
        compiler_params=_cparams(2), name="dil_attn")(qk3, qk3, qk3, qk3, v3, v3, v3)
    return o.reshape(n_seq * ls, w), lse.reshape(n_seq * ls, LANES)


def _out_proj_even_kernel(hy_ref, at_ref, x_ref, w_ref, o_ref, *, nb):
    mixed = jnp.concatenate([_load_time_tiles(hy_ref, nb).astype(BF16), at_ref[...]], axis=1)
    o_ref[...] = x_ref[...] + jnp.dot(mixed, w_ref[...], preferred_element_type=F32)


def out_proj_even(hy, at, x2d, w_bf, *, tm=1024):
    m, d = x2d.shape
    batch, n1h = hy.shape[0], hy.shape[3]
    tps = n1h * N2 // tm
    nb = tm // N2
    half = pl.BlockSpec((tm, HC), lambda b, i: (b * tps + i, 0))
    full = pl.BlockSpec((tm, d), lambda b, i: (b * tps + i, 0))
    return pl.pallas_call(
        functools.partial(_out_proj_even_kernel, nb=nb), grid=(batch, tps),
        in_specs=[pl.BlockSpec((None, SLABS, J2, nb, SUBLANES, LANES), lambda b, i: (b, 0, 0, i, 0, 0)), half, full,
                  pl.BlockSpec((d, d), lambda b, i: (0, 0))],
        out_specs=full, out_shape=jax.ShapeDtypeStruct((m, d), F32),
        compiler_params=_cparams(2), name="out_proj_even")(hy, at, x2d, w_bf)


def _merge_out_kernel(*refs, final, dilations):
    nh = C_HEADS
    o_nat, lse_nat = refs[0], refs[1]
    pos = 2
    strided = []
    for _ in dilations:
        strided.append((refs[pos:pos + nh], refs[pos + nh]))
        pos += nh + 1
    gate_ref, x_ref, w_ref = refs[pos:pos + 3]
    pos += 3
    g_ref = refs[pos] if final else None
    out_ref, o_scr, l_scr = refs[-3:]
    for gi, (d, (o_slabs, lse_ref)) in enumerate(zip(dilations, strided)):
        per = o_nat.shape[0] // d
        for r in range(d):
            l_scr[gi, pl.ds(r, per, stride=d), :] = lse_ref[r]
            for h in range(nh):
                o_scr[gi, h, pl.ds(r, per, stride=d), :] = o_slabs[h][r].astype(F32)
    ls = [lse_nat[...]] + [l_scr[gi] for gi in range(len(dilations))]
    mx = functools.reduce(jnp.maximum, ls)
    es = [jnp.exp(l - mx) for l in ls]
    den = functools.reduce(lambda a, b: a + b, es)
    alphas = [e / den for e in es]
    hd = C_HEAD_DIM
    parts = []
    for h in range(nh):
        acc = alphas[0][:, h:h + 1] * o_nat[:, h * hd:(h + 1) * hd].astype(F32)
        for gi in range(len(dilations)):
            acc = acc + alphas[gi + 1][:, h:h + 1] * o_scr[gi, h]
        parts.append(acc)
    y = (jnp.concatenate(parts, axis=1) * _silu(gate_ref[...])).astype(BF16)
    r = x_ref[...] + jnp.dot(y, w_ref[...], preferred_element_type=F32)
    if final:
        ms = jnp.mean(r * r, axis=-1, keepdims=True)
        r = r * lax.rsqrt(ms + NORM_EPS) * g_ref[...]
    out_ref[...] = r


def merge_out(o_nat, lse_nat, strided, gate, x2d, w_bf, batch, seq_len, final_g=None, *, tm=1024):
    m, dm = x2d.shape
    tps = seq_len // tm
    nat = lambda wd: pl.BlockSpec((tm, wd), lambda b, i: (b * tps + i, 0))
    in_specs, args = [nat(dm), nat(LANES)], [o_nat, lse_nat]
    for d, o, lse in strided:
        per = tm // d
        in_specs += [pl.BlockSpec((None, d, per, LANES), functools.partial(lambda h, b, i: (b, 0, i, h), h))
                     for h in range(C_HEADS)]
        in_specs.append(pl.BlockSpec((None, d, per, LANES), lambda b, i: (b, 0, i, 0)))
        args += [o] * C_HEADS + [lse]
    in_specs += [nat(dm), nat(dm), pl.BlockSpec((dm, dm), lambda b, i: (0, 0))]
    args += [gate, x2d, w_bf]
    if final_g is not None:
        in_specs.append(pl.BlockSpec((1, dm), lambda b, i: (0, 0)))
        args.append(final_g.reshape(1, dm))
    n_str = len(strided)
    return pl.pallas_call(
        functools.partial(_merge_out_kernel, final=final_g is not None, dilations=tuple(d for d, _, _ in strided)),
        grid=(batch, tps), in_specs=in_specs, out_specs=nat(dm), out_shape=jax.ShapeDtypeStruct((m, dm), F32),
        scratch_shapes=[pltpu.VMEM((n_str, C_HEADS, tm, LANES), F32), pltpu.VMEM((n_str, tm, LANES), F32)],
        compiler_params=_cparams(2), name="merge_out")(*args)


def _even_w_in(w_in):
    return jnp.concatenate([w_in[:, 2048:2688], w_in[:, :2048], w_in[:, 2816:], w_in[:, 2688:2816]], axis=1).astype(BF16)


def _odd_w_in(w_in):
    w = C_HEADS * C_HEAD_DIM
    blk = lambda g, t: w_in[:, (3 * g + t) * w:(3 * g + t + 1) * w]
    per = [[blk(g, 0), blk(g, 1), blk(g, 2)] for g in range(len(C_PATTERNS))]
    per[0].append(w_in[:, 9 * w:])
    return [jnp.concatenate(cols, axis=1).astype(BF16) for cols in per]


def hyena_spectrum(seq_len, tabs, fw1, fb1, ff1, fw2, fb2, ff2, fw3):
    n1h = seq_len // N2
    w1_half, _, mat, _ = tabs
    filt = hyena_filter(seq_len, fw1, fb1, ff1, fw2, fb2, ff2, fw3)
    a = dft_stage1(w1_half, filt.reshape(1, 4 * SLABS, J2, n1h * SUBLANES, LANES))
    return dft_stage2_filter(mat, a.reshape(1, 4 * SLABS, J2, 2, n1h, SUBLANES, LANES))


def hybrid_layer(x2d, batch, seq_len, norm_g, w_in_bf, conv_w, conv_b, hyena_d, sink, w_out_bf, spec, tabs, rope):
    rope_t, half = rope
    qk, x1, x2s, v, at_gate, v_att = hybrid_proj(x2d, norm_g, w_in_bf, conv_w, conv_b, batch, seq_len, rope_t, half)
    z = hyena_conv_fused(v, x1, hyena_d[0], spec, 0, tabs)
    z = hyena_conv_fused(z, x2s, hyena_d[1], spec, 1, tabs)
    at = band_attn(qk, v_att, at_gate, sink, batch, seq_len)
    return out_proj_even(z, at, x2d, w_out_bf)


def dilated_layer(x2d, batch, seq_len, norm_g, w_in_bfs, w_out_bf, rope, final_g):
    w = C_HEADS * C_HEAD_DIM
    m = batch * seq_len
    rope_t, half = rope
    o_nat = lse_nat = gate = None
    strided = []
    for gi, (_, d) in enumerate(C_PATTERNS):
        ls = seq_len // d
        if d == 1:
            qk, v, gate = norm_proj(x2d, norm_g, w_in_bfs[gi], seq_len, rope_t, half,
                                    ((2 * w, BF16), (w, BF16), (w, F32)))
            o_nat, lse_nat = dil_attn(qk, v, batch, ls)
        else:
            qk, v = norm_proj_strided(x2d, norm_g, w_in_bfs[gi], batch, seq_len, d, rope_t, half,
                                      ((2 * w, BF16), (w, BF16)))
            o, lse = dil_attn(qk.reshape(m, 2 * w), v.reshape(m, w), batch * d, ls)
            strided.append((d, o.reshape(batch, d, ls, w), lse.reshape(batch, d, ls, LANES)))
    return merge_out(o_nat, lse_nat, strided, gate, x2d, w_out_bf, batch, seq_len, final_g)


def kernel(x_prompt, x_sample, a_norm, a_w_in, a_conv_w, a_conv_b, a_filt_w1, a_filt_b1, a_filt_f1, a_filt_w2, a_filt_b2, a_filt_f2, a_filt_w3, a_hyena_d, a_sink, a_w_out, c_norm, c_w_in, c_w_out, final_norm):
    depth = a_norm.shape[0] + c_norm.shape[0]
    xs = [x_prompt, x_sample]
    shapes = [(x.shape[0], x.shape[1]) for x in xs]
    acts = [x.reshape(-1, D_MODEL) for x in xs]
    seq_lens = sorted({s[1] for s in shapes})
    tabs = {sl: dft_tables(sl // N2) for sl in seq_lens}
    rope_even = {sl: rope_tables(sl, B_HEAD_DIM) for sl in seq_lens}
    rope_odd = {sl: rope_tables(sl, C_HEAD_DIM) for sl in seq_lens}
    for layer in range(depth):
        i = layer // 2
        if layer % 2 == 0:
            w_in_bf = _even_w_in(a_w_in[i])
            w_out_bf = a_w_out[i].astype(BF16)
            specs = {sl: hyena_spectrum(sl, tabs[sl], a_filt_w1[i], a_filt_b1[i], a_filt_f1[i], a_filt_w2[i],
                                        a_filt_b2[i], a_filt_f2[i], a_filt_w3[i]) for sl in seq_lens}
            acts = [hybrid_layer(x2d, b, sl, a_norm[i], w_in_bf, a_conv_w[i], a_conv_b[i], a_hyena_d[i], a_sink[i],
                                 w_out_bf, specs[sl], tabs[sl], rope_even[sl])
                    for x2d, (b, sl) in zip(acts, shapes)]
        else:
            w_in_bfs = _odd_w_in(c_w_in[i])
            w_out_bf = c_w_out[i].astype(BF16)
            final_g = final_norm if layer == depth - 1 else None
            acts = [dilated_layer(x2d, b, sl, c_norm[i], w_in_bfs, w_out_bf, rope_odd[sl], final_g)
                    for x2d, (b, sl) in zip(acts, shapes)]
    assert depth % 2 == 0
    return tuple(a.reshape(b, sl, D_MODEL) for a, (b, sl) in zip(acts, shapes))
```

```python
import functools
import math

import numpy as np
import jax
import jax.numpy as jnp
from jax import lax
from jax.experimental import pallas as pl
from jax.experimental.pallas import tpu as pltpu

F32 = jnp.float32
BF16 = jnp.bfloat16

D_MODEL = 1024
HC = 512
FILTER_BANDS = 16
DECAY_TARGET = 1e-2
FAST_DECAY_PCT = 0.3
SLOW_DECAY_PCT = 1.5
DECAY_SHIFT = 0.05
B_HEAD_DIM = 64
B_HEADS = 8
B_GROUP = 4
B_BLOCK = 128
C_PATTERNS = ((128, 1), (512, 4), (2048, 16))
C_HEADS = 8
C_HEAD_DIM = 128
C_RADIUS = 64
ROPE_THETA = 500000.0
NORM_EPS = 1e-6
NEG_INF = -1e30
LOG2E = math.log2(math.e)

LANES = 128
SUBLANES = 8
N2 = 128
J2 = N2 // SUBLANES
SLABS = HC // LANES
K1_BLOCK = 8
ATTN_SUB = 128
VMEM_LIMIT = 56 * 1024 * 1024


def _cparams(n_axes):
    return pltpu.CompilerParams(dimension_semantics=("arbitrary",) * n_axes,
                                vmem_limit_bytes=VMEM_LIMIT)


def _rope_chunk(blk, cos, sa, sb, half):
    return (blk * cos + pltpu.roll(blk, LANES - half, 1) * sa + pltpu.roll(blk, half, 1) * sb)


def _two_deep(chunks, matmul, epilogue):
    prev = None
    for chunk in chunks:
        acc = matmul(chunk)
        if prev is not None:
            epilogue(*prev)
        prev = (chunk, acc)
    epilogue(*prev)


def _norm_proj_kernel(x_ref, g_ref, w_ref, cos_ref, sa_ref, sb_ref, *outs, half, groups, row_chunk):
    g = g_ref[...]
    chunks = [slice(r0, r0 + row_chunk) for r0 in range(0, x_ref.shape[0], row_chunk)]
    matmul = lambda rows: jnp.dot(_rms_bf16(x_ref[rows, :], g), w_ref[...], preferred_element_type=F32)

    def epilogue(rows, acc):
        col = 0
        for gi, (o_ref, (width, dtype)) in enumerate(zip(outs, groups)):
            if gi == 0:
                cos, sa, sb = cos_ref[rows, :], sa_ref[rows, :], sb_ref[rows, :]
                for ci in range(width // LANES):
                    src = slice(col + ci * LANES, col + (ci + 1) * LANES)
                    o_ref[rows, ci * LANES:(ci + 1) * LANES] = _rope_chunk(acc[:, src], cos, sa, sb, half).astype(dtype)
            else:
                o_ref[rows, :] = acc[:, col:col + width].astype(dtype)
            col += width

    _two_deep(chunks, matmul, epilogue)


def _resident(shape):
    return pl.BlockSpec(shape, lambda *_: (0,) * len(shape), pipeline_mode=pl.Buffered(1))


def norm_proj(x2d, g, w_bf, seq_len, rope, half, groups, *, tm=1024, row_chunk=256):
    m, d = x2d.shape
    n = w_bf.shape[1]
    assert m % tm == 0 and seq_len % tm == 0 and sum(wd for wd, _ in groups) == n
    tiles_per_seq = seq_len // tm
    in_specs = [pl.BlockSpec((tm, d), lambda i: (i, 0)), _resident((1, d)), _resident((d, n))]
    in_specs += [pl.BlockSpec((tm, LANES), lambda i: (i % tiles_per_seq, 0))] * 3
    return pl.pallas_call(
        functools.partial(_norm_proj_kernel, half=half, groups=tuple(groups), row_chunk=row_chunk),
        grid=(m // tm,), in_specs=in_specs,
        out_specs=tuple(pl.BlockSpec((tm, wd), lambda i: (i, 0)) for wd, _ in groups),
        out_shape=tuple(jax.ShapeDtypeStruct((m, wd), dt) for wd, dt in groups),
        compiler_params=_cparams(1), name="norm_proj")(x2d, g.reshape(1, d), w_bf, *rope)


def _by_residue(ref, d, residues):
    per = ref.shape[0] // d
    return jnp.concatenate([ref[pl.ds(r, per, stride=d), :] for r in residues], axis=0)


REGROUP_RADIX = 4


def _norm_proj_strided_kernel(*refs, half, groups, d, row_chunk):
    n_slab = D_MODEL // LANES
    x_slabs = refs[:n_slab]
    g_ref, w_ref, cos_ref, sa_ref, sb_ref = refs[n_slab:n_slab + 5]
    two_pass = d > REGROUP_RADIX
    outs = refs[n_slab + 5:-1] if two_pass else refs[n_slab + 5:]
    tm = x_slabs[0].shape[0]
    per = tm // d
    g = g_ref[...]
    sources = list(x_slabs) + [cos_ref, sa_ref, sb_ref]
    if two_pass:
        scr = refs[-1]
        group = tm // REGROUP_RADIX
        chunks = [[k + REGROUP_RADIX * r2 for r2 in range(d // REGROUP_RADIX)] for k in range(REGROUP_RADIX)]

        def regroup(i, res):
            k = res[0]
            scr[i, k * group:(k + 1) * group, :] = sources[i][pl.ds(k, group, stride=REGROUP_RADIX), :]
            return jnp.concatenate([scr[i, pl.ds(k * group + r // REGROUP_RADIX, per, stride=REGROUP_RADIX), :]
                                    for r in res], axis=0)
    else:
        chunks = [list(range(r0, r0 + row_chunk // per)) for r0 in range(0, d, row_chunk // per)]
        regroup = lambda i, res: _by_residue(sources[i], d, res)
    matmul = lambda res: jnp.dot(_rms_bf16(jnp.concatenate([regroup(i, res) for i in range(n_slab)], axis=1), g),
                                 w_ref[...], preferred_element_type=F32)

    def epilogue(res, acc):
        col = 0
        for gi, (o_ref, (width, dtype)) in enumerate(zip(outs, groups)):
            if gi == 0:
                cos, sa, sb = (regroup(n_slab + i, res) for i in range(3))
                val = jnp.concatenate([_rope_chunk(acc[:, col + ci * LANES:col + (ci + 1) * LANES], cos, sa, sb, half)
                                       for ci in range(width // LANES)], axis=1).astype(dtype)
            else:
                val = acc[:, col:col + width].astype(dtype)
            for k, r in enumerate(res):
                o_ref[r] = val[k * per:(k + 1) * per]
            col += width

    _two_deep(chunks, matmul, epilogue)


def norm_proj_strided(x2d, g, w_bf, batch, seq_len, d, rope, half, groups, *, tm=1024, row_chunk=256):
    m, dm = x2d.shape
    n = w_bf.shape[1]
    tps = seq_len // tm
    per = tm // d
    assert seq_len % tm == 0 and row_chunk % per == 0 and sum(wd for wd, _ in groups) == n
    n_slab = dm // LANES
    in_specs = [pl.BlockSpec((tm, LANES), functools.partial(lambda c, t: (t, c), c)) for c in range(n_slab)]
    in_specs += [_resident((1, dm)), _resident((dm, n))]
    in_specs += [pl.BlockSpec((tm, LANES), lambda t: (t % tps, 0))] * 3
    scratch = [pltpu.VMEM((n_slab + 3, tm, LANES), F32)] if d > REGROUP_RADIX else []
    return pl.pallas_call(
        functools.partial(_norm_proj_strided_kernel, half=half, groups=tuple(groups), d=d, row_chunk=row_chunk),
        grid=(m // tm,), in_specs=in_specs,
        out_specs=tuple(pl.BlockSpec((None, d, per, wd), lambda t: (t // tps, 0, t % tps, 0)) for wd, _ in groups),
        out_shape=tuple(jax.ShapeDtypeStruct((batch, d, seq_len // d, wd), dt) for wd, dt in groups),
        scratch_shapes=scratch,
        compiler_params=_cparams(1), name="norm_proj_strided")(*([x2d] * n_slab), g.reshape(1, dm), w_bf, *rope)


def _rms_bf16(x, g):
    ms = jnp.mean(x * x, axis=-1, keepdims=True)
    return (x * lax.rsqrt(ms + NORM_EPS) * g).astype(BF16)


_QK_W, _HY_W, _ATT_V_W = 640, 3 * HC, 128
_HY0 = _QK_W
_HYG0 = _HY0 + _HY_W
_ATG0 = _HYG0 + HC
_VATT0 = _ATG0 + HC


def _hybrid_proj_kernel(x_ref, xp_ref, xn_ref, g_ref, w_ref, cos_ref, sa_ref, sb_ref, cw_ref, cb_ref,
                        qk_ref, x1_ref, x2_ref, v_ref, atg_ref, vatt_ref, u_scr, *, half, tiles_per_seq, row_chunk):
    tm = x_ref.shape[0]
    i = pl.program_id(0) % tiles_per_seq
    g = g_ref[...]
    halo = jnp.concatenate([xp_ref[...], xn_ref[...]], axis=0)
    acc_h = jnp.dot(_rms_bf16(halo, g), w_ref[:, _HY0:_HYG0], preferred_element_type=F32)
    prev_row = jnp.where(i > 0, acc_h[SUBLANES - 1:SUBLANES], 0.0)
    next_row = jnp.where(i < tiles_per_seq - 1, acc_h[SUBLANES:SUBLANES + 1], 0.0)
    chunks = [slice(r0, r0 + row_chunk) for r0 in range(0, tm, row_chunk)]
    accs = [jnp.dot(_rms_bf16(x_ref[rows, :], g), w_ref[...], preferred_element_type=F32) for rows in chunks]
    for rows, acc in zip(chunks, accs):
        cos, sa, sb = cos_ref[rows, :], sa_ref[rows, :], sb_ref[rows, :]
        for ci in range(_QK_W // LANES):
            sl = slice(ci * LANES, (ci + 1) * LANES)
            qk_ref[rows, sl] = _rope_chunk(acc[:, sl], cos, sa, sb, half).astype(BF16)
        u_scr[rows, :] = acc[:, _HY0:_ATG0]
        atg_ref[rows, :] = acc[:, _ATG0:_VATT0]
        vatt_ref[rows, :] = acc[:, _VATT0:].astype(BF16)
    u = u_scr[:, :_HY_W]
    row = lax.broadcasted_iota(jnp.int32, u.shape, 0)
    um1 = jnp.where(row == 0, prev_row, pltpu.roll(u, 1, 0))
    up1 = jnp.where(row == tm - 1, next_row, pltpu.roll(u, tm - 1, 0))
    w = cw_ref[...]
    y = um1 * w[0:1, :] + u * w[1:2, :] + up1 * w[2:3, :] + cb_ref[...]
    _store_time_tiles(x1_ref, y[:, :HC])
    _store_time_tiles(x2_ref, y[:, HC:2 * HC] * _silu(u_scr[:, _HY_W:]))
    _store_time_tiles(v_ref, y[:, 2 * HC:])


def hybrid_proj(x2d, g, w_bf, conv_w, conv_b, batch, seq_len, rope, half, *, tm=512, row_chunk=256):
    m, d = x2d.shape
    n = w_bf.shape[1]
    tiles_per_seq = seq_len // tm
    sub = tm // SUBLANES
    last8 = m // SUBLANES - 1
    hy_shape = jax.ShapeDtypeStruct((batch, SLABS, J2, seq_len // N2, SUBLANES, LANES), F32)
    hy_spec = pl.BlockSpec((None, SLABS, J2, tm // N2, SUBLANES, LANES),
                           lambda t: (t // tiles_per_seq, 0, 0, t % tiles_per_seq, 0, 0))
    row_blk = lambda wd: pl.BlockSpec((tm, wd), lambda t: (t, 0))
    return pl.pallas_call(
        functools.partial(_hybrid_proj_kernel, half=half, tiles_per_seq=tiles_per_seq, row_chunk=row_chunk),
        grid=(m // tm,),
        in_specs=[row_blk(d),
                  pl.BlockSpec((SUBLANES, d), lambda t: (jnp.maximum(t * sub - 1, 0), 0)),
                  pl.BlockSpec((SUBLANES, d), lambda t: (jnp.minimum((t + 1) * sub, last8), 0)),
                  _resident((1, d)), _resident((d, n))]
                 + [pl.BlockSpec((tm, LANES), lambda t: (t % tiles_per_seq, 0))] * 3
                 + [_resident((3, _HY_W)), _resident((1, _HY_W))],
        out_specs=(row_blk(_QK_W), hy_spec, hy_spec, hy_spec, row_blk(HC), row_blk(_ATT_V_W)),
        out_shape=(jax.ShapeDtypeStruct((m, _QK_W), BF16), hy_shape, hy_shape, hy_shape,
                   jax.ShapeDtypeStruct((m, HC), F32), jax.ShapeDtypeStruct((m, _ATT_V_W), BF16)),
        scratch_shapes=[pltpu.VMEM((tm, _HY_W + HC), F32)],
        compiler_params=_cparams(1), name="hybrid_proj")(x2d, x2d, x2d, g.reshape(1, d), w_bf, *rope,
                                                         conv_w, conv_b.reshape(1, -1))


def rope_tables(seq_len, head_dim):
    rot = head_dim // 4
    half = rot // 2
    inv = jnp.power(ROPE_THETA, -2.0 * jnp.arange(half, dtype=F32) / rot)
    ang = jnp.arange(seq_len).astype(F32)[:, None] * inv[None, :]
    cos, sin = jnp.cos(ang), jnp.sin(ang)
    lane = np.arange(LANES) % head_dim
    idx = lane % half
    in_rot = jnp.asarray(lane < rot)[None, :]
    first = jnp.asarray(lane < half)[None, :]
    cos_t = jnp.where(in_rot, cos[:, idx], 1.0)
    sa = jnp.where(first, -sin[:, idx], 0.0)
    sb = jnp.where(in_rot & ~first, sin[:, idx], 0.0)
    return (cos_t, sa, sb), half


def _store_time_tiles(ref, val):
    for s in range(SLABS):
        for i in range(val.shape[0] // N2):
            ref[s, :, i] = val[i * N2:(i + 1) * N2, s * LANES:(s + 1) * LANES].reshape(J2, SUBLANES, LANES)


def _load_time_tiles(ref, nb):
    return jnp.concatenate([jnp.concatenate([ref[s, :, i].reshape(N2, LANES) for s in range(SLABS)], axis=1)
                            for i in range(nb)], axis=0)


def _filter_kernel(band_ref, w1_ref, b1_ref, f1_ref, w2_ref, b2_ref, f2_ref, w3_ref, dl_ref, o_ref,
                   *, seq_len, tr):
    j = (pl.program_id(0) * tr + lax.broadcasted_iota(jnp.int32, (tr, 1), 0)).astype(F32)
    t = j * (1.0 / (seq_len - 1))
    wpos = (2.0 * math.pi) * j / seq_len
    lane = lax.broadcasted_iota(jnp.int32, (tr, LANES), 1)
    arg = band_ref[...] * wpos
    feats = jnp.where(lane == 0, t,
                      jnp.where(lane <= FILTER_BANDS, jnp.cos(arg),
                                jnp.where(lane <= 2 * FILTER_BANDS, -jnp.sin(arg), 0.0)))
    h = jnp.dot(feats.astype(BF16), w1_ref[...], preferred_element_type=F32) + b1_ref[...]
    h = jnp.sin(f1_ref[...] * h)
    h = jnp.dot(h.astype(BF16), w2_ref[...], preferred_element_type=F32) + b2_ref[...]
    h = jnp.sin(f2_ref[...] * h)
    h = jnp.dot(h.astype(BF16), w3_ref[...], preferred_element_type=F32)
    win = jnp.exp(-t * dl_ref[...]) + DECAY_SHIFT
    for grp in range(4):
        _store_time_tiles(o_ref.at[grp * SLABS:(grp + 1) * SLABS], h[:, grp * HC:(grp + 1) * HC] * win)


def hyena_filter(seq_len, w1, b1, f1, w2, b2, f2, w3, *, tr=512):
    hid = w1.shape[1]
    pad = LANES - hid
    w1p = jnp.pad(w1, ((0, LANES - w1.shape[0]), (0, pad))).astype(BF16)
    w2p = jnp.pad(w2, ((0, pad), (0, pad))).astype(BF16)
    w3r = w3.reshape(hid, 2, 2, HC).transpose(0, 2, 1, 3).reshape(hid, 4 * HC)
    w3p = jnp.pad(w3r, ((0, pad), (0, 0))).astype(BF16)
    vec = lambda v: jnp.pad(v, (0, pad)).reshape(1, LANES)
    bands = jnp.linspace(1e-4, FILTER_BANDS - 1, FILTER_BANDS, dtype=F32)
    band_l = jnp.concatenate([jnp.zeros((1,), F32), bands, bands,
                              jnp.zeros((LANES - 1 - 2 * FILTER_BANDS,), F32)]).reshape(1, LANES)
    max_decay = math.log(DECAY_TARGET) / FAST_DECAY_PCT
    min_decay = math.log(DECAY_TARGET) / SLOW_DECAY_PCT
    deltas = jnp.abs(jnp.linspace(min_decay, max_decay, HC, dtype=F32)).reshape(1, HC)
    full = lambda a: pl.BlockSpec(a.shape, lambda i: (0, 0))
    args = [band_l, w1p, vec(b1), vec(f1), w2p, vec(b2), vec(f2), w3p, deltas]
    return pl.pallas_call(
        functools.partial(_filter_kernel, seq_len=seq_len, tr=tr),
        grid=(seq_len // tr,),
        in_specs=[full(a) for a in args],
        out_specs=pl.BlockSpec((4 * SLABS, J2, tr // N2, SUBLANES, LANES), lambda i: (0, 0, i, 0, 0)),
        out_shape=jax.ShapeDtypeStruct((4 * SLABS, J2, seq_len // N2, SUBLANES, LANES), F32),
        compiler_params=_cparams(1), name="hyena_filter")(*args)


def dft_tables(n1h):
    n1 = 2 * n1h
    n = n1 * N2
    k1 = jnp.arange(n1h, dtype=jnp.int32)
    m1 = (jnp.arange(n1, dtype=jnp.int32)[None, :] * (2 * k1[:, None] + 1)) % (2 * n1)
    a1 = m1.astype(F32) * (math.pi / n1)
    c1, s1 = jnp.cos(a1)[:, :n1h], jnp.sin(a1)[:, :n1h]
    w1_half = jnp.concatenate([c1, -s1], axis=0).astype(BF16)
    w1_inv = jnp.concatenate([c1.T, -s1.T], axis=1).astype(BF16)
    k2 = jnp.arange(N2, dtype=jnp.int32)
    n2 = jnp.arange(N2, dtype=jnp.int32)
    f = 2 * (k2[None, :, None] * n1 + k1[:, None, None]) + 1
    m2 = (n2[None, None, :] * f) % (2 * n)
    th = m2.astype(F32) * (math.pi / n)
    c, s = jnp.cos(th), jnp.sin(th)
    mat = jnp.concatenate([jnp.concatenate([c, s], axis=2),
                           jnp.concatenate([-s, c], axis=2)], axis=1)
    return w1_half, w1_inv, mat.astype(BF16), jnp.swapaxes(mat, 1, 2).astype(BF16)


def _rows_of(ref2, m, n_rows):
    return jnp.concatenate([ref2[cb, pl.ds(m, n_rows, stride=SUBLANES), :] for cb in range(SLABS)], axis=1)


def _store_rows(ref2, m, val):
    for cb in range(SLABS):
        ref2[cb, pl.ds(m, val.shape[0], stride=SUBLANES), :] = val[:, cb * LANES:(cb + 1) * LANES]


def _stage1_kernel(w_ref, x_ref, o_ref, *, k, r):
    w = w_ref[...]
    for m in range(SUBLANES):
        _store_rows(o_ref, m, jnp.dot(w, _rows_of(x_ref, m, k).astype(BF16), preferred_element_type=F32))


def dft_stage1(w, x5):
    b, s = x5.shape[:2]
    r, k = w.shape
    assert s % SLABS == 0 and x5.shape[3] == k * SUBLANES
    blk = lambda rows: pl.BlockSpec((None, SLABS, None, rows * SUBLANES, LANES), lambda bi, si, j: (bi, si, j, 0, 0))
    return pl.pallas_call(
        functools.partial(_stage1_kernel, k=k, r=r), grid=(b, s // SLABS, J2),
        in_specs=[pl.BlockSpec((r, k), lambda bi, si, j: (0, 0)), blk(k)],
        out_specs=blk(r),
        out_shape=jax.ShapeDtypeStruct((b, s, J2, r * SUBLANES, LANES), F32),
        compiler_params=_cparams(3), name="dft_stage1")(w, x5)


def _slab_cat(a_ref, kk, n_slabs):
    return jnp.concatenate([jnp.concatenate([a_ref[cb, :, 0, kk].reshape(N2, LANES),
                                             a_ref[cb, :, 1, kk].reshape(N2, LANES)], axis=0)
                            for cb in range(n_slabs)], axis=1).astype(BF16)


def _stage2_filter_kernel(m_ref, af_ref, ab_ref, o_ref):
    for kk in range(K1_BLOCK):
        zf = jnp.dot(m_ref[kk], _slab_cat(af_ref, kk, SLABS), preferred_element_type=F32)
        zb = jnp.dot(m_ref[kk], _slab_cat(ab_ref, kk, SLABS), preferred_element_type=F32)
        o_ref[kk] = jnp.concatenate([zf[:N2] + zb[:N2], zf[N2:] - zb[N2:]], axis=0)


def dft_stage2_filter(mat, a7):
    n1h = a7.shape[4]
    grp = lambda first: pl.BlockSpec((None, SLABS, J2, 2, K1_BLOCK, SUBLANES, LANES),
                                     lambda kb, o: (0, first + o, 0, 0, kb, 0, 0))
    return pl.pallas_call(
        _stage2_filter_kernel, grid=(n1h // K1_BLOCK, 2),
        in_specs=[pl.BlockSpec((K1_BLOCK, 2 * N2, 2 * N2), lambda kb, o: (kb, 0, 0)), grp(0), grp(2)],
        out_specs=pl.BlockSpec((K1_BLOCK, 2 * N2, HC), lambda kb, o: (kb, 0, o)),
        out_shape=jax.ShapeDtypeStruct((n1h, 2 * N2, 2 * HC), F32),
        compiler_params=_cparams(2), name="dft_stage2_filter")(mat, a7, a7)


CONV_SCRATCH_BYTES = 16 * 1024 * 1024
CONV_J_BLOCK = 4
EXPANDED_STAGE1_MAX_N1H = 32


def _conv_fused_kernel(w1_ref, wi_ref, m_ref, mt_ref, h_ref, z_ref, gate_ref, d_ref, o_ref, a_scr,
                       *, n1h, sg, nb, scale, expanded):
    t = pl.program_id(2)
    n_a = J2 // CONV_J_BLOCK
    cat = lambda pieces: jnp.concatenate(pieces, axis=1) if len(pieces) > 1 else pieces[0]
    strided = lambda ref, idx, m, rows: cat([ref[(cb,) + idx + (pl.ds(m, rows, stride=SUBLANES), slice(None))]
                                             for cb in range(sg)])
    whole = lambda ref, idx: cat([ref[(cb,) + idx] for cb in range(sg)])

    @pl.when(t < n_a)
    def _():
        w = w1_ref[...]
        for jj in range(CONV_J_BLOCK):
            j = t * CONV_J_BLOCK + jj
            if expanded:
                res = jnp.dot(w, whole(z_ref, (jj,)).astype(BF16), preferred_element_type=F32)
                for cb in range(sg):
                    a_scr[cb, j] = res[:, cb * LANES:(cb + 1) * LANES]
                continue
            for m in range(SUBLANES):
                res = jnp.dot(w, strided(z_ref, (jj,), m, n1h).astype(BF16), preferred_element_type=F32)
                for cb in range(sg):
                    a_scr[cb, j, pl.ds(m, 2 * n1h, stride=SUBLANES), :] = res[:, cb * LANES:(cb + 1) * LANES]

    @pl.when((t >= n_a) & (t < n_a + nb))
    def _():
        kb = t - n_a
        re0 = [pl.multiple_of((kb * K1_BLOCK + kk) * SUBLANES, SUBLANES) for kk in range(K1_BLOCK)]
        im0 = [pl.multiple_of((n1h + kb * K1_BLOCK + kk) * SUBLANES, SUBLANES) for kk in range(K1_BLOCK)]
        tile = lambda cb, r0: a_scr[cb, :, pl.ds(r0, SUBLANES), :].reshape(N2, LANES)
        zs = [jnp.dot(m_ref[kk], cat([jnp.concatenate([tile(cb, re0[kk]), tile(cb, im0[kk])], axis=0)
                                      for cb in range(sg)]).astype(BF16), preferred_element_type=F32)
              for kk in range(K1_BLOCK)]
        ys = []
        for kk, z in enumerate(zs):
            zr, zi = z[:N2], z[N2:]
            hr, hi = h_ref[kk, :N2], h_ref[kk, N2:]
            ys.append(jnp.concatenate([zr * hr - zi * hi, zr * hi + zi * hr], axis=0).astype(BF16))
        gs = [jnp.dot(mt_ref[kk], y, preferred_element_type=F32) for kk, y in enumerate(ys)]
        for kk, g in enumerate(gs):
            for cb in range(sg):
                lanes = slice(cb * LANES, (cb + 1) * LANES)
                a_scr[cb, :, pl.ds(re0[kk], SUBLANES), :] = g[:N2, lanes].reshape(J2, SUBLANES, LANES)
                a_scr[cb, :, pl.ds(im0[kk], SUBLANES), :] = g[N2:, lanes].reshape(J2, SUBLANES, LANES)

    @pl.when(t >= n_a + nb)
    def _():
        w, d = wi_ref[...], d_ref[...]
        for jj in range(CONV_J_BLOCK):
            j = (t - n_a - nb) * CONV_J_BLOCK + jj
            if expanded:
                y = jnp.dot(w, cat([a_scr[cb, j] for cb in range(sg)]).astype(BF16), preferred_element_type=F32) * scale
                res = whole(gate_ref, (jj,)) * (y + d * whole(z_ref, (jj,)))
                for cb in range(sg):
                    o_ref[cb, jj] = res[:, cb * LANES:(cb + 1) * LANES]
                continue
            for m in range(SUBLANES):
                gm = cat([a_scr[cb, j, pl.ds(m, 2 * n1h, stride=SUBLANES), :] for cb in range(sg)]).astype(BF16)
                y = jnp.dot(w, gm, preferred_element_type=F32) * scale
                res = strided(gate_ref, (jj,), m, n1h) * (y + d * strided(z_ref, (jj,), m, n1h))
                for cb in range(sg):
                    o_ref[cb, jj, pl.ds(m, n1h, stride=SUBLANES), :] = res[:, cb * LANES:(cb + 1) * LANES]


def hyena_conv_fused(z, gate, d_vec, spec, order, tabs):
    batch, n1h = z.shape[0], z.shape[3]
    w1_half, w1_inv, mat, mat_t = tabs
    expanded = n1h <= EXPANDED_STAGE1_MAX_N1H
    if expanded:
        eye = jnp.eye(SUBLANES, dtype=BF16)
        w1_half, w1_inv = jnp.kron(w1_half, eye), jnp.kron(w1_inv, eye)
    sg = max(1, min(SLABS, CONV_SCRATCH_BYTES // (J2 * 2 * n1h * SUBLANES * LANES * 4)))
    nb = n1h // K1_BLOCK
    n_a = J2 // CONV_J_BLOCK
    rows = lambda a: a.reshape(batch, SLABS, J2, n1h * SUBLANES, LANES)
    phase_c = lambda t: jnp.maximum(t - n_a - nb, 0)
    z_idx = lambda t: jnp.where(t < n_a + nb, jnp.minimum(t, n_a - 1), t - n_a - nb)
    tile = lambda f: pl.BlockSpec((None, sg, CONV_J_BLOCK, n1h * SUBLANES, LANES), lambda b, s, t: (b, s, f(t), 0, 0))
    kblk = lambda t: jnp.clip(t - n_a, 0, nb - 1)
    const = lambda a: pl.BlockSpec(a.shape, lambda b, s, t: (0, 0))
    mspec = pl.BlockSpec((K1_BLOCK, 2 * N2, 2 * N2), lambda b, s, t: (kblk(t), 0, 0))
    out = pl.pallas_call(
        functools.partial(_conv_fused_kernel, n1h=n1h, sg=sg, nb=nb, scale=1.0 / (n1h * N2), expanded=expanded),
        grid=(batch, SLABS // sg, 2 * n_a + nb),
        in_specs=[const(w1_half), const(w1_inv), mspec, mspec,
                  pl.BlockSpec((K1_BLOCK, 2 * N2, sg * LANES),
                               lambda b, s, t: (kblk(t), 0, order * (SLABS // sg) + s)),
                  tile(z_idx), tile(phase_c),
                  pl.BlockSpec((1, sg * LANES), lambda b, s, t: (0, s))],
        out_specs=tile(phase_c),
        out_shape=jax.ShapeDtypeStruct((batch, SLABS, J2, n1h * SUBLANES, LANES), F32),
        scratch_shapes=[pltpu.VMEM((sg, J2, 2 * n1h * SUBLANES, LANES), F32)],
        compiler_params=_cparams(3), name="hyena_conv")(w1_half, w1_inv, mat, mat_t, spec, rows(z), rows(gate),
                                                        d_vec.reshape(1, HC))
    return out.reshape(z.shape)


def _silu(g):
    return g / (1.0 + jnp.exp(-g))


def _band_mask(t, halo, q0, seq_len):
    qi = lax.broadcasted_iota(jnp.int32, (t, t + 2 * halo), 0)
    kj = lax.broadcasted_iota(jnp.int32, (t, t + 2 * halo), 1)
    kpos = q0 - halo + kj
    return (jnp.abs(kj - halo - qi) <= halo) & (kpos >= 0) & (kpos < seq_len)


def _head_pair_operands(win):
    x = win.astype(F32)
    swapped = pltpu.roll(x, B_HEAD_DIM, 1)
    low = lax.broadcasted_iota(jnp.int32, x.shape, 1) < B_HEAD_DIM
    place = lambda cond, val: jnp.where(cond, val, 0.0).astype(BF16)
    return ((place(low, x), place(~low, swapped)), (place(low, swapped), place(~low, x)))


def _band_attn_kernel(sink_ref, q_ref, kp_ref, kc_ref, kn_ref, vp_ref, vc_ref, vn_ref, gate_ref, o_ref, *, seq_len):
    t, hd, halo = ATTN_SUB, B_HEAD_DIM, B_BLOCK
    kpads = _head_pair_operands(jnp.concatenate([kp_ref[...], kc_ref[...], kn_ref[...]], axis=0))
    vpads = _head_pair_operands(jnp.concatenate([vp_ref[...], vc_ref[...], vn_ref[...]], axis=0))
    tq = q_ref.shape[0]
    heads = [(pair, par) for pair in range(B_HEADS // 2) for par in range(2)]
    for u in range(tq // t):
        valid = _band_mask(t, halo, pl.program_id(1) * tq + u * t, seq_len)
        rows = slice(u * t, (u + 1) * t)
        win = slice(u * t, (u + 1) * t + 2 * halo)
        scores = [lax.dot_general(q_ref[rows, pair * LANES:(pair + 1) * LANES], kpads[(2 * pair) // B_GROUP][par][win],
                                  (((1,), (1,)), ((), ())), preferred_element_type=F32) for pair, par in heads]
        probs, denoms = [], []
        scale = hd ** -0.5
        for (pair, par), s in zip(heads, scores):
            s = jnp.where(valid, s, NEG_INF)
            sink = sink_ref[2 * pair + par]
            m_raw = jnp.maximum(jnp.max(s, axis=-1, keepdims=True), sink * (1.0 / scale))
            p = jnp.exp2((s - m_raw) * (scale * LOG2E))
            denoms.append(jnp.sum(p, axis=-1, keepdims=True) + jnp.exp2((sink * (1.0 / scale) - m_raw) * (scale * LOG2E)))
            probs.append(p.astype(BF16))
        outs = [jnp.dot(p, vpads[(2 * pair) // B_GROUP][par][win], preferred_element_type=F32) / d
                for (pair, par), p, d in zip(heads, probs, denoms)]
        for pair in range(B_HEADS // 2):
            lanes = slice(pair * LANES, (pair + 1) * LANES)
            o_ref[rows, lanes] = ((outs[2 * pair] + outs[2 * pair + 1]) * _silu(gate_ref[rows, lanes])).astype(BF16)


def band_attn(qk, v_att, at_gate, sink, batch, seq_len, *, nsub=4):
    t = B_BLOCK
    tq = t * nsub
    nblk = seq_len // t
    qk3 = qk.reshape(batch, seq_len, qk.shape[1])
    v3 = v_att.reshape(batch, seq_len, LANES)
    g3 = at_gate.reshape(batch, seq_len, at_gate.shape[1])
    k_col = B_HEADS * B_HEAD_DIM // LANES
    halo = lambda col, f: pl.BlockSpec((None, t, LANES), lambda b, i: (b, f(i), col))
    cur = lambda col: pl.BlockSpec((None, tq, LANES), lambda b, i: (b, i, col))
    prev = lambda i: jnp.maximum(i * nsub - 1, 0)
    nxt = lambda i: jnp.minimum((i + 1) * nsub, nblk - 1)
    wide = pl.BlockSpec((None, tq, B_HEADS * B_HEAD_DIM), lambda b, i: (b, i, 0))
    out = pl.pallas_call(
        functools.partial(_band_attn_kernel, seq_len=seq_len), grid=(batch, seq_len // tq),
        in_specs=[pl.BlockSpec(memory_space=pltpu.SMEM), wide,
                  halo(k_col, prev), cur(k_col), halo(k_col, nxt), halo(0, prev), cur(0), halo(0, nxt), wide],
        out_specs=wide,
        out_shape=jax.ShapeDtypeStruct((batch, seq_len, B_HEADS * B_HEAD_DIM), BF16),
        compiler_params=_cparams(2), name="band_attn")(sink, qk3, qk3, qk3, qk3, v3, v3, v3, g3)
    return out.reshape(batch * seq_len, -1)


def _dil_attn_kernel(q_ref, kp_ref, kc_ref, kn_ref, vp_ref, vc_ref, vn_ref, o_ref, lse_ref, *, seq_len):
    t, hd, halo = ATTN_SUB, C_HEAD_DIM, C_RADIUS
    tq = q_ref.shape[0]
    nsub = tq // t
    lane = lax.broadcasted_iota(jnp.int32, (t, LANES), 1)
    cols = [slice(h * hd, (h + 1) * hd) for h in range(C_HEADS)]
    scale = hd ** -0.5
    rows = lambda u: slice(u * t, (u + 1) * t)
    window = lambda u, p_ref, c_ref, n_ref, sl: jnp.concatenate(
        [p_ref[:, sl], c_ref[:, sl], n_ref[:, sl]], axis=0)[u * t:(u + 1) * t + 2 * halo]

    def qk_stage(u):
        return [lax.dot_general(q_ref[rows(u), sl], window(u, kp_ref, kc_ref, kn_ref, sl), (((1,), (1,)), ((), ())),
                                preferred_element_type=F32) for sl in cols]

    def softmax_stage(u, scores):
        valid = _band_mask(t, halo, pl.program_id(1) * tq + u * t, seq_len)
        probs, denoms = [], []
        lse_tile = jnp.zeros((t, LANES), F32)
        for h, s in enumerate(scores):
            s = jnp.where(valid, s, NEG_INF)
            m_raw = jnp.max(s, axis=-1, keepdims=True)
            p = jnp.exp2((s - m_raw) * (scale * LOG2E))
            denom = jnp.sum(p, axis=-1, keepdims=True)
            lse_tile = jnp.where(lane == h, m_raw * scale + jnp.log(denom), lse_tile)
            probs.append(p.astype(BF16))
            denoms.append(denom)
        lse_ref[rows(u), :] = lse_tile
        return probs, denoms

    def pv_stage(u, probs, denoms):
        for sl, p, denom in zip(cols, probs, denoms):
            o = jnp.dot(p, window(u, vp_ref, vc_ref, vn_ref, sl), preferred_element_type=F32) / denom
            o_ref[rows(u), sl] = o.astype(BF16)

    for u in range(nsub):
        pv_stage(u, *softmax_stage(u, qk_stage(u)))


def dil_attn(qk, v, n_seq, ls):
    w = C_HEADS * C_HEAD_DIM
    t = 2 * C_RADIUS
    nsub = next(n for n in (8, 4, 2, 1) if ls % (n * t) == 0)
    tq = t * nsub
    n_halo = ls // C_RADIUS
    qk3 = qk.reshape(n_seq, ls, qk.shape[1])
    v3 = v.reshape(n_seq, ls, w)
    prev = lambda i: jnp.maximum(i * (tq // C_RADIUS) - 1, 0)
    nxt = lambda i: jnp.minimum((i + 1) * (tq // C_RADIUS), n_halo - 1)
    big = lambda col: pl.BlockSpec((None, tq, w), lambda b, i: (b, i, col))
    halo = lambda col, f: pl.BlockSpec((None, C_RADIUS, w), lambda b, i: (b, f(i), col))
    o, lse = pl.pallas_call(
        functools.partial(_dil_attn_kernel, seq_len=ls), grid=(n_seq, ls // tq),
        in_specs=[big(0), halo(1, prev), big(1), halo(1, nxt), halo(0, prev), big(0), halo(0, nxt)],
        out_specs=(big(0), pl.BlockSpec((None, tq, LANES), lambda b, i: (b, i, 0))),
        out_shape=(jax.ShapeDtypeStruct((n_seq, ls, w), BF16),
                   jax.ShapeDtypeStruct((n_seq, ls, LANES), F32)),
        compiler_params=_cparams(2), name="dil_attn")(qk3, qk3, qk3, qk3, v3, v3, v3)
    return o.reshape(n_seq * ls, w), lse.reshape(n_seq * ls, LANES)


def _out_proj_even_kernel(hy_ref, at_ref, x_ref, w_ref, o_ref, *, nb):
    mixed = jnp.concatenate([_load_time_tiles(hy_ref, nb).astype(BF16), at_ref[...]], axis=1)
    o_ref[...] = x_ref[...] + jnp.dot(mixed, w_ref[...], preferred_element_type=F32)


def out_proj_even(hy, at, x2d, w_bf, *, tm=1024):
    m, d = x2d.shape
    batch, n1h = hy.shape[0], hy.shape[3]
    tps = n1h * N2 // tm
    nb = tm // N2
    half = pl.BlockSpec((tm, HC), lambda b, i: (b * tps + i, 0))
    full = pl.BlockSpec((tm, d), lambda b, i: (b * tps + i, 0))
    return pl.pallas_call(
        functools.partial(_out_proj_even_kernel, nb=nb), grid=(batch, tps),
        in_specs=[pl.BlockSpec((None, SLABS, J2, nb, SUBLANES, LANES), lambda b, i: (b, 0, 0, i, 0, 0)), half, full,
                  pl.BlockSpec((d, d), lambda b, i: (0, 0))],
        out_specs=full, out_shape=jax.ShapeDtypeStruct((m, d), F32),
        compiler_params=_cparams(2), name="out_proj_even")(hy, at, x2d, w_bf)


def _merge_out_kernel(*refs, final, dilations):
    nh = C_HEADS
    o_nat, lse_nat = refs[0], refs[1]
    pos = 2
    strided = []
    for _ in dilations:
        strided.append((refs[pos:pos + nh], refs[pos + nh]))
        pos += nh + 1
    gate_ref, x_ref, w_ref = refs[pos:pos + 3]
    pos += 3
    g_ref = refs[pos] if final else None
    out_ref, o_scr, l_scr = refs[-3:]
    for gi, (d, (o_slabs, lse_ref)) in enumerate(zip(dilations, strided)):
        per = o_nat.shape[0] // d
        for r in range(d):
            l_scr[gi, pl.ds(r, per, stride=d), :] = lse_ref[r]
            for h in range(nh):
                o_scr[gi, h, pl.ds(r, per, stride=d), :] = o_slabs[h][r].astype(F32)
    ls = [lse_nat[...]] + [l_scr[gi] for gi in range(len(dilations))]
    mx = functools.reduce(jnp.maximum, ls)
    es = [jnp.exp(l - mx) for l in ls]
    den = functools.reduce(lambda a, b: a + b, es)
    alphas = [e / den for e in es]
    hd = C_HEAD_DIM
    parts = []
    for h in range(nh):
        acc = alphas[0][:, h:h + 1] * o_nat[:, h * hd:(h + 1) * hd].astype(F32)
        for gi in range(len(dilations)):
            acc = acc + alphas[gi + 1][:, h:h + 1] * o_scr[gi, h]
        parts.append(acc)
    y = (jnp.concatenate(parts, axis=1) * _silu(gate_ref[...])).astype(BF16)
    r = x_ref[...] + jnp.dot(y, w_ref[...], preferred_element_type=F32)
    if final:
        ms = jnp.mean(r * r, axis=-1, keepdims=True)
        r = r * lax.rsqrt(ms + NORM_EPS) * g_ref[...]
    out_ref[...] = r


def merge_out(o_nat, lse_nat, strided, gate, x2d, w_bf, batch, seq_len, final_g=None, *, tm=1024):
    m, dm = x2d.shape
    tps = seq_len // tm
    nat = lambda wd: pl.BlockSpec((tm, wd), lambda b, i: (b * tps + i, 0))
    in_specs, args = [nat(dm), nat(LANES)], [o_nat, lse_nat]
    for d, o, lse in strided:
        per = tm // d
        in_specs += [pl.BlockSpec((None, d, per, LANES), functools.partial(lambda h, b, i: (b, 0, i, h), h))
                     for h in range(C_HEADS)]
        in_specs.append(pl.BlockSpec((None, d, per, LANES), lambda b, i: (b, 0, i, 0)))
        args += [o] * C_HEADS + [lse]
    in_specs += [nat(dm), nat(dm), pl.BlockSpec((dm, dm), lambda b, i: (0, 0))]
    args += [gate, x2d, w_bf]
    if final_g is not None:
        in_specs.append(pl.BlockSpec((1, dm), lambda b, i: (0, 0)))
        args.append(final_g.reshape(1, dm))
    n_str = len(strided)
    return pl.pallas_call(
        functools.partial(_merge_out_kernel, final=final_g is not None, dilations=tuple(d for d, _, _ in strided)),
        grid=(batch, tps), in_specs=in_specs, out_specs=nat(dm), out_shape=jax.ShapeDtypeStruct((m, dm), F32),
        scratch_shapes=[pltpu.VMEM((n_str, C_HEADS, tm, LANES), F32), pltpu.VMEM((n_str, tm, LANES), F32)],
        compiler_params=_cparams(2), name="merge_out")(*args)


def _even_w_in(w_in):
    return jnp.concatenate([w_in[:, 2048:2688], w_in[:, :2048], w_in[:, 2816:], w_in[:, 2688:2816]], axis=1).astype(BF16)


def _odd_w_in(w_in):
    w = C_HEADS * C_HEAD_DIM
    blk = lambda g, t: w_in[:, (3 * g + t) * w:(3 * g + t + 1) * w]
    per = [[blk(g, 0), blk(g, 1), blk(g, 2)] for g in range(len(C_PATTERNS))]
    per[0].append(w_in[:, 9 * w:])
    return [jnp.concatenate(cols, axis=1).astype(BF16) for cols in per]


def hyena_spectrum(seq_len, tabs, fw1, fb1, ff1, fw2, fb2, ff2, fw3):
    n1h = seq_len // N2
    w1_half, _, mat, _ = tabs
    filt = hyena_filter(seq_len, fw1, fb1, ff1, fw2, fb2, ff2, fw3)
    a = dft_stage1(w1_half, filt.reshape(1, 4 * SLABS, J2, n1h * SUBLANES, LANES))
    return dft_stage2_filter(mat, a.reshape(1, 4 * SLABS, J2, 2, n1h, SUBLANES, LANES))


def hybrid_layer(x2d, batch, seq_len, norm_g, w_in_bf, conv_w, conv_b, hyena_d, sink, w_out_bf, spec, tabs, rope):
    rope_t, half = rope
    qk, x1, x2s, v, at_gate, v_att = hybrid_proj(x2d, norm_g, w_in_bf, conv_w, conv_b, batch, seq_len, rope_t, half)
    z = hyena_conv_fused(v, x1, hyena_d[0], spec, 0, tabs)
    z = hyena_conv_fused(z, x2s, hyena_d[1], spec, 1, tabs)
    at = band_attn(qk, v_att, at_gate, sink, batch, seq_len)
    return out_proj_even(z, at, x2d, w_out_bf)


def dilated_layer(x2d, batch, seq_len, norm_g, w_in_bfs, w_out_bf, rope, final_g):
    w = C_HEADS * C_HEAD_DIM
    m = batch * seq_len
    rope_t, half = rope
    o_nat = lse_nat = gate = None
    strided = []
    for gi, (_, d) in enumerate(C_PATTERNS):
        ls = seq_len // d
        if d == 1:
            qk, v, gate = norm_proj(x2d, norm_g, w_in_bfs[gi], seq_len, rope_t, half,
                                    ((2 * w, BF16), (w, BF16), (w, F32)))
            o_nat, lse_nat = dil_attn(qk, v, batch, ls)
        else:
            qk, v = norm_proj_strided(x2d, norm_g, w_in_bfs[gi], batch, seq_len, d, rope_t, half,
                                      ((2 * w, BF16), (w, BF16)))
            o, lse = dil_attn(qk.reshape(m, 2 * w), v.reshape(m, w), batch * d, ls)
            strided.append((d, o.reshape(batch, d, ls, w), lse.reshape(batch, d, ls, LANES)))
    return merge_out(o_nat, lse_nat, strided, gate, x2d, w_out_bf, batch, seq_len, final_g)


def kernel(x_prompt, x_sample, a_norm, a_w_in, a_conv_w, a_conv_b, a_filt_w1, a_filt_b1, a_filt_f1, a_filt_w2, a_filt_b2, a_filt_f2, a_filt_w3, a_hyena_d, a_sink, a_w_out, c_norm, c_w_in, c_w_out, final_norm):
    depth = a_norm.shape[0] + c_norm.shape[0]
    xs = [x_prompt, x_sample]
    shapes = [(x.shape[0], x.shape[1]) for x in xs]
    acts = [x.reshape(-1, D_MODEL) for x in xs]
    seq_lens = sorted({s[1] for s in shapes})
    tabs = {sl: dft_tables(sl // N2) for sl in seq_lens}
    rope_even = {sl: rope_tables(sl, B_HEAD_DIM) for sl in seq_lens}
    rope_odd = {sl: rope_tables(sl, C_HEAD_DIM) for sl in seq_lens}
    for layer in range(depth):
        i = layer // 2
        if layer % 2 == 0:
            w_in_bf = _even_w_in(a_w_in[i])
            w_out_bf = a_w_out[i].astype(BF16)
            specs = {sl: hyena_spectrum(sl, tabs[sl], a_filt_w1[i], a_filt_b1[i], a_filt_f1[i], a_filt_w2[i],
                                        a_filt_b2[i], a_filt_f2[i], a_filt_w3[i]) for sl in seq_lens}
            acts = [hybrid_layer(x2d, b, sl, a_norm[i], w_in_bf, a_conv_w[i], a_conv_b[i], a_hyena_d[i], a_sink[i],
                                 w_out_bf, specs[sl], tabs[sl], rope_even[sl])
                    for x2d, (b, sl) in zip(acts, shapes)]
        else:
            w_in_bfs = _odd_w_in(c_w_in[i])
            w_out_bf = c_w_out[i].astype(BF16)
            final_g = final_norm if layer == depth - 1 else None
            acts = [dilated_layer(x2d, b, sl, c_norm[i], w_in_bfs, w_out_bf, rope_odd[sl], final_g)
                    for x2d, (b, sl) in zip(acts, shapes)]
    assert depth % 2 == 0
    return tuple(a.reshape(b, sl, D_MODEL) for a, (b, sl) in zip(acts, shapes))
```

```python
import functools
import math

import numpy as np
import jax
import jax.numpy as jnp
from jax import lax
from jax.experimental import pallas as pl
from jax.experimental.pallas import tpu as pltpu

F32 = jnp.float32
BF16 = jnp.bfloat16

D_MODEL = 1024
HC = 512
FILTER_BANDS = 16
DECAY_TARGET = 1e-2
FAST_DECAY_PCT = 0.3
SLOW_DECAY_PCT = 1.5
DECAY_SHIFT = 0.05
B_HEAD_DIM = 64
B_HEADS = 8
B_GROUP = 4
B_BLOCK = 128
C_PATTERNS = ((128, 1), (512, 4), (2048, 16))
C_HEADS = 8
C_HEAD_DIM = 128
C_RADIUS = 64
ROPE_THETA = 500000.0
NORM_EPS = 1e-6
NEG_INF = -1e30
LOG2E = math.log2(math.e)

LANES = 128
SUBLANES = 8
N2 = 128
J2 = N2 // SUBLANES
SLABS = HC // LANES
K1_BLOCK = 8
ATTN_SUB = 128
VMEM_LIMIT = 56 * 1024 * 1024


def _cparams(n_axes):
    return pltpu.CompilerParams(dimension_semantics=("arbitrary",) * n_axes,
                                vmem_limit_bytes=VMEM_LIMIT)


def _rope_chunk(blk, cos, sa, sb, half):
    return (blk * cos + pltpu.roll(blk, LANES - half, 1) * sa + pltpu.roll(blk, half, 1) * sb)


def _two_deep(chunks, matmul, epilogue):
    prev = None
    for chunk in chunks:
        acc = matmul(chunk)
        if prev is not None:
            epilogue(*prev)
        prev = (chunk, acc)
    epilogue(*prev)


def _norm_proj_kernel(x_ref, g_ref, w_ref, cos_ref, sa_ref, sb_ref, *outs, half, groups, row_chunk):
    g = g_ref[...]
    chunks = [slice(r0, r0 + row_chunk) for r0 in range(0, x_ref.shape[0], row_chunk)]
    matmul = lambda rows: jnp.dot(_rms_bf16(x_ref[rows, :], g), w_ref[...], preferred_element_type=F32)

    def epilogue(rows, acc):
        col = 0
        for gi, (o_ref, (width, dtype)) in enumerate(zip(outs, groups)):
            if gi == 0:
                cos, sa, sb = cos_ref[rows, :], sa_ref[rows, :], sb_ref[rows, :]
                for ci in range(width // LANES):
                    src = slice(col + ci * LANES, col + (ci + 1) * LANES)
                    o_ref[rows, ci * LANES:(ci + 1) * LANES] = _rope_chunk(acc[:, src], cos, sa, sb, half).astype(dtype)
            else:
                o_ref[rows, :] = acc[:, col:col + width].astype(dtype)
            col += width

    _two_deep(chunks, matmul, epilogue)


def _resident(shape):
    return pl.BlockSpec(shape, lambda *_: (0,) * len(shape), pipeline_mode=pl.Buffered(1))


def norm_proj(x2d, g, w_bf, seq_len, rope, half, groups, *, tm=1024, row_chunk=256):
    m, d = x2d.shape
    n = w_bf.shape[1]
    assert m % tm == 0 and seq_len % tm == 0 and sum(wd for wd, _ in groups) == n
    tiles_per_seq = seq_len // tm
    in_specs = [pl.BlockSpec((tm, d), lambda i: (i, 0)), _resident((1, d)), _resident((d, n))]
    in_specs += [pl.BlockSpec((tm, LANES), lambda i: (i % tiles_per_seq, 0))] * 3
    return pl.pallas_call(
        functools.partial(_norm_proj_kernel, half=half, groups=tuple(groups), row_chunk=row_chunk),
        grid=(m // tm,), in_specs=in_specs,
        out_specs=tuple(pl.BlockSpec((tm, wd), lambda i: (i, 0)) for wd, _ in groups),
        out_shape=tuple(jax.ShapeDtypeStruct((m, wd), dt) for wd, dt in groups),
        compiler_params=_cparams(1), name="norm_proj")(x2d, g.reshape(1, d), w_bf, *rope)


def _by_residue(ref, d, residues):
    per = ref.shape[0] // d
    return jnp.concatenate([ref[pl.ds(r, per, stride=d), :] for r in residues], axis=0)


REGROUP_RADIX = 4


def _norm_proj_strided_kernel(*refs, half, groups, d, row_chunk):
    n_slab = D_MODEL // LANES
    x_slabs = refs[:n_slab]
    g_ref, w_ref, cos_ref, sa_ref, sb_ref = refs[n_slab:n_slab + 5]
    two_pass = d > REGROUP_RADIX
    outs = refs[n_slab + 5:-1] if two_pass else refs[n_slab + 5:]
    tm = x_slabs[0].shape[0]
    per = tm // d
    g = g_ref[...]
    sources = list(x_slabs) + [cos_ref, sa_ref, sb_ref]
    if two_pass:
        scr = refs[-1]
        group = tm // REGROUP_RADIX
        chunks = [[k + REGROUP_RADIX * r2 for r2 in range(d // REGROUP_RADIX)] for k in range(REGROUP_RADIX)]

        def regroup(i, res):
            k = res[0]
            scr[i, k * group:(k + 1) * group, :] = sources[i][pl.ds(k, group, stride=REGROUP_RADIX), :]
            return jnp.concatenate([scr[i, pl.ds(k * group + r // REGROUP_RADIX, per, stride=REGROUP_RADIX), :]
                                    for r in res], axis=0)
    else:
        chunks = [list(range(r0, r0 + row_chunk // per)) for r0 in range(0, d, row_chunk // per)]
        regroup = lambda i, res: _by_residue(sources[i], d, res)
    matmul = lambda res: jnp.dot(_rms_bf16(jnp.concatenate([regroup(i, res) for i in range(n_slab)], axis=1), g),
                                 w_ref[...], preferred_element_type=F32)

    def epilogue(res, acc):
        col = 0
        for gi, (o_ref, (width, dtype)) in enumerate(zip(outs, groups)):
            if gi == 0:
                cos, sa, sb = (regroup(n_slab + i, res) for i in range(3))
                val = jnp.concatenate([_rope_chunk(acc[:, col + ci * LANES:col + (ci + 1) * LANES], cos, sa, sb, half)
                                       for ci in range(width // LANES)], axis=1).astype(dtype)
            else:
                val = acc[:, col:col + width].astype(dtype)
            for k, r in enumerate(res):
                o_ref[r] = val[k * per:(k + 1) * per]
            col += width

    _two_deep(chunks, matmul, epilogue)


def norm_proj_strided(x2d, g, w_bf, batch, seq_len, d, rope, half, groups, *, tm=1024, row_chunk=256):
    m, dm = x2d.shape
    n = w_bf.shape[1]
    tps = seq_len // tm
    per = tm // d
    assert seq_len % tm == 0 and row_chunk % per == 0 and sum(wd for wd, _ in groups) == n
    n_slab = dm // LANES
    in_specs = [pl.BlockSpec((tm, LANES), functools.partial(lambda c, t: (t, c), c)) for c in range(n_slab)]
    in_specs += [_resident((1, dm)), _resident((dm, n))]
    in_specs += [pl.BlockSpec((tm, LANES), lambda t: (t % tps, 0))] * 3
    scratch = [pltpu.VMEM((n_slab + 3, tm, LANES), F32)] if d > REGROUP_RADIX else []
    return pl.pallas_call(
        functools.partial(_norm_proj_strided_kernel, half=half, groups=tuple(groups), d=d, row_chunk=row_chunk),
        grid=(m // tm,), in_specs=in_specs,
        out_specs=tuple(pl.BlockSpec((None, d, per, wd), lambda t: (t // tps, 0, t % tps, 0)) for wd, _ in groups),
        out_shape=tuple(jax.ShapeDtypeStruct((batch, d, seq_len // d, wd), dt) for wd, dt in groups),
        scratch_shapes=scratch,
        compiler_params=_cparams(1), name="norm_proj_strided")(*([x2d] * n_slab), g.reshape(1, dm), w_bf, *rope)


def _rms_bf16(x, g):
    ms = jnp.mean(x * x, axis=-1, keepdims=True)
    return (x * lax.rsqrt(ms + NORM_EPS) * g).astype(BF16)


_QK_W, _HY_W, _ATT_V_W = 640, 3 * HC, 128
_HY0 = _QK_W
_HYG0 = _HY0 + _HY_W
_ATG0 = _HYG0 + HC
_VATT0 = _ATG0 + HC


def _hybrid_proj_kernel(x_ref, xp_ref, xn_ref, g_ref, w_ref, cos_ref, sa_ref, sb_ref, cw_ref, cb_ref,
                        qk_ref, x1_ref, x2_ref, v_ref, atg_ref, vatt_ref, u_scr, *, half, tiles_per_seq, row_chunk):
    tm = x_ref.shape[0]
    i = pl.program_id(0) % tiles_per_seq
    g = g_ref[...]
    halo = jnp.concatenate([xp_ref[...], xn_ref[...]], axis=0)
    acc_h = jnp.dot(_rms_bf16(halo, g), w_ref[:, _HY0:_HYG0], preferred_element_type=F32)
    prev_row = jnp.where(i > 0, acc_h[SUBLANES - 1:SUBLANES], 0.0)
    next_row = jnp.where(i < tiles_per_seq - 1, acc_h[SUBLANES:SUBLANES + 1], 0.0)
    chunks = [slice(r0, r0 + row_chunk) for r0 in range(0, tm, row_chunk)]
    accs = [jnp.dot(_rms_bf16(x_ref[rows, :], g), w_ref[...], preferred_element_type=F32) for rows in chunks]
    for rows, acc in zip(chunks, accs):
        cos, sa, sb = cos_ref[rows, :], sa_ref[rows, :], sb_ref[rows, :]
        for ci in range(_QK_W // LANES):
            sl = slice(ci * LANES, (ci + 1) * LANES)
            qk_ref[rows, sl] = _rope_chunk(acc[:, sl], cos, sa, sb, half).astype(BF16)
        u_scr[rows, :] = acc[:, _HY0:_ATG0]
        atg_ref[rows, :] = acc[:, _ATG0:_VATT0]
        vatt_ref[rows, :] = acc[:, _VATT0:].astype(BF16)
    u = u_scr[:, :_HY_W]
    row = lax.broadcasted_iota(jnp.int32, u.shape, 0)
    um1 = jnp.where(row == 0, prev_row, pltpu.roll(u, 1, 0))
    up1 = jnp.where(row == tm - 1, next_row, pltpu.roll(u, tm - 1, 0))
    w = cw_ref[...]
    y = um1 * w[0:1, :] + u * w[1:2, :] + up1 * w[2:3, :] + cb_ref[...]
    _store_time_tiles(x1_ref, y[:, :HC])
    _store_time_tiles(x2_ref, y[:, HC:2 * HC] * _silu(u_scr[:, _HY_W:]))
    _store_time_tiles(v_ref, y[:, 2 * HC:])


def hybrid_proj(x2d, g, w_bf, conv_w, conv_b, batch, seq_len, rope, half, *, tm=512, row_chunk=256):
    m, d = x2d.shape
    n = w_bf.shape[1]
    tiles_per_seq = seq_len // tm
    sub = tm // SUBLANES
    last8 = m // SUBLANES - 1
    hy_shape = jax.ShapeDtypeStruct((batch, SLABS, J2, seq_len // N2, SUBLANES, LANES), F32)
    hy_spec = pl.BlockSpec((None, SLABS, J2, tm // N2, SUBLANES, LANES),
                           lambda t: (t // tiles_per_seq, 0, 0, t % tiles_per_seq, 0, 0))
    row_blk = lambda wd: pl.BlockSpec((tm, wd), lambda t: (t, 0))
    return pl.pallas_call(
        functools.partial(_hybrid_proj_kernel, half=half, tiles_per_seq=tiles_per_seq, row_chunk=row_chunk),
        grid=(m // tm,),
        in_specs=[row_blk(d),
                  pl.BlockSpec((SUBLANES, d), lambda t: (jnp.maximum(t * sub - 1, 0), 0)),
                  pl.BlockSpec((SUBLANES, d), lambda t: (jnp.minimum((t + 1) * sub, last8), 0)),
                  _resident((1, d)), _resident((d, n))]
                 + [pl.BlockSpec((tm, LANES), lambda t: (t % tiles_per_seq, 0))] * 3
                 + [_resident((3, _HY_W)), _resident((1, _HY_W))],
        out_specs=(row_blk(_QK_W), hy_spec, hy_spec, hy_spec, row_blk(HC), row_blk(_ATT_V_W)),
        out_shape=(jax.ShapeDtypeStruct((m, _QK_W), BF16), hy_shape, hy_shape, hy_shape,
                   jax.ShapeDtypeStruct((m, HC), F32), jax.ShapeDtypeStruct((m, _ATT_V_W), BF16)),
        scratch_shapes=[pltpu.VMEM((tm, _HY_W + HC), F32)],
        compiler_params=_cparams(1), name="hybrid_proj")(x2d, x2d, x2d, g.reshape(1, d), w_bf, *rope,
                                                         conv_w, conv_b.reshape(1, -1))


def rope_tables(seq_len, head_dim):
    rot = head_dim // 4
    half = rot // 2
    inv = jnp.power(ROPE_THETA, -2.0 * jnp.arange(half, dtype=F32) / rot)
    ang = jnp.arange(seq_len).astype(F32)[:, None] * inv[None, :]
    cos, sin = jnp.cos(ang), jnp.sin(ang)
    lane = np.arange(LANES) % head_dim
    idx = lane % half
    in_rot = jnp.asarray(lane < rot)[None, :]
    first = jnp.asarray(lane < half)[None, :]
    cos_t = jnp.where(in_rot, cos[:, idx], 1.0)
    sa = jnp.where(first, -sin[:, idx], 0.0)
    sb = jnp.where(in_rot & ~first, sin[:, idx], 0.0)
    return (cos_t, sa, sb), half


def _store_time_tiles(ref, val):
    for s in range(SLABS):
        for i in range(val.shape[0] // N2):
            ref[s, :, i] = val[i * N2:(i + 1) * N2, s * LANES:(s + 1) * LANES].reshape(J2, SUBLANES, LANES)


def _load_time_tiles(ref, nb):
    return jnp.concatenate([jnp.concatenate([ref[s, :, i].reshape(N2, LANES) for s in range(SLABS)], axis=1)
                            for i in range(nb)], axis=0)


def _filter_kernel(band_ref, w1_ref, b1_ref, f1_ref, w2_ref, b2_ref, f2_ref, w3_ref, dl_ref, o_ref,
                   *, seq_len, tr):
    j = (pl.program_id(0) * tr + lax.broadcasted_iota(jnp.int32, (tr, 1), 0)).astype(F32)
    t = j * (1.0 / (seq_len - 1))
    wpos = (2.0 * math.pi) * j / seq_len
    lane = lax.broadcasted_iota(jnp.int32, (tr, LANES), 1)
    arg = band_ref[...] * wpos
    feats = jnp.where(lane == 0, t,
                      jnp.where(lane <= FILTER_BANDS, jnp.cos(arg),
                                jnp.where(lane <= 2 * FILTER_BANDS, -jnp.sin(arg), 0.0)))
    h = jnp.dot(feats.astype(BF16), w1_ref[...], preferred_element_type=F32) + b1_ref[...]
    h = jnp.sin(f1_ref[...] * h)
    h = jnp.dot(h.astype(BF16), w2_ref[...], preferred_element_type=F32) + b2_ref[...]
    h = jnp.sin(f2_ref[...] * h)
    h = jnp.dot(h.astype(BF16), w3_ref[...], preferred_element_type=F32)
    win = jnp.exp(-t * dl_ref[...]) + DECAY_SHIFT
    for grp in range(4):
        _store_time_tiles(o_ref.at[grp * SLABS:(grp + 1) * SLABS], h[:, grp * HC:(grp + 1) * HC] * win)


def hyena_filter(seq_len, w1, b1, f1, w2, b2, f2, w3, *, tr=512):
    hid = w1.shape[1]
    pad = LANES - hid
    w1p = jnp.pad(w1, ((0, LANES - w1.shape[0]), (0, pad))).astype(BF16)
    w2p = jnp.pad(w2, ((0, pad), (0, pad))).astype(BF16)
    w3r = w3.reshape(hid, 2, 2, HC).transpose(0, 2, 1, 3).reshape(hid, 4 * HC)
    w3p = jnp.pad(w3r, ((0, pad), (0, 0))).astype(BF16)
    vec = lambda v: jnp.pad(v, (0, pad)).reshape(1, LANES)
    bands = jnp.linspace(1e-4, FILTER_BANDS - 1, FILTER_BANDS, dtype=F32)
    band_l = jnp.concatenate([jnp.zeros((1,), F32), bands, bands,
                              jnp.zeros((LANES - 1 - 2 * FILTER_BANDS,), F32)]).reshape(1, LANES)
    max_decay = math.log(DECAY_TARGET) / FAST_DECAY_PCT
    min_decay = math.log(DECAY_TARGET) / SLOW_DECAY_PCT
    deltas = jnp.abs(jnp.linspace(min_decay, max_decay, HC, dtype=F32)).reshape(1, HC)
    full = lambda a: pl.BlockSpec(a.shape, lambda i: (0, 0))
    args = [band_l, w1p, vec(b1), vec(f1), w2p, vec(b2), vec(f2), w3p, deltas]
    return pl.pallas_call(
        functools.partial(_filter_kernel, seq_len=seq_len, tr=tr),
        grid=(seq_len // tr,),
        in_specs=[full(a) for a in args],
        out_specs=pl.BlockSpec((4 * SLABS, J2, tr // N2, SUBLANES, LANES), lambda i: (0, 0, i, 0, 0)),
        out_shape=jax.ShapeDtypeStruct((4 * SLABS, J2, seq_len // N2, SUBLANES, LANES), F32),
        compiler_params=_cparams(1), name="hyena_filter")(*args)


def dft_tables(n1h):
    n1 = 2 * n1h
    n = n1 * N2
    k1 = jnp.arange(n1h, dtype=jnp.int32)
    m1 = (jnp.arange(n1, dtype=jnp.int32)[None, :] * (2 * k1[:, None] + 1)) % (2 * n1)
    a1 = m1.astype(F32) * (math.pi / n1)
    c1, s1 = jnp.cos(a1)[:, :n1h], jnp.sin(a1)[:, :n1h]
    w1_half = jnp.concatenate([c1, -s1], axis=0).astype(BF16)
    w1_inv = jnp.concatenate([c1.T, -s1.T], axis=1).astype(BF16)
    k2 = jnp.arange(N2, dtype=jnp.int32)
    n2 = jnp.arange(N2, dtype=jnp.int32)
    f = 2 * (k2[None, :, None] * n1 + k1[:, None, None]) + 1
    m2 = (n2[None, None, :] * f) % (2 * n)
    th = m2.astype(F32) * (math.pi / n)
    c, s = jnp.cos(th), jnp.sin(th)
    mat = jnp.concatenate([jnp.concatenate([c, s], axis=2),
                           jnp.concatenate([-s, c], axis=2)], axis=1)
    return w1_half, w1_inv, mat.astype(BF16), jnp.swapaxes(mat, 1, 2).astype(BF16)


def _rows_of(ref2, m, n_rows):
    return jnp.concatenate([ref2[cb, pl.ds(m, n_rows, stride=SUBLANES), :] for cb in range(SLABS)], axis=1)


def _store_rows(ref2, m, val):
    for cb in range(SLABS):
        ref2[cb, pl.ds(m, val.shape[0], stride=SUBLANES), :] = val[:, cb * LANES:(cb + 1) * LANES]


def _stage1_kernel(w_ref, x_ref, o_ref, *, k, r):
    w = w_ref[...]
    for m in range(SUBLANES):
        _store_rows(o_ref, m, jnp.dot(w, _rows_of(x_ref, m, k).astype(BF16), preferred_element_type=F32))


def dft_stage1(w, x5):
    b, s = x5.shape[:2]
    r, k = w.shape
    assert s % SLABS == 0 and x5.shape[3] == k * SUBLANES
    blk = lambda rows: pl.BlockSpec((None, SLABS, None, rows * SUBLANES, LANES), lambda bi, si, j: (bi, si, j, 0, 0))
    return pl.pallas_call(
        functools.partial(_stage1_kernel, k=k, r=r), grid=(b, s // SLABS, J2),
        in_specs=[pl.BlockSpec((r, k), lambda bi, si, j: (0, 0)), blk(k)],
        out_specs=blk(r),
        out_shape=jax.ShapeDtypeStruct((b, s, J2, r * SUBLANES, LANES), F32),
        compiler_params=_cparams(3), name="dft_stage1")(w, x5)


def _slab_cat(a_ref, kk, n_slabs):
    return jnp.concatenate([jnp.concatenate([a_ref[cb, :, 0, kk].reshape(N2, LANES),
                                             a_ref[cb, :, 1, kk].reshape(N2, LANES)], axis=0)
                            for cb in range(n_slabs)], axis=1).astype(BF16)


def _stage2_filter_kernel(m_ref, af_ref, ab_ref, o_ref):
    for kk in range(K1_BLOCK):
        zf = jnp.dot(m_ref[kk], _slab_cat(af_ref, kk, SLABS), preferred_element_type=F32)
        zb = jnp.dot(m_ref[kk], _slab_cat(ab_ref, kk, SLABS), preferred_element_type=F32)
        o_ref[kk] = jnp.concatenate([zf[:N2] + zb[:N2], zf[N2:] - zb[N2:]], axis=0)


def dft_stage2_filter(mat, a7):
    n1h = a7.shape[4]
    grp = lambda first: pl.BlockSpec((None, SLABS, J2, 2, K1_BLOCK, SUBLANES, LANES),
                                     lambda kb, o: (0, first + o, 0, 0, kb, 0, 0))
    return pl.pallas_call(
        _stage2_filter_kernel, grid=(n1h // K1_BLOCK, 2),
        in_specs=[pl.BlockSpec((K1_BLOCK, 2 * N2, 2 * N2), lambda kb, o: (kb, 0, 0)), grp(0), grp(2)],
        out_specs=pl.BlockSpec((K1_BLOCK, 2 * N2, HC), lambda kb, o: (kb, 0, o)),
        out_shape=jax.ShapeDtypeStruct((n1h, 2 * N2, 2 * HC), F32),
        compiler_params=_cparams(2), name="dft_stage2_filter")(mat, a7, a7)


CONV_SCRATCH_BYTES = 16 * 1024 * 1024
CONV_J_BLOCK = 4
EXPANDED_STAGE1_MAX_N1H = 32


def _conv_fused_kernel(w1_ref, wi_ref, m_ref, mt_ref, h_ref, z_ref, gate_ref, d_ref, o_ref, a_scr, z_scr,
                       *, n1h, sg, nb, scale, expanded):
    t = pl.program_id(2)
    n_a = J2 // CONV_J_BLOCK
    cat = lambda pieces: jnp.concatenate(pieces, axis=1) if len(pieces) > 1 else pieces[0]
    strided = lambda ref, idx, m, rows: cat([ref[(cb,) + idx + (pl.ds(m, rows, stride=SUBLANES), slice(None))]
                                             for cb in range(sg)])
    whole = lambda ref, idx: cat([ref[(cb,) + idx] for cb in range(sg)])

    @pl.when(t < n_a)
    def _():
        w = w1_ref[...]
        for jj in range(CONV_J_BLOCK):
            j = t * CONV_J_BLOCK + jj
            for cb in range(sg):
                z_scr[cb, j] = z_ref[cb, jj]
            if expanded:
                res = jnp.dot(w, whole(z_ref, (jj,)).astype(BF16), preferred_element_type=F32)
                for cb in range(sg):
                    a_scr[cb, j] = res[:, cb * LANES:(cb + 1) * LANES]
                continue
            for m in range(SUBLANES):
                res = jnp.dot(w, strided(z_ref, (jj,), m, n1h).astype(BF16), preferred_element_type=F32)
                for cb in range(sg):
                    a_scr[cb, j, pl.ds(m, 2 * n1h, stride=SUBLANES), :] = res[:, cb * LANES:(cb + 1) * LANES]

    @pl.when((t >= n_a) & (t < n_a + nb))
    def _():
        kb = t - n_a
        re0 = [pl.multiple_of((kb * K1_BLOCK + kk) * SUBLANES, SUBLANES) for kk in range(K1_BLOCK)]
        im0 = [pl.multiple_of((n1h + kb * K1_BLOCK + kk) * SUBLANES, SUBLANES) for kk in range(K1_BLOCK)]
        tile = lambda cb, r0: a_scr[cb, :, pl.ds(r0, SUBLANES), :].reshape(N2, LANES)
        zs = [jnp.dot(m_ref[kk], cat([jnp.concatenate([tile(cb, re0[kk]), tile(cb, im0[kk])], axis=0)
                                      for cb in range(sg)]).astype(BF16), preferred_element_type=F32)
              for kk in range(K1_BLOCK)]
        ys = []
        for kk, z in enumerate(zs):
            zr, zi = z[:N2], z[N2:]
            hr, hi = h_ref[kk, :N2], h_ref[kk, N2:]
            ys.append(jnp.concatenate([zr * hr - zi * hi, zr * hi + zi * hr], axis=0).astype(BF16))
        gs = [jnp.dot(mt_ref[kk], y, preferred_element_type=F32) for kk, y in enumerate(ys)]
        for kk, g in enumerate(gs):
            for cb in range(sg):
                lanes = slice(cb * LANES, (cb + 1) * LANES)
                a_scr[cb, :, pl.ds(re0[kk], SUBLANES), :] = g[:N2, lanes].reshape(J2, SUBLANES, LANES)
                a_scr[cb, :, pl.ds(im0[kk], SUBLANES), :] = g[N2:, lanes].reshape(J2, SUBLANES, LANES)

    @pl.when(t >= n_a + nb)
    def _():
        w, d = wi_ref[...], d_ref[...]
        for jj in range(CONV_J_BLOCK):
            j = (t - n_a - nb) * CONV_J_BLOCK + jj
            if expanded:
                y = jnp.dot(w, cat([a_scr[cb, j] for cb in range(sg)]).astype(BF16), preferred_element_type=F32) * scale
                res = whole(gate_ref, (jj,)) * (y + d * whole(z_scr, (j,)))
                for cb in range(sg):
                    o_ref[cb, jj] = res[:, cb * LANES:(cb + 1) * LANES]
                continue
            for m in range(SUBLANES):
                gm = cat([a_scr[cb, j, pl.ds(m, 2 * n1h, stride=SUBLANES), :] for cb in range(sg)]).astype(BF16)
                y = jnp.dot(w, gm, preferred_element_type=F32) * scale
                res = strided(gate_ref, (jj,), m, n1h) * (y + d * strided(z_scr, (j,), m, n1h))
                for cb in range(sg):
                    o_ref[cb, jj, pl.ds(m, n1h, stride=SUBLANES), :] = res[:, cb * LANES:(cb + 1) * LANES]


def hyena_conv_fused(z, gate, d_vec, spec, order, tabs):
    batch, n1h = z.shape[0], z.shape[3]
    w1_half, w1_inv, mat, mat_t = tabs
    expanded = n1h <= EXPANDED_STAGE1_MAX_N1H
    if expanded:
        eye = jnp.eye(SUBLANES, dtype=BF16)
        w1_half, w1_inv = jnp.kron(w1_half, eye), jnp.kron(w1_inv, eye)
    sg = max(1, min(SLABS, CONV_SCRATCH_BYTES // (J2 * 2 * n1h * SUBLANES * LANES * 4)))
    nb = n1h // K1_BLOCK
    n_a = J2 // CONV_J_BLOCK
    rows = lambda a: a.reshape(batch, SLABS, J2, n1h * SUBLANES, LANES)
    phase_c = lambda t: jnp.maximum(t - n_a - nb, 0)
    z_idx = lambda t: jnp.minimum(t, n_a - 1)
    tile = lambda f: pl.BlockSpec((None, sg, CONV_J_BLOCK, n1h * SUBLANES, LANES), lambda b, s, t: (b, s, f(t), 0, 0))
    kblk = lambda t: jnp.clip(t - n_a, 0, nb - 1)
    const = lambda a: pl.BlockSpec(a.shape, lambda b, s, t: (0, 0))
    mspec = pl.BlockSpec((K1_BLOCK, 2 * N2, 2 * N2), lambda b, s, t: (kblk(t), 0, 0))
    out = pl.pallas_call(
        functools.partial(_conv_fused_kernel, n1h=n1h, sg=sg, nb=nb, scale=1.0 / (n1h * N2), expanded=expanded),
        grid=(batch, SLABS // sg, 2 * n_a + nb),
        in_specs=[const(w1_half), const(w1_inv), mspec, mspec,
                  pl.BlockSpec((K1_BLOCK, 2 * N2, sg * LANES),
                               lambda b, s, t: (kblk(t), 0, order * (SLABS // sg) + s)),
                  tile(z_idx), tile(phase_c),
                  pl.BlockSpec((1, sg * LANES), lambda b, s, t: (0, s))],
        out_specs=tile(phase_c),
        out_shape=jax.ShapeDtypeStruct((batch, SLABS, J2, n1h * SUBLANES, LANES), F32),
        scratch_shapes=[pltpu.VMEM((sg, J2, 2 * n1h * SUBLANES, LANES), F32),
                        pltpu.VMEM((sg, J2, n1h * SUBLANES, LANES), F32)],
        compiler_params=_cparams(3), name="hyena_conv")(w1_half, w1_inv, mat, mat_t, spec, rows(z), rows(gate),
                                                        d_vec.reshape(1, HC))
    return out.reshape(z.shape)


def _silu(g):
    return g / (1.0 + jnp.exp(-g))


def _band_mask(t, halo, q0, seq_len):
    qi = lax.broadcasted_iota(jnp.int32, (t, t + 2 * halo), 0)
    kj = lax.broadcasted_iota(jnp.int32, (t, t + 2 * halo), 1)
    kpos = q0 - halo + kj
    return (jnp.abs(kj - halo - qi) <= halo) & (kpos >= 0) & (kpos < seq_len)


def _head_pair_operands(win):
    x = win.astype(F32)
    swapped = pltpu.roll(x, B_HEAD_DIM, 1)
    low = lax.broadcasted_iota(jnp.int32, x.shape, 1) < B_HEAD_DIM
    place = lambda cond, val: jnp.where(cond, val, 0.0).astype(BF16)
    return ((place(low, x), place(~low, swapped)), (place(low, swapped), place(~low, x)))


def _band_attn_kernel(sink_ref, q_ref, kp_ref, kc_ref, kn_ref, vp_ref, vc_ref, vn_ref, gate_ref, o_ref, *, seq_len):
    t, hd, halo = ATTN_SUB, B_HEAD_DIM, B_BLOCK
    kpads = _head_pair_operands(jnp.concatenate([kp_ref[...], kc_ref[...], kn_ref[...]], axis=0))
    vpads = _head_pair_operands(jnp.concatenate([vp_ref[...], vc_ref[...], vn_ref[...]], axis=0))
    tq = q_ref.shape[0]
    heads = [(pair, par) for pair in range(B_HEADS // 2) for par in range(2)]
    for u in range(tq // t):
        valid = _band_mask(t, halo, pl.program_id(1) * tq + u * t, seq_len)
        rows = slice(u * t, (u + 1) * t)
        win = slice(u * t, (u + 1) * t + 2 * halo)
        scores = [lax.dot_general(q_ref[rows, pair * LANES:(pair + 1) * LANES], kpads[(2 * pair) // B_GROUP][par][win],
                                  (((1,), (1,)), ((), ())), preferred_element_type=F32) for pair, par in heads]
        probs, denoms = [], []
        scale = hd ** -0.5
        for (pair, par), s in zip(heads, scores):
            s = jnp.where(valid, s, NEG_INF)
            sink = sink_ref[2 * pair + par]
            m_raw = jnp.maximum(jnp.max(s, axis=-1, keepdims=True), sink * (1.0 / scale))
            p = jnp.exp2((s - m_raw) * (scale * LOG2E))
            denoms.append(jnp.sum(p, axis=-1, keepdims=True) + jnp.exp2((sink * (1.0 / scale) - m_raw) * (scale * LOG2E)))
            probs.append(p.astype(BF16))
        outs = [jnp.dot(p, vpads[(2 * pair) // B_GROUP][par][win], preferred_element_type=F32) / d
                for (pair, par), p, d in zip(heads, probs, denoms)]
        for pair in range(B_HEADS // 2):
            lanes = slice(pair * LANES, (pair + 1) * LANES)
            o_ref[rows, lanes] = ((outs[2 * pair] + outs[2 * pair + 1]) * _silu(gate_ref[rows, lanes])).astype(BF16)


def band_attn(qk, v_att, at_gate, sink, batch, seq_len, *, nsub=4):
    t = B_BLOCK
    tq = t * nsub
    nblk = seq_len // t
    qk3 = qk.reshape(batch, seq_len, qk.shape[1])
    v3 = v_att.reshape(batch, seq_len, LANES)
    g3 = at_gate.reshape(batch, seq_len, at_gate.shape[1])
    k_col = B_HEADS * B_HEAD_DIM // LANES
    halo = lambda col, f: pl.BlockSpec((None, t, LANES), lambda b, i: (b, f(i), col))
    cur = lambda col: pl.BlockSpec((None, tq, LANES), lambda b, i: (b, i, col))
    prev = lambda i: jnp.maximum(i * nsub - 1, 0)
    nxt = lambda i: jnp.minimum((i + 1) * nsub, nblk - 1)
    wide = pl.BlockSpec((None, tq, B_HEADS * B_HEAD_DIM), lambda b, i: (b, i, 0))
    out = pl.pallas_call(
        functools.partial(_band_attn_kernel, seq_len=seq_len), grid=(batch, seq_len // tq),
        in_specs=[pl.BlockSpec(memory_space=pltpu.SMEM), wide,
                  halo(k_col, prev), cur(k_col), halo(k_col, nxt), halo(0, prev), cur(0), halo(0, nxt), wide],
        out_specs=wide,
        out_shape=jax.ShapeDtypeStruct((batch, seq_len, B_HEADS * B_HEAD_DIM), BF16),
        compiler_params=_cparams(2), name="band_attn")(sink, qk3, qk3, qk3, qk3, v3, v3, v3, g3)
    return out.reshape(batch * seq_len, -1)


def _dil_attn_kernel(q_ref, kp_ref, kc_ref, kn_ref, vp_ref, vc_ref, vn_ref, o_ref, lse_ref, *, seq_len):
    t, hd, halo = ATTN_SUB, C_HEAD_DIM, C_RADIUS
    tq = q_ref.shape[0]
    nsub = tq // t
    lane = lax.broadcasted_iota(jnp.int32, (t, LANES), 1)
    cols = [slice(h * hd, (h + 1) * hd) for h in range(C_HEADS)]
    scale = hd ** -0.5
    rows = lambda u: slice(u * t, (u + 1) * t)
    window = lambda u, p_ref, c_ref, n_ref, sl: jnp.concatenate(
        [p_ref[:, sl], c_ref[:, sl], n_ref[:, sl]], axis=0)[u * t:(u + 1) * t + 2 * halo]

    def qk_stage(u):
        return [lax.dot_general(q_ref[rows(u), sl], window(u, kp_ref, kc_ref, kn_ref, sl), (((1,), (1,)), ((), ())),
                                preferred_element_type=F32) for sl in cols]

    def softmax_stage(u, scores):
        valid = _band_mask(t, halo, pl.program_id(1) * tq + u * t, seq_len)
        probs, denoms = [], []
        lse_tile = jnp.zeros((t, LANES), F32)
        for h, s in enumerate(scores):
            s = jnp.where(valid, s, NEG_INF)
            m_raw = jnp.max(s, axis=-1, keepdims=True)
            p = jnp.exp2((s - m_raw) * (scale * LOG2E))
            denom = jnp.sum(p, axis=-1, keepdims=True)
            lse_tile = jnp.where(lane == h, m_raw * scale + jnp.log(denom), lse_tile)
            probs.append(p.astype(BF16))
            denoms.append(denom)
        lse_ref[rows(u), :] = lse_tile
        return probs, denoms

    def pv_stage(u, probs, denoms):
        for sl, p, denom in zip(cols, probs, denoms):
            o = jnp.dot(p, window(u, vp_ref, vc_ref, vn_ref, sl), preferred_element_type=F32) / denom
            o_ref[rows(u), sl] = o.astype(BF16)

    for u in range(nsub):
        pv_stage(u, *softmax_stage(u, qk_stage(u)))


def dil_attn(qk, v, n_seq, ls):
    w = C_HEADS * C_HEAD_DIM
    t = 2 * C_RADIUS
    nsub = next(n for n in (8, 4, 2, 1) if ls % (n * t) == 0)
    tq = t * nsub
    n_halo = ls // C_RADIUS
    qk3 = qk.reshape(n_seq, ls, qk.shape[1])
    v3 = v.reshape(n_seq, ls, w)
    prev = lambda i: jnp.maximum(i * (tq // C_RADIUS) - 1, 0)
    nxt = lambda i: jnp.minimum((i + 1) * (tq // C_RADIUS), n_halo - 1)
    big = lambda col: pl.BlockSpec((None, tq, w), lambda b, i: (b, i, col))
    halo = lambda col, f: pl.BlockSpec((None, C_RADIUS, w), lambda b, i: (b, f(i), col))
    o, lse = pl.pallas_call(
        functools.partial(_dil_attn_kernel, seq_len=ls), grid=(n_seq, ls // tq),
        in_specs=[big(0), halo(1, prev), big(1), halo(1, nxt), halo(0, prev), big(0), halo(0, nxt)],
        out_specs=(big(0), pl.BlockSpec((None, tq, LANES), lambda b, i: (b, i, 0))),
        out_shape=(jax.ShapeDtypeStruct((n_seq, ls, w), BF16),
                   jax.ShapeDtypeStruct((n_seq, ls, LANES), F32)),
        compiler_params=_cparams(2), name="dil_attn")(qk3, qk3, qk3, qk3, v3, v3, v3)
    return o.reshape(n_seq * ls, w), lse.reshape(n_seq * ls, LANES)


def _out_proj_even_kernel(hy_ref, at_ref, x_ref, w_ref, o_ref, *, nb):
    mixed = jnp.concatenate([_load_time_tiles(hy_ref, nb).astype(BF16), at_ref[...]], axis=1)
    o_ref[...] = x_ref[...] + jnp.dot(mixed, w_ref[...], preferred_element_type=F32)


def out_proj_even(hy, at, x2d, w_bf, *, tm=1024):
    m, d = x2d.shape
    batch, n1h = hy.shape[0], hy.shape[3]
    tps = n1h * N2 // tm
    nb = tm // N2
    half = pl.BlockSpec((tm, HC), lambda b, i: (b * tps + i, 0))
    full = pl.BlockSpec((tm, d), lambda b, i: (b * tps + i, 0))
    return pl.pallas_call(
        functools.partial(_out_proj_even_kernel, nb=nb), grid=(batch, tps),
        in_specs=[pl.BlockSpec((None, SLABS, J2, nb, SUBLANES, LANES), lambda b, i: (b, 0, 0, i, 0, 0)), half, full,
                  pl.BlockSpec((d, d), lambda b, i: (0, 0))],
        out_specs=full, out_shape=jax.ShapeDtypeStruct((m, d), F32),
        compiler_params=_cparams(2), name="out_proj_even")(hy, at, x2d, w_bf)


def _merge_out_kernel(*refs, final, dilations):
    nh = C_HEADS
    o_nat, lse_nat = refs[0], refs[1]
    pos = 2
    strided = []
    for _ in dilations:
        strided.append((refs[pos:pos + nh], refs[pos + nh]))
        pos += nh + 1
    gate_ref, x_ref, w_ref = refs[pos:pos + 3]
    pos += 3
    g_ref = refs[pos] if final else None
    out_ref, o_scr, l_scr = refs[-3:]
    for gi, (d, (o_slabs, lse_ref)) in enumerate(zip(dilations, strided)):
        per = o_nat.shape[0] // d
        for r in range(d):
            l_scr[gi, pl.ds(r, per, stride=d), :] = lse_ref[r]
            for h in range(nh):
                o_scr[gi, h, pl.ds(r, per, stride=d), :] = o_slabs[h][r].astype(F32)
    ls = [lse_nat[...]] + [l_scr[gi] for gi in range(len(dilations))]
    mx = functools.reduce(jnp.maximum, ls)
    es = [jnp.exp(l - mx) for l in ls]
    den = functools.reduce(lambda a, b: a + b, es)
    alphas = [e / den for e in es]
    hd = C_HEAD_DIM
    parts = []
    for h in range(nh):
        acc = alphas[0][:, h:h + 1] * o_nat[:, h * hd:(h + 1) * hd].astype(F32)
        for gi in range(len(dilations)):
            acc = acc + alphas[gi + 1][:, h:h + 1] * o_scr[gi, h]
        parts.append(acc)
    y = (jnp.concatenate(parts, axis=1) * _silu(gate_ref[...])).astype(BF16)
    r = x_ref[...] + jnp.dot(y, w_ref[...], preferred_element_type=F32)
    if final:
        ms = jnp.mean(r * r, axis=-1, keepdims=True)
        r = r * lax.rsqrt(ms + NORM_EPS) * g_ref[...]
    out_ref[...] = r


def merge_out(o_nat, lse_nat, strided, gate, x2d, w_bf, batch, seq_len, final_g=None, *, tm=1024):
    m, dm = x2d.shape
    tps = seq_len // tm
    nat = lambda wd: pl.BlockSpec((tm, wd), lambda b, i: (b * tps + i, 0))
    in_specs, args = [nat(dm), nat(LANES)], [o_nat, lse_nat]
    for d, o, lse in strided:
        per = tm // d
        in_specs += [pl.BlockSpec((None, d, per, LANES), functools.partial(lambda h, b, i: (b, 0, i, h), h))
                     for h in range(C_HEADS)]
        in_specs.append(pl.BlockSpec((None, d, per, LANES), lambda b, i: (b, 0, i, 0)))
        args += [o] * C_HEADS + [lse]
    in_specs += [nat(dm), nat(dm), pl.BlockSpec((dm, dm), lambda b, i: (0, 0))]
    args += [gate, x2d, w_bf]
    if final_g is not None:
        in_specs.append(pl.BlockSpec((1, dm), lambda b, i: (0, 0)))
        args.append(final_g.reshape(1, dm))
    n_str = len(strided)
    return pl.pallas_call(
        functools.partial(_merge_out_kernel, final=final_g is not None, dilations=tuple(d for d, _, _ in strided)),
        grid=(batch, tps), in_specs=in_specs, out_specs=nat(dm), out_shape=jax.ShapeDtypeStruct((m, dm), F32),
        scratch_shapes=[pltpu.VMEM((n_str, C_HEADS, tm, LANES), F32), pltpu.VMEM((n_str, tm, LANES), F32)],
        compiler_params=_cparams(2), name="merge_out")(*args)


def _even_w_in(w_in):
    return jnp.concatenate([w_in[:, 2048:2688], w_in[:, :2048], w_in[:, 2816:], w_in[:, 2688:2816]], axis=1).astype(BF16)


def _odd_w_in(w_in):
    w = C_HEADS * C_HEAD_DIM
    blk = lambda g, t: w_in[:, (3 * g + t) * w:(3 * g + t + 1) * w]
    per = [[blk(g, 0), blk(g, 1), blk(g, 2)] for g in range(len(C_PATTERNS))]
    per[0].append(w_in[:, 9 * w:])
    return [jnp.concatenate(cols, axis=1).astype(BF16) for cols in per]


def hyena_spectrum(seq_len, tabs, fw1, fb1, ff1, fw2, fb2, ff2, fw3):
    n1h = seq_len // N2
    w1_half, _, mat, _ = tabs
    filt = hyena_filter(seq_len, fw1, fb1, ff1, fw2, fb2, ff2, fw3)
    a = dft_stage1(w1_half, filt.reshape(1, 4 * SLABS, J2, n1h * SUBLANES, LANES))
    return dft_stage2_filter(mat, a.reshape(1, 4 * SLABS, J2, 2, n1h, SUBLANES, LANES))


def hybrid_layer(x2d, batch, seq_len, norm_g, w_in_bf, conv_w, conv_b, hyena_d, sink, w_out_bf, spec, tabs, rope):
    rope_t, half = rope
    qk, x1, x2s, v, at_gate, v_att = hybrid_proj(x2d, norm_g, w_in_bf, conv_w, conv_b, batch, seq_len, rope_t, half)
    z = hyena_conv_fused(v, x1, hyena_d[0], spec, 0, tabs)
    z = hyena_conv_fused(z, x2s, hyena_d[1], spec, 1, tabs)
    at = band_attn(qk, v_att, at_gate, sink, batch, seq_len)
    return out_proj_even(z, at, x2d, w_out_bf)


def dilated_layer(x2d, batch, seq_len, norm_g, w_in_bfs, w_out_bf, rope, final_g):
    w = C_HEADS * C_HEAD_DIM
    m = batch * seq_len
    rope_t, half = rope
    o_nat = lse_nat = gate = None
    strided = []
    for gi, (_, d) in enumerate(C_PATTERNS):
        ls = seq_len // d
        if d == 1:
            qk, v, gate = norm_proj(x2d, norm_g, w_in_bfs[gi], seq_len, rope_t, half,
                                    ((2 * w, BF16), (w, BF16), (w, F32)))
            o_nat, lse_nat = dil_attn(qk, v, batch, ls)
        else:
            qk, v = norm_proj_strided(x2d, norm_g, w_in_bfs[gi], batch, seq_len, d, rope_t, half,
                                      ((2 * w, BF16), (w, BF16)))
            o, lse = dil_attn(qk.reshape(m, 2 * w), v.reshape(m, w), batch * d, ls)
            strided.append((d, o.reshape(batch, d, ls, w), lse.reshape(batch, d, ls, LANES)))
    return merge_out(o_nat, lse_nat, strided, gate, x2d, w_out_bf, batch, seq_len, final_g)


def kernel(x_prompt, x_sample, a_norm, a_w_in, a_conv_w, a_conv_b, a_filt_w1, a_filt_b1, a_filt_f1, a_filt_w2, a_filt_b2, a_filt_f2, a_filt_w3, a_hyena_d, a_sink, a_w_out, c_norm, c_w_in, c_w_out, final_norm):
    depth = a_norm.shape[0] + c_norm.shape[0]
    xs = [x_prompt, x_sample]
    shapes = [(x.shape[0], x.shape[1]) for x in xs]
    acts = [x.reshape(-1, D_MODEL) for x in xs]
    seq_lens = sorted({s[1] for s in shapes})
    tabs = {sl: dft_tables(sl // N2) for sl in seq_lens}
    rope_even = {sl: rope_tables(sl, B_HEAD_DIM) for sl in seq_lens}
    rope_odd = {sl: rope_tables(sl, C_HEAD_DIM) for sl in seq_lens}
    for layer in range(depth):
        i = layer // 2
        if layer % 2 == 0:
            w_in_bf = _even_w_in(a_w_in[i])
            w_out_bf = a_w_out[i].astype(BF16)
            specs = {sl: hyena_spectrum(sl, tabs[sl], a_filt_w1[i], a_filt_b1[i], a_filt_f1[i], a_filt_w2[i],
                                        a_filt_b2[i], a_filt_f2[i], a_filt_w3[i]) for sl in seq_lens}
            acts = [hybrid_layer(x2d, b, sl, a_norm[i], w_in_bf, a_conv_w[i], a_conv_b[i], a_hyena_d[i], a_sink[i],
                                 w_out_bf, specs[sl], tabs[sl], rope_even[sl])
                    for x2d, (b, sl) in zip(acts, shapes)]
        else:
            w_in_bfs = _odd_w_in(c_w_in[i])
            w_out_bf = c_w_out[i].astype(BF16)
            final_g = final_norm if layer == depth - 1 else None
            acts = [dilated_layer(x2d, b, sl, c_norm[i], w_in_bfs, w_out_bf, rope_odd[sl], final_g)
                    for x2d, (b, sl) in zip(acts, shapes)]
    assert depth % 2 == 0
    return tuple(a.reshape(b, sl, D_MODEL) for a, (b, sl) in zip(acts, shapes))
```

```python
import functools
import math

import numpy as np
import jax
import jax.numpy as jnp
from jax import lax
from jax.experimental import pallas as pl
from jax.experimental.pallas import tpu as pltpu

F32 = jnp.float32
BF16 = jnp.bfloat16

D_MODEL = 1024
HC = 512
FILTER_BANDS = 16
DECAY_TARGET = 1e-2
FAST_DECAY_PCT = 0.3
SLOW_DECAY_PCT = 1.5
DECAY_SHIFT = 0.05
B_HEAD_DIM = 64
B_HEADS = 8
B_GROUP = 4
B_BLOCK = 128
C_PATTERNS = ((128, 1), (512, 4), (2048, 16))
C_HEADS = 8
C_HEAD_DIM = 128
C_RADIUS = 64
ROPE_THETA = 500000.0
NORM_EPS = 1e-6
NEG_INF = -1e30
LOG2E = math.log2(math.e)

LANES = 128
SUBLANES = 8
N2 = 128
J2 = N2 // SUBLANES
SLABS = HC // LANES
K1_BLOCK = 8
ATTN_SUB = 128
VMEM_LIMIT = 56 * 1024 * 1024


def _cparams(n_axes):
    return pltpu.CompilerParams(dimension_semantics=("arbitrary",) * n_axes,
                                vmem_limit_bytes=VMEM_LIMIT)


def _rope_chunk(blk, cos, sa, sb, half):
    return (blk * cos + pltpu.roll(blk, LANES - half, 1) * sa + pltpu.roll(blk, half, 1) * sb)


def _two_deep(chunks, matmul, epilogue):
    prev = None
    for chunk in chunks:
        acc = matmul(chunk)
        if prev is not None:
            epilogue(*prev)
        prev = (chunk, acc)
    epilogue(*prev)


def _norm_proj_kernel(x_ref, g_ref, w_ref, cos_ref, sa_ref, sb_ref, *outs, half, groups, row_chunk):
    g = g_ref[...]
    chunks = [slice(r0, r0 + row_chunk) for r0 in range(0, x_ref.shape[0], row_chunk)]
    matmul = lambda rows: jnp.dot(_rms_bf16(x_ref[rows, :], g), w_ref[...], preferred_element_type=F32)

    def epilogue(rows, acc):
        col = 0
        for gi, (o_ref, (width, dtype)) in enumerate(zip(outs, groups)):
            if gi == 0:
                cos, sa, sb = cos_ref[rows, :], sa_ref[rows, :], sb_ref[rows, :]
                for ci in range(width // LANES):
                    src = slice(col + ci * LANES, col + (ci + 1) * LANES)
                    o_ref[rows, ci * LANES:(ci + 1) * LANES] = _rope_chunk(acc[:, src], cos, sa, sb, half).astype(dtype)
            else:
                o_ref[rows, :] = acc[:, col:col + width].astype(dtype)
            col += width

    _two_deep(chunks, matmul, epilogue)


def _resident(shape):
    return pl.BlockSpec(shape, lambda *_: (0,) * len(shape), pipeline_mode=pl.Buffered(1))


def norm_proj(x2d, g, w_bf, seq_len, rope, half, groups, *, tm=1024, row_chunk=256):
    m, d = x2d.shape
    n = w_bf.shape[1]
    assert m % tm == 0 and seq_len % tm == 0 and sum(wd for wd, _ in groups) == n
    tiles_per_seq = seq_len // tm
    in_specs = [pl.BlockSpec((tm, d), lambda i: (i, 0)), _resident((1, d)), _resident((d, n))]
    in_specs += [pl.BlockSpec((tm, LANES), lambda i: (i % tiles_per_seq, 0))] * 3
    return pl.pallas_call(
        functools.partial(_norm_proj_kernel, half=half, groups=tuple(groups), row_chunk=row_chunk),
        grid=(m // tm,), in_specs=in_specs,
        out_specs=tuple(pl.BlockSpec((tm, wd), lambda i: (i, 0)) for wd, _ in groups),
        out_shape=tuple(jax.ShapeDtypeStruct((m, wd), dt) for wd, dt in groups),
        compiler_params=_cparams(1), name="norm_proj")(x2d, g.reshape(1, d), w_bf, *rope)


def _by_residue(ref, d, residues):
    per = ref.shape[0] // d
    return jnp.concatenate([ref[pl.ds(r, per, stride=d), :] for r in residues], axis=0)


REGROUP_RADIX = 4


def _norm_proj_strided_kernel(*refs, half, groups, d, row_chunk):
    n_slab = D_MODEL // LANES
    x_slabs = refs[:n_slab]
    g_ref, w_ref, cos_ref, sa_ref, sb_ref = refs[n_slab:n_slab + 5]
    two_pass = d > REGROUP_RADIX
    outs = refs[n_slab + 5:-1] if two_pass else refs[n_slab + 5:]
    tm = x_slabs[0].shape[0]
    per = tm // d
    g = g_ref[...]
    sources = list(x_slabs) + [cos_ref, sa_ref, sb_ref]
    if two_pass:
        scr = refs[-1]
        group = tm // REGROUP_RADIX
        chunks = [[k + REGROUP_RADIX * r2 for r2 in range(d // REGROUP_RADIX)] for k in range(REGROUP_RADIX)]

        def regroup(i, res):
            k = res[0]
            scr[i, k * group:(k + 1) * group, :] = sources[i][pl.ds(k, group, stride=REGROUP_RADIX), :]
            return jnp.concatenate([scr[i, pl.ds(k * group + r // REGROUP_RADIX, per, stride=REGROUP_RADIX), :]
                                    for r in res], axis=0)
    else:
        chunks = [list(range(r0, r0 + row_chunk // per)) for r0 in range(0, d, row_chunk // per)]
        regroup = lambda i, res: _by_residue(sources[i], d, res)
    matmul = lambda res: jnp.dot(_rms_bf16(jnp.concatenate([regroup(i, res) for i in range(n_slab)], axis=1), g),
                                 w_ref[...], preferred_element_type=F32)

    def epilogue(res, acc):
        col = 0
        for gi, (o_ref, (width, dtype)) in enumerate(zip(outs, groups)):
            if gi == 0:
                cos, sa, sb = (regroup(n_slab + i, res) for i in range(3))
                val = jnp.concatenate([_rope_chunk(acc[:, col + ci * LANES:col + (ci + 1) * LANES], cos, sa, sb, half)
                                       for ci in range(width // LANES)], axis=1).astype(dtype)
            else:
                val = acc[:, col:col + width].astype(dtype)
            for k, r in enumerate(res):
                o_ref[r] = val[k * per:(k + 1) * per]
            col += width

    _two_deep(chunks, matmul, epilogue)


def norm_proj_strided(x2d, g, w_bf, batch, seq_len, d, rope, half, groups, *, tm=1024, row_chunk=256):
    m, dm = x2d.shape
    n = w_bf.shape[1]
    tps = seq_len // tm
    per = tm // d
    assert seq_len % tm == 0 and row_chunk % per == 0 and sum(wd for wd, _ in groups) == n
    n_slab = dm // LANES
    in_specs = [pl.BlockSpec((tm, LANES), functools.partial(lambda c, t: (t, c), c)) for c in range(n_slab)]
    in_specs += [_resident((1, dm)), _resident((dm, n))]
    in_specs += [pl.BlockSpec((tm, LANES), lambda t: (t % tps, 0))] * 3
    scratch = [pltpu.VMEM((n_slab + 3, tm, LANES), F32)] if d > REGROUP_RADIX else []
    return pl.pallas_call(
        functools.partial(_norm_proj_strided_kernel, half=half, groups=tuple(groups), d=d, row_chunk=row_chunk),
        grid=(m // tm,), in_specs=in_specs,
        out_specs=tuple(pl.BlockSpec((None, d, per, wd), lambda t: (t // tps, 0, t % tps, 0)) for wd, _ in groups),
        out_shape=tuple(jax.ShapeDtypeStruct((batch, d, seq_len // d, wd), dt) for wd, dt in groups),
        scratch_shapes=scratch,
        compiler_params=_cparams(1), name="norm_proj_strided")(*([x2d] * n_slab), g.reshape(1, dm), w_bf, *rope)


def _rms_bf16(x, g):
    ms = jnp.mean(x * x, axis=-1, keepdims=True)
    return (x * lax.rsqrt(ms + NORM_EPS) * g).astype(BF16)


_QK_W, _HY_W, _ATT_V_W = 640, 3 * HC, 128
_HY0 = _QK_W
_HYG0 = _HY0 + _HY_W
_ATG0 = _HYG0 + HC
_VATT0 = _ATG0 + HC


def _hybrid_proj_kernel(x_ref, xp_ref, xn_ref, g_ref, w_ref, cos_ref, sa_ref, sb_ref, cw_ref, cb_ref,
                        qk_ref, x1_ref, x2_ref, v_ref, atg_ref, vatt_ref, u_scr, *, half, tiles_per_seq, row_chunk):
    tm = x_ref.shape[0]
    i = pl.program_id(0) % tiles_per_seq
    g = g_ref[...]
    halo = jnp.concatenate([xp_ref[...], xn_ref[...]], axis=0)
    acc_h = jnp.dot(_rms_bf16(halo, g), w_ref[:, _HY0:_HYG0], preferred_element_type=F32)
    prev_row = jnp.where(i > 0, acc_h[SUBLANES - 1:SUBLANES], 0.0)
    next_row = jnp.where(i < tiles_per_seq - 1, acc_h[SUBLANES:SUBLANES + 1], 0.0)
    chunks = [slice(r0, r0 + row_chunk) for r0 in range(0, tm, row_chunk)]
    accs = [jnp.dot(_rms_bf16(x_ref[rows, :], g), w_ref[...], preferred_element_type=F32) for rows in chunks]
    for rows, acc in zip(chunks, accs):
        cos, sa, sb = cos_ref[rows, :], sa_ref[rows, :], sb_ref[rows, :]
        for ci in range(_QK_W // LANES):
            sl = slice(ci * LANES, (ci + 1) * LANES)
            qk_ref[rows, sl] = _rope_chunk(acc[:, sl], cos, sa, sb, half).astype(BF16)
        u_scr[rows, :] = acc[:, _HY0:_ATG0]
        atg_ref[rows, :] = acc[:, _ATG0:_VATT0]
        vatt_ref[rows, :] = acc[:, _VATT0:].astype(BF16)
    u = u_scr[:, :_HY_W]
    row = lax.broadcasted_iota(jnp.int32, u.shape, 0)
    um1 = jnp.where(row == 0, prev_row, pltpu.roll(u, 1, 0))
    up1 = jnp.where(row == tm - 1, next_row, pltpu.roll(u, tm - 1, 0))
    w = cw_ref[...]
    y = um1 * w[0:1, :] + u * w[1:2, :] + up1 * w[2:3, :] + cb_ref[...]
    _store_time_tiles(x1_ref, y[:, :HC])
    _store_time_tiles(x2_ref, y[:, HC:2 * HC] * _silu(u_scr[:, _HY_W:]))
    _store_time_tiles(v_ref, y[:, 2 * HC:])


def hybrid_proj(x2d, g, w_bf, conv_w, conv_b, batch, seq_len, rope, half, *, tm=512, row_chunk=256):
    m, d = x2d.shape
    n = w_bf.shape[1]
    tiles_per_seq = seq_len // tm
    sub = tm // SUBLANES
    last8 = m // SUBLANES - 1
    hy_shape = jax.ShapeDtypeStruct((batch, SLABS, J2, seq_len // N2, SUBLANES, LANES), F32)
    hy_spec = pl.BlockSpec((None, SLABS, J2, tm // N2, SUBLANES, LANES),
                           lambda t: (t // tiles_per_seq, 0, 0, t % tiles_per_seq, 0, 0))
    row_blk = lambda wd: pl.BlockSpec((tm, wd), lambda t: (t, 0))
    return pl.pallas_call(
        functools.partial(_hybrid_proj_kernel, half=half, tiles_per_seq=tiles_per_seq, row_chunk=row_chunk),
        grid=(m // tm,),
        in_specs=[row_blk(d),
                  pl.BlockSpec((SUBLANES, d), lambda t: (jnp.maximum(t * sub - 1, 0), 0)),
                  pl.BlockSpec((SUBLANES, d), lambda t: (jnp.minimum((t + 1) * sub, last8), 0)),
                  _resident((1, d)), _resident((d, n))]
                 + [pl.BlockSpec((tm, LANES), lambda t: (t % tiles_per_seq, 0))] * 3
                 + [_resident((3, _HY_W)), _resident((1, _HY_W))],
        out_specs=(row_blk(_QK_W), hy_spec, hy_spec, hy_spec, row_blk(HC), row_blk(_ATT_V_W)),
        out_shape=(jax.ShapeDtypeStruct((m, _QK_W), BF16), hy_shape, hy_shape, hy_shape,
                   jax.ShapeDtypeStruct((m, HC), F32), jax.ShapeDtypeStruct((m, _ATT_V_W), BF16)),
        scratch_shapes=[pltpu.VMEM((tm, _HY_W + HC), F32)],
        compiler_params=_cparams(1), name="hybrid_proj")(x2d, x2d, x2d, g.reshape(1, d), w_bf, *rope,
                                                         conv_w, conv_b.reshape(1, -1))


def rope_tables(seq_len, head_dim):
    rot = head_dim // 4
    half = rot // 2
    inv = jnp.power(ROPE_THETA, -2.0 * jnp.arange(half, dtype=F32) / rot)
    ang = jnp.arange(seq_len).astype(F32)[:, None] * inv[None, :]
    cos, sin = jnp.cos(ang), jnp.sin(ang)
    lane = np.arange(LANES) % head_dim
    idx = lane % half
    in_rot = jnp.asarray(lane < rot)[None, :]
    first = jnp.asarray(lane < half)[None, :]
    cos_t = jnp.where(in_rot, cos[:, idx], 1.0)
    sa = jnp.where(first, -sin[:, idx], 0.0)
    sb = jnp.where(in_rot & ~first, sin[:, idx], 0.0)
    return (cos_t, sa, sb), half


def _store_time_tiles(ref, val):
    for s in range(SLABS):
        for i in range(val.shape[0] // N2):
            ref[s, :, i] = val[i * N2:(i + 1) * N2, s * LANES:(s + 1) * LANES].reshape(J2, SUBLANES, LANES)


def _load_time_tiles(ref, nb):
    return jnp.concatenate([jnp.concatenate([ref[s, :, i].reshape(N2, LANES) for s in range(SLABS)], axis=1)
                            for i in range(nb)], axis=0)


def _filter_kernel(band_ref, w1_ref, b1_ref, f1_ref, w2_ref, b2_ref, f2_ref, w3_ref, dl_ref, o_ref,
                   *, seq_len, tr):
    j = (pl.program_id(0) * tr + lax.broadcasted_iota(jnp.int32, (tr, 1), 0)).astype(F32)
    t = j * (1.0 / (seq_len - 1))
    wpos = (2.0 * math.pi) * j / seq_len
    lane = lax.broadcasted_iota(jnp.int32, (tr, LANES), 1)
    arg = band_ref[...] * wpos
    feats = jnp.where(lane == 0, t,
                      jnp.where(lane <= FILTER_BANDS, jnp.cos(arg),
                                jnp.where(lane <= 2 * FILTER_BANDS, -jnp.sin(arg), 0.0)))
    h = jnp.dot(feats.astype(BF16), w1_ref[...], preferred_element_type=F32) + b1_ref[...]
    h = jnp.sin(f1_ref[...] * h)
    h = jnp.dot(h.astype(BF16), w2_ref[...], preferred_element_type=F32) + b2_ref[...]
    h = jnp.sin(f2_ref[...] * h)
    h = jnp.dot(h.astype(BF16), w3_ref[...], preferred_element_type=F32)
    win = jnp.exp(-t * dl_ref[...]) + DECAY_SHIFT
    for grp in range(4):
        _store_time_tiles(o_ref.at[grp * SLABS:(grp + 1) * SLABS], h[:, grp * HC:(grp + 1) * HC] * win)


def hyena_filter(seq_len, w1, b1, f1, w2, b2, f2, w3, *, tr=512):
    hid = w1.shape[1]
    pad = LANES - hid
    w1p = jnp.pad(w1, ((0, LANES - w1.shape[0]), (0, pad))).astype(BF16)
    w2p = jnp.pad(w2, ((0, pad), (0, pad))).astype(BF16)
    w3r = w3.reshape(hid, 2, 2, HC).transpose(0, 2, 1, 3).reshape(hid, 4 * HC)
    w3p = jnp.pad(w3r, ((0, pad), (0, 0))).astype(BF16)
    vec = lambda v: jnp.pad(v, (0, pad)).reshape(1, LANES)
    bands = jnp.linspace(1e-4, FILTER_BANDS - 1, FILTER_BANDS, dtype=F32)
    band_l = jnp.concatenate([jnp.zeros((1,), F32), bands, bands,
                              jnp.zeros((LANES - 1 - 2 * FILTER_BANDS,), F32)]).reshape(1, LANES)
    max_decay = math.log(DECAY_TARGET) / FAST_DECAY_PCT
    min_decay = math.log(DECAY_TARGET) / SLOW_DECAY_PCT
    deltas = jnp.abs(jnp.linspace(min_decay, max_decay, HC, dtype=F32)).reshape(1, HC)
    full = lambda a: pl.BlockSpec(a.shape, lambda i: (0, 0))
    args = [band_l, w1p, vec(b1), vec(f1), w2p, vec(b2), vec(f2), w3p, deltas]
    return pl.pallas_call(
        functools.partial(_filter_kernel, seq_len=seq_len, tr=tr),
        grid=(seq_len // tr,),
        in_specs=[full(a) for a in args],
        out_specs=pl.BlockSpec((4 * SLABS, J2, tr // N2, SUBLANES, LANES), lambda i: (0, 0, i, 0, 0)),
        out_shape=jax.ShapeDtypeStruct((4 * SLABS, J2, seq_len // N2, SUBLANES, LANES), F32),
        compiler_params=_cparams(1), name="hyena_filter")(*args)


def dft_tables(n1h):
    n1 = 2 * n1h
    n = n1 * N2
    k1 = jnp.arange(n1h, dtype=jnp.int32)
    m1 = (jnp.arange(n1, dtype=jnp.int32)[None, :] * (2 * k1[:, None] + 1)) % (2 * n1)
    a1 = m1.astype(F32) * (math.pi / n1)
    c1, s1 = jnp.cos(a1)[:, :n1h], jnp.sin(a1)[:, :n1h]
    w1_half = jnp.concatenate([c1, -s1], axis=0).astype(BF16)
    w1_inv = jnp.concatenate([c1.T, -s1.T], axis=1).astype(BF16)
    k2 = jnp.arange(N2, dtype=jnp.int32)
    n2 = jnp.arange(N2, dtype=jnp.int32)
    f = 2 * (k2[None, :, None] * n1 + k1[:, None, None]) + 1
    m2 = (n2[None, None, :] * f) % (2 * n)
    th = m2.astype(F32) * (math.pi / n)
    c, s = jnp.cos(th), jnp.sin(th)
    mat = jnp.concatenate([jnp.concatenate([c, s], axis=2),
                           jnp.concatenate([-s, c], axis=2)], axis=1)
    return w1_half, w1_inv, mat.astype(BF16), jnp.swapaxes(mat, 1, 2).astype(BF16)


def _rows_of(ref2, m, n_rows):
    return jnp.concatenate([ref2[cb, pl.ds(m, n_rows, stride=SUBLANES), :] for cb in range(SLABS)], axis=1)


def _store_rows(ref2, m, val):
    for cb in range(SLABS):
        ref2[cb, pl.ds(m, val.shape[0], stride=SUBLANES), :] = val[:, cb * LANES:(cb + 1) * LANES]


def _stage1_kernel(w_ref, x_ref, o_ref, *, k, r):
    w = w_ref[...]
    for m in range(SUBLANES):
        _store_rows(o_ref, m, jnp.dot(w, _rows_of(x_ref, m, k).astype(BF16), preferred_element_type=F32))


def dft_stage1(w, x5):
    b, s = x5.shape[:2]
    r, k = w.shape
    assert s % SLABS == 0 and x5.shape[3] == k * SUBLANES
    blk = lambda rows: pl.BlockSpec((None, SLABS, None, rows * SUBLANES, LANES), lambda bi, si, j: (bi, si, j, 0, 0))
    return pl.pallas_call(
        functools.partial(_stage1_kernel, k=k, r=r), grid=(b, s // SLABS, J2),
        in_specs=[pl.BlockSpec((r, k), lambda bi, si, j: (0, 0)), blk(k)],
        out_specs=blk(r),
        out_shape=jax.ShapeDtypeStruct((b, s, J2, r * SUBLANES, LANES), F32),
        compiler_params=_cparams(3), name="dft_stage1")(w, x5)


def _slab_cat(a_ref, kk, n_slabs):
    return jnp.concatenate([jnp.concatenate([a_ref[cb, :, 0, kk].reshape(N2, LANES),
                                             a_ref[cb, :, 1, kk].reshape(N2, LANES)], axis=0)
                            for cb in range(n_slabs)], axis=1).astype(BF16)


def _stage2_filter_kernel(m_ref, af_ref, ab_ref, o_ref):
    for kk in range(K1_BLOCK):
        zf = jnp.dot(m_ref[kk], _slab_cat(af_ref, kk, SLABS), preferred_element_type=F32)
        zb = jnp.dot(m_ref[kk], _slab_cat(ab_ref, kk, SLABS), preferred_element_type=F32)
        o_ref[kk] = jnp.concatenate([zf[:N2] + zb[:N2], zf[N2:] - zb[N2:]], axis=0)


def dft_stage2_filter(mat, a7):
    n1h = a7.shape[4]
    grp = lambda first: pl.BlockSpec((None, SLABS, J2, 2, K1_BLOCK, SUBLANES, LANES),
                                     lambda kb, o: (0, first + o, 0, 0, kb, 0, 0))
    return pl.pallas_call(
        _stage2_filter_kernel, grid=(n1h // K1_BLOCK, 2),
        in_specs=[pl.BlockSpec((K1_BLOCK, 2 * N2, 2 * N2), lambda kb, o: (kb, 0, 0)), grp(0), grp(2)],
        out_specs=pl.BlockSpec((K1_BLOCK, 2 * N2, HC), lambda kb, o: (kb, 0, o)),
        out_shape=jax.ShapeDtypeStruct((n1h, 2 * N2, 2 * HC), F32),
        compiler_params=_cparams(2), name="dft_stage2_filter")(mat, a7, a7)


CONV_SCRATCH_BYTES = 16 * 1024 * 1024
CONV_J_BLOCK = 4
EXPANDED_STAGE1_MAX_N1H = 32


def _conv_fused_kernel(w1_ref, wi_ref, m_ref, mt_ref, h_ref, z_ref, gate_ref, d_ref, o_ref, a_scr, z_scr,
                       *, n1h, sg, nb, scale, expanded):
    t = pl.program_id(2)
    n_a = J2 // CONV_J_BLOCK
    cat = lambda pieces: jnp.concatenate(pieces, axis=1) if len(pieces) > 1 else pieces[0]
    strided = lambda ref, idx, m, rows: cat([ref[(cb,) + idx + (pl.ds(m, rows, stride=SUBLANES), slice(None))]
                                             for cb in range(sg)])
    whole = lambda ref, idx: cat([ref[(cb,) + idx] for cb in range(sg)])

    @pl.when(t < n_a)
    def _():
        w = w1_ref[...]
        for jj in range(CONV_J_BLOCK):
            j = t * CONV_J_BLOCK + jj
            for cb in range(sg):
                z_scr[cb, j] = z_ref[cb, jj]
            if expanded:
                res = jnp.dot(w, whole(z_ref, (jj,)).astype(BF16), preferred_element_type=F32)
                for cb in range(sg):
                    a_scr[cb, j] = res[:, cb * LANES:(cb + 1) * LANES]
                continue
            for m in range(SUBLANES):
                res = jnp.dot(w, strided(z_ref, (jj,), m, n1h).astype(BF16), preferred_element_type=F32)
                for cb in range(sg):
                    a_scr[cb, j, pl.ds(m, 2 * n1h, stride=SUBLANES), :] = res[:, cb * LANES:(cb + 1) * LANES]

    @pl.when((t >= n_a) & (t < n_a + nb))
    def _():
        kb = t - n_a
        re0 = [pl.multiple_of((kb * K1_BLOCK + kk) * SUBLANES, SUBLANES) for kk in range(K1_BLOCK)]
        im0 = [pl.multiple_of((n1h + kb * K1_BLOCK + kk) * SUBLANES, SUBLANES) for kk in range(K1_BLOCK)]
        tile = lambda cb, r0: a_scr[cb, :, pl.ds(r0, SUBLANES), :].reshape(N2, LANES)
        zs = [jnp.dot(m_ref[kk], cat([jnp.concatenate([tile(cb, re0[kk]), tile(cb, im0[kk])], axis=0)
                                      for cb in range(sg)]).astype(BF16), preferred_element_type=F32)
              for kk in range(K1_BLOCK)]
        ys = []
        for kk, z in enumerate(zs):
            zr, zi = z[:N2], z[N2:]
            hr, hi = h_ref[kk, :N2], h_ref[kk, N2:]
            ys.append(jnp.concatenate([zr * hr - zi * hi, zr * hi + zi * hr], axis=0).astype(BF16))
        gs = [jnp.dot(mt_ref[kk], y, preferred_element_type=F32) for kk, y in enumerate(ys)]
        for kk, g in enumerate(gs):
            for cb in range(sg):
                lanes = slice(cb * LANES, (cb + 1) * LANES)
                a_scr[cb, :, pl.ds(re0[kk], SUBLANES), :] = g[:N2, lanes].reshape(J2, SUBLANES, LANES)
                a_scr[cb, :, pl.ds(im0[kk], SUBLANES), :] = g[N2:, lanes].reshape(J2, SUBLANES, LANES)

    @pl.when(t >= n_a + nb)
    def _():
        w, d = wi_ref[...], d_ref[...]
        for jj in range(CONV_J_BLOCK):
            j = (t - n_a - nb) * CONV_J_BLOCK + jj
            if expanded:
                y = jnp.dot(w, cat([a_scr[cb, j] for cb in range(sg)]).astype(BF16), preferred_element_type=F32) * scale
                res = whole(gate_ref, (jj,)) * (y + d * whole(z_scr, (j,)))
                for cb in range(sg):
                    o_ref[cb, jj] = res[:, cb * LANES:(cb + 1) * LANES]
                continue
            for m in range(SUBLANES):
                gm = cat([a_scr[cb, j, pl.ds(m, 2 * n1h, stride=SUBLANES), :] for cb in range(sg)]).astype(BF16)
                y = jnp.dot(w, gm, preferred_element_type=F32) * scale
                res = strided(gate_ref, (jj,), m, n1h) * (y + d * strided(z_scr, (j,), m, n1h))
                for cb in range(sg):
                    o_ref[cb, jj, pl.ds(m, n1h, stride=SUBLANES), :] = res[:, cb * LANES:(cb + 1) * LANES]


def hyena_conv_fused(z, gate, d_vec, spec, order, tabs):
    batch, n1h = z.shape[0], z.shape[3]
    w1_half, w1_inv, mat, mat_t = tabs
    expanded = n1h <= EXPANDED_STAGE1_MAX_N1H
    if expanded:
        eye = jnp.eye(SUBLANES, dtype=BF16)
        w1_half, w1_inv = jnp.kron(w1_half, eye), jnp.kron(w1_inv, eye)
    sg = max(1, min(SLABS, CONV_SCRATCH_BYTES // (J2 * 2 * n1h * SUBLANES * LANES * 4)))
    nb = n1h // K1_BLOCK
    n_a = J2 // CONV_J_BLOCK
    rows = lambda a: a.reshape(batch, SLABS, J2, n1h * SUBLANES, LANES)
    phase_c = lambda t: jnp.maximum(t - n_a - nb, 0)
    z_idx = lambda t: jnp.minimum(t, n_a - 1)
    tile = lambda f: pl.BlockSpec((None, sg, CONV_J_BLOCK, n1h * SUBLANES, LANES), lambda b, s, t: (b, s, f(t), 0, 0))
    kblk = lambda t: jnp.clip(t - n_a, 0, nb - 1)
    const = lambda a: pl.BlockSpec(a.shape, lambda b, s, t: (0, 0))
    mspec = pl.BlockSpec((K1_BLOCK, 2 * N2, 2 * N2), lambda b, s, t: (kblk(t), 0, 0))
    out = pl.pallas_call(
        functools.partial(_conv_fused_kernel, n1h=n1h, sg=sg, nb=nb, scale=1.0 / (n1h * N2), expanded=expanded),
        grid=(batch, SLABS // sg, 2 * n_a + nb),
        in_specs=[const(w1_half), const(w1_inv), mspec, mspec,
                  pl.BlockSpec((K1_BLOCK, 2 * N2, sg * LANES),
                               lambda b, s, t: (kblk(t), 0, order * (SLABS // sg) + s)),
                  tile(z_idx), tile(phase_c),
                  pl.BlockSpec((1, sg * LANES), lambda b, s, t: (0, s))],
        out_specs=tile(phase_c),
        out_shape=jax.ShapeDtypeStruct((batch, SLABS, J2, n1h * SUBLANES, LANES), F32),
        scratch_shapes=[pltpu.VMEM((sg, J2, 2 * n1h * SUBLANES, LANES), F32),
                        pltpu.VMEM((sg, J2, n1h * SUBLANES, LANES), F32)],
        compiler_params=_cparams(3), name="hyena_conv")(w1_half, w1_inv, mat, mat_t, spec, rows(z), rows(gate),
                                                        d_vec.reshape(1, HC))
    return out.reshape(z.shape)


def _silu(g):
    return g / (1.0 + jnp.exp(-g))


def _band_mask(t, halo, q0, seq_len):
    qi = lax.broadcasted_iota(jnp.int32, (t, t + 2 * halo), 0)
    kj = lax.broadcasted_iota(jnp.int32, (t, t + 2 * halo), 1)
    kpos = q0 - halo + kj
    return (jnp.abs(kj - halo - qi) <= halo) & (kpos >= 0) & (kpos < seq_len)


def _head_pair_operands(win):
    x = win.astype(F32)
    swapped = pltpu.roll(x, B_HEAD_DIM, 1)
    low = lax.broadcasted_iota(jnp.int32, x.shape, 1) < B_HEAD_DIM
    place = lambda cond, val: jnp.where(cond, val, 0.0).astype(BF16)
    return ((place(low, x), place(~low, swapped)), (place(low, swapped), place(~low, x)))


def _band_attn_kernel(sink_ref, q_ref, kp_ref, kc_ref, kn_ref, vp_ref, vc_ref, vn_ref, gate_ref, o_ref, *, seq_len):
    t, hd, halo = ATTN_SUB, B_HEAD_DIM, B_BLOCK
    kpads = _head_pair_operands(jnp.concatenate([kp_ref[...], kc_ref[...], kn_ref[...]], axis=0))
    vpads = _head_pair_operands(jnp.concatenate([vp_ref[...], vc_ref[...], vn_ref[...]], axis=0))
    tq = q_ref.shape[0]
    heads = [(pair, par) for pair in range(B_HEADS // 2) for par in range(2)]
    for u in range(tq // t):
        valid = _band_mask(t, halo, pl.program_id(1) * tq + u * t, seq_len)
        rows = slice(u * t, (u + 1) * t)
        win = slice(u * t, (u + 1) * t + 2 * halo)
        scores = [lax.dot_general(q_ref[rows, pair * LANES:(pair + 1) * LANES], kpads[(2 * pair) // B_GROUP][par][win],
                                  (((1,), (1,)), ((), ())), preferred_element_type=F32) for pair, par in heads]
        probs, denoms = [], []
        scale = hd ** -0.5
        for (pair, par), s in zip(heads, scores):
            s = jnp.where(valid, s, NEG_INF)
            sink = sink_ref[2 * pair + par]
            m_raw = jnp.maximum(jnp.max(s, axis=-1, keepdims=True), sink * (1.0 / scale))
            p = jnp.exp2((s - m_raw) * (scale * LOG2E))
            denoms.append(jnp.sum(p, axis=-1, keepdims=True) + jnp.exp2((sink * (1.0 / scale) - m_raw) * (scale * LOG2E)))
            probs.append(p.astype(BF16))
        outs = [jnp.dot(p, vpads[(2 * pair) // B_GROUP][par][win], preferred_element_type=F32) / d
                for (pair, par), p, d in zip(heads, probs, denoms)]
        for pair in range(B_HEADS // 2):
            lanes = slice(pair * LANES, (pair + 1) * LANES)
            o_ref[rows, lanes] = ((outs[2 * pair] + outs[2 * pair + 1]) * _silu(gate_ref[rows, lanes])).astype(BF16)


def band_attn(qk, v_att, at_gate, sink, batch, seq_len, *, nsub=8):
    t = B_BLOCK
    tq = t * nsub
    nblk = seq_len // t
    qk3 = qk.reshape(batch, seq_len, qk.shape[1])
    v3 = v_att.reshape(batch, seq_len, LANES)
    g3 = at_gate.reshape(batch, seq_len, at_gate.shape[1])
    k_col = B_HEADS * B_HEAD_DIM // LANES
    halo = lambda col, f: pl.BlockSpec((None, t, LANES), lambda b, i: (b, f(i), col))
    cur = lambda col: pl.BlockSpec((None, tq, LANES), lambda b, i: (b, i, col))
    prev = lambda i: jnp.maximum(i * nsub - 1, 0)
    nxt = lambda i: jnp.minimum((i + 1) * nsub, nblk - 1)
    wide = pl.BlockSpec((None, tq, B_HEADS * B_HEAD_DIM), lambda b, i: (b, i, 0))
    out = pl.pallas_call(
        functools.partial(_band_attn_kernel, seq_len=seq_len), grid=(batch, seq_len // tq),
        in_specs=[pl.BlockSpec(memory_space=pltpu.SMEM), wide,
                  halo(k_col, prev), cur(k_col), halo(k_col, nxt), halo(0, prev), cur(0), halo(0, nxt), wide],
        out_specs=wide,
        out_shape=jax.ShapeDtypeStruct((batch, seq_len, B_HEADS * B_HEAD_DIM), BF16),
        compiler_params=_cparams(2), name="band_attn")(sink, qk3, qk3, qk3, qk3, v3, v3, v3, g3)
    return out.reshape(batch * seq_len, -1)


def _dil_attn_kernel(q_ref, kp_ref, kc_ref, kn_ref, vp_ref, vc_ref, vn_ref, o_ref, lse_ref, *, seq_len):
    t, hd, halo = ATTN_SUB, C_HEAD_DIM, C_RADIUS
    tq = q_ref.shape[0]
    nsub = tq // t
    lane = lax.broadcasted_iota(jnp.int32, (t, LANES), 1)
    cols = [slice(h * hd, (h + 1) * hd) for h in range(C_HEADS)]
    scale = hd ** -0.5
    rows = lambda u: slice(u * t, (u + 1) * t)
    window = lambda u, p_ref, c_ref, n_ref, sl: jnp.concatenate(
        [p_ref[:, sl], c_ref[:, sl], n_ref[:, sl]], axis=0)[u * t:(u + 1) * t + 2 * halo]

    def qk_stage(u):
        return [lax.dot_general(q_ref[rows(u), sl], window(u, kp_ref, kc_ref, kn_ref, sl), (((1,), (1,)), ((), ())),
                                preferred_element_type=F32) for sl in cols]

    def softmax_stage(u, scores):
        valid = _band_mask(t, halo, pl.program_id(1) * tq + u * t, seq_len)
        probs, denoms = [], []
        lse_tile = jnp.zeros((t, LANES), F32)
        for h, s in enumerate(scores):
            s = jnp.where(valid, s, NEG_INF)
            m_raw = jnp.max(s, axis=-1, keepdims=True)
            p = jnp.exp2((s - m_raw) * (scale * LOG2E))
            denom = jnp.sum(p, axis=-1, keepdims=True)
            lse_tile = jnp.where(lane == h, m_raw * scale + jnp.log(denom), lse_tile)
            probs.append(p.astype(BF16))
            denoms.append(denom)
        lse_ref[rows(u), :] = lse_tile
        return probs, denoms

    def pv_stage(u, probs, denoms):
        for sl, p, denom in zip(cols, probs, denoms):
            o = jnp.dot(p, window(u, vp_ref, vc_ref, vn_ref, sl), preferred_element_type=F32) / denom
            o_ref[rows(u), sl] = o.astype(BF16)

    for u in range(nsub):
        pv_stage(u, *softmax_stage(u, qk_stage(u)))


def dil_attn(qk, v, n_seq, ls):
    w = C_HEADS * C_HEAD_DIM
    t = 2 * C_RADIUS
    nsub = next(n for n in (8, 4, 2, 1) if ls % (n * t) == 0)
    tq = t * nsub
    n_halo = ls // C_RADIUS
    qk3 = qk.reshape(n_seq, ls, qk.shape[1])
    v3 = v.reshape(n_seq, ls, w)
    prev = lambda i: jnp.maximum(i * (tq // C_RADIUS) - 1, 0)
    nxt = lambda i: jnp.minimum((i + 1) * (tq // C_RADIUS), n_halo - 1)
    big = lambda col: pl.BlockSpec((None, tq, w), lambda b, i: (b, i, col))
    halo = lambda col, f: pl.BlockSpec((None, C_RADIUS, w), lambda b, i: (b, f(i), col))
    o, lse = pl.pallas_call(
        functools.partial(_dil_attn_kernel, seq_len=ls), grid=(n_seq, ls // tq),
        in_specs=[big(0), halo(1, prev), big(1), halo(1, nxt), halo(0, prev), big(0), halo(0, nxt)],
        out_specs=(big(0), pl.BlockSpec((None, tq, LANES), lambda b, i: (b, i, 0))),
        out_shape=(jax.ShapeDtypeStruct((n_seq, ls, w), BF16),
                   jax.ShapeDtypeStruct((n_seq, ls, LANES), F32)),
        compiler_params=_cparams(2), name="dil_attn")(qk3, qk3, qk3, qk3, v3, v3, v3)
    return o.reshape(n_seq * ls, w), lse.reshape(n_seq * ls, LANES)


def _out_proj_even_kernel(hy_ref, at_ref, x_ref, w_ref, o_ref, *, nb):
    mixed = jnp.concatenate([_load_time_tiles(hy_ref, nb).astype(BF16), at_ref[...]], axis=1)
    o_ref[...] = x_ref[...] + jnp.dot(mixed, w_ref[...], preferred_element_type=F32)


def out_proj_even(hy, at, x2d, w_bf, *, tm=1024):
    m, d = x2d.shape
    batch, n1h = hy.shape[0], hy.shape[3]
    tps = n1h * N2 // tm
    nb = tm // N2
    half = pl.BlockSpec((tm, HC), lambda b, i: (b * tps + i, 0))
    full = pl.BlockSpec((tm, d), lambda b, i: (b * tps + i, 0))
    return pl.pallas_call(
        functools.partial(_out_proj_even_kernel, nb=nb), grid=(batch, tps),
        in_specs=[pl.BlockSpec((None, SLABS, J2, nb, SUBLANES, LANES), lambda b, i: (b, 0, 0, i, 0, 0)), half, full,
                  pl.BlockSpec((d, d), lambda b, i: (0, 0))],
        out_specs=full, out_shape=jax.ShapeDtypeStruct((m, d), F32),
        compiler_params=_cparams(2), name="out_proj_even")(hy, at, x2d, w_bf)


def _merge_out_kernel(*refs, final, dilations):
    nh = C_HEADS
    o_nat, lse_nat = refs[0], refs[1]
    pos = 2
    strided = []
    for _ in dilations:
        strided.append((refs[pos:pos + nh], refs[pos + nh]))
        pos += nh + 1
    gate_ref, x_ref, w_ref = refs[pos:pos + 3]
    pos += 3
    g_ref = refs[pos] if final else None
    out_ref, o_scr, l_scr = refs[-3:]
    for gi, (d, (o_slabs, lse_ref)) in enumerate(zip(dilations, strided)):
        per = o_nat.shape[0] // d
        for r in range(d):
            l_scr[gi, pl.ds(r, per, stride=d), :] = lse_ref[r]
            for h in range(nh):
                o_scr[gi, h, pl.ds(r, per, stride=d), :] = o_slabs[h][r].astype(F32)
    ls = [lse_nat[...]] + [l_scr[gi] for gi in range(len(dilations))]
    mx = functools.reduce(jnp.maximum, ls)
    es = [jnp.exp(l - mx) for l in ls]
    den = functools.reduce(lambda a, b: a + b, es)
    alphas = [e / den for e in es]
    hd = C_HEAD_DIM
    parts = []
    for h in range(nh):
        acc = alphas[0][:, h:h + 1] * o_nat[:, h * hd:(h + 1) * hd].astype(F32)
        for gi in range(len(dilations)):
            acc = acc + alphas[gi + 1][:, h:h + 1] * o_scr[gi, h]
        parts.append(acc)
    y = (jnp.concatenate(parts, axis=1) * _silu(gate_ref[...])).astype(BF16)
    r = x_ref[...] + jnp.dot(y, w_ref[...], preferred_element_type=F32)
    if final:
        ms = jnp.mean(r * r, axis=-1, keepdims=True)
        r = r * lax.rsqrt(ms + NORM_EPS) * g_ref[...]
    out_ref[...] = r


def merge_out(o_nat, lse_nat, strided, gate, x2d, w_bf, batch, seq_len, final_g=None, *, tm=1024):
    m, dm = x2d.shape
    tps = seq_len // tm
    nat = lambda wd: pl.BlockSpec((tm, wd), lambda b, i: (b * tps + i, 0))
    in_specs, args = [nat(dm), nat(LANES)], [o_nat, lse_nat]
    for d, o, lse in strided:
        per = tm // d
        in_specs += [pl.BlockSpec((None, d, per, LANES), functools.partial(lambda h, b, i: (b, 0, i, h), h))
                     for h in range(C_HEADS)]
        in_specs.append(pl.BlockSpec((None, d, per, LANES), lambda b, i: (b, 0, i, 0)))
        args += [o] * C_HEADS + [lse]
    in_specs += [nat(dm), nat(dm), pl.BlockSpec((dm, dm), lambda b, i: (0, 0))]
    args += [gate, x2d, w_bf]
    if final_g is not None:
        in_specs.append(pl.BlockSpec((1, dm), lambda b, i: (0, 0)))
        args.append(final_g.reshape(1, dm))
    n_str = len(strided)
    return pl.pallas_call(
        functools.partial(_merge_out_kernel, final=final_g is not None, dilations=tuple(d for d, _, _ in strided)),
        grid=(batch, tps), in_specs=in_specs, out_specs=nat(dm), out_shape=jax.ShapeDtypeStruct((m, dm), F32),
        scratch_shapes=[pltpu.VMEM((n_str, C_HEADS, tm, LANES), F32), pltpu.VMEM((n_str, tm, LANES), F32)],
        compiler_params=_cparams(2), name="merge_out")(*args)


def _even_w_in(w_in):
    return jnp.concatenate([w_in[:, 2048:2688], w_in[:, :2048], w_in[:, 2816:], w_in[:, 2688:2816]], axis=1).astype(BF16)


def _odd_w_in(w_in):
    w = C_HEADS * C_HEAD_DIM
    blk = lambda g, t: w_in[:, (3 * g + t) * w:(3 * g + t + 1) * w]
    per = [[blk(g, 0), blk(g, 1), blk(g, 2)] for g in range(len(C_PATTERNS))]
    per[0].append(w_in[:, 9 * w:])
    return [jnp.concatenate(cols, axis=1).astype(BF16) for cols in per]


def hyena_spectrum(seq_len, tabs, fw1, fb1, ff1, fw2, fb2, ff2, fw3):
    n1h = seq_len // N2
    w1_half, _, mat, _ = tabs
    filt = hyena_filter(seq_len, fw1, fb1, ff1, fw2, fb2, ff2, fw3)
    a = dft_stage1(w1_half, filt.reshape(1, 4 * SLABS, J2, n1h * SUBLANES, LANES))
    return dft_stage2_filter(mat, a.reshape(1, 4 * SLABS, J2, 2, n1h, SUBLANES, LANES))


def hybrid_layer(x2d, batch, seq_len, norm_g, w_in_bf, conv_w, conv_b, hyena_d, sink, w_out_bf, spec, tabs, rope):
    rope_t, half = rope
    qk, x1, x2s, v, at_gate, v_att = hybrid_proj(x2d, norm_g, w_in_bf, conv_w, conv_b, batch, seq_len, rope_t, half)
    z = hyena_conv_fused(v, x1, hyena_d[0], spec, 0, tabs)
    z = hyena_conv_fused(z, x2s, hyena_d[1], spec, 1, tabs)
    at = band_attn(qk, v_att, at_gate, sink, batch, seq_len)
    return out_proj_even(z, at, x2d, w_out_bf)


def dilated_layer(x2d, batch, seq_len, norm_g, w_in_bfs, w_out_bf, rope, final_g):
    w = C_HEADS * C_HEAD_DIM
    m = batch * seq_len
    rope_t, half = rope
    o_nat = lse_nat = gate = None
    strided = []
    for gi, (_, d) in enumerate(C_PATTERNS):
        ls = seq_len // d
        if d == 1:
            qk, v, gate = norm_proj(x2d, norm_g, w_in_bfs[gi], seq_len, rope_t, half,
                                    ((2 * w, BF16), (w, BF16), (w, F32)))
            o_nat, lse_nat = dil_attn(qk, v, batch, ls)
        else:
            qk, v = norm_proj_strided(x2d, norm_g, w_in_bfs[gi], batch, seq_len, d, rope_t, half,
                                      ((2 * w, BF16), (w, BF16)))
            o, lse = dil_attn(qk.reshape(m, 2 * w), v.reshape(m, w), batch * d, ls)
            strided.append((d, o.reshape(batch, d, ls, w), lse.reshape(batch, d, ls, LANES)))
    return merge_out(o_nat, lse_nat, strided, gate, x2d, w_out_bf, batch, seq_len, final_g)


def kernel(x_prompt, x_sample, a_norm, a_w_in, a_conv_w, a_conv_b, a_filt_w1, a_filt_b1, a_filt_f1, a_filt_w2, a_filt_b2, a_filt_f2, a_filt_w3, a_hyena_d, a_sink, a_w_out, c_norm, c_w_in, c_w_out, final_norm):
    depth = a_norm.shape[0] + c_norm.shape[0]
    xs = [x_prompt, x_sample]
    shapes = [(x.shape[0], x.shape[1]) for x in xs]
    acts = [x.reshape(-1, D_MODEL) for x in xs]
    seq_lens = sorted({s[1] for s in shapes})
    tabs = {sl: dft_tables(sl // N2) for sl in seq_lens}
    rope_even = {sl: rope_tables(sl, B_HEAD_DIM) for sl in seq_lens}
    rope_odd = {sl: rope_tables(sl, C_HEAD_DIM) for sl in seq_lens}
    for layer in range(depth):
        i = layer // 2
        if layer % 2 == 0:
            w_in_bf = _even_w_in(a_w_in[i])
            w_out_bf = a_w_out[i].astype(BF16)
            specs = {sl: hyena_spectrum(sl, tabs[sl], a_filt_w1[i], a_filt_b1[i], a_filt_f1[i], a_filt_w2[i],
                                        a_filt_b2[i], a_filt_f2[i], a_filt_w3[i]) for sl in seq_lens}
            acts = [hybrid_layer(x2d, b, sl, a_norm[i], w_in_bf, a_conv_w[i], a_conv_b[i], a_hyena_d[i], a_sink[i],
                                 w_out_bf, specs[sl], tabs[sl], rope_even[sl])
                    for x2d, (b, sl) in zip(acts, shapes)]
        else:
            w_in_bfs = _odd_w_in(c_w_in[i])
            w_out_bf = c_w_out[i].astype(BF16)
            final_g = final_norm if layer == depth - 1 else None
            acts = [dilated_layer(x2d, b, sl, c_norm[i], w_in_bfs, w_out_bf, rope_odd[sl], final_g)
                    for x2d, (b, sl) in zip(acts, shapes)]
    assert depth % 2 == 0
    return tuple(a.reshape(b, sl, D_MODEL) for a, (b, sl) in zip(acts, shapes))
```
